```python
import math
import jax
import jax.numpy as jnp
from jax import lax
import numpy as np

D_MODEL = 2048
BATCH = 1
SEQ = 16384
DEPTH = 2

GRID_W = 64
HEAD_DIM = 128
N_FOURIER_GROUPS = 4
FOURIER_W = N_FOURIER_GROUPS * HEAD_DIM
N_NA_HEADS = 8
NA_W = N_NA_HEADS * HEAD_DIM
WIN_H = 8
WIN_W = 16
N_MEM = 256
N_MEM_HEADS = 4
MEM_W = N_MEM_HEADS * HEAD_DIM
N_BRANCHES = 3
IN_W = FOURIER_W + 3 * NA_W + MEM_W + N_BRANCHES * D_MODEL
D_FF = 5632
N_EXPERTS = 8
TOP_K = 2
MOE_BLOCK = 128
EPS = 1e-6

kernel_name = "hybrid_fourier_natten_memory_moe_encoder"


def rms_norm(x, g):
    xf = x.astype(jnp.float32)
    y = xf * lax.rsqrt(jnp.mean(xf * xf, axis=-1, keepdims=True) + EPS)
    return (y * g.astype(jnp.float32)).astype(x.dtype)


def fourier_mix(u):
    b, s, _ = u.shape
    ug = u.astype(jnp.float32).reshape(b, s, N_FOURIER_GROUPS, HEAD_DIM)
    f = jnp.fft.fft2(ug, axes=(1, 3), norm="ortho").real
    return f.reshape(b, s, FOURIER_W).astype(u.dtype)


def neighbourhood_attention(q, k, v, rpb):
    b, s, h, dh = q.shape
    rows = s // GRID_W
    kh = min(WIN_H, rows)
    cols = np.arange(GRID_W)
    col_start = np.clip(cols - WIN_W // 2, 0, GRID_W - WIN_W)
    col_idx = col_start[:, None] + np.arange(WIN_W)[None, :]
    col_off = col_idx - cols[:, None] + (WIN_W - 1)
    q_rows = jnp.moveaxis(q.reshape(b, rows, GRID_W, h, dh), 1, 0)
    k_grid = k.reshape(b, rows, GRID_W, h, dh)
    v_grid = v.reshape(b, rows, GRID_W, h, dh)
    scale = dh ** -0.5

    def row_step(args):
        r, q_r = args
        r_start = jnp.clip(r - kh // 2, 0, rows - kh)
        k_rows = lax.dynamic_slice_in_dim(k_grid, r_start, kh, axis=1)
        v_rows = lax.dynamic_slice_in_dim(v_grid, r_start, kh, axis=1)
        k_win = k_rows[:, :, col_idx]
        v_win = v_rows[:, :, col_idx]
        row_off = r_start + jnp.arange(kh) - r + (WIN_H - 1)
        bias = rpb[:, row_off][:, :, col_off]
        bias = jnp.transpose(bias, (0, 2, 1, 3)).astype(jnp.float32)
        sc = jnp.einsum('bchd,bicjhd->bhcij', q_r, k_win).astype(jnp.float32) * scale
        sc = sc + bias[None]
        p = jax.nn.softmax(sc.reshape(b, h, GRID_W, kh * WIN_W), axis=-1)
        p = p.reshape(b, h, GRID_W, kh, WIN_W).astype(v.dtype)
        return jnp.einsum('bhcij,bicjhd->bchd', p, v_win)

    out = lax.map(row_step, (jnp.arange(rows, dtype=jnp.int32), q_rows))
    return jnp.moveaxis(out, 0, 1).reshape(b, s, h * dh)


def memory_attention(q, mem_h, w_kv, q_g, k_g):
    b, s, _ = q.shape
    m = mem_h.shape[1]
    kv = mem_h @ w_kv
    k, v = jnp.split(kv, 2, axis=-1)
    q = rms_norm(q.reshape(b, s, N_MEM_HEADS, HEAD_DIM), q_g)
    k = rms_norm(k.reshape(b, m, N_MEM_HEADS, HEAD_DIM), k_g)
    v = v.reshape(b, m, N_MEM_HEADS, HEAD_DIM)
    sc = jnp.einsum('bshd,bmhd->bhsm', q, k).astype(jnp.float32) * (HEAD_DIM ** -0.5)
    p = jax.nn.softmax(sc, axis=-1).astype(v.dtype)
    o = jnp.einsum('bhsm,bmhd->bshd', p, v)
    return o.reshape(b, s, MEM_W)


def swiglu(h, w_gate, w_up, w_down):
    return (jax.nn.silu(h @ w_gate) * (h @ w_up)) @ w_down


def moe_swiglu(h, w_router, w_gate, w_up, w_down):
    b, s, d = h.shape
    n = b * s
    hf = h.reshape(n, d)
    logits = (hf @ w_router).astype(jnp.float32)
    top_logit, top_e = lax.top_k(logits, TOP_K)
    gates = jax.nn.softmax(top_logit, axis=-1)
    e_flat = top_e.reshape(-1)
    t_flat = jnp.repeat(jnp.arange(n, dtype=jnp.int32), TOP_K)
    w_flat = gates.reshape(-1)
    order = jnp.argsort(e_flat)
    e_sorted = e_flat[order]
    counts = jnp.bincount(e_flat, length=N_EXPERTS)
    starts = jnp.cumsum(counts) - counts
    padded = (counts + MOE_BLOCK - 1) // MOE_BLOCK * MOE_BLOCK
    pad_ends = jnp.cumsum(padded)
    pad_starts = pad_ends - padded
    n_assign = n * TOP_K
    dest = pad_starts[e_sorted] + jnp.arange(n_assign) - starts[e_sorted]
    cap = n_assign + N_EXPERTS * MOE_BLOCK
    n_blocks = cap // MOE_BLOCK
    tok_buf = jnp.full((cap,), n, jnp.int32).at[dest].set(t_flat[order])
    wt_buf = jnp.zeros((cap,), jnp.float32).at[dest].set(w_flat[order])
    blk_e = jnp.clip(jnp.searchsorted(pad_ends, jnp.arange(n_blocks) * MOE_BLOCK, side='right'),
                     0, N_EXPERTS - 1)
    x_pad = jnp.concatenate([hf, jnp.zeros((1, d), hf.dtype)], axis=0)
    xb = x_pad[tok_buf].reshape(n_blocks, MOE_BLOCK, d)

    def expert_block(args):
        xblk, e = args
        return swiglu(xblk, w_gate[e], w_up[e], w_down[e])

    yb = lax.map(expert_block, (xb, blk_e)).reshape(cap, d)
    y = jnp.zeros((n + 1, d), jnp.float32).at[tok_buf].add(yb.astype(jnp.float32) * wt_buf[:, None])
    return y[:n].reshape(b, s, d).astype(h.dtype)


def setup_inputs(seed: int = 0) -> dict:
    key = jax.random.key(seed)
    ks = jax.random.split(key, 24)
    L = DEPTH
    n_dense = (DEPTH + 1) // 2
    n_moe = DEPTH // 2
    f32 = jnp.float32

    def nrm(k, shape, scale):
        return jax.random.normal(k, shape, f32) * scale

    def gain(k, shape):
        return 1.0 + 0.02 * jax.random.normal(k, shape, f32)

    return {
        "x": nrm(ks[0], (BATCH, SEQ, D_MODEL), 1.0),
        "mem": nrm(ks[1], (BATCH, N_MEM, D_MODEL), 1.0),
        "ln_mix_g": gain(ks[2], (L, D_MODEL)),
        "w_in": nrm(ks[3], (L, D_MODEL, IN_W), D_MODEL ** -0.5),
        "na_q_g": gain(ks[4], (L, HEAD_DIM)),
        "na_k_g": gain(ks[5], (L, HEAD_DIM)),
        "na_rpb": nrm(ks[6], (L, N_NA_HEADS, 2 * WIN_H - 1, 2 * WIN_W - 1), 0.02),
        "mem_ln_g": gain(ks[7], (L, D_MODEL)),
        "w_mem_kv": nrm(ks[8], (L, D_MODEL, 2 * MEM_W), D_MODEL ** -0.5),
        "mem_q_g": gain(ks[9], (L, HEAD_DIM)),
        "mem_k_g": gain(ks[10], (L, HEAD_DIM)),
        "w_fourier_out": nrm(ks[11], (L, FOURIER_W, D_MODEL), FOURIER_W ** -0.5),
        "w_na_out": nrm(ks[12], (L, NA_W, D_MODEL), NA_W ** -0.5),
        "w_mem_out": nrm(ks[13], (L, MEM_W, D_MODEL), MEM_W ** -0.5),
        "w_o": nrm(ks[14], (L, D_MODEL, D_MODEL), D_MODEL ** -0.5),
        "ln_ffn_g": gain(ks[15], (L, D_MODEL)),
        "ffn_w_gate": nrm(ks[16], (n_dense, D_MODEL, D_FF), D_MODEL ** -0.5),
        "ffn_w_up": nrm(ks[17], (n_dense, D_MODEL, D_FF), D_MODEL ** -0.5),
        "ffn_w_down": nrm(ks[18], (n_dense, D_FF, D_MODEL), D_FF ** -0.5),
        "moe_router": nrm(ks[19], (n_moe, D_MODEL, N_EXPERTS), D_MODEL ** -0.5),
        "moe_w_gate": nrm(ks[20], (n_moe, N_EXPERTS, D_MODEL, D_FF), D_MODEL ** -0.5),
        "moe_w_up": nrm(ks[21], (n_moe, N_EXPERTS, D_MODEL, D_FF), D_MODEL ** -0.5),
        "moe_w_down": nrm(ks[22], (n_moe, N_EXPERTS, D_FF, D_MODEL), D_FF ** -0.5),
    }


def reference(x, mem, ln_mix_g, w_in, na_q_g, na_k_g, na_rpb, mem_ln_g, w_mem_kv,
              mem_q_g, mem_k_g, w_fourier_out, w_na_out, w_mem_out, w_o, ln_ffn_g,
              ffn_w_gate, ffn_w_up, ffn_w_down, moe_router, moe_w_gate, moe_w_up,
              moe_w_down):
    b, s, d = x.shape
    splits = [FOURIER_W,
              FOURIER_W + NA_W,
              FOURIER_W + 2 * NA_W,
              FOURIER_W + 3 * NA_W,
              FOURIER_W + 3 * NA_W + MEM_W]
    for l in range(DEPTH):
        h = rms_norm(x, ln_mix_g[l])
        proj = h @ w_in[l]
        f_in, q_na, k_na, v_na, q_mem, gate_logits = jnp.split(proj, splits, axis=-1)

        o_f = fourier_mix(f_in) @ w_fourier_out[l]

        q_na = rms_norm(q_na.reshape(b, s, N_NA_HEADS, HEAD_DIM), na_q_g[l])
        k_na = rms_norm(k_na.reshape(b, s, N_NA_HEADS, HEAD_DIM), na_k_g[l])
        v_na = v_na.reshape(b, s, N_NA_HEADS, HEAD_DIM)
        o_n = neighbourhood_attention(q_na, k_na, v_na, na_rpb[l]) @ w_na_out[l]

        mem_h = rms_norm(mem, mem_ln_g[l])
        o_m = memory_attention(q_mem, mem_h, w_mem_kv[l], mem_q_g[l], mem_k_g[l]) @ w_mem_out[l]

        g = jax.nn.sigmoid(gate_logits.astype(jnp.float32)).astype(x.dtype)
        g = g.reshape(b, s, N_BRANCHES, d)
        merged = g[:, :, 0] * o_f + g[:, :, 1] * o_n + g[:, :, 2] * o_m
        x = x + merged @ w_o[l]

        h2 = rms_norm(x, ln_ffn_g[l])
        if l % 2 == 0:
            i = l // 2
            y = swiglu(h2, ffn_w_gate[i], ffn_w_up[i], ffn_w_down[i])
        else:
            i = l // 2
            y = moe_swiglu(h2, moe_router[i], moe_w_gate[i], moe_w_up[i], moe_w_down[i])
        x = x + y
    return x
```

```python
import functools
import math

import numpy as np
import jax
import jax.numpy as jnp
from jax import lax
from jax.experimental import pallas as pl
from jax.experimental.pallas import tpu as pltpu

F32 = jnp.float32
BF16 = jnp.bfloat16

D_MODEL = 2048
SEQ = 16384
DEPTH = 2
GRID_W = 64
ROWS = SEQ // GRID_W
HEAD_DIM = 128
N_FOURIER_GROUPS = 4
FOURIER_W = N_FOURIER_GROUPS * HEAD_DIM
N_NA_HEADS = 8
NA_W = N_NA_HEADS * HEAD_DIM
WIN_H = 8
WIN_W = 16
N_MEM = 256
N_MEM_HEADS = 4
MEM_W = N_MEM_HEADS * HEAD_DIM
IN_W = FOURIER_W + 3 * NA_W + MEM_W + 3 * D_MODEL
D_FF = 5632
N_EXPERTS = 8
TOP_K = 2
EPS = 1e-6

Q_NA_OFF = FOURIER_W
K_NA_OFF = Q_NA_OFF + NA_W
V_NA_OFF = K_NA_OFF + NA_W
Q_MEM_OFF = V_NA_OFF + NA_W
GATE_OFF = Q_MEM_OFF + MEM_W

MIB = 1024 * 1024
NEG_BIG = -1e30

FFT_A = 128
FFT_B = SEQ // FFT_A


def _params(semantics, vmem_mib):
    return pltpu.CompilerParams(dimension_semantics=semantics,
                                vmem_limit_bytes=int(vmem_mib * MIB))


def _rms_rows(x, gain):
    ms = jnp.mean(x * x, axis=-1, keepdims=True)
    return x * lax.rsqrt(ms + EPS) * gain


def _norm_mm_kernel(x_ref, g_ref, w_ref, o_ref, h_ref, *, row_chunk):
    @pl.when(pl.program_id(1) == 0)
    def _():
        def body(c, carry):
            rows = pl.ds(pl.multiple_of(c * row_chunk, row_chunk), row_chunk)
            h_ref[rows, :] = _rms_rows(x_ref[rows, :].astype(F32), g_ref[...]).astype(BF16)
            return carry
        lax.fori_loop(0, h_ref.shape[0] // row_chunk, body, 0)

    o_ref[...] = jnp.dot(h_ref[...], w_ref[...].astype(BF16),
                         preferred_element_type=F32).astype(o_ref.dtype)


def _norm_mm(x, g, w, *, tm, tn, out_dtype, vmem_mib, name):
    m, k = x.shape
    n = w.shape[1]
    row_chunk = min(tm, 128)
    return pl.pallas_call(
        functools.partial(_norm_mm_kernel, row_chunk=row_chunk),
        grid=(m // tm, n // tn),
        in_specs=[pl.BlockSpec((tm, k), lambda i, j: (i, 0)),
                  pl.BlockSpec((1, k), lambda i, j: (0, 0)),
                  pl.BlockSpec((k, tn), lambda i, j: (0, j))],
        out_specs=pl.BlockSpec((tm, tn), lambda i, j: (i, j)),
        out_shape=jax.ShapeDtypeStruct((m, n), out_dtype),
        scratch_shapes=[pltpu.VMEM((tm, k), BF16)],
        compiler_params=_params(("parallel", "arbitrary"), vmem_mib),
        name=name,
    )(x, g.reshape(1, k), w)


def _dft_tables():
    a = np.arange(FFT_A)
    ang1 = 2.0 * np.pi * np.outer(a, a) / FFT_A
    w1 = np.concatenate([np.cos(ang1), -np.sin(ang1)], axis=0)
    k1 = np.arange(FFT_A)[:, None, None]
    k2 = np.arange(FFT_B)[None, :, None]
    s2 = np.arange(FFT_B)[None, None, :]
    ang2 = 2.0 * np.pi * ((s2 * (k1 + FFT_A * k2)) % SEQ) / SEQ
    g2 = np.concatenate([np.cos(ang2), np.sin(ang2)], axis=2)
    c = np.arange(HEAD_DIM)
    angc = 2.0 * np.pi * np.outer(c, c) / HEAD_DIM
    eye = np.eye(N_FOURIER_GROUPS)
    bd = np.concatenate([np.kron(eye, np.cos(angc)), np.kron(eye, np.sin(angc))], axis=0)
    return (jnp.asarray(w1, dtype=BF16), jnp.asarray(g2, dtype=BF16),
            jnp.asarray(bd, dtype=BF16))


def _dft1_kernel(w_ref, x_ref, br_ref, bi_ref):
    r = jnp.dot(w_ref[...], x_ref[...], preferred_element_type=F32)
    br_ref[...] = r[:FFT_A].astype(BF16)
    bi_ref[...] = r[FFT_A:].astype(BF16)


def _dft2_kernel(br_ref, bi_ref, g_ref, bd_ref, o_ref, *, kb, norm):
    for i in range(kb):
        br = br_ref[i]
        bi = bi_ref[i]
        g = g_ref[i]
        zr = jnp.dot(g, jnp.concatenate([br, bi], axis=0), preferred_element_type=F32)
        zi = jnp.dot(g, jnp.concatenate([bi, -br], axis=0), preferred_element_type=F32)
        z = jnp.concatenate([zr, zi], axis=1).astype(BF16)
        y = jnp.dot(z, bd_ref[...], preferred_element_type=F32) * norm
        o_ref[:, i * FOURIER_W:(i + 1) * FOURIER_W] = y.astype(o_ref.dtype)


def _fourier_mix(f_in, tables):
    w1, g2, bd = tables
    lanes = FFT_B * FOURIER_W
    x2d = f_in.reshape(FFT_A, lanes)
    tn = 8192
    br, bi = pl.pallas_call(
        _dft1_kernel,
        grid=(lanes // tn,),
        in_specs=[pl.BlockSpec((2 * FFT_A, FFT_A), lambda j: (0, 0)),
                  pl.BlockSpec((FFT_A, tn), lambda j: (0, j))],
        out_specs=[pl.BlockSpec((FFT_A, tn), lambda j: (0, j)),
                   pl.BlockSpec((FFT_A, tn), lambda j: (0, j))],
        out_shape=[jax.ShapeDtypeStruct((FFT_A, lanes), BF16)] * 2,
        compiler_params=_params(("parallel",), 40),
        name="fourier_stage1",
    )(w1, x2d)
    br3 = br.reshape(FFT_A, FFT_B, FOURIER_W)
    bi3 = bi.reshape(FFT_A, FFT_B, FOURIER_W)
    kb = 8
    norm = 1.0 / math.sqrt(SEQ * HEAD_DIM)
    y2d = pl.pallas_call(
        functools.partial(_dft2_kernel, kb=kb, norm=norm),
        grid=(FFT_A // kb,),
        in_specs=[pl.BlockSpec((kb, FFT_B, FOURIER_W), lambda j: (j, 0, 0)),
                  pl.BlockSpec((kb, FFT_B, FOURIER_W), lambda j: (j, 0, 0)),
                  pl.BlockSpec((kb, FFT_B, 2 * FFT_B), lambda j: (j, 0, 0)),
                  pl.BlockSpec((2 * FOURIER_W, FOURIER_W), lambda j: (0, 0))],
        out_specs=pl.BlockSpec((FFT_B, kb * FOURIER_W), lambda j: (0, j)),
        out_shape=jax.ShapeDtypeStruct((FFT_B, FFT_A * FOURIER_W), BF16),
        compiler_params=_params(("parallel",), 40),
        name="fourier_stage2",
    )(br3, bi3, g2, bd)
    return y2d.reshape(SEQ, FOURIER_W)


NA_SUB_ROWS = 4
NA_KEY_ROWS = NA_SUB_ROWS + WIN_H
NA_SUBS_PER_STEP = 4
NA_N_SUB = ROWS // NA_SUB_ROWS
NA_MAX_KEY_START = ROWS - NA_KEY_ROWS


def _na_bias_indices():
    nq = NA_SUB_ROWS * GRID_W
    nk = NA_KEY_ROWS * GRID_W
    dr = np.zeros((3, nq, nk), np.int32)
    dc = np.zeros((3, nq, nk), np.int32)
    valid = np.zeros((3, nq, nk), bool)
    qi, qc = np.divmod(np.arange(nq), GRID_W)
    kj, kc = np.divmod(np.arange(nk), GRID_W)
    col_start = np.clip(qc - WIN_W // 2, 0, GRID_W - WIN_W)
    for case, r0 in enumerate((0, 2 * NA_SUB_ROWS, ROWS - NA_SUB_ROWS)):
        ks = int(np.clip(r0 - WIN_H // 2, 0, NA_MAX_KEY_START))
        r = r0 + qi
        r_start = np.clip(r - WIN_H // 2, 0, ROWS - WIN_H)
        key_row = ks + kj
        v_row = (key_row[None, :] >= r_start[:, None]) & (key_row[None, :] < r_start[:, None] + WIN_H)
        v_col = (kc[None, :] >= col_start[:, None]) & (kc[None, :] < col_start[:, None] + WIN_W)
        v = v_row & v_col
        valid[case] = v
        dr[case] = np.where(v, key_row[None, :] - r[:, None] + (WIN_H - 1), 0)
        dc[case] = np.where(v, kc[None, :] - qc[:, None] + (WIN_W - 1), 0)
    return dr, dc, valid


def _na_bias_table(rpb):
    dr, dc, valid = _na_bias_indices()
    vals = rpb.astype(F32)[:, dr, dc]
    return jnp.where(valid[None], vals, NEG_BIG)


def _na_kernel(q_ref, k_ref, v_ref, qg_ref, kg_ref, bias_ref, o_ref, kn_ref):
    step = pl.program_id(1)
    nq = NA_SUB_ROWS * GRID_W
    nk = NA_KEY_ROWS * GRID_W

    @pl.when(step == 0)
    def _():
        chunk = 1024

        def body(c, carry):
            rows = pl.ds(pl.multiple_of(c * chunk, chunk), chunk)
            kn_ref[rows, :] = _rms_rows(k_ref[rows, :].astype(F32), kg_ref[...]).astype(BF16)
            return carry
        lax.fori_loop(0, SEQ // chunk, body, 0)

    scale = HEAD_DIM ** -0.5
    for sb in range(NA_SUBS_PER_STEP):
        sub = step * NA_SUBS_PER_STEP + sb
        key_row0 = jnp.clip(sub * NA_SUB_ROWS - WIN_H // 2, 0, NA_MAX_KEY_START)
        case = jnp.where(sub == 0, 0, jnp.where(sub == NA_N_SUB - 1, 2, 1))
        q = q_ref[sb * nq:(sb + 1) * nq, :].astype(F32)
        qn = (_rms_rows(q, qg_ref[...]) * scale).astype(BF16)
        kstart = pl.multiple_of(key_row0 * GRID_W, GRID_W)
        kw = kn_ref[pl.ds(kstart, nk), :]
        vw = v_ref[pl.ds(kstart, nk), :]
        s = lax.dot_general(qn, kw, (((1,), (1,)), ((), ())), preferred_element_type=F32)
        s = s + bias_ref[0, case]
        m = jnp.max(s, axis=-1, keepdims=True)
        p = jnp.exp(s - m)
        l = jnp.sum(p, axis=-1, keepdims=True)
        o = jnp.dot(p.astype(BF16), vw, preferred_element_type=F32) / l
        o_ref[sb * nq:(sb + 1) * nq, :] = o.astype(o_ref.dtype)


def _na_attention(proj, q_g, k_g, bias):
    nq = NA_SUB_ROWS * GRID_W
    nk = NA_KEY_ROWS * GRID_W
    tq = NA_SUBS_PER_STEP * nq
    qb, kb_, vb = Q_NA_OFF // HEAD_DIM, K_NA_OFF // HEAD_DIM, V_NA_OFF // HEAD_DIM
    return pl.pallas_call(
        _na_kernel,
        grid=(N_NA_HEADS, SEQ // tq),
        in_specs=[pl.BlockSpec((tq, HEAD_DIM), lambda h, i: (i, qb + h)),
                  pl.BlockSpec((SEQ, HEAD_DIM), lambda h, i: (0, kb_ + h)),
                  pl.BlockSpec((SEQ, HEAD_DIM), lambda h, i: (0, vb + h)),
                  pl.BlockSpec((1, HEAD_DIM), lambda h, i: (0, 0)),
                  pl.BlockSpec((1, HEAD_DIM), lambda h, i: (0, 0)),
                  pl.BlockSpec((1, 3, nq, nk), lambda h, i: (h, 0, 0, 0))],
        out_specs=pl.BlockSpec((tq, HEAD_DIM), lambda h, i: (i, h)),
        out_shape=jax.ShapeDtypeStruct((SEQ, NA_W), BF16),
        scratch_shapes=[pltpu.VMEM((SEQ, HEAD_DIM), BF16)],
        compiler_params=_params(("parallel", "arbitrary"), 48),
        name="na_attention",
    )(proj, proj, proj, q_g.reshape(1, HEAD_DIM), k_g.reshape(1, HEAD_DIM), bias)


def _mem_attn_kernel(q_ref, kv_ref, qg_ref, kg_ref, o_ref):
    scale = HEAD_DIM ** -0.5
    for h in range(N_MEM_HEADS):
        cols = slice(h * HEAD_DIM, (h + 1) * HEAD_DIM)
        qn = (_rms_rows(q_ref[:, cols].astype(F32), qg_ref[...]) * scale).astype(BF16)
        kn = _rms_rows(kv_ref[:, cols].astype(F32), kg_ref[...]).astype(BF16)
        v = kv_ref[:, MEM_W + h * HEAD_DIM:MEM_W + (h + 1) * HEAD_DIM].astype(BF16)
        s = lax.dot_general(qn, kn, (((1,), (1,)), ((), ())), preferred_element_type=F32)
        m = jnp.max(s, axis=-1, keepdims=True)
        p = jnp.exp(s - m)
        l = jnp.sum(p, axis=-1, keepdims=True)
        o = jnp.dot(p.astype(BF16), v, preferred_element_type=F32) / l
        o_ref[:, cols] = o.astype(o_ref.dtype)


def _mem_attention(proj, kv, q_g, k_g):
    tm = 512
    return pl.pallas_call(
        _mem_attn_kernel,
        grid=(SEQ // tm,),
        in_specs=[pl.BlockSpec((tm, MEM_W), lambda i: (i, Q_MEM_OFF // MEM_W)),
                  pl.BlockSpec((N_MEM, 2 * MEM_W), lambda i: (0, 0)),
                  pl.BlockSpec((1, HEAD_DIM), lambda i: (0, 0)),
                  pl.BlockSpec((1, HEAD_DIM), lambda i: (0, 0))],
        out_specs=pl.BlockSpec((tm, MEM_W), lambda i: (i, 0)),
        out_shape=jax.ShapeDtypeStruct((SEQ, MEM_W), BF16),
        compiler_params=_params(("parallel",), 32),
        name="mem_attention",
    )(proj, kv, q_g.reshape(1, HEAD_DIM), k_g.reshape(1, HEAD_DIM))


def _merge_kernel(fm_ref, na_ref, mo_ref, g0_ref, g1_ref, g2_ref, wf_ref, wn_ref, wm_ref, o_ref,
                  *, col_chunk):
    for c in range(D_MODEL // col_chunk):
        cols = slice(c * col_chunk, (c + 1) * col_chunk)
        o_f = jnp.dot(fm_ref[...], wf_ref[:, cols], preferred_element_type=F32)
        o_n = jnp.dot(na_ref[...], wn_ref[:, cols], preferred_element_type=F32)
        o_m = jnp.dot(mo_ref[...], wm_ref[:, cols], preferred_element_type=F32)
        acc = jax.nn.sigmoid(g0_ref[:, cols].astype(F32)) * o_f
        acc = acc + jax.nn.sigmoid(g1_ref[:, cols].astype(F32)) * o_n
        acc = acc + jax.nn.sigmoid(g2_ref[:, cols].astype(F32)) * o_m
        o_ref[:, cols] = acc.astype(o_ref.dtype)


def _merge(fm, o_na, o_mem, proj, wf, wn, wm):
    tm = 512
    gb = GATE_OFF // D_MODEL
    return pl.pallas_call(
        functools.partial(_merge_kernel, col_chunk=512),
        grid=(SEQ // tm,),
        in_specs=[pl.BlockSpec((tm, FOURIER_W), lambda i: (i, 0)),
                  pl.BlockSpec((tm, NA_W), lambda i: (i, 0)),
                  pl.BlockSpec((tm, MEM_W), lambda i: (i, 0)),
                  pl.BlockSpec((tm, D_MODEL), lambda i: (i, gb)),
                  pl.BlockSpec((tm, D_MODEL), lambda i: (i, gb + 1)),
                  pl.BlockSpec((tm, D_MODEL), lambda i: (i, gb + 2)),
                  pl.BlockSpec((FOURIER_W, D_MODEL), lambda i: (0, 0)),
                  pl.BlockSpec((NA_W, D_MODEL), lambda i: (0, 0)),
                  pl.BlockSpec((MEM_W, D_MODEL), lambda i: (0, 0))],
        out_specs=pl.BlockSpec((tm, D_MODEL), lambda i: (i, 0)),
        out_shape=jax.ShapeDtypeStruct((SEQ, D_MODEL), BF16),
        compiler_params=_params(("parallel",), 48),
        name="gated_merge",
    )(fm, o_na, o_mem, proj, proj, proj, wf, wn, wm)


def _oproj_kernel(a_ref, w_ref, x_ref, g_ref, xo_ref, h_ref, *, row_chunk):
    xo_ref[...] = jnp.dot(a_ref[...], w_ref[...], preferred_element_type=F32) + x_ref[...]

    def body(c, carry):
        rows = pl.ds(pl.multiple_of(c * row_chunk, row_chunk), row_chunk)
        h_ref[rows, :] = _rms_rows(xo_ref[rows, :], g_ref[...]).astype(h_ref.dtype)
        return carry
    lax.fori_loop(0, xo_ref.shape[0] // row_chunk, body, 0)


def _out_proj(merged, w_o, x, g):
    tm = 512
    return pl.pallas_call(
        functools.partial(_oproj_kernel, row_chunk=128),
        grid=(SEQ // tm,),
        in_specs=[pl.BlockSpec((tm, D_MODEL), lambda i: (i, 0)),
                  pl.BlockSpec((D_MODEL, D_MODEL), lambda i: (0, 0)),
                  pl.BlockSpec((tm, D_MODEL), lambda i: (i, 0)),
                  pl.BlockSpec((1, D_MODEL), lambda i: (0, 0))],
        out_specs=[pl.BlockSpec((tm, D_MODEL), lambda i: (i, 0)),
                   pl.BlockSpec((tm, D_MODEL), lambda i: (i, 0))],
        out_shape=[jax.ShapeDtypeStruct((SEQ, D_MODEL), F32),
                   jax.ShapeDtypeStruct((SEQ, D_MODEL), BF16)],
        compiler_params=_params(("parallel",), 48),
        name="out_proj",
    )(merged, w_o, x, g.reshape(1, D_MODEL))


def _swiglu_kernel(be_ref, nu_ref, x_ref, wg_ref, wu_ref, wd_ref, *rest, nf, has_scale, has_resid):
    rest = list(rest)
    scale_ref = rest.pop(0) if has_scale else None
    resid_ref = rest.pop(0) if has_resid else None
    o_ref, acc_ref = rest
    i = pl.program_id(0)
    f = pl.program_id(1)
    active = i < nu_ref[0]

    @pl.when(active)
    def _():
        x = x_ref[...]
        g = jnp.dot(x, wg_ref[0].astype(BF16), preferred_element_type=F32)
        u = jnp.dot(x, wu_ref[0].astype(BF16), preferred_element_type=F32)
        a = (g * jax.nn.sigmoid(g) * u).astype(BF16)
        y = jnp.dot(a, wd_ref[0].astype(BF16), preferred_element_type=F32)

        @pl.when(f == 0)
        def _():
            acc_ref[...] = y

        @pl.when(f > 0)
        def _():
            acc_ref[...] += y

    @pl.when(f == nf - 1)
    def _():
        @pl.when(active)
        def _():
            y = acc_ref[...]
            if has_scale:
                y = y * scale_ref[...]
            if has_resid:
                y = y + resid_ref[...]
            o_ref[...] = y.astype(o_ref.dtype)

        @pl.when(jnp.logical_not(active))
        def _():
            o_ref[...] = jnp.zeros(o_ref.shape, o_ref.dtype)


def _swiglu(x, blk_e, n_used, wg, wu, wd, *, tm, tf, scale=None, resid=None, out_dtype,
            vmem_mib, name):
    m, d = x.shape
    nf = D_FF // tf
    n_tiles = m // tm

    def x_map(i, f, be, nu):
        return (jnp.minimum(i, nu[0] - 1), 0)

    def f_idx(i, f, nu):
        return jnp.where(i < nu[0], f, nf - 1)

    in_specs = [pl.BlockSpec((tm, d), x_map),
                pl.BlockSpec((1, d, tf), lambda i, f, be, nu: (be[i], 0, f_idx(i, f, nu))),
                pl.BlockSpec((1, d, tf), lambda i, f, be, nu: (be[i], 0, f_idx(i, f, nu))),
                pl.BlockSpec((1, tf, d), lambda i, f, be, nu: (be[i], f_idx(i, f, nu), 0))]
    args = [x, wg, wu, wd]
    if scale is not None:
        in_specs.append(pl.BlockSpec((tm, 1), x_map))
        args.append(scale)
    if resid is not None:
        in_specs.append(pl.BlockSpec((tm, d), x_map))
        args.append(resid)
    grid_spec = pltpu.PrefetchScalarGridSpec(
        num_scalar_prefetch=2,
        grid=(n_tiles, nf),
        in_specs=in_specs,
        out_specs=pl.BlockSpec((tm, d), lambda i, f, be, nu: (i, 0)),
        scratch_shapes=[pltpu.VMEM((tm, d), F32)],
    )
    return pl.pallas_call(
        functools.partial(_swiglu_kernel, nf=nf, has_scale=scale is not None,
                          has_resid=resid is not None),
        grid_spec=grid_spec,
        out_shape=jax.ShapeDtypeStruct((m, d), out_dtype),
        compiler_params=_params(("arbitrary", "arbitrary"), vmem_mib),
        name=name,
    )(blk_e, n_used, *args)


ROUTER_LANES = 128


def _router_kernel(x_ref, g_ref, w_ref, idx_ref, gate_ref):
    h = _rms_rows(x_ref[...], g_ref[...])
    logits = jnp.dot(h, w_ref[...], preferred_element_type=F32,
                     precision=lax.Precision.HIGHEST)
    lane = lax.broadcasted_iota(jnp.int32, logits.shape, 1).astype(F32)
    logits = jnp.where(lane < N_EXPERTS, logits, -jnp.inf)
    m1 = jnp.max(logits, axis=-1, keepdims=True)
    i1 = jnp.min(jnp.where(logits == m1, lane, float(ROUTER_LANES)), axis=-1, keepdims=True)
    rest = jnp.where(lane == i1, -jnp.inf, logits)
    m2 = jnp.max(rest, axis=-1, keepdims=True)
    i2 = jnp.min(jnp.where(rest == m2, lane, float(ROUTER_LANES)), axis=-1, keepdims=True)
    e21 = jnp.exp(m2 - m1)
    g1 = 1.0 / (1.0 + e21)
    g2 = e21 * g1
    idx_ref[...] = jnp.where(lane == 0.0, i1, i2).astype(jnp.int32)
    gate_ref[...] = jnp.where(lane == 0.0, g1, g2)


def _router(x, g, w_router):
    tm = 256
    w_pad = jnp.zeros((D_MODEL, ROUTER_LANES), F32).at[:, :N_EXPERTS].set(w_router.astype(F32))
    return pl.pallas_call(
        _router_kernel,
        grid=(SEQ // tm,),
        in_specs=[pl.BlockSpec((tm, D_MODEL), lambda i: (i, 0)),
                  pl.BlockSpec((1, D_MODEL), lambda i: (0, 0)),
                  pl.BlockSpec((D_MODEL, ROUTER_LANES), lambda i: (0, 0))],
        out_specs=[pl.BlockSpec((tm, ROUTER_LANES), lambda i: (i, 0)),
                   pl.BlockSpec((tm, ROUTER_LANES), lambda i: (i, 0))],
        out_shape=[jax.ShapeDtypeStruct((SEQ, ROUTER_LANES), jnp.int32),
                   jax.ShapeDtypeStruct((SEQ, ROUTER_LANES), F32)],
        compiler_params=_params(("parallel",), 32),
        name="moe_router",
    )(x, g.reshape(1, D_MODEL), w_pad)


MOE_TM = 512


def _moe(x_new, h2, g_ffn, w_router, wg, wu, wd):
    n = SEQ
    n_assign = n * TOP_K
    idx, gate = _router(x_new, g_ffn, w_router)
    e_flat = idx[:, :TOP_K].reshape(-1)
    w_flat = gate[:, :TOP_K].reshape(-1)
    onehot = (e_flat[:, None] == jnp.arange(N_EXPERTS, dtype=jnp.int32)[None, :]).astype(jnp.int32)
    csum = jnp.cumsum(onehot, axis=0)
    counts = csum[-1]
    rank = jnp.sum((csum - onehot) * onehot, axis=1)
    padded = (counts + MOE_TM - 1) // MOE_TM * MOE_TM
    pad_ends = jnp.cumsum(padded)
    pad_starts = pad_ends - padded
    dest = jnp.sum(onehot * pad_starts[None, :], axis=1) + rank
    cap = n_assign + N_EXPERTS * MOE_TM
    n_tiles = cap // MOE_TM
    src = jnp.full((cap,), -1, jnp.int32).at[dest].set(jnp.arange(n_assign, dtype=jnp.int32))
    live = src >= 0
    src_c = jnp.maximum(src, 0)
    tok = jnp.where(live, src_c // TOP_K, 0)
    wt = jnp.where(live, w_flat[src_c], 0.0).reshape(cap, 1)
    n_used = (pad_ends[-1] // MOE_TM).astype(jnp.int32)
    tile_start = jnp.arange(n_tiles, dtype=jnp.int32) * MOE_TM
    tile_start = jnp.minimum(tile_start, pad_ends[-1] - 1)
    blk_e = jnp.clip(jnp.searchsorted(pad_ends, tile_start, side="right"), 0,
                     N_EXPERTS - 1).astype(jnp.int32)
    xb = h2[tok]
    yb = _swiglu(xb, blk_e, n_used.reshape(1), wg, wu, wd, tm=MOE_TM, tf=512, scale=wt,
                 out_dtype=BF16, vmem_mib=48, name="moe_experts")
    pos = dest.reshape(n, TOP_K)
    return x_new + yb[pos[:, 0]].astype(F32) + yb[pos[:, 1]].astype(F32)


def kernel(x, mem, ln_mix_g, w_in, na_q_g, na_k_g, na_rpb, mem_ln_g, w_mem_kv, mem_q_g, mem_k_g,
           w_fourier_out, w_na_out, w_mem_out, w_o, ln_ffn_g, ffn_w_gate, ffn_w_up, ffn_w_down,
           moe_router, moe_w_gate, moe_w_up, moe_w_down):
    assert x.shape == (1, SEQ, D_MODEL) and mem.shape == (1, N_MEM, D_MODEL)
    xs = x.reshape(SEQ, D_MODEL)
    mem2 = mem.reshape(N_MEM, D_MODEL)
    tables = _dft_tables()
    dense_tm = 512
    dense_e = jnp.zeros((SEQ // dense_tm,), jnp.int32)
    dense_n = jnp.full((1,), SEQ // dense_tm, jnp.int32)
    for l in range(DEPTH):
        proj = _norm_mm(xs, ln_mix_g[l], w_in[l], tm=1024, tn=1024, out_dtype=BF16,
                        vmem_mib=56, name="in_proj")
        fm = _fourier_mix(proj[:, :FOURIER_W], tables)
        o_na = _na_attention(proj, na_q_g[l], na_k_g[l], _na_bias_table(na_rpb[l]))
        kv = _norm_mm(mem2, mem_ln_g[l], w_mem_kv[l], tm=N_MEM, tn=2 * MEM_W, out_dtype=F32,
                      vmem_mib=40, name="mem_kv_proj")
        o_mem = _mem_attention(proj, kv, mem_q_g[l], mem_k_g[l])
        merged = _merge(fm, o_na, o_mem, proj, w_fourier_out[l].astype(BF16),
                        w_na_out[l].astype(BF16), w_mem_out[l].astype(BF16))
        x_new, h2 = _out_proj(merged, w_o[l].astype(BF16), xs, ln_ffn_g[l])
        i = l // 2
        if l % 2 == 0:
            xs = _swiglu(h2, dense_e, dense_n, ffn_w_gate[i:i + 1].astype(BF16),
                         ffn_w_up[i:i + 1].astype(BF16), ffn_w_down[i:i + 1].astype(BF16),
                         tm=dense_tm, tf=512, resid=x_new, out_dtype=F32, vmem_mib=48,
                         name="dense_ffn")
        else:
            xs = _moe(x_new, h2, ln_ffn_g[l], moe_router[i], moe_w_gate[i].astype(BF16),
                      moe_w_up[i].astype(BF16), moe_w_down[i].astype(BF16))
    return xs.reshape(1, SEQ, D_MODEL)
```

```python
import functools
import math

import numpy as np
import jax
import jax.numpy as jnp
from jax import lax
from jax.experimental import pallas as pl
from jax.experimental.pallas import tpu as pltpu

F32 = jnp.float32
BF16 = jnp.bfloat16

D_MODEL = 2048
SEQ = 16384
DEPTH = 2
GRID_W = 64
ROWS = SEQ // GRID_W
HEAD_DIM = 128
N_FOURIER_GROUPS = 4
FOURIER_W = N_FOURIER_GROUPS * HEAD_DIM
N_NA_HEADS = 8
NA_W = N_NA_HEADS * HEAD_DIM
WIN_H = 8
WIN_W = 16
N_MEM = 256
N_MEM_HEADS = 4
MEM_W = N_MEM_HEADS * HEAD_DIM
IN_W = FOURIER_W + 3 * NA_W + MEM_W + 3 * D_MODEL
D_FF = 5632
N_EXPERTS = 8
TOP_K = 2
EPS = 1e-6

Q_NA_OFF = FOURIER_W
K_NA_OFF = Q_NA_OFF + NA_W
V_NA_OFF = K_NA_OFF + NA_W
Q_MEM_OFF = V_NA_OFF + NA_W
GATE_OFF = Q_MEM_OFF + MEM_W

MIB = 1024 * 1024
NEG_BIG = -1e30

FFT_A = 128
FFT_B = SEQ // FFT_A


def _params(semantics, vmem_mib):
    return pltpu.CompilerParams(dimension_semantics=semantics,
                                vmem_limit_bytes=int(vmem_mib * MIB))


def _rms_rows(x, gain):
    ms = jnp.mean(x * x, axis=-1, keepdims=True)
    return x * lax.rsqrt(ms + EPS) * gain


def _norm_mm_kernel(x_ref, g_ref, w_ref, o_ref, h_ref, *, row_chunk):
    @pl.when(pl.program_id(1) == 0)
    def _():
        def body(c, carry):
            rows = pl.ds(pl.multiple_of(c * row_chunk, row_chunk), row_chunk)
            h_ref[rows, :] = _rms_rows(x_ref[rows, :].astype(F32), g_ref[...]).astype(BF16)
            return carry
        lax.fori_loop(0, h_ref.shape[0] // row_chunk, body, 0)

    o_ref[...] = jnp.dot(h_ref[...], w_ref[0].astype(BF16),
                         preferred_element_type=F32).astype(o_ref.dtype)


def _norm_mm(x, g, w, layer, *, tm, tn, out_dtype, vmem_mib, name):
    m, k = x.shape
    n = w.shape[2]
    row_chunk = min(tm, 128)
    return pl.pallas_call(
        functools.partial(_norm_mm_kernel, row_chunk=row_chunk),
        grid=(m // tm, n // tn),
        in_specs=[pl.BlockSpec((tm, k), lambda i, j: (i, 0)),
                  pl.BlockSpec((1, k), lambda i, j: (0, 0)),
                  pl.BlockSpec((1, k, tn), lambda i, j: (layer, 0, j))],
        out_specs=pl.BlockSpec((tm, tn), lambda i, j: (i, j)),
        out_shape=jax.ShapeDtypeStruct((m, n), out_dtype),
        scratch_shapes=[pltpu.VMEM((tm, k), BF16)],
        compiler_params=_params(("parallel", "arbitrary"), vmem_mib),
        name=name,
    )(x, g.reshape(1, k), w)


def _dft_tables():
    a = np.arange(FFT_A)
    ang1 = 2.0 * np.pi * np.outer(a, a) / FFT_A
    w1 = np.concatenate([np.cos(ang1), -np.sin(ang1)], axis=0)
    k1 = np.arange(FFT_A)[:, None, None]
    k2 = np.arange(FFT_B)[None, :, None]
    s2 = np.arange(FFT_B)[None, None, :]
    ang2 = 2.0 * np.pi * ((s2 * (k1 + FFT_A * k2)) % SEQ) / SEQ
    g2 = np.concatenate([np.cos(ang2), np.sin(ang2)], axis=2)
    c = np.arange(HEAD_DIM)
    angc = 2.0 * np.pi * np.outer(c, c) / HEAD_DIM
    eye = np.eye(N_FOURIER_GROUPS)
    bd = np.concatenate([np.kron(eye, np.cos(angc)), np.kron(eye, np.sin(angc))], axis=0)
    return (jnp.asarray(w1, dtype=BF16), jnp.asarray(g2, dtype=BF16),
            jnp.asarray(bd, dtype=BF16))


def _dft1_kernel(w_ref, x_ref, br_ref, bi_ref):
    r = jnp.dot(w_ref[...], x_ref[...], preferred_element_type=F32)
    br_ref[...] = r[:FFT_A].astype(BF16)
    bi_ref[...] = r[FFT_A:].astype(BF16)


def _dft2_kernel(br_ref, bi_ref, g_ref, bd_ref, o_ref, *, kb, norm):
    for i in range(kb):
        br = br_ref[i]
        bi = bi_ref[i]
        g = g_ref[i]
        zr = jnp.dot(g, jnp.concatenate([br, bi], axis=0), preferred_element_type=F32)
        zi = jnp.dot(g, jnp.concatenate([bi, -br], axis=0), preferred_element_type=F32)
        z = jnp.concatenate([zr, zi], axis=1).astype(BF16)
        y = jnp.dot(z, bd_ref[...], preferred_element_type=F32) * norm
        o_ref[:, i * FOURIER_W:(i + 1) * FOURIER_W] = y.astype(o_ref.dtype)


def _fourier_mix(f_in, tables):
    w1, g2, bd = tables
    lanes = FFT_B * FOURIER_W
    x2d = f_in.reshape(FFT_A, lanes)
    tn = 8192
    br, bi = pl.pallas_call(
        _dft1_kernel,
        grid=(lanes // tn,),
        in_specs=[pl.BlockSpec((2 * FFT_A, FFT_A), lambda j: (0, 0)),
                  pl.BlockSpec((FFT_A, tn), lambda j: (0, j))],
        out_specs=[pl.BlockSpec((FFT_A, tn), lambda j: (0, j)),
                   pl.BlockSpec((FFT_A, tn), lambda j: (0, j))],
        out_shape=[jax.ShapeDtypeStruct((FFT_A, lanes), BF16)] * 2,
        compiler_params=_params(("parallel",), 40),
        name="fourier_stage1",
    )(w1, x2d)
    br3 = br.reshape(FFT_A, FFT_B, FOURIER_W)
    bi3 = bi.reshape(FFT_A, FFT_B, FOURIER_W)
    kb = 8
    norm = 1.0 / math.sqrt(SEQ * HEAD_DIM)
    y2d = pl.pallas_call(
        functools.partial(_dft2_kernel, kb=kb, norm=norm),
        grid=(FFT_A // kb,),
        in_specs=[pl.BlockSpec((kb, FFT_B, FOURIER_W), lambda j: (j, 0, 0)),
                  pl.BlockSpec((kb, FFT_B, FOURIER_W), lambda j: (j, 0, 0)),
                  pl.BlockSpec((kb, FFT_B, 2 * FFT_B), lambda j: (j, 0, 0)),
                  pl.BlockSpec((2 * FOURIER_W, FOURIER_W), lambda j: (0, 0))],
        out_specs=pl.BlockSpec((FFT_B, kb * FOURIER_W), lambda j: (0, j)),
        out_shape=jax.ShapeDtypeStruct((FFT_B, FFT_A * FOURIER_W), BF16),
        compiler_params=_params(("parallel",), 40),
        name="fourier_stage2",
    )(br3, bi3, g2, bd)
    return y2d.reshape(SEQ, FOURIER_W)


NA_SUB_ROWS = 4
NA_KEY_ROWS = NA_SUB_ROWS + WIN_H
NA_SUBS_PER_STEP = 4
NA_N_SUB = ROWS // NA_SUB_ROWS
NA_MAX_KEY_START = ROWS - NA_KEY_ROWS


NA_N_ROW_OFFSETS = 2 * WIN_H - 1
NA_MASKED_BLOCK = NA_N_ROW_OFFSETS


def _na_row_offsets():
    out = np.full((3, NA_SUB_ROWS, NA_KEY_ROWS), NA_MASKED_BLOCK, np.int32)
    for case, r0 in enumerate((0, 2 * NA_SUB_ROWS, ROWS - NA_SUB_ROWS)):
        ks = int(np.clip(r0 - WIN_H // 2, 0, NA_MAX_KEY_START))
        for qi in range(NA_SUB_ROWS):
            r = r0 + qi
            r_start = int(np.clip(r - WIN_H // 2, 0, ROWS - WIN_H))
            for kj in range(NA_KEY_ROWS):
                if r_start <= ks + kj < r_start + WIN_H:
                    out[case, qi, kj] = ks + kj - r + (WIN_H - 1)
    return out


def _na_col_blocks(rpb):
    qc = np.arange(GRID_W)[:, None]
    kc = np.arange(GRID_W)[None, :]
    col_start = np.clip(qc - WIN_W // 2, 0, GRID_W - WIN_W)
    col_valid = (kc >= col_start) & (kc < col_start + WIN_W)
    dc = kc - qc + (WIN_W - 1)
    sel = (np.arange(2 * WIN_W - 1)[:, None, None] == dc[None]) & col_valid[None]
    sel = jnp.asarray(sel.astype(np.float32))
    t = jnp.sum(rpb.astype(F32)[:, :, :, None, None] * sel[None, None], axis=2)
    t = jnp.where(col_valid[None, None], t, NEG_BIG)
    masked = jnp.full((N_NA_HEADS, 1, GRID_W, GRID_W), NEG_BIG, F32)
    t = jnp.concatenate([t, masked], axis=1)
    return jnp.concatenate([t, t], axis=-1)


def _na_kernel(q_ref, k_ref, v_ref, qg_ref, kg_ref, cb_ref, o_ref, kn_ref, bias_ref):
    step = pl.program_id(1)
    nq = NA_SUB_ROWS * GRID_W
    nk = NA_KEY_ROWS * GRID_W

    @pl.when(step == 0)
    def _():
        chunk = 1024

        def body(c, carry):
            rows = pl.ds(pl.multiple_of(c * chunk, chunk), chunk)
            kn_ref[rows, :] = _rms_rows(k_ref[rows, :].astype(F32), kg_ref[...]).astype(BF16)
            return carry
        lax.fori_loop(0, SEQ // chunk, body, 0)

        row_off = _na_row_offsets()
        for case in range(3):
            for qi in range(NA_SUB_ROWS):
                for kj in range(NA_KEY_ROWS):
                    lanes = slice(kj * GRID_W, (kj + 1) * GRID_W)
                    src = slice((kj % 2) * GRID_W, (kj % 2 + 1) * GRID_W)
                    bias_ref[case, qi * GRID_W:(qi + 1) * GRID_W, lanes] = (
                        cb_ref[0, int(row_off[case, qi, kj]), :, src])

    scale = HEAD_DIM ** -0.5
    for sb in range(NA_SUBS_PER_STEP):
        sub = step * NA_SUBS_PER_STEP + sb
        key_row0 = jnp.clip(sub * NA_SUB_ROWS - WIN_H // 2, 0, NA_MAX_KEY_START)
        case = jnp.where(sub == 0, 0, jnp.where(sub == NA_N_SUB - 1, 2, 1))
        q = q_ref[sb * nq:(sb + 1) * nq, :].astype(F32)
        qn = (_rms_rows(q, qg_ref[...]) * scale).astype(BF16)
        kstart = pl.multiple_of(key_row0 * GRID_W, GRID_W)
        kw = kn_ref[pl.ds(kstart, nk), :]
        vw = v_ref[pl.ds(kstart, nk), :]
        s = lax.dot_general(qn, kw, (((1,), (1,)), ((), ())), preferred_element_type=F32)
        s = s + bias_ref[case]
        m = jnp.max(s, axis=-1, keepdims=True)
        p = jnp.exp(s - m)
        l = jnp.sum(p, axis=-1, keepdims=True)
        o = jnp.dot(p.astype(BF16), vw, preferred_element_type=F32) / l
        o_ref[sb * nq:(sb + 1) * nq, :] = o.astype(o_ref.dtype)


def _na_attention(proj, q_g, k_g, col_blocks):
    nq = NA_SUB_ROWS * GRID_W
    nk = NA_KEY_ROWS * GRID_W
    tq = NA_SUBS_PER_STEP * nq
    n_blk = NA_N_ROW_OFFSETS + 1
    qb, kb_, vb = Q_NA_OFF // HEAD_DIM, K_NA_OFF // HEAD_DIM, V_NA_OFF // HEAD_DIM
    return pl.pallas_call(
        _na_kernel,
        grid=(N_NA_HEADS, SEQ // tq),
        in_specs=[pl.BlockSpec((tq, HEAD_DIM), lambda h, i: (i, qb + h)),
                  pl.BlockSpec((SEQ, HEAD_DIM), lambda h, i: (0, kb_ + h)),
                  pl.BlockSpec((SEQ, HEAD_DIM), lambda h, i: (0, vb + h)),
                  pl.BlockSpec((1, HEAD_DIM), lambda h, i: (0, 0)),
                  pl.BlockSpec((1, HEAD_DIM), lambda h, i: (0, 0)),
                  pl.BlockSpec((1, n_blk, GRID_W, 2 * GRID_W), lambda h, i: (h, 0, 0, 0))],
        out_specs=pl.BlockSpec((tq, HEAD_DIM), lambda h, i: (i, h)),
        out_shape=jax.ShapeDtypeStruct((SEQ, NA_W), BF16),
        scratch_shapes=[pltpu.VMEM((SEQ, HEAD_DIM), BF16),
                        pltpu.VMEM((3, nq, nk), F32)],
        compiler_params=_params(("parallel", "arbitrary"), 48),
        name="na_attention",
    )(proj, proj, proj, q_g.reshape(1, HEAD_DIM), k_g.reshape(1, HEAD_DIM), col_blocks)


def _mem_attn_kernel(q_ref, kv_ref, qg_ref, kg_ref, o_ref):
    scale = HEAD_DIM ** -0.5
    for h in range(N_MEM_HEADS):
        cols = slice(h * HEAD_DIM, (h + 1) * HEAD_DIM)
        qn = (_rms_rows(q_ref[:, cols].astype(F32), qg_ref[...]) * scale).astype(BF16)
        kn = _rms_rows(kv_ref[:, cols].astype(F32), kg_ref[...]).astype(BF16)
        v = kv_ref[:, MEM_W + h * HEAD_DIM:MEM_W + (h + 1) * HEAD_DIM].astype(BF16)
        s = lax.dot_general(qn, kn, (((1,), (1,)), ((), ())), preferred_element_type=F32)
        m = jnp.max(s, axis=-1, keepdims=True)
        p = jnp.exp(s - m)
        l = jnp.sum(p, axis=-1, keepdims=True)
        o = jnp.dot(p.astype(BF16), v, preferred_element_type=F32) / l
        o_ref[:, cols] = o.astype(o_ref.dtype)


def _mem_attention(proj, kv, q_g, k_g):
    tm = 512
    return pl.pallas_call(
        _mem_attn_kernel,
        grid=(SEQ // tm,),
        in_specs=[pl.BlockSpec((tm, MEM_W), lambda i: (i, Q_MEM_OFF // MEM_W)),
                  pl.BlockSpec((N_MEM, 2 * MEM_W), lambda i: (0, 0)),
                  pl.BlockSpec((1, HEAD_DIM), lambda i: (0, 0)),
                  pl.BlockSpec((1, HEAD_DIM), lambda i: (0, 0))],
        out_specs=pl.BlockSpec((tm, MEM_W), lambda i: (i, 0)),
        out_shape=jax.ShapeDtypeStruct((SEQ, MEM_W), BF16),
        compiler_params=_params(("parallel",), 32),
        name="mem_attention",
    )(proj, kv, q_g.reshape(1, HEAD_DIM), k_g.reshape(1, HEAD_DIM))


def _merge_kernel(fm_ref, na_ref, mo_ref, g0_ref, g1_ref, g2_ref, wf_ref, wn_ref, wm_ref, o_ref,
                  *, col_chunk):
    for c in range(D_MODEL // col_chunk):
        cols = slice(c * col_chunk, (c + 1) * col_chunk)
        o_f = jnp.dot(fm_ref[...], wf_ref[:, cols], preferred_element_type=F32)
        o_n = jnp.dot(na_ref[...], wn_ref[:, cols], preferred_element_type=F32)
        o_m = jnp.dot(mo_ref[...], wm_ref[:, cols], preferred_element_type=F32)
        acc = jax.nn.sigmoid(g0_ref[:, cols].astype(F32)) * o_f
        acc = acc + jax.nn.sigmoid(g1_ref[:, cols].astype(F32)) * o_n
        acc = acc + jax.nn.sigmoid(g2_ref[:, cols].astype(F32)) * o_m
        o_ref[:, cols] = acc.astype(o_ref.dtype)


def _merge(fm, o_na, o_mem, proj, wf, wn, wm):
    tm = 512
    gb = GATE_OFF // D_MODEL
    return pl.pallas_call(
        functools.partial(_merge_kernel, col_chunk=512),
        grid=(SEQ // tm,),
        in_specs=[pl.BlockSpec((tm, FOURIER_W), lambda i: (i, 0)),
                  pl.BlockSpec((tm, NA_W), lambda i: (i, 0)),
                  pl.BlockSpec((tm, MEM_W), lambda i: (i, 0)),
                  pl.BlockSpec((tm, D_MODEL), lambda i: (i, gb)),
                  pl.BlockSpec((tm, D_MODEL), lambda i: (i, gb + 1)),
                  pl.BlockSpec((tm, D_MODEL), lambda i: (i, gb + 2)),
                  pl.BlockSpec((FOURIER_W, D_MODEL), lambda i: (0, 0)),
                  pl.BlockSpec((NA_W, D_MODEL), lambda i: (0, 0)),
                  pl.BlockSpec((MEM_W, D_MODEL), lambda i: (0, 0))],
        out_specs=pl.BlockSpec((tm, D_MODEL), lambda i: (i, 0)),
        out_shape=jax.ShapeDtypeStruct((SEQ, D_MODEL), BF16),
        compiler_params=_params(("parallel",), 48),
        name="gated_merge",
    )(fm, o_na, o_mem, proj, proj, proj, wf, wn, wm)


def _oproj_kernel(a_ref, w_ref, x_ref, g_ref, xo_ref, h_ref, *, row_chunk):
    xo_ref[...] = jnp.dot(a_ref[...], w_ref[...], preferred_element_type=F32) + x_ref[...]

    def body(c, carry):
        rows = pl.ds(pl.multiple_of(c * row_chunk, row_chunk), row_chunk)
        h_ref[rows, :] = _rms_rows(xo_ref[rows, :], g_ref[...]).astype(h_ref.dtype)
        return carry
    lax.fori_loop(0, xo_ref.shape[0] // row_chunk, body, 0)


def _out_proj(merged, w_o, x, g):
    tm = 512
    return pl.pallas_call(
        functools.partial(_oproj_kernel, row_chunk=128),
        grid=(SEQ // tm,),
        in_specs=[pl.BlockSpec((tm, D_MODEL), lambda i: (i, 0)),
                  pl.BlockSpec((D_MODEL, D_MODEL), lambda i: (0, 0)),
                  pl.BlockSpec((tm, D_MODEL), lambda i: (i, 0)),
                  pl.BlockSpec((1, D_MODEL), lambda i: (0, 0))],
        out_specs=[pl.BlockSpec((tm, D_MODEL), lambda i: (i, 0)),
                   pl.BlockSpec((tm, D_MODEL), lambda i: (i, 0))],
        out_shape=[jax.ShapeDtypeStruct((SEQ, D_MODEL), F32),
                   jax.ShapeDtypeStruct((SEQ, D_MODEL), BF16)],
        compiler_params=_params(("parallel",), 48),
        name="out_proj",
    )(merged, w_o, x, g.reshape(1, D_MODEL))


def _swiglu_kernel(be_ref, nu_ref, x_ref, wg_ref, wu_ref, wd_ref, *rest, nf, has_scale, has_resid):
    rest = list(rest)
    scale_ref = rest.pop(0) if has_scale else None
    resid_ref = rest.pop(0) if has_resid else None
    o_ref, acc_ref = rest
    i = pl.program_id(0)
    f = pl.program_id(1)
    active = i < nu_ref[0]

    @pl.when(jnp.logical_and(active, f == 0))
    def _():
        acc_ref[...] = jnp.zeros(acc_ref.shape, acc_ref.dtype)

    @pl.when(active)
    def _():
        x = x_ref[...]
        g = jnp.dot(x, wg_ref[0].astype(BF16), preferred_element_type=F32)
        u = jnp.dot(x, wu_ref[0].astype(BF16), preferred_element_type=F32)
        a = (g * jax.nn.sigmoid(g) * u).astype(BF16)
        acc_ref[...] += jnp.dot(a, wd_ref[0].astype(BF16), preferred_element_type=F32)

    @pl.when(f == nf - 1)
    def _():
        @pl.when(active)
        def _():
            y = acc_ref[...]
            if has_scale:
                y = y * scale_ref[...]
            if has_resid:
                y = y + resid_ref[...]
            o_ref[...] = y.astype(o_ref.dtype)

        @pl.when(jnp.logical_not(active))
        def _():
            o_ref[...] = jnp.zeros(o_ref.shape, o_ref.dtype)


def _swiglu(x, blk_e, n_used, wg, wu, wd, *, tm, tf, scale=None, resid=None, out_dtype,
            vmem_mib, name):
    m, d = x.shape
    nf = D_FF // tf
    n_tiles = m // tm

    def x_map(i, f, be, nu):
        return (jnp.minimum(i, nu[0] - 1), 0)

    def f_idx(i, f, nu):
        return jnp.where(i < nu[0], f, nf - 1)

    in_specs = [pl.BlockSpec((tm, d), x_map),
                pl.BlockSpec((1, d, tf), lambda i, f, be, nu: (be[i], 0, f_idx(i, f, nu))),
                pl.BlockSpec((1, d, tf), lambda i, f, be, nu: (be[i], 0, f_idx(i, f, nu))),
                pl.BlockSpec((1, tf, d), lambda i, f, be, nu: (be[i], f_idx(i, f, nu), 0))]
    args = [x, wg, wu, wd]
    if scale is not None:
        in_specs.append(pl.BlockSpec((tm, 1), x_map))
        args.append(scale)
    if resid is not None:
        in_specs.append(pl.BlockSpec((tm, d), x_map))
        args.append(resid)
    grid_spec = pltpu.PrefetchScalarGridSpec(
        num_scalar_prefetch=2,
        grid=(n_tiles, nf),
        in_specs=in_specs,
        out_specs=pl.BlockSpec((tm, d), lambda i, f, be, nu: (i, 0)),
        scratch_shapes=[pltpu.VMEM((tm, d), F32)],
    )
    return pl.pallas_call(
        functools.partial(_swiglu_kernel, nf=nf, has_scale=scale is not None,
                          has_resid=resid is not None),
        grid_spec=grid_spec,
        out_shape=jax.ShapeDtypeStruct((m, d), out_dtype),
        compiler_params=_params(("arbitrary", "arbitrary"), vmem_mib),
        name=name,
    )(blk_e, n_used, *args)


ROUTER_LANES = 128


def _router_kernel(x_ref, g_ref, w_ref, idx_ref, gate_ref):
    h = _rms_rows(x_ref[...], g_ref[...])
    logits = jnp.dot(h, w_ref[...], preferred_element_type=F32,
                     precision=lax.Precision.HIGHEST)
    lane = lax.broadcasted_iota(jnp.int32, logits.shape, 1).astype(F32)
    logits = jnp.where(lane < N_EXPERTS, logits, -jnp.inf)
    m1 = jnp.max(logits, axis=-1, keepdims=True)
    i1 = jnp.min(jnp.where(logits == m1, lane, float(ROUTER_LANES)), axis=-1, keepdims=True)
    rest = jnp.where(lane == i1, -jnp.inf, logits)
    m2 = jnp.max(rest, axis=-1, keepdims=True)
    i2 = jnp.min(jnp.where(rest == m2, lane, float(ROUTER_LANES)), axis=-1, keepdims=True)
    e21 = jnp.exp(m2 - m1)
    g1 = 1.0 / (1.0 + e21)
    g2 = e21 * g1
    idx_ref[...] = jnp.where(lane == 0.0, i1, i2).astype(jnp.int32)
    gate_ref[...] = jnp.where(lane == 0.0, g1, g2)


def _router(x, g, w_router):
    tm = 256
    w_pad = jnp.zeros((D_MODEL, ROUTER_LANES), F32).at[:, :N_EXPERTS].set(w_router.astype(F32))
    return pl.pallas_call(
        _router_kernel,
        grid=(SEQ // tm,),
        in_specs=[pl.BlockSpec((tm, D_MODEL), lambda i: (i, 0)),
                  pl.BlockSpec((1, D_MODEL), lambda i: (0, 0)),
                  pl.BlockSpec((D_MODEL, ROUTER_LANES), lambda i: (0, 0))],
        out_specs=[pl.BlockSpec((tm, ROUTER_LANES), lambda i: (i, 0)),
                   pl.BlockSpec((tm, ROUTER_LANES), lambda i: (i, 0))],
        out_shape=[jax.ShapeDtypeStruct((SEQ, ROUTER_LANES), jnp.int32),
                   jax.ShapeDtypeStruct((SEQ, ROUTER_LANES), F32)],
        compiler_params=_params(("parallel",), 32),
        name="moe_router",
    )(x, g.reshape(1, D_MODEL), w_pad)


MOE_TM = 1024
MOE_TF = 256


def _moe(x_new, h2, g_ffn, w_router, wg, wu, wd):
    n = SEQ
    n_assign = n * TOP_K
    idx, gate = _router(x_new, g_ffn, w_router)
    e_flat = idx[:, :TOP_K].reshape(-1)
    w_flat = gate[:, :TOP_K].reshape(-1)
    onehot = (e_flat[:, None] == jnp.arange(N_EXPERTS, dtype=jnp.int32)[None, :]).astype(jnp.int32)
    csum = jnp.cumsum(onehot, axis=0)
    counts = csum[-1]
    rank = jnp.sum((csum - onehot) * onehot, axis=1)
    padded = (counts + MOE_TM - 1) // MOE_TM * MOE_TM
    pad_ends = jnp.cumsum(padded)
    pad_starts = pad_ends - padded
    dest = jnp.sum(onehot * pad_starts[None, :], axis=1) + rank
    cap = n_assign + N_EXPERTS * MOE_TM
    n_tiles = cap // MOE_TM
    src = jnp.full((cap,), -1, jnp.int32).at[dest].set(jnp.arange(n_assign, dtype=jnp.int32))
    live = src >= 0
    src_c = jnp.maximum(src, 0)
    tok = jnp.where(live, src_c // TOP_K, 0)
    wt = jnp.where(live, w_flat[src_c], 0.0).reshape(cap, 1)
    n_used = (pad_ends[-1] // MOE_TM).astype(jnp.int32)
    tile_start = jnp.arange(n_tiles, dtype=jnp.int32) * MOE_TM
    tile_start = jnp.minimum(tile_start, pad_ends[-1] - 1)
    blk_e = jnp.sum((tile_start[:, None] >= pad_ends[None, :]).astype(jnp.int32), axis=1)
    blk_e = jnp.clip(blk_e, 0, N_EXPERTS - 1)
    xb = h2[tok]
    yb = _swiglu(xb, blk_e, n_used.reshape(1), wg, wu, wd, tm=MOE_TM, tf=MOE_TF, scale=wt,
                 out_dtype=BF16, vmem_mib=52, name="moe_experts")
    pos = dest.reshape(n, TOP_K)
    return x_new + yb[pos[:, 0]].astype(F32) + yb[pos[:, 1]].astype(F32)


def kernel(x, mem, ln_mix_g, w_in, na_q_g, na_k_g, na_rpb, mem_ln_g, w_mem_kv, mem_q_g, mem_k_g,
           w_fourier_out, w_na_out, w_mem_out, w_o, ln_ffn_g, ffn_w_gate, ffn_w_up, ffn_w_down,
           moe_router, moe_w_gate, moe_w_up, moe_w_down):
    assert x.shape == (1, SEQ, D_MODEL) and mem.shape == (1, N_MEM, D_MODEL)
    xs = x.reshape(SEQ, D_MODEL)
    mem2 = mem.reshape(N_MEM, D_MODEL)
    tables = _dft_tables()
    dense_tm = 512
    dense_e = jnp.zeros((SEQ // dense_tm,), jnp.int32)
    dense_n = jnp.full((1,), SEQ // dense_tm, jnp.int32)
    for l in range(DEPTH):
        proj = _norm_mm(xs, ln_mix_g[l], w_in, l, tm=1024, tn=1024, out_dtype=BF16,
                        vmem_mib=56, name="in_proj")
        fm = _fourier_mix(proj[:, :FOURIER_W], tables)
        o_na = _na_attention(proj, na_q_g[l], na_k_g[l], _na_col_blocks(na_rpb[l]))
        kv = _norm_mm(mem2, mem_ln_g[l], w_mem_kv, l, tm=N_MEM, tn=2 * MEM_W, out_dtype=F32,
                      vmem_mib=40, name="mem_kv_proj")
        o_mem = _mem_attention(proj, kv, mem_q_g[l], mem_k_g[l])
        merged = _merge(fm, o_na, o_mem, proj, w_fourier_out[l].astype(BF16),
                        w_na_out[l].astype(BF16), w_mem_out[l].astype(BF16))
        x_new, h2 = _out_proj(merged, w_o[l].astype(BF16), xs, ln_ffn_g[l])
        i = l // 2
        if l % 2 == 0:
            xs = _swiglu(h2, dense_e, dense_n, ffn_w_gate[i:i + 1].astype(BF16),
                         ffn_w_up[i:i + 1].astype(BF16), ffn_w_down[i:i + 1].astype(BF16),
                         tm=dense_tm, tf=512, resid=x_new, out_dtype=F32, vmem_mib=48,
                         name="dense_ffn")
        else:
            xs = _moe(x_new, h2, ln_ffn_g[l], moe_router[i], moe_w_gate[i], moe_w_up[i],
                      moe_w_down[i])
    return xs.reshape(1, SEQ, D_MODEL)
```

```python
import functools
import math

import numpy as np
import jax
import jax.numpy as jnp
from jax import lax
from jax.experimental import pallas as pl
from jax.experimental.pallas import tpu as pltpu

F32 = jnp.float32
BF16 = jnp.bfloat16

D_MODEL = 2048
SEQ = 16384
DEPTH = 2
GRID_W = 64
ROWS = SEQ // GRID_W
HEAD_DIM = 128
N_FOURIER_GROUPS = 4
FOURIER_W = N_FOURIER_GROUPS * HEAD_DIM
N_NA_HEADS = 8
NA_W = N_NA_HEADS * HEAD_DIM
WIN_H = 8
WIN_W = 16
N_MEM = 256
N_MEM_HEADS = 4
MEM_W = N_MEM_HEADS * HEAD_DIM
IN_W = FOURIER_W + 3 * NA_W + MEM_W + 3 * D_MODEL
D_FF = 5632
N_EXPERTS = 8
TOP_K = 2
EPS = 1e-6

Q_NA_OFF = FOURIER_W
K_NA_OFF = Q_NA_OFF + NA_W
V_NA_OFF = K_NA_OFF + NA_W
Q_MEM_OFF = V_NA_OFF + NA_W
GATE_OFF = Q_MEM_OFF + MEM_W

MIB = 1024 * 1024
NEG_BIG = -1e30

FFT_A = 128
FFT_B = SEQ // FFT_A


def _params(semantics, vmem_mib):
    return pltpu.CompilerParams(dimension_semantics=semantics,
                                vmem_limit_bytes=int(vmem_mib * MIB))


def _rms_rows(x, gain):
    ms = jnp.mean(x * x, axis=-1, keepdims=True)
    return x * lax.rsqrt(ms + EPS) * gain


def _bf16_bits_rtne(v):
    bits = pltpu.bitcast(v, jnp.uint32)
    lsb = lax.shift_right_logical(bits, jnp.uint32(16)) & jnp.uint32(1)
    return bits + jnp.uint32(0x7FFF) + lsb


def _pack_halves(v):
    half = v.shape[1] // 2
    lo = lax.shift_right_logical(_bf16_bits_rtne(v[:, :half]), jnp.uint32(16))
    hi = _bf16_bits_rtne(v[:, half:]) & jnp.uint32(0xFFFF0000)
    return lo | hi


def _unpack_halves(w):
    lo = pltpu.bitcast(lax.shift_left(w, jnp.uint32(16)), F32)
    hi = pltpu.bitcast(w & jnp.uint32(0xFFFF0000), F32)
    return lo, hi


def _norm_mm_kernel(x_ref, g_ref, w_ref, o_ref, h_ref, *, row_chunk):
    @pl.when(pl.program_id(1) == 0)
    def _():
        def body(c, carry):
            rows = pl.ds(pl.multiple_of(c * row_chunk, row_chunk), row_chunk)
            h_ref[rows, :] = _rms_rows(x_ref[rows, :].astype(F32), g_ref[...]).astype(BF16)
            return carry
        lax.fori_loop(0, h_ref.shape[0] // row_chunk, body, 0)

    o_ref[...] = jnp.dot(h_ref[...], w_ref[0].astype(BF16),
                         preferred_element_type=F32).astype(o_ref.dtype)


def _norm_mm(x, g, w, layer, *, tm, tn, out_dtype, vmem_mib, name):
    m, k = x.shape
    n = w.shape[2]
    row_chunk = min(tm, 128)
    return pl.pallas_call(
        functools.partial(_norm_mm_kernel, row_chunk=row_chunk),
        grid=(m // tm, n // tn),
        in_specs=[pl.BlockSpec((tm, k), lambda i, j: (i, 0)),
                  pl.BlockSpec((1, k), lambda i, j: (0, 0)),
                  pl.BlockSpec((1, k, tn), lambda i, j: (layer, 0, j))],
        out_specs=pl.BlockSpec((tm, tn), lambda i, j: (i, j)),
        out_shape=jax.ShapeDtypeStruct((m, n), out_dtype),
        scratch_shapes=[pltpu.VMEM((tm, k), BF16)],
        compiler_params=_params(("parallel", "arbitrary"), vmem_mib),
        name=name,
    )(x, g.reshape(1, k), w)


def _dft_tables():
    a = np.arange(FFT_A)
    ang1 = 2.0 * np.pi * np.outer(a, a) / FFT_A
    w1 = np.concatenate([np.cos(ang1), -np.sin(ang1)], axis=0)
    k1 = np.arange(FFT_A)[:, None, None]
    k2 = np.arange(FFT_B)[None, :, None]
    s2 = np.arange(FFT_B)[None, None, :]
    ang2 = 2.0 * np.pi * ((s2 * (k1 + FFT_A * k2)) % SEQ) / SEQ
    g2 = np.concatenate([np.cos(ang2), np.sin(ang2)], axis=2)
    c = np.arange(HEAD_DIM)
    angc = 2.0 * np.pi * np.outer(c, c) / HEAD_DIM
    eye = np.eye(N_FOURIER_GROUPS)
    bd = np.concatenate([np.kron(eye, np.cos(angc)), np.kron(eye, np.sin(angc))], axis=0)
    return (jnp.asarray(w1, dtype=BF16), jnp.asarray(g2, dtype=BF16),
            jnp.asarray(bd, dtype=BF16))


def _dft1_kernel(w_ref, x_ref, br_ref, bi_ref):
    r = jnp.dot(w_ref[...], x_ref[...], preferred_element_type=F32)
    br_ref[...] = r[:FFT_A].astype(BF16)
    bi_ref[...] = r[FFT_A:].astype(BF16)


def _dft2_kernel(br_ref, bi_ref, g_ref, bd_ref, o_ref, *, kb, norm):
    for i in range(kb):
        br = br_ref[i]
        bi = bi_ref[i]
        g = g_ref[i]
        zr = jnp.dot(g, jnp.concatenate([br, bi], axis=0), preferred_element_type=F32)
        zi = jnp.dot(g, jnp.concatenate([bi, -br], axis=0), preferred_element_type=F32)
        z = jnp.concatenate([zr, zi], axis=1).astype(BF16)
        y = jnp.dot(z, bd_ref[...], preferred_element_type=F32) * norm
        o_ref[:, i * FOURIER_W:(i + 1) * FOURIER_W] = y.astype(o_ref.dtype)


def _fourier_mix(f_in, tables):
    w1, g2, bd = tables
    lanes = FFT_B * FOURIER_W
    x2d = f_in.reshape(FFT_A, lanes)
    tn = 8192
    br, bi = pl.pallas_call(
        _dft1_kernel,
        grid=(lanes // tn,),
        in_specs=[pl.BlockSpec((2 * FFT_A, FFT_A), lambda j: (0, 0)),
                  pl.BlockSpec((FFT_A, tn), lambda j: (0, j))],
        out_specs=[pl.BlockSpec((FFT_A, tn), lambda j: (0, j)),
                   pl.BlockSpec((FFT_A, tn), lambda j: (0, j))],
        out_shape=[jax.ShapeDtypeStruct((FFT_A, lanes), BF16)] * 2,
        compiler_params=_params(("parallel",), 40),
        name="fourier_stage1",
    )(w1, x2d)
    br3 = br.reshape(FFT_A, FFT_B, FOURIER_W)
    bi3 = bi.reshape(FFT_A, FFT_B, FOURIER_W)
    kb = 8
    norm = 1.0 / math.sqrt(SEQ * HEAD_DIM)
    y2d = pl.pallas_call(
        functools.partial(_dft2_kernel, kb=kb, norm=norm),
        grid=(FFT_A // kb,),
        in_specs=[pl.BlockSpec((kb, FFT_B, FOURIER_W), lambda j: (j, 0, 0)),
                  pl.BlockSpec((kb, FFT_B, FOURIER_W), lambda j: (j, 0, 0)),
                  pl.BlockSpec((kb, FFT_B, 2 * FFT_B), lambda j: (j, 0, 0)),
                  pl.BlockSpec((2 * FOURIER_W, FOURIER_W), lambda j: (0, 0))],
        out_specs=pl.BlockSpec((FFT_B, kb * FOURIER_W), lambda j: (0, j)),
        out_shape=jax.ShapeDtypeStruct((FFT_B, FFT_A * FOURIER_W), BF16),
        compiler_params=_params(("parallel",), 40),
        name="fourier_stage2",
    )(br3, bi3, g2, bd)
    return y2d.reshape(SEQ, FOURIER_W)


NA_SUB_ROWS = 4
NA_KEY_ROWS = NA_SUB_ROWS + WIN_H
NA_SUBS_PER_STEP = 4
NA_N_SUB = ROWS // NA_SUB_ROWS
NA_MAX_KEY_START = ROWS - NA_KEY_ROWS


NA_N_ROW_OFFSETS = 2 * WIN_H - 1
NA_MASKED_BLOCK = NA_N_ROW_OFFSETS


def _na_row_offsets():
    out = np.full((3, NA_SUB_ROWS, NA_KEY_ROWS), NA_MASKED_BLOCK, np.int32)
    for case, r0 in enumerate((0, 2 * NA_SUB_ROWS, ROWS - NA_SUB_ROWS)):
        ks = int(np.clip(r0 - WIN_H // 2, 0, NA_MAX_KEY_START))
        for qi in range(NA_SUB_ROWS):
            r = r0 + qi
            r_start = int(np.clip(r - WIN_H // 2, 0, ROWS - WIN_H))
            for kj in range(NA_KEY_ROWS):
                if r_start <= ks + kj < r_start + WIN_H:
                    out[case, qi, kj] = ks + kj - r + (WIN_H - 1)
    return out


def _na_col_blocks(rpb):
    qc = np.arange(GRID_W)[:, None]
    kc = np.arange(GRID_W)[None, :]
    col_start = np.clip(qc - WIN_W // 2, 0, GRID_W - WIN_W)
    col_valid = (kc >= col_start) & (kc < col_start + WIN_W)
    dc = kc - qc + (WIN_W - 1)
    sel = (np.arange(2 * WIN_W - 1)[:, None, None] == dc[None]) & col_valid[None]
    sel = jnp.asarray(sel.astype(np.float32))
    t = jnp.sum(rpb.astype(F32)[:, :, :, None, None] * sel[None, None], axis=2)
    t = jnp.where(col_valid[None, None], t, NEG_BIG)
    masked = jnp.full((N_NA_HEADS, 1, GRID_W, GRID_W), NEG_BIG, F32)
    t = jnp.concatenate([t, masked], axis=1)
    return jnp.concatenate([t, t], axis=-1)


def _na_kernel(q_ref, k_ref, v_ref, qg_ref, kg_ref, cb_ref, o_ref, kn_ref, bias_ref):
    step = pl.program_id(1)
    nq = NA_SUB_ROWS * GRID_W
    nk = NA_KEY_ROWS * GRID_W

    @pl.when(step == 0)
    def _():
        chunk = 1024

        def body(c, carry):
            rows = pl.ds(pl.multiple_of(c * chunk, chunk), chunk)
            kn_ref[rows, :] = _rms_rows(k_ref[rows, :].astype(F32), kg_ref[...]).astype(BF16)
            return carry
        lax.fori_loop(0, SEQ // chunk, body, 0)

        row_off = _na_row_offsets()
        for case in range(3):
            for qi in range(NA_SUB_ROWS):
                for kj in range(NA_KEY_ROWS):
                    lanes = slice(kj * GRID_W, (kj + 1) * GRID_W)
                    src = slice((kj % 2) * GRID_W, (kj % 2 + 1) * GRID_W)
                    bias_ref[case, qi * GRID_W:(qi + 1) * GRID_W, lanes] = (
                        cb_ref[0, int(row_off[case, qi, kj]), :, src])

    scale = HEAD_DIM ** -0.5
    for sb in range(NA_SUBS_PER_STEP):
        sub = step * NA_SUBS_PER_STEP + sb
        key_row0 = jnp.clip(sub * NA_SUB_ROWS - WIN_H // 2, 0, NA_MAX_KEY_START)
        case = jnp.where(sub == 0, 0, jnp.where(sub == NA_N_SUB - 1, 2, 1))
        q = q_ref[sb * nq:(sb + 1) * nq, :].astype(F32)
        qn = (_rms_rows(q, qg_ref[...]) * scale).astype(BF16)
        kstart = pl.multiple_of(key_row0 * GRID_W, GRID_W)
        kw = kn_ref[pl.ds(kstart, nk), :]
        vw = v_ref[pl.ds(kstart, nk), :]
        s = lax.dot_general(qn, kw, (((1,), (1,)), ((), ())), preferred_element_type=F32)
        s = s + bias_ref[case]
        m = jnp.max(s, axis=-1, keepdims=True)
        p = jnp.exp(s - m)
        l = jnp.sum(p, axis=-1, keepdims=True)
        o = jnp.dot(p.astype(BF16), vw, preferred_element_type=F32) / l
        o_ref[sb * nq:(sb + 1) * nq, :] = o.astype(o_ref.dtype)


def _na_attention(proj, q_g, k_g, col_blocks):
    nq = NA_SUB_ROWS * GRID_W
    nk = NA_KEY_ROWS * GRID_W
    tq = NA_SUBS_PER_STEP * nq
    n_blk = NA_N_ROW_OFFSETS + 1
    qb, kb_, vb = Q_NA_OFF // HEAD_DIM, K_NA_OFF // HEAD_DIM, V_NA_OFF // HEAD_DIM
    return pl.pallas_call(
        _na_kernel,
        grid=(N_NA_HEADS, SEQ // tq),
        in_specs=[pl.BlockSpec((tq, HEAD_DIM), lambda h, i: (i, qb + h)),
                  pl.BlockSpec((SEQ, HEAD_DIM), lambda h, i: (0, kb_ + h)),
                  pl.BlockSpec((SEQ, HEAD_DIM), lambda h, i: (0, vb + h)),
                  pl.BlockSpec((1, HEAD_DIM), lambda h, i: (0, 0)),
                  pl.BlockSpec((1, HEAD_DIM), lambda h, i: (0, 0)),
                  pl.BlockSpec((1, n_blk, GRID_W, 2 * GRID_W), lambda h, i: (h, 0, 0, 0))],
        out_specs=pl.BlockSpec((tq, HEAD_DIM), lambda h, i: (i, h)),
        out_shape=jax.ShapeDtypeStruct((SEQ, NA_W), BF16),
        scratch_shapes=[pltpu.VMEM((SEQ, HEAD_DIM), BF16),
                        pltpu.VMEM((3, nq, nk), F32)],
        compiler_params=_params(("parallel", "arbitrary"), 48),
        name="na_attention",
    )(proj, proj, proj, q_g.reshape(1, HEAD_DIM), k_g.reshape(1, HEAD_DIM), col_blocks)


def _mem_attn_kernel(q_ref, kv_ref, qg_ref, kg_ref, o_ref):
    scale = HEAD_DIM ** -0.5
    for h in range(N_MEM_HEADS):
        cols = slice(h * HEAD_DIM, (h + 1) * HEAD_DIM)
        qn = (_rms_rows(q_ref[:, cols].astype(F32), qg_ref[...]) * scale).astype(BF16)
        kn = _rms_rows(kv_ref[:, cols].astype(F32), kg_ref[...]).astype(BF16)
        v = kv_ref[:, MEM_W + h * HEAD_DIM:MEM_W + (h + 1) * HEAD_DIM].astype(BF16)
        s = lax.dot_general(qn, kn, (((1,), (1,)), ((), ())), preferred_element_type=F32)
        m = jnp.max(s, axis=-1, keepdims=True)
        p = jnp.exp(s - m)
        l = jnp.sum(p, axis=-1, keepdims=True)
        o = jnp.dot(p.astype(BF16), v, preferred_element_type=F32) / l
        o_ref[:, cols] = o.astype(o_ref.dtype)


def _mem_attention(proj, kv, q_g, k_g):
    tm = 512
    return pl.pallas_call(
        _mem_attn_kernel,
        grid=(SEQ // tm,),
        in_specs=[pl.BlockSpec((tm, MEM_W), lambda i: (i, Q_MEM_OFF // MEM_W)),
                  pl.BlockSpec((N_MEM, 2 * MEM_W), lambda i: (0, 0)),
                  pl.BlockSpec((1, HEAD_DIM), lambda i: (0, 0)),
                  pl.BlockSpec((1, HEAD_DIM), lambda i: (0, 0))],
        out_specs=pl.BlockSpec((tm, MEM_W), lambda i: (i, 0)),
        out_shape=jax.ShapeDtypeStruct((SEQ, MEM_W), BF16),
        compiler_params=_params(("parallel",), 32),
        name="mem_attention",
    )(proj, kv, q_g.reshape(1, HEAD_DIM), k_g.reshape(1, HEAD_DIM))


def _merge_kernel(fm_ref, na_ref, mo_ref, g0_ref, g1_ref, g2_ref, wf_ref, wn_ref, wm_ref, o_ref,
                  *, col_chunk):
    for c in range(D_MODEL // col_chunk):
        cols = slice(c * col_chunk, (c + 1) * col_chunk)
        o_f = jnp.dot(fm_ref[...], wf_ref[:, cols], preferred_element_type=F32)
        o_n = jnp.dot(na_ref[...], wn_ref[:, cols], preferred_element_type=F32)
        o_m = jnp.dot(mo_ref[...], wm_ref[:, cols], preferred_element_type=F32)
        acc = jax.nn.sigmoid(g0_ref[:, cols].astype(F32)) * o_f
        acc = acc + jax.nn.sigmoid(g1_ref[:, cols].astype(F32)) * o_n
        acc = acc + jax.nn.sigmoid(g2_ref[:, cols].astype(F32)) * o_m
        o_ref[:, cols] = acc.astype(o_ref.dtype)


def _merge(fm, o_na, o_mem, proj, wf, wn, wm):
    tm = 512
    gb = GATE_OFF // D_MODEL
    return pl.pallas_call(
        functools.partial(_merge_kernel, col_chunk=512),
        grid=(SEQ // tm,),
        in_specs=[pl.BlockSpec((tm, FOURIER_W), lambda i: (i, 0)),
                  pl.BlockSpec((tm, NA_W), lambda i: (i, 0)),
                  pl.BlockSpec((tm, MEM_W), lambda i: (i, 0)),
                  pl.BlockSpec((tm, D_MODEL), lambda i: (i, gb)),
                  pl.BlockSpec((tm, D_MODEL), lambda i: (i, gb + 1)),
                  pl.BlockSpec((tm, D_MODEL), lambda i: (i, gb + 2)),
                  pl.BlockSpec((FOURIER_W, D_MODEL), lambda i: (0, 0)),
                  pl.BlockSpec((NA_W, D_MODEL), lambda i: (0, 0)),
                  pl.BlockSpec((MEM_W, D_MODEL), lambda i: (0, 0))],
        out_specs=pl.BlockSpec((tm, D_MODEL), lambda i: (i, 0)),
        out_shape=jax.ShapeDtypeStruct((SEQ, D_MODEL), BF16),
        compiler_params=_params(("parallel",), 48),
        name="gated_merge",
    )(fm, o_na, o_mem, proj, proj, proj, wf, wn, wm)


def _oproj_kernel(a_ref, w_ref, x_ref, g_ref, xo_ref, h_ref, *, row_chunk, packed):
    xo_ref[...] = jnp.dot(a_ref[...], w_ref[...], preferred_element_type=F32) + x_ref[...]

    def body(c, carry):
        rows = pl.ds(pl.multiple_of(c * row_chunk, row_chunk), row_chunk)
        h = _rms_rows(xo_ref[rows, :], g_ref[...])
        h_ref[rows, :] = _pack_halves(h) if packed else h.astype(h_ref.dtype)
        return carry
    lax.fori_loop(0, xo_ref.shape[0] // row_chunk, body, 0)


def _out_proj(merged, w_o, x, g, *, packed):
    tm = 512
    h_cols, h_dtype = (D_MODEL // 2, jnp.uint32) if packed else (D_MODEL, BF16)
    return pl.pallas_call(
        functools.partial(_oproj_kernel, row_chunk=128, packed=packed),
        grid=(SEQ // tm,),
        in_specs=[pl.BlockSpec((tm, D_MODEL), lambda i: (i, 0)),
                  pl.BlockSpec((D_MODEL, D_MODEL), lambda i: (0, 0)),
                  pl.BlockSpec((tm, D_MODEL), lambda i: (i, 0)),
                  pl.BlockSpec((1, D_MODEL), lambda i: (0, 0))],
        out_specs=[pl.BlockSpec((tm, D_MODEL), lambda i: (i, 0)),
                   pl.BlockSpec((tm, h_cols), lambda i: (i, 0))],
        out_shape=[jax.ShapeDtypeStruct((SEQ, D_MODEL), F32),
                   jax.ShapeDtypeStruct((SEQ, h_cols), h_dtype)],
        compiler_params=_params(("parallel",), 48),
        name="out_proj",
    )(merged, w_o, x, g.reshape(1, D_MODEL))


def _swiglu_kernel(be_ref, nu_ref, x_ref, wg_ref, wu_ref, wd_ref, *rest, nf, has_scale, has_resid,
                   packed):
    rest = list(rest)
    scale_ref = rest.pop(0) if has_scale else None
    resid_ref = rest.pop(0) if has_resid else None
    o_ref, acc_ref = rest[:2]
    xb_ref = rest[2] if packed else x_ref
    i = pl.program_id(0)
    f = pl.program_id(1)
    active = i < nu_ref[0]
    row_chunk = 64
    n_chunks = acc_ref.shape[0] // row_chunk

    @pl.when(jnp.logical_and(active, f == 0))
    def _():
        acc_ref[...] = jnp.zeros(acc_ref.shape, acc_ref.dtype)
        if packed:
            half = xb_ref.shape[1] // 2

            def unpack_rows(c, carry):
                rows = pl.ds(pl.multiple_of(c * row_chunk, row_chunk), row_chunk)
                lo, hi = _unpack_halves(x_ref[rows, :])
                xb_ref[rows, :half] = lo.astype(BF16)
                xb_ref[rows, half:] = hi.astype(BF16)
                return carry
            lax.fori_loop(0, n_chunks, unpack_rows, 0)

    @pl.when(active)
    def _():
        x = xb_ref[...]
        g = jnp.dot(x, wg_ref[0].astype(BF16), preferred_element_type=F32)
        u = jnp.dot(x, wu_ref[0].astype(BF16), preferred_element_type=F32)
        a = (g * jax.nn.sigmoid(g) * u).astype(BF16)
        acc_ref[...] += jnp.dot(a, wd_ref[0].astype(BF16), preferred_element_type=F32)

    @pl.when(f == nf - 1)
    def _():
        @pl.when(active)
        def _():
            def finish_rows(c, carry):
                rows = pl.ds(pl.multiple_of(c * row_chunk, row_chunk), row_chunk)
                y = acc_ref[rows, :]
                if has_scale:
                    y = y * scale_ref[rows, :]
                if has_resid:
                    y = y + resid_ref[rows, :]
                o_ref[rows, :] = _pack_halves(y) if packed else y.astype(o_ref.dtype)
                return carry
            lax.fori_loop(0, n_chunks, finish_rows, 0)

        @pl.when(jnp.logical_not(active))
        def _():
            o_ref[...] = jnp.zeros(o_ref.shape, o_ref.dtype)


def _swiglu(x, blk_e, n_used, wg, wu, wd, *, tm, tf, scale=None, resid=None, out_dtype=None,
            packed=False, vmem_mib, name):
    m = x.shape[0]
    d = D_MODEL
    xd = x.shape[1]
    nf = D_FF // tf
    n_tiles = m // tm
    if packed:
        out_dtype = jnp.uint32

    def x_map(i, f, be, nu):
        return (jnp.minimum(i, nu[0] - 1), 0)

    def f_idx(i, f, nu):
        return jnp.where(i < nu[0], f, nf - 1)

    in_specs = [pl.BlockSpec((tm, xd), x_map),
                pl.BlockSpec((1, d, tf), lambda i, f, be, nu: (be[i], 0, f_idx(i, f, nu))),
                pl.BlockSpec((1, d, tf), lambda i, f, be, nu: (be[i], 0, f_idx(i, f, nu))),
                pl.BlockSpec((1, tf, d), lambda i, f, be, nu: (be[i], f_idx(i, f, nu), 0))]
    args = [x, wg, wu, wd]
    if scale is not None:
        in_specs.append(pl.BlockSpec((tm, 1), x_map))
        args.append(scale)
    if resid is not None:
        in_specs.append(pl.BlockSpec((tm, d), x_map))
        args.append(resid)
    scratch = [pltpu.VMEM((tm, d), F32)]
    if packed:
        scratch.append(pltpu.VMEM((tm, d), BF16))
    grid_spec = pltpu.PrefetchScalarGridSpec(
        num_scalar_prefetch=2,
        grid=(n_tiles, nf),
        in_specs=in_specs,
        out_specs=pl.BlockSpec((tm, xd), lambda i, f, be, nu: (i, 0)),
        scratch_shapes=scratch,
    )
    return pl.pallas_call(
        functools.partial(_swiglu_kernel, nf=nf, has_scale=scale is not None,
                          has_resid=resid is not None, packed=packed),
        grid_spec=grid_spec,
        out_shape=jax.ShapeDtypeStruct((m, xd), out_dtype),
        compiler_params=_params(("arbitrary", "arbitrary"), vmem_mib),
        name=name,
    )(blk_e, n_used, *args)


ROUTER_LANES = 128


def _router_kernel(x_ref, g_ref, w_ref, idx_ref, gate_ref):
    h = _rms_rows(x_ref[...], g_ref[...])
    logits = jnp.dot(h, w_ref[...], preferred_element_type=F32,
                     precision=lax.Precision.HIGHEST)
    lane = lax.broadcasted_iota(jnp.int32, logits.shape, 1).astype(F32)
    logits = jnp.where(lane < N_EXPERTS, logits, -jnp.inf)
    m1 = jnp.max(logits, axis=-1, keepdims=True)
    i1 = jnp.min(jnp.where(logits == m1, lane, float(ROUTER_LANES)), axis=-1, keepdims=True)
    rest = jnp.where(lane == i1, -jnp.inf, logits)
    m2 = jnp.max(rest, axis=-1, keepdims=True)
    i2 = jnp.min(jnp.where(rest == m2, lane, float(ROUTER_LANES)), axis=-1, keepdims=True)
    e21 = jnp.exp(m2 - m1)
    g1 = 1.0 / (1.0 + e21)
    g2 = e21 * g1
    idx_ref[...] = jnp.where(lane == 0.0, i1, i2).astype(jnp.int32)
    gate_ref[...] = jnp.where(lane == 0.0, g1, g2)


def _router(x, g, w_router):
    tm = 256
    w_pad = jnp.zeros((D_MODEL, ROUTER_LANES), F32).at[:, :N_EXPERTS].set(w_router.astype(F32))
    return pl.pallas_call(
        _router_kernel,
        grid=(SEQ // tm,),
        in_specs=[pl.BlockSpec((tm, D_MODEL), lambda i: (i, 0)),
                  pl.BlockSpec((1, D_MODEL), lambda i: (0, 0)),
                  pl.BlockSpec((D_MODEL, ROUTER_LANES), lambda i: (0, 0))],
        out_specs=[pl.BlockSpec((tm, ROUTER_LANES), lambda i: (i, 0)),
                   pl.BlockSpec((tm, ROUTER_LANES), lambda i: (i, 0))],
        out_shape=[jax.ShapeDtypeStruct((SEQ, ROUTER_LANES), jnp.int32),
                   jax.ShapeDtypeStruct((SEQ, ROUTER_LANES), F32)],
        compiler_params=_params(("parallel",), 32),
        name="moe_router",
    )(x, g.reshape(1, D_MODEL), w_pad)


MOE_TM = 1024
MOE_TF = 256
HALF_D = D_MODEL // 2


def _start_row_gather(idx_ref, idx_base, src_ref, dst_ref, sem, n_rows):
    def body(r, carry):
        row = idx_ref[idx_base + r]
        pltpu.make_async_copy(src_ref.at[pl.ds(row, 1), :], dst_ref.at[pl.ds(r, 1), :], sem).start()
        return carry
    lax.fori_loop(0, n_rows, body, 0, unroll=8)


def _wait_row_gather(src_ref, dst_ref, sem, n_rows):
    pltpu.make_async_copy(src_ref.at[pl.ds(0, n_rows), :], dst_ref, sem).wait()


def _moe_gather_kernel(tok_ref, src_ref, o_ref, buf_ref, sems, *, rows):
    i = pl.program_id(0)
    slot = i % 2

    @pl.when(i == 0)
    def _():
        _start_row_gather(tok_ref, 0, src_ref, buf_ref.at[0], sems.at[0], rows)

    @pl.when(i + 1 < pl.num_programs(0))
    def _():
        _start_row_gather(tok_ref, (i + 1) * rows, src_ref, buf_ref.at[1 - slot],
                          sems.at[1 - slot], rows)

    _wait_row_gather(src_ref, buf_ref.at[slot], sems.at[slot], rows)
    o_ref[...] = buf_ref[slot]


def _moe_gather(h_packed, tok):
    rows = 1024
    cap = tok.shape[0]
    grid_spec = pltpu.PrefetchScalarGridSpec(
        num_scalar_prefetch=1,
        grid=(cap // rows,),
        in_specs=[pl.BlockSpec(memory_space=pl.ANY)],
        out_specs=pl.BlockSpec((rows, HALF_D), lambda i, tok_ref: (i, 0)),
        scratch_shapes=[pltpu.VMEM((2, rows, HALF_D), jnp.uint32),
                        pltpu.SemaphoreType.DMA((2,))],
    )
    return pl.pallas_call(
        functools.partial(_moe_gather_kernel, rows=rows),
        grid_spec=grid_spec,
        out_shape=jax.ShapeDtypeStruct((cap, HALF_D), jnp.uint32),
        compiler_params=_params(("arbitrary",), 32),
        name="moe_gather",
    )(tok, h_packed)


def _moe_combine_kernel(pos_ref, y_ref, x_ref, o_ref, buf_ref, sems, *, tokens):
    i = pl.program_id(0)
    slot = i % 2
    rows = TOP_K * tokens

    @pl.when(i == 0)
    def _():
        _start_row_gather(pos_ref, 0, y_ref, buf_ref.at[0], sems.at[0], rows)

    @pl.when(i + 1 < pl.num_programs(0))
    def _():
        _start_row_gather(pos_ref, (i + 1) * rows, y_ref, buf_ref.at[1 - slot],
                          sems.at[1 - slot], rows)

    _wait_row_gather(y_ref, buf_ref.at[slot], sems.at[slot], rows)
    lo0, hi0 = _unpack_halves(buf_ref[slot, :tokens, :])
    lo1, hi1 = _unpack_halves(buf_ref[slot, tokens:, :])
    o_ref[:, :HALF_D] = x_ref[:, :HALF_D] + (lo0 + lo1)
    o_ref[:, HALF_D:] = x_ref[:, HALF_D:] + (hi0 + hi1)


def _moe_combine(y_packed, pos, x):
    tokens = 512
    n = x.shape[0]
    steps = n // tokens
    pos_tiled = pos.reshape(steps, tokens, TOP_K).transpose(0, 2, 1).reshape(-1)
    grid_spec = pltpu.PrefetchScalarGridSpec(
        num_scalar_prefetch=1,
        grid=(steps,),
        in_specs=[pl.BlockSpec(memory_space=pl.ANY),
                  pl.BlockSpec((tokens, D_MODEL), lambda i, pos_ref: (i, 0))],
        out_specs=pl.BlockSpec((tokens, D_MODEL), lambda i, pos_ref: (i, 0)),
        scratch_shapes=[pltpu.VMEM((2, TOP_K * tokens, HALF_D), jnp.uint32),
                        pltpu.SemaphoreType.DMA((2,))],
    )
    return pl.pallas_call(
        functools.partial(_moe_combine_kernel, tokens=tokens),
        grid_spec=grid_spec,
        out_shape=jax.ShapeDtypeStruct((n, D_MODEL), F32),
        compiler_params=_params(("arbitrary",), 40),
        name="moe_combine",
    )(pos_tiled, y_packed, x)


def _moe(x_new, h2, g_ffn, w_router, wg, wu, wd):
    n = SEQ
    n_assign = n * TOP_K
    idx, gate = _router(x_new, g_ffn, w_router)
    e_flat = idx[:, :TOP_K].reshape(-1)
    w_flat = gate[:, :TOP_K].reshape(-1)
    onehot = (e_flat[:, None] == jnp.arange(N_EXPERTS, dtype=jnp.int32)[None, :]).astype(jnp.int32)
    csum = jnp.cumsum(onehot, axis=0)
    counts = csum[-1]
    rank = jnp.sum((csum - onehot) * onehot, axis=1)
    padded = (counts + MOE_TM - 1) // MOE_TM * MOE_TM
    pad_ends = jnp.cumsum(padded)
    pad_starts = pad_ends - padded
    dest = jnp.sum(onehot * pad_starts[None, :], axis=1) + rank
    cap = n_assign + N_EXPERTS * MOE_TM
    n_tiles = cap // MOE_TM
    src = jnp.full((cap,), -1, jnp.int32).at[dest].set(jnp.arange(n_assign, dtype=jnp.int32))
    live = src >= 0
    src_c = jnp.maximum(src, 0)
    tok = jnp.where(live, src_c // TOP_K, 0)
    wt = jnp.where(live, w_flat[src_c], 0.0).reshape(cap, 1)
    n_used = (pad_ends[-1] // MOE_TM).astype(jnp.int32)
    tile_start = jnp.arange(n_tiles, dtype=jnp.int32) * MOE_TM
    tile_start = jnp.minimum(tile_start, pad_ends[-1] - 1)
    blk_e = jnp.sum((tile_start[:, None] >= pad_ends[None, :]).astype(jnp.int32), axis=1)
    blk_e = jnp.clip(blk_e, 0, N_EXPERTS - 1)
    xb = _moe_gather(h2, tok)
    yb = _swiglu(xb, blk_e, n_used.reshape(1), wg, wu, wd, tm=MOE_TM, tf=MOE_TF, scale=wt,
                 packed=True, vmem_mib=60, name="moe_experts")
    return _moe_combine(yb, dest.reshape(n, TOP_K), x_new)


def kernel(x, mem, ln_mix_g, w_in, na_q_g, na_k_g, na_rpb, mem_ln_g, w_mem_kv, mem_q_g, mem_k_g,
           w_fourier_out, w_na_out, w_mem_out, w_o, ln_ffn_g, ffn_w_gate, ffn_w_up, ffn_w_down,
           moe_router, moe_w_gate, moe_w_up, moe_w_down):
    assert x.shape == (1, SEQ, D_MODEL) and mem.shape == (1, N_MEM, D_MODEL)
    xs = x.reshape(SEQ, D_MODEL)
    mem2 = mem.reshape(N_MEM, D_MODEL)
    tables = _dft_tables()
    dense_tm = 512
    dense_e = jnp.zeros((SEQ // dense_tm,), jnp.int32)
    dense_n = jnp.full((1,), SEQ // dense_tm, jnp.int32)
    for l in range(DEPTH):
        proj = _norm_mm(xs, ln_mix_g[l], w_in, l, tm=1024, tn=1024, out_dtype=BF16,
                        vmem_mib=56, name="in_proj")
        fm = _fourier_mix(proj[:, :FOURIER_W], tables)
        o_na = _na_attention(proj, na_q_g[l], na_k_g[l], _na_col_blocks(na_rpb[l]))
        kv = _norm_mm(mem2, mem_ln_g[l], w_mem_kv, l, tm=N_MEM, tn=2 * MEM_W, out_dtype=F32,
                      vmem_mib=40, name="mem_kv_proj")
        o_mem = _mem_attention(proj, kv, mem_q_g[l], mem_k_g[l])
        merged = _merge(fm, o_na, o_mem, proj, w_fourier_out[l].astype(BF16),
                        w_na_out[l].astype(BF16), w_mem_out[l].astype(BF16))
        x_new, h2 = _out_proj(merged, w_o[l].astype(BF16), xs, ln_ffn_g[l], packed=(l % 2 == 1))
        i = l // 2
        if l % 2 == 0:
            xs = _swiglu(h2, dense_e, dense_n, ffn_w_gate[i:i + 1].astype(BF16),
                         ffn_w_up[i:i + 1].astype(BF16), ffn_w_down[i:i + 1].astype(BF16),
                         tm=dense_tm, tf=512, resid=x_new, out_dtype=F32, vmem_mib=48,
                         name="dense_ffn")
        else:
            xs = _moe(x_new, h2, ln_ffn_g[l], moe_router[i], moe_w_gate[i], moe_w_up[i],
                      moe_w_down[i])
    return xs.reshape(1, SEQ, D_MODEL)
```

```python
import functools
import math

import numpy as np
import jax
import jax.numpy as jnp
from jax import lax
from jax.experimental import pallas as pl
from jax.experimental.pallas import tpu as pltpu

F32 = jnp.float32
BF16 = jnp.bfloat16

D_MODEL = 2048
SEQ = 16384
DEPTH = 2
GRID_W = 64
ROWS = SEQ // GRID_W
HEAD_DIM = 128
N_FOURIER_GROUPS = 4
FOURIER_W = N_FOURIER_GROUPS * HEAD_DIM
N_NA_HEADS = 8
NA_W = N_NA_HEADS * HEAD_DIM
WIN_H = 8
WIN_W = 16
N_MEM = 256
N_MEM_HEADS = 4
MEM_W = N_MEM_HEADS * HEAD_DIM
IN_W = FOURIER_W + 3 * NA_W + MEM_W + 3 * D_MODEL
D_FF = 5632
N_EXPERTS = 8
TOP_K = 2
EPS = 1e-6
HALF_D = D_MODEL // 2

Q_NA_OFF = FOURIER_W
K_NA_OFF = Q_NA_OFF + NA_W
V_NA_OFF = K_NA_OFF + NA_W
Q_MEM_OFF = V_NA_OFF + NA_W
GATE_OFF = Q_MEM_OFF + MEM_W

MIB = 1024 * 1024
NEG_BIG = -1e30

FFT_A = 128
FFT_B = SEQ // FFT_A


def _params(semantics, vmem_mib):
    return pltpu.CompilerParams(dimension_semantics=semantics,
                                vmem_limit_bytes=int(vmem_mib * MIB))


def _rms_rows(x, gain):
    ms = jnp.mean(x * x, axis=-1, keepdims=True)
    return x * lax.rsqrt(ms + EPS) * gain


def _bf16_bits_rtne(v):
    bits = pltpu.bitcast(v, jnp.uint32)
    lsb = lax.shift_right_logical(bits, jnp.uint32(16)) & jnp.uint32(1)
    return bits + jnp.uint32(0x7FFF) + lsb


def _pack_halves(v):
    half = v.shape[1] // 2
    lo = lax.shift_right_logical(_bf16_bits_rtne(v[:, :half]), jnp.uint32(16))
    hi = _bf16_bits_rtne(v[:, half:]) & jnp.uint32(0xFFFF0000)
    return lo | hi


def _unpack_halves(w):
    lo = pltpu.bitcast(lax.shift_left(w, jnp.uint32(16)), F32)
    hi = pltpu.bitcast(w & jnp.uint32(0xFFFF0000), F32)
    return lo, hi


def _norm_mm_kernel(x_ref, g_ref, w_ref, o_ref, h_ref, *, row_chunk):
    @pl.when(pl.program_id(1) == 0)
    def _():
        def body(c, carry):
            rows = pl.ds(pl.multiple_of(c * row_chunk, row_chunk), row_chunk)
            h_ref[rows, :] = _rms_rows(x_ref[rows, :].astype(F32), g_ref[...]).astype(BF16)
            return carry
        lax.fori_loop(0, h_ref.shape[0] // row_chunk, body, 0)

    o_ref[...] = jnp.dot(h_ref[...], w_ref[0].astype(BF16),
                         preferred_element_type=F32).astype(o_ref.dtype)


def _norm_mm(x, g, w, layer, *, tm, tn, out_dtype, vmem_mib, name):
    m, k = x.shape
    n = w.shape[2]
    row_chunk = min(tm, 128)
    return pl.pallas_call(
        functools.partial(_norm_mm_kernel, row_chunk=row_chunk),
        grid=(m // tm, n // tn),
        in_specs=[pl.BlockSpec((tm, k), lambda i, j: (i, 0)),
                  pl.BlockSpec((1, k), lambda i, j: (0, 0)),
                  pl.BlockSpec((1, k, tn), lambda i, j: (layer, 0, j))],
        out_specs=pl.BlockSpec((tm, tn), lambda i, j: (i, j)),
        out_shape=jax.ShapeDtypeStruct((m, n), out_dtype),
        scratch_shapes=[pltpu.VMEM((tm, k), BF16)],
        compiler_params=_params(("parallel", "arbitrary"), vmem_mib),
        name=name,
    )(x, g.reshape(1, k), w)


def _dft_tables():
    a = np.arange(FFT_A)
    ang1 = 2.0 * np.pi * np.outer(a, a) / FFT_A
    w1 = np.concatenate([np.cos(ang1), -np.sin(ang1)], axis=0)
    k1 = np.arange(FFT_A)[:, None, None]
    k2 = np.arange(FFT_B)[None, :, None]
    s2 = np.arange(FFT_B)[None, None, :]
    ang2 = 2.0 * np.pi * ((s2 * (k1 + FFT_A * k2)) % SEQ) / SEQ
    g2 = np.concatenate([np.cos(ang2), np.sin(ang2)], axis=2)
    c = np.arange(HEAD_DIM)
    angc = 2.0 * np.pi * np.outer(c, c) / HEAD_DIM
    eye = np.eye(N_FOURIER_GROUPS)
    bd = np.concatenate([np.kron(eye, np.cos(angc)), np.kron(eye, np.sin(angc))], axis=0)
    return (jnp.asarray(w1, dtype=BF16), jnp.asarray(g2, dtype=BF16),
            jnp.asarray(bd, dtype=BF16))


def _dft1_kernel(w_ref, x_ref, br_ref, bi_ref):
    r = jnp.dot(w_ref[...], x_ref[...], preferred_element_type=F32)
    br_ref[...] = r[:FFT_A].astype(BF16)
    bi_ref[...] = r[FFT_A:].astype(BF16)


def _dft2_kernel(br_ref, bi_ref, g_ref, bd_ref, o_ref, *, kb, norm):
    for i in range(kb):
        br = br_ref[i]
        bi = bi_ref[i]
        g = g_ref[i]
        zr = jnp.dot(g, jnp.concatenate([br, bi], axis=0), preferred_element_type=F32)
        zi = jnp.dot(g, jnp.concatenate([bi, -br], axis=0), preferred_element_type=F32)
        z = jnp.concatenate([zr, zi], axis=1).astype(BF16)
        y = jnp.dot(z, bd_ref[...], preferred_element_type=F32) * norm
        o_ref[:, i * FOURIER_W:(i + 1) * FOURIER_W] = y.astype(o_ref.dtype)


def _fourier_mix(f_in, tables):
    w1, g2, bd = tables
    lanes = FFT_B * FOURIER_W
    x2d = f_in.reshape(FFT_A, lanes)
    tn = 8192
    br, bi = pl.pallas_call(
        _dft1_kernel,
        grid=(lanes // tn,),
        in_specs=[pl.BlockSpec((2 * FFT_A, FFT_A), lambda j: (0, 0)),
                  pl.BlockSpec((FFT_A, tn), lambda j: (0, j))],
        out_specs=[pl.BlockSpec((FFT_A, tn), lambda j: (0, j)),
                   pl.BlockSpec((FFT_A, tn), lambda j: (0, j))],
        out_shape=[jax.ShapeDtypeStruct((FFT_A, lanes), BF16)] * 2,
        compiler_params=_params(("parallel",), 40),
        name="fourier_stage1",
    )(w1, x2d)
    br3 = br.reshape(FFT_A, FFT_B, FOURIER_W)
    bi3 = bi.reshape(FFT_A, FFT_B, FOURIER_W)
    kb = 8
    norm = 1.0 / math.sqrt(SEQ * HEAD_DIM)
    y2d = pl.pallas_call(
        functools.partial(_dft2_kernel, kb=kb, norm=norm),
        grid=(FFT_A // kb,),
        in_specs=[pl.BlockSpec((kb, FFT_B, FOURIER_W), lambda j: (j, 0, 0)),
                  pl.BlockSpec((kb, FFT_B, FOURIER_W), lambda j: (j, 0, 0)),
                  pl.BlockSpec((kb, FFT_B, 2 * FFT_B), lambda j: (j, 0, 0)),
                  pl.BlockSpec((2 * FOURIER_W, FOURIER_W), lambda j: (0, 0))],
        out_specs=pl.BlockSpec((FFT_B, kb * FOURIER_W), lambda j: (0, j)),
        out_shape=jax.ShapeDtypeStruct((FFT_B, FFT_A * FOURIER_W), BF16),
        compiler_params=_params(("parallel",), 40),
        name="fourier_stage2",
    )(br3, bi3, g2, bd)
    return y2d.reshape(SEQ, FOURIER_W)


NA_SUB_ROWS = 4
NA_KEY_ROWS = NA_SUB_ROWS + WIN_H
NA_SUBS_PER_STEP = 4
NA_N_SUB = ROWS // NA_SUB_ROWS
NA_MAX_KEY_START = ROWS - NA_KEY_ROWS
NA_N_ROW_OFFSETS = 2 * WIN_H - 1
NA_MASKED_BLOCK = NA_N_ROW_OFFSETS


def _na_row_offsets():
    out = np.full((3, NA_SUB_ROWS, NA_KEY_ROWS), NA_MASKED_BLOCK, np.int32)
    for case, r0 in enumerate((0, 2 * NA_SUB_ROWS, ROWS - NA_SUB_ROWS)):
        ks = int(np.clip(r0 - WIN_H // 2, 0, NA_MAX_KEY_START))
        for qi in range(NA_SUB_ROWS):
            r = r0 + qi
            r_start = int(np.clip(r - WIN_H // 2, 0, ROWS - WIN_H))
            for kj in range(NA_KEY_ROWS):
                if r_start <= ks + kj < r_start + WIN_H:
                    out[case, qi, kj] = ks + kj - r + (WIN_H - 1)
    return out


def _na_col_blocks(rpb):
    qc = np.arange(GRID_W)[:, None]
    kc = np.arange(GRID_W)[None, :]
    col_start = np.clip(qc - WIN_W // 2, 0, GRID_W - WIN_W)
    col_valid = (kc >= col_start) & (kc < col_start + WIN_W)
    dc = kc - qc + (WIN_W - 1)
    sel = (np.arange(2 * WIN_W - 1)[:, None, None] == dc[None]) & col_valid[None]
    sel = jnp.asarray(sel.astype(np.float32))
    t = jnp.sum(rpb.astype(F32)[:, :, :, None, None] * sel[None, None], axis=2)
    t = jnp.where(col_valid[None, None], t, NEG_BIG)
    masked = jnp.full((N_NA_HEADS, 1, GRID_W, GRID_W), NEG_BIG, F32)
    t = jnp.concatenate([t, masked], axis=1)
    return jnp.concatenate([t, t], axis=-1)


def _na_kernel(q_ref, k_ref, v_ref, qg_ref, kg_ref, cb_ref, o_ref, kn_ref, bias_ref):
    step = pl.program_id(1)
    nq = NA_SUB_ROWS * GRID_W
    nk = NA_KEY_ROWS * GRID_W

    @pl.when(step == 0)
    def _():
        chunk = 1024

        def body(c, carry):
            rows = pl.ds(pl.multiple_of(c * chunk, chunk), chunk)
            kn_ref[rows, :] = _rms_rows(k_ref[rows, :].astype(F32), kg_ref[...]).astype(BF16)
            return carry
        lax.fori_loop(0, SEQ // chunk, body, 0)

        row_off = _na_row_offsets()
        for case in range(3):
            for qi in range(NA_SUB_ROWS):
                for kj in range(NA_KEY_ROWS):
                    lanes = slice(kj * GRID_W, (kj + 1) * GRID_W)
                    src = slice((kj % 2) * GRID_W, (kj % 2 + 1) * GRID_W)
                    bias_ref[case, qi * GRID_W:(qi + 1) * GRID_W, lanes] = (
                        cb_ref[0, int(row_off[case, qi, kj]), :, src])

    scale = HEAD_DIM ** -0.5
    for sb in range(NA_SUBS_PER_STEP):
        sub = step * NA_SUBS_PER_STEP + sb
        key_row0 = jnp.clip(sub * NA_SUB_ROWS - WIN_H // 2, 0, NA_MAX_KEY_START)
        case = jnp.where(sub == 0, 0, jnp.where(sub == NA_N_SUB - 1, 2, 1))
        q = q_ref[sb * nq:(sb + 1) * nq, :].astype(F32)
        qn = (_rms_rows(q, qg_ref[...]) * scale).astype(BF16)
        kstart = pl.multiple_of(key_row0 * GRID_W, GRID_W)
        kw = kn_ref[pl.ds(kstart, nk), :]
        vw = v_ref[pl.ds(kstart, nk), :]
        s = lax.dot_general(qn, kw, (((1,), (1,)), ((), ())), preferred_element_type=F32)
        s = s + bias_ref[case]
        m = jnp.max(s, axis=-1, keepdims=True)
        p = jnp.exp(s - m)
        l = jnp.sum(p, axis=-1, keepdims=True)
        o = jnp.dot(p.astype(BF16), vw, preferred_element_type=F32) / l
        o_ref[sb * nq:(sb + 1) * nq, :] = o.astype(o_ref.dtype)


def _na_attention(proj, q_g, k_g, col_blocks):
    nq = NA_SUB_ROWS * GRID_W
    nk = NA_KEY_ROWS * GRID_W
    tq = NA_SUBS_PER_STEP * nq
    n_blk = NA_N_ROW_OFFSETS + 1
    qb, kb_, vb = Q_NA_OFF // HEAD_DIM, K_NA_OFF // HEAD_DIM, V_NA_OFF // HEAD_DIM
    return pl.pallas_call(
        _na_kernel,
        grid=(N_NA_HEADS, SEQ // tq),
        in_specs=[pl.BlockSpec((tq, HEAD_DIM), lambda h, i: (i, qb + h)),
                  pl.BlockSpec((SEQ, HEAD_DIM), lambda h, i: (0, kb_ + h)),
                  pl.BlockSpec((SEQ, HEAD_DIM), lambda h, i: (0, vb + h)),
                  pl.BlockSpec((1, HEAD_DIM), lambda h, i: (0, 0)),
                  pl.BlockSpec((1, HEAD_DIM), lambda h, i: (0, 0)),
                  pl.BlockSpec((1, n_blk, GRID_W, 2 * GRID_W), lambda h, i: (h, 0, 0, 0))],
        out_specs=pl.BlockSpec((tq, HEAD_DIM), lambda h, i: (i, h)),
        out_shape=jax.ShapeDtypeStruct((SEQ, NA_W), BF16),
        scratch_shapes=[pltpu.VMEM((SEQ, HEAD_DIM), BF16),
                        pltpu.VMEM((3, nq, nk), F32)],
        compiler_params=_params(("parallel", "arbitrary"), 48),
        name="na_attention",
    )(proj, proj, proj, q_g.reshape(1, HEAD_DIM), k_g.reshape(1, HEAD_DIM), col_blocks)


def _mem_attn_kernel(q_ref, kv_ref, qg_ref, kg_ref, o_ref):
    scale = HEAD_DIM ** -0.5
    for h in range(N_MEM_HEADS):
        cols = slice(h * HEAD_DIM, (h + 1) * HEAD_DIM)
        qn = (_rms_rows(q_ref[:, cols].astype(F32), qg_ref[...]) * scale).astype(BF16)
        kn = _rms_rows(kv_ref[:, cols].astype(F32), kg_ref[...]).astype(BF16)
        v = kv_ref[:, MEM_W + h * HEAD_DIM:MEM_W + (h + 1) * HEAD_DIM].astype(BF16)
        s = lax.dot_general(qn, kn, (((1,), (1,)), ((), ())), preferred_element_type=F32)
        m = jnp.max(s, axis=-1, keepdims=True)
        p = jnp.exp(s - m)
        l = jnp.sum(p, axis=-1, keepdims=True)
        o = jnp.dot(p.astype(BF16), v, preferred_element_type=F32) / l
        o_ref[:, cols] = o.astype(o_ref.dtype)


def _mem_attention(proj, kv, q_g, k_g):
    tm = 512
    return pl.pallas_call(
        _mem_attn_kernel,
        grid=(SEQ // tm,),
        in_specs=[pl.BlockSpec((tm, MEM_W), lambda i: (i, Q_MEM_OFF // MEM_W)),
                  pl.BlockSpec((N_MEM, 2 * MEM_W), lambda i: (0, 0)),
                  pl.BlockSpec((1, HEAD_DIM), lambda i: (0, 0)),
                  pl.BlockSpec((1, HEAD_DIM), lambda i: (0, 0))],
        out_specs=pl.BlockSpec((tm, MEM_W), lambda i: (i, 0)),
        out_shape=jax.ShapeDtypeStruct((SEQ, MEM_W), BF16),
        compiler_params=_params(("parallel",), 32),
        name="mem_attention",
    )(proj, kv, q_g.reshape(1, HEAD_DIM), k_g.reshape(1, HEAD_DIM))


def _merge_kernel(fm_ref, na_ref, mo_ref, g0_ref, g1_ref, g2_ref, wf_ref, wn_ref, wm_ref, o_ref,
                  *, col_chunk):
    for c in range(D_MODEL // col_chunk):
        cols = slice(c * col_chunk, (c + 1) * col_chunk)
        o_f = jnp.dot(fm_ref[...], wf_ref[:, cols], preferred_element_type=F32)
        o_n = jnp.dot(na_ref[...], wn_ref[:, cols], preferred_element_type=F32)
        o_m = jnp.dot(mo_ref[...], wm_ref[:, cols], preferred_element_type=F32)
        acc = jax.nn.sigmoid(g0_ref[:, cols].astype(F32)) * o_f
        acc = acc + jax.nn.sigmoid(g1_ref[:, cols].astype(F32)) * o_n
        acc = acc + jax.nn.sigmoid(g2_ref[:, cols].astype(F32)) * o_m
        o_ref[:, cols] = acc.astype(o_ref.dtype)


def _merge(fm, o_na, o_mem, proj, wf, wn, wm):
    tm = 512
    gb = GATE_OFF // D_MODEL
    return pl.pallas_call(
        functools.partial(_merge_kernel, col_chunk=512),
        grid=(SEQ // tm,),
        in_specs=[pl.BlockSpec((tm, FOURIER_W), lambda i: (i, 0)),
                  pl.BlockSpec((tm, NA_W), lambda i: (i, 0)),
                  pl.BlockSpec((tm, MEM_W), lambda i: (i, 0)),
                  pl.BlockSpec((tm, D_MODEL), lambda i: (i, gb)),
                  pl.BlockSpec((tm, D_MODEL), lambda i: (i, gb + 1)),
                  pl.BlockSpec((tm, D_MODEL), lambda i: (i, gb + 2)),
                  pl.BlockSpec((FOURIER_W, D_MODEL), lambda i: (0, 0)),
                  pl.BlockSpec((NA_W, D_MODEL), lambda i: (0, 0)),
                  pl.BlockSpec((MEM_W, D_MODEL), lambda i: (0, 0))],
        out_specs=pl.BlockSpec((tm, D_MODEL), lambda i: (i, 0)),
        out_shape=jax.ShapeDtypeStruct((SEQ, D_MODEL), BF16),
        compiler_params=_params(("parallel",), 48),
        name="gated_merge",
    )(fm, o_na, o_mem, proj, proj, proj, wf, wn, wm)


def _oproj_kernel(a_ref, w_ref, x_ref, g_ref, xo_ref, h_ref, *, row_chunk, packed):
    xo_ref[...] = jnp.dot(a_ref[...], w_ref[...], preferred_element_type=F32) + x_ref[...]

    def body(c, carry):
        rows = pl.ds(pl.multiple_of(c * row_chunk, row_chunk), row_chunk)
        h = _rms_rows(xo_ref[rows, :], g_ref[...])
        h_ref[rows, :] = _pack_halves(h) if packed else h.astype(h_ref.dtype)
        return carry
    lax.fori_loop(0, xo_ref.shape[0] // row_chunk, body, 0)


def _out_proj(merged, w_o, x, g, *, packed):
    tm = 512
    h_cols, h_dtype = (HALF_D, jnp.uint32) if packed else (D_MODEL, BF16)
    return pl.pallas_call(
        functools.partial(_oproj_kernel, row_chunk=128, packed=packed),
        grid=(SEQ // tm,),
        in_specs=[pl.BlockSpec((tm, D_MODEL), lambda i: (i, 0)),
                  pl.BlockSpec((D_MODEL, D_MODEL), lambda i: (0, 0)),
                  pl.BlockSpec((tm, D_MODEL), lambda i: (i, 0)),
                  pl.BlockSpec((1, D_MODEL), lambda i: (0, 0))],
        out_specs=[pl.BlockSpec((tm, D_MODEL), lambda i: (i, 0)),
                   pl.BlockSpec((tm, h_cols), lambda i: (i, 0))],
        out_shape=[jax.ShapeDtypeStruct((SEQ, D_MODEL), F32),
                   jax.ShapeDtypeStruct((SEQ, h_cols), h_dtype)],
        compiler_params=_params(("parallel",), 48),
        name="out_proj",
    )(merged, w_o, x, g.reshape(1, D_MODEL))


def _swiglu_step(x, wg_ref, wu_ref, wd_ref, acc_ref):
    g = jnp.dot(x, wg_ref[0].astype(BF16), preferred_element_type=F32)
    u = jnp.dot(x, wu_ref[0].astype(BF16), preferred_element_type=F32)
    a = (g * jax.nn.sigmoid(g) * u).astype(BF16)
    acc_ref[...] += jnp.dot(a, wd_ref[0].astype(BF16), preferred_element_type=F32)


def _dense_ffn_kernel(x_ref, wg_ref, wu_ref, wd_ref, resid_ref, o_ref, acc_ref, *, nf, row_chunk):
    f = pl.program_id(1)

    @pl.when(f == 0)
    def _():
        acc_ref[...] = jnp.zeros(acc_ref.shape, acc_ref.dtype)

    _swiglu_step(x_ref[...], wg_ref, wu_ref, wd_ref, acc_ref)

    @pl.when(f == nf - 1)
    def _():
        def finish_rows(c, carry):
            rows = pl.ds(pl.multiple_of(c * row_chunk, row_chunk), row_chunk)
            o_ref[rows, :] = acc_ref[rows, :] + resid_ref[rows, :]
            return carry
        lax.fori_loop(0, acc_ref.shape[0] // row_chunk, finish_rows, 0)


def _dense_ffn(h, wg, wu, wd, layer, resid):
    tm, tf = 512, 512
    m, d = h.shape
    nf = D_FF // tf
    return pl.pallas_call(
        functools.partial(_dense_ffn_kernel, nf=nf, row_chunk=64),
        grid=(m // tm, nf),
        in_specs=[pl.BlockSpec((tm, d), lambda i, f: (i, 0)),
                  pl.BlockSpec((1, d, tf), lambda i, f: (layer, 0, f)),
                  pl.BlockSpec((1, d, tf), lambda i, f: (layer, 0, f)),
                  pl.BlockSpec((1, tf, d), lambda i, f: (layer, f, 0)),
                  pl.BlockSpec((tm, d), lambda i, f: (i, 0))],
        out_specs=pl.BlockSpec((tm, d), lambda i, f: (i, 0)),
        out_shape=jax.ShapeDtypeStruct((m, d), F32),
        scratch_shapes=[pltpu.VMEM((tm, d), F32)],
        compiler_params=_params(("parallel", "arbitrary"), 48),
        name="dense_ffn",
    )(h, wg, wu, wd, resid)


MOE_TF = 256
MOE_NF = D_FF // MOE_TF
MOE_ROWS_PER_STEP = 48
MOE_TM = MOE_NF * MOE_ROWS_PER_STEP
MOE_ROW_CHUNK = 96
MOE_N_TILES = -(-SEQ * TOP_K // MOE_TM) + N_EXPERTS


def _tile_row_copy(tok_ref, h_ref, xg_ref, sems, tile, slot, r):
    row = tok_ref[tile * MOE_TM + r]
    return pltpu.make_async_copy(h_ref.at[pl.ds(row, 1), :], xg_ref.at[slot, pl.ds(r, 1), :],
                                 sems.at[slot])


def _wait_tile_rows(h_ref, xg_ref, sems, slot):
    pltpu.make_async_copy(h_ref.at[pl.ds(0, MOE_TM), :], xg_ref.at[slot], sems.at[slot]).wait()


def _moe_expert_kernel(be_ref, nu_ref, tok_ref, h_ref, wg_ref, wu_ref, wd_ref, o_ref,
                       xg_ref, xb_ref, acc_ref, sems):
    i = pl.program_id(0)
    f = pl.program_id(1)
    n_used = nu_ref[0]
    active = i < n_used
    slot = i % 2
    n_chunks = MOE_TM // MOE_ROW_CHUNK

    @pl.when(jnp.logical_and(i == 0, f == 0))
    def _():
        def start_row(r, carry):
            _tile_row_copy(tok_ref, h_ref, xg_ref, sems, 0, 0, r).start()
            return carry
        lax.fori_loop(0, MOE_TM, start_row, 0, unroll=8)

    @pl.when(jnp.logical_and(active, f == 0))
    def _():
        _wait_tile_rows(h_ref, xg_ref, sems, slot)
        acc_ref[...] = jnp.zeros(acc_ref.shape, acc_ref.dtype)

        def unpack_rows(c, carry):
            rows = pl.ds(pl.multiple_of(c * MOE_ROW_CHUNK, MOE_ROW_CHUNK), MOE_ROW_CHUNK)
            lo, hi = _unpack_halves(xg_ref[slot, rows, :])
            xb_ref[rows, :HALF_D] = lo.astype(BF16)
            xb_ref[rows, HALF_D:] = hi.astype(BF16)
            return carry
        lax.fori_loop(0, n_chunks, unpack_rows, 0)

    @pl.when(active)
    def _():
        for j in range(MOE_ROWS_PER_STEP):
            _tile_row_copy(tok_ref, h_ref, xg_ref, sems, i + 1, 1 - slot,
                           f * MOE_ROWS_PER_STEP + j).start()
        _swiglu_step(xb_ref[...], wg_ref, wu_ref, wd_ref, acc_ref)

    @pl.when(jnp.logical_and(i == n_used, f == 0))
    def _():
        _wait_tile_rows(h_ref, xg_ref, sems, slot)

    @pl.when(f == MOE_NF - 1)
    def _():
        @pl.when(active)
        def _():
            def finish_rows(c, carry):
                rows = pl.ds(pl.multiple_of(c * MOE_ROW_CHUNK, MOE_ROW_CHUNK), MOE_ROW_CHUNK)
                o_ref[rows, :] = _pack_halves(acc_ref[rows, :])
                return carry
            lax.fori_loop(0, n_chunks, finish_rows, 0)

        @pl.when(jnp.logical_not(active))
        def _():
            o_ref[...] = jnp.zeros(o_ref.shape, o_ref.dtype)


def _moe_experts(h_packed, tok, blk_e, n_used, wg, wu, wd):
    d, tf = D_MODEL, MOE_TF

    def f_idx(i, f, nu):
        return jnp.where(i < nu[0], f, MOE_NF - 1)

    grid_spec = pltpu.PrefetchScalarGridSpec(
        num_scalar_prefetch=3,
        grid=(MOE_N_TILES, MOE_NF),
        in_specs=[pl.BlockSpec(memory_space=pl.ANY),
                  pl.BlockSpec((1, d, tf), lambda i, f, be, nu, tok: (be[i], 0, f_idx(i, f, nu))),
                  pl.BlockSpec((1, d, tf), lambda i, f, be, nu, tok: (be[i], 0, f_idx(i, f, nu))),
                  pl.BlockSpec((1, tf, d), lambda i, f, be, nu, tok: (be[i], f_idx(i, f, nu), 0))],
        out_specs=pl.BlockSpec((MOE_TM, HALF_D), lambda i, f, be, nu, tok: (i, 0)),
        scratch_shapes=[pltpu.VMEM((2, MOE_TM, HALF_D), jnp.uint32),
                        pltpu.VMEM((MOE_TM, d), BF16),
                        pltpu.VMEM((MOE_TM, d), F32),
                        pltpu.SemaphoreType.DMA((2,))],
    )
    return pl.pallas_call(
        _moe_expert_kernel,
        grid_spec=grid_spec,
        out_shape=jax.ShapeDtypeStruct((MOE_N_TILES * MOE_TM, HALF_D), jnp.uint32),
        compiler_params=_params(("arbitrary", "arbitrary"), 56),
        name="moe_experts",
    )(blk_e, n_used, tok, h_packed, wg, wu, wd)


ROUTER_LANES = 128


def _router_kernel(x_ref, g_ref, w_ref, idx_ref, gate_ref):
    h = _rms_rows(x_ref[...], g_ref[...])
    logits = jnp.dot(h, w_ref[...], preferred_element_type=F32,
                     precision=lax.Precision.HIGHEST)
    lane = lax.broadcasted_iota(jnp.int32, logits.shape, 1).astype(F32)
    logits = jnp.where(lane < N_EXPERTS, logits, -jnp.inf)
    m1 = jnp.max(logits, axis=-1, keepdims=True)
    i1 = jnp.min(jnp.where(logits == m1, lane, float(ROUTER_LANES)), axis=-1, keepdims=True)
    rest = jnp.where(lane == i1, -jnp.inf, logits)
    m2 = jnp.max(rest, axis=-1, keepdims=True)
    i2 = jnp.min(jnp.where(rest == m2, lane, float(ROUTER_LANES)), axis=-1, keepdims=True)
    e21 = jnp.exp(m2 - m1)
    g1 = 1.0 / (1.0 + e21)
    g2 = e21 * g1
    idx_ref[...] = jnp.where(lane == 0.0, i1, i2).astype(jnp.int32)
    gate_ref[...] = jnp.where(lane == 0.0, g1, g2)


def _router(x, g, w_router):
    tm = 256
    w_pad = jnp.zeros((D_MODEL, ROUTER_LANES), F32).at[:, :N_EXPERTS].set(w_router.astype(F32))
    return pl.pallas_call(
        _router_kernel,
        grid=(SEQ // tm,),
        in_specs=[pl.BlockSpec((tm, D_MODEL), lambda i: (i, 0)),
                  pl.BlockSpec((1, D_MODEL), lambda i: (0, 0)),
                  pl.BlockSpec((D_MODEL, ROUTER_LANES), lambda i: (0, 0))],
        out_specs=[pl.BlockSpec((tm, ROUTER_LANES), lambda i: (i, 0)),
                   pl.BlockSpec((tm, ROUTER_LANES), lambda i: (i, 0))],
        out_shape=[jax.ShapeDtypeStruct((SEQ, ROUTER_LANES), jnp.int32),
                   jax.ShapeDtypeStruct((SEQ, ROUTER_LANES), F32)],
        compiler_params=_params(("parallel",), 32),
        name="moe_router",
    )(x, g.reshape(1, D_MODEL), w_pad)


def _start_row_gather(idx_ref, idx_base, src_ref, dst_ref, sem, n_rows):
    def body(r, carry):
        row = idx_ref[idx_base + r]
        pltpu.make_async_copy(src_ref.at[pl.ds(row, 1), :], dst_ref.at[pl.ds(r, 1), :], sem).start()
        return carry
    lax.fori_loop(0, n_rows, body, 0, unroll=8)


def _wait_row_gather(src_ref, dst_ref, sem, n_rows):
    pltpu.make_async_copy(src_ref.at[pl.ds(0, n_rows), :], dst_ref, sem).wait()


def _moe_combine_kernel(pos_ref, y_ref, x_ref, gate_ref, o_ref, buf_ref, sems, *, tokens):
    i = pl.program_id(0)
    slot = i % 2
    rows = TOP_K * tokens

    @pl.when(i == 0)
    def _():
        _start_row_gather(pos_ref, 0, y_ref, buf_ref.at[0], sems.at[0], rows)

    @pl.when(i + 1 < pl.num_programs(0))
    def _():
        _start_row_gather(pos_ref, (i + 1) * rows, y_ref, buf_ref.at[1 - slot],
                          sems.at[1 - slot], rows)

    _wait_row_gather(y_ref, buf_ref.at[slot], sems.at[slot], rows)
    g0 = gate_ref[:, 0:1]
    g1 = gate_ref[:, 1:2]
    lo0, hi0 = _unpack_halves(buf_ref[slot, :tokens, :])
    lo1, hi1 = _unpack_halves(buf_ref[slot, tokens:, :])
    o_ref[:, :HALF_D] = x_ref[:, :HALF_D] + (g0 * lo0 + g1 * lo1)
    o_ref[:, HALF_D:] = x_ref[:, HALF_D:] + (g0 * hi0 + g1 * hi1)


def _moe_combine(y_packed, pos, x, gate):
    tokens = 512
    n = x.shape[0]
    steps = n // tokens
    pos_tiled = pos.reshape(steps, tokens, TOP_K).transpose(0, 2, 1).reshape(-1)
    grid_spec = pltpu.PrefetchScalarGridSpec(
        num_scalar_prefetch=1,
        grid=(steps,),
        in_specs=[pl.BlockSpec(memory_space=pl.ANY),
                  pl.BlockSpec((tokens, D_MODEL), lambda i, pos_ref: (i, 0)),
                  pl.BlockSpec((tokens, ROUTER_LANES), lambda i, pos_ref: (i, 0))],
        out_specs=pl.BlockSpec((tokens, D_MODEL), lambda i, pos_ref: (i, 0)),
        scratch_shapes=[pltpu.VMEM((2, TOP_K * tokens, HALF_D), jnp.uint32),
                        pltpu.SemaphoreType.DMA((2,))],
    )
    return pl.pallas_call(
        functools.partial(_moe_combine_kernel, tokens=tokens),
        grid_spec=grid_spec,
        out_shape=jax.ShapeDtypeStruct((n, D_MODEL), F32),
        compiler_params=_params(("arbitrary",), 40),
        name="moe_combine",
    )(pos_tiled, y_packed, x, gate)


def _moe(x_new, h_packed, g_ffn, w_router, wg, wu, wd):
    n = SEQ
    n_assign = n * TOP_K
    cap = MOE_N_TILES * MOE_TM
    idx, gate = _router(x_new, g_ffn, w_router)
    e_flat = idx[:, :TOP_K].reshape(-1)
    onehot = (e_flat[:, None] == jnp.arange(N_EXPERTS, dtype=jnp.int32)[None, :]).astype(jnp.int32)
    csum = jnp.cumsum(onehot, axis=0)
    counts = csum[-1]
    rank = jnp.sum((csum - onehot) * onehot, axis=1)
    padded = (counts + MOE_TM - 1) // MOE_TM * MOE_TM
    pad_ends = jnp.cumsum(padded)
    pad_starts = pad_ends - padded
    dest = jnp.sum(onehot * pad_starts[None, :], axis=1) + rank
    tok = jnp.zeros((cap,), jnp.int32).at[dest].set(
        jnp.arange(n_assign, dtype=jnp.int32) // TOP_K)
    n_used = (pad_ends[-1] // MOE_TM).astype(jnp.int32).reshape(1)
    tile_start = jnp.arange(MOE_N_TILES, dtype=jnp.int32) * MOE_TM
    tile_start = jnp.minimum(tile_start, pad_ends[-1] - 1)
    blk_e = jnp.sum((tile_start[:, None] >= pad_ends[None, :]).astype(jnp.int32), axis=1)
    blk_e = jnp.clip(blk_e, 0, N_EXPERTS - 1)
    yb = _moe_experts(h_packed, tok, blk_e, n_used, wg, wu, wd)
    return _moe_combine(yb, dest.reshape(n, TOP_K), x_new, gate)


def kernel(x, mem, ln_mix_g, w_in, na_q_g, na_k_g, na_rpb, mem_ln_g, w_mem_kv, mem_q_g, mem_k_g,
           w_fourier_out, w_na_out, w_mem_out, w_o, ln_ffn_g, ffn_w_gate, ffn_w_up, ffn_w_down,
           moe_router, moe_w_gate, moe_w_up, moe_w_down):
    assert x.shape == (1, SEQ, D_MODEL) and mem.shape == (1, N_MEM, D_MODEL)
    xs = x.reshape(SEQ, D_MODEL)
    mem2 = mem.reshape(N_MEM, D_MODEL)
    tables = _dft_tables()
    for l in range(DEPTH):
        proj = _norm_mm(xs, ln_mix_g[l], w_in, l, tm=1024, tn=1024, out_dtype=BF16,
                        vmem_mib=56, name="in_proj")
        fm = _fourier_mix(proj[:, :FOURIER_W], tables)
        o_na = _na_attention(proj, na_q_g[l], na_k_g[l], _na_col_blocks(na_rpb[l]))
        kv = _norm_mm(mem2, mem_ln_g[l], w_mem_kv, l, tm=N_MEM, tn=2 * MEM_W, out_dtype=F32,
                      vmem_mib=40, name="mem_kv_proj")
        o_mem = _mem_attention(proj, kv, mem_q_g[l], mem_k_g[l])
        merged = _merge(fm, o_na, o_mem, proj, w_fourier_out[l].astype(BF16),
                        w_na_out[l].astype(BF16), w_mem_out[l].astype(BF16))
        x_new, h2 = _out_proj(merged, w_o[l].astype(BF16), xs, ln_ffn_g[l], packed=(l % 2 == 1))
        i = l // 2
        if l % 2 == 0:
            xs = _dense_ffn(h2, ffn_w_gate.astype(BF16), ffn_w_up.astype(BF16),
                            ffn_w_down.astype(BF16), i, x_new)
        else:
            xs = _moe(x_new, h2, ln_ffn_g[l], moe_router[i], moe_w_gate[i], moe_w_up[i],
                      moe_w_down[i])
    return xs.reshape(1, SEQ, D_MODEL)
```

```python
import functools
import math

import numpy as np
import jax
import jax.numpy as jnp
from jax import lax
from jax.experimental import pallas as pl
from jax.experimental.pallas import tpu as pltpu

F32 = jnp.float32
BF16 = jnp.bfloat16

D_MODEL = 2048
SEQ = 16384
DEPTH = 2
GRID_W = 64
ROWS = SEQ // GRID_W
HEAD_DIM = 128
N_FOURIER_GROUPS = 4
FOURIER_W = N_FOURIER_GROUPS * HEAD_DIM
N_NA_HEADS = 8
NA_W = N_NA_HEADS * HEAD_DIM
WIN_H = 8
WIN_W = 16
N_MEM = 256
N_MEM_HEADS = 4
MEM_W = N_MEM_HEADS * HEAD_DIM
IN_W = FOURIER_W + 3 * NA_W + MEM_W + 3 * D_MODEL
D_FF = 5632
N_EXPERTS = 8
TOP_K = 2
EPS = 1e-6
HALF_D = D_MODEL // 2

Q_NA_OFF = FOURIER_W
K_NA_OFF = Q_NA_OFF + NA_W
V_NA_OFF = K_NA_OFF + NA_W
Q_MEM_OFF = V_NA_OFF + NA_W
GATE_OFF = Q_MEM_OFF + MEM_W

MIB = 1024 * 1024
NEG_BIG = -1e30

FFT_A = 128
FFT_B = SEQ // FFT_A


def _params(semantics, vmem_mib):
    return pltpu.CompilerParams(dimension_semantics=semantics,
                                vmem_limit_bytes=int(vmem_mib * MIB))


def _rms_rows(x, gain):
    ms = jnp.mean(x * x, axis=-1, keepdims=True)
    return x * lax.rsqrt(ms + EPS) * gain


def _bf16_bits_rtne(v):
    bits = pltpu.bitcast(v, jnp.uint32)
    lsb = lax.shift_right_logical(bits, jnp.uint32(16)) & jnp.uint32(1)
    return bits + jnp.uint32(0x7FFF) + lsb


def _pack_halves(v):
    half = v.shape[1] // 2
    lo = lax.shift_right_logical(_bf16_bits_rtne(v[:, :half]), jnp.uint32(16))
    hi = _bf16_bits_rtne(v[:, half:]) & jnp.uint32(0xFFFF0000)
    return lo | hi


def _unpack_halves(w):
    lo = pltpu.bitcast(lax.shift_left(w, jnp.uint32(16)), F32)
    hi = pltpu.bitcast(w & jnp.uint32(0xFFFF0000), F32)
    return lo, hi


def _norm_mm_kernel(x_ref, g_ref, w_ref, o_ref, h_ref, *, row_chunk):
    @pl.when(pl.program_id(1) == 0)
    def _():
        def body(c, carry):
            rows = pl.ds(pl.multiple_of(c * row_chunk, row_chunk), row_chunk)
            h_ref[rows, :] = _rms_rows(x_ref[rows, :].astype(F32), g_ref[...]).astype(BF16)
            return carry
        lax.fori_loop(0, h_ref.shape[0] // row_chunk, body, 0)

    o_ref[...] = jnp.dot(h_ref[...], w_ref[0].astype(BF16),
                         preferred_element_type=F32).astype(o_ref.dtype)


def _norm_mm(x, g, w, layer, *, tm, tn, out_dtype, vmem_mib, name):
    m, k = x.shape
    n = w.shape[2]
    row_chunk = min(tm, 128)
    return pl.pallas_call(
        functools.partial(_norm_mm_kernel, row_chunk=row_chunk),
        grid=(m // tm, n // tn),
        in_specs=[pl.BlockSpec((tm, k), lambda i, j: (i, 0)),
                  pl.BlockSpec((1, k), lambda i, j: (0, 0)),
                  pl.BlockSpec((1, k, tn), lambda i, j: (layer, 0, j))],
        out_specs=pl.BlockSpec((tm, tn), lambda i, j: (i, j)),
        out_shape=jax.ShapeDtypeStruct((m, n), out_dtype),
        scratch_shapes=[pltpu.VMEM((tm, k), BF16)],
        compiler_params=_params(("parallel", "arbitrary"), vmem_mib),
        name=name,
    )(x, g.reshape(1, k), w)


def _dft_tables():
    a = np.arange(FFT_A)
    ang1 = 2.0 * np.pi * np.outer(a, a) / FFT_A
    w1 = np.concatenate([np.cos(ang1), -np.sin(ang1)], axis=0)
    k1 = np.arange(FFT_A)[:, None, None]
    k2 = np.arange(FFT_B)[None, :, None]
    s2 = np.arange(FFT_B)[None, None, :]
    ang2 = 2.0 * np.pi * ((s2 * (k1 + FFT_A * k2)) % SEQ) / SEQ
    g2 = np.concatenate([np.cos(ang2), np.sin(ang2)], axis=2)
    c = np.arange(HEAD_DIM)
    angc = 2.0 * np.pi * np.outer(c, c) / HEAD_DIM
    eye = np.eye(N_FOURIER_GROUPS)
    bd = np.concatenate([np.kron(eye, np.cos(angc)), np.kron(eye, np.sin(angc))], axis=0)
    return (jnp.asarray(w1, dtype=BF16), jnp.asarray(g2, dtype=BF16),
            jnp.asarray(bd, dtype=BF16))


def _dft1_kernel(w_ref, x_ref, br_ref, bi_ref):
    r = jnp.dot(w_ref[...], x_ref[...], preferred_element_type=F32)
    br_ref[...] = r[:FFT_A].astype(BF16)
    bi_ref[...] = r[FFT_A:].astype(BF16)


def _dft2_kernel(br_ref, bi_ref, g_ref, bd_ref, o_ref, *, kb, norm):
    for i in range(kb):
        br = br_ref[i]
        bi = bi_ref[i]
        g = g_ref[i]
        zr = jnp.dot(g, jnp.concatenate([br, bi], axis=0), preferred_element_type=F32)
        zi = jnp.dot(g, jnp.concatenate([bi, -br], axis=0), preferred_element_type=F32)
        z = jnp.concatenate([zr, zi], axis=1).astype(BF16)
        y = jnp.dot(z, bd_ref[...], preferred_element_type=F32) * norm
        o_ref[:, i * FOURIER_W:(i + 1) * FOURIER_W] = y.astype(o_ref.dtype)


def _fourier_mix(f_in, tables):
    w1, g2, bd = tables
    lanes = FFT_B * FOURIER_W
    x2d = f_in.reshape(FFT_A, lanes)
    tn = 8192
    br, bi = pl.pallas_call(
        _dft1_kernel,
        grid=(lanes // tn,),
        in_specs=[pl.BlockSpec((2 * FFT_A, FFT_A), lambda j: (0, 0)),
                  pl.BlockSpec((FFT_A, tn), lambda j: (0, j))],
        out_specs=[pl.BlockSpec((FFT_A, tn), lambda j: (0, j)),
                   pl.BlockSpec((FFT_A, tn), lambda j: (0, j))],
        out_shape=[jax.ShapeDtypeStruct((FFT_A, lanes), BF16)] * 2,
        compiler_params=_params(("parallel",), 40),
        name="fourier_stage1",
    )(w1, x2d)
    br3 = br.reshape(FFT_A, FFT_B, FOURIER_W)
    bi3 = bi.reshape(FFT_A, FFT_B, FOURIER_W)
    kb = 8
    norm = 1.0 / math.sqrt(SEQ * HEAD_DIM)
    y2d = pl.pallas_call(
        functools.partial(_dft2_kernel, kb=kb, norm=norm),
        grid=(FFT_A // kb,),
        in_specs=[pl.BlockSpec((kb, FFT_B, FOURIER_W), lambda j: (j, 0, 0)),
                  pl.BlockSpec((kb, FFT_B, FOURIER_W), lambda j: (j, 0, 0)),
                  pl.BlockSpec((kb, FFT_B, 2 * FFT_B), lambda j: (j, 0, 0)),
                  pl.BlockSpec((2 * FOURIER_W, FOURIER_W), lambda j: (0, 0))],
        out_specs=pl.BlockSpec((FFT_B, kb * FOURIER_W), lambda j: (0, j)),
        out_shape=jax.ShapeDtypeStruct((FFT_B, FFT_A * FOURIER_W), BF16),
        compiler_params=_params(("parallel",), 40),
        name="fourier_stage2",
    )(br3, bi3, g2, bd)
    return y2d.reshape(SEQ, FOURIER_W)


NA_SUB_ROWS = 4
NA_KEY_ROWS = NA_SUB_ROWS + WIN_H
NA_SUBS_PER_STEP = 4
NA_N_SUB = ROWS // NA_SUB_ROWS
NA_MAX_KEY_START = ROWS - NA_KEY_ROWS
NA_N_ROW_OFFSETS = 2 * WIN_H - 1
NA_MASKED_BLOCK = NA_N_ROW_OFFSETS


def _na_row_offsets():
    out = np.full((3, NA_SUB_ROWS, NA_KEY_ROWS), NA_MASKED_BLOCK, np.int32)
    for case, r0 in enumerate((0, 2 * NA_SUB_ROWS, ROWS - NA_SUB_ROWS)):
        ks = int(np.clip(r0 - WIN_H // 2, 0, NA_MAX_KEY_START))
        for qi in range(NA_SUB_ROWS):
            r = r0 + qi
            r_start = int(np.clip(r - WIN_H // 2, 0, ROWS - WIN_H))
            for kj in range(NA_KEY_ROWS):
                if r_start <= ks + kj < r_start + WIN_H:
                    out[case, qi, kj] = ks + kj - r + (WIN_H - 1)
    return out


def _na_col_blocks(rpb):
    qc = np.arange(GRID_W)[:, None]
    kc = np.arange(GRID_W)[None, :]
    col_start = np.clip(qc - WIN_W // 2, 0, GRID_W - WIN_W)
    col_valid = (kc >= col_start) & (kc < col_start + WIN_W)
    dc = kc - qc + (WIN_W - 1)
    sel = (np.arange(2 * WIN_W - 1)[:, None, None] == dc[None]) & col_valid[None]
    sel = jnp.asarray(sel.astype(np.float32))
    t = jnp.sum(rpb.astype(F32)[:, :, :, None, None] * sel[None, None], axis=2)
    t = jnp.where(col_valid[None, None], t, NEG_BIG)
    masked = jnp.full((N_NA_HEADS, 1, GRID_W, GRID_W), NEG_BIG, F32)
    t = jnp.concatenate([t, masked], axis=1)
    return jnp.concatenate([t, t], axis=-1)


def _na_kernel(q_ref, k_ref, v_ref, qg_ref, kg_ref, cb_ref, o_ref, kn_ref, bias_ref):
    step = pl.program_id(1)
    nq = NA_SUB_ROWS * GRID_W
    nk = NA_KEY_ROWS * GRID_W

    @pl.when(step == 0)
    def _():
        chunk = 1024

        def body(c, carry):
            rows = pl.ds(pl.multiple_of(c * chunk, chunk), chunk)
            kn_ref[rows, :] = _rms_rows(k_ref[rows, :].astype(F32), kg_ref[...]).astype(BF16)
            return carry
        lax.fori_loop(0, SEQ // chunk, body, 0)

        row_off = _na_row_offsets()
        for case in range(3):
            for qi in range(NA_SUB_ROWS):
                for kj in range(NA_KEY_ROWS):
                    lanes = slice(kj * GRID_W, (kj + 1) * GRID_W)
                    src = slice((kj % 2) * GRID_W, (kj % 2 + 1) * GRID_W)
                    bias_ref[case, qi * GRID_W:(qi + 1) * GRID_W, lanes] = (
                        cb_ref[0, int(row_off[case, qi, kj]), :, src])

    scale = HEAD_DIM ** -0.5
    for sb in range(NA_SUBS_PER_STEP):
        sub = step * NA_SUBS_PER_STEP + sb
        key_row0 = jnp.clip(sub * NA_SUB_ROWS - WIN_H // 2, 0, NA_MAX_KEY_START)
        case = jnp.where(sub == 0, 0, jnp.where(sub == NA_N_SUB - 1, 2, 1))
        q = q_ref[sb * nq:(sb + 1) * nq, :].astype(F32)
        qn = (_rms_rows(q, qg_ref[...]) * scale).astype(BF16)
        kstart = pl.multiple_of(key_row0 * GRID_W, GRID_W)
        kw = kn_ref[pl.ds(kstart, nk), :]
        vw = v_ref[pl.ds(kstart, nk), :]
        s = lax.dot_general(qn, kw, (((1,), (1,)), ((), ())), preferred_element_type=F32)
        s = s + bias_ref[case]
        m = jnp.max(s, axis=-1, keepdims=True)
        p = jnp.exp(s - m)
        l = jnp.sum(p, axis=-1, keepdims=True)
        o = jnp.dot(p.astype(BF16), vw, preferred_element_type=F32) / l
        o_ref[sb * nq:(sb + 1) * nq, :] = o.astype(o_ref.dtype)


def _na_attention(proj, q_g, k_g, col_blocks):
    nq = NA_SUB_ROWS * GRID_W
    nk = NA_KEY_ROWS * GRID_W
    tq = NA_SUBS_PER_STEP * nq
    n_blk = NA_N_ROW_OFFSETS + 1
    qb, kb_, vb = Q_NA_OFF // HEAD_DIM, K_NA_OFF // HEAD_DIM, V_NA_OFF // HEAD_DIM
    return pl.pallas_call(
        _na_kernel,
        grid=(N_NA_HEADS, SEQ // tq),
        in_specs=[pl.BlockSpec((tq, HEAD_DIM), lambda h, i: (i, qb + h)),
                  pl.BlockSpec((SEQ, HEAD_DIM), lambda h, i: (0, kb_ + h)),
                  pl.BlockSpec((SEQ, HEAD_DIM), lambda h, i: (0, vb + h)),
                  pl.BlockSpec((1, HEAD_DIM), lambda h, i: (0, 0)),
                  pl.BlockSpec((1, HEAD_DIM), lambda h, i: (0, 0)),
                  pl.BlockSpec((1, n_blk, GRID_W, 2 * GRID_W), lambda h, i: (h, 0, 0, 0))],
        out_specs=pl.BlockSpec((tq, HEAD_DIM), lambda h, i: (i, h)),
        out_shape=jax.ShapeDtypeStruct((SEQ, NA_W), BF16),
        scratch_shapes=[pltpu.VMEM((SEQ, HEAD_DIM), BF16),
                        pltpu.VMEM((3, nq, nk), F32)],
        compiler_params=_params(("parallel", "arbitrary"), 48),
        name="na_attention",
    )(proj, proj, proj, q_g.reshape(1, HEAD_DIM), k_g.reshape(1, HEAD_DIM), col_blocks)


def _mem_attn_kernel(q_ref, kv_ref, qg_ref, kg_ref, o_ref):
    scale = HEAD_DIM ** -0.5
    for h in range(N_MEM_HEADS):
        cols = slice(h * HEAD_DIM, (h + 1) * HEAD_DIM)
        qn = (_rms_rows(q_ref[:, cols].astype(F32), qg_ref[...]) * scale).astype(BF16)
        kn = _rms_rows(kv_ref[:, cols].astype(F32), kg_ref[...]).astype(BF16)
        v = kv_ref[:, MEM_W + h * HEAD_DIM:MEM_W + (h + 1) * HEAD_DIM].astype(BF16)
        s = lax.dot_general(qn, kn, (((1,), (1,)), ((), ())), preferred_element_type=F32)
        m = jnp.max(s, axis=-1, keepdims=True)
        p = jnp.exp(s - m)
        l = jnp.sum(p, axis=-1, keepdims=True)
        o = jnp.dot(p.astype(BF16), v, preferred_element_type=F32) / l
        o_ref[:, cols] = o.astype(o_ref.dtype)


def _mem_attention(proj, kv, q_g, k_g):
    tm = 512
    return pl.pallas_call(
        _mem_attn_kernel,
        grid=(SEQ // tm,),
        in_specs=[pl.BlockSpec((tm, MEM_W), lambda i: (i, Q_MEM_OFF // MEM_W)),
                  pl.BlockSpec((N_MEM, 2 * MEM_W), lambda i: (0, 0)),
                  pl.BlockSpec((1, HEAD_DIM), lambda i: (0, 0)),
                  pl.BlockSpec((1, HEAD_DIM), lambda i: (0, 0))],
        out_specs=pl.BlockSpec((tm, MEM_W), lambda i: (i, 0)),
        out_shape=jax.ShapeDtypeStruct((SEQ, MEM_W), BF16),
        compiler_params=_params(("parallel",), 32),
        name="mem_attention",
    )(proj, kv, q_g.reshape(1, HEAD_DIM), k_g.reshape(1, HEAD_DIM))


def _merge_kernel(fm_ref, na_ref, mo_ref, g0_ref, g1_ref, g2_ref, wf_ref, wn_ref, wm_ref, o_ref,
                  *, col_chunk):
    for c in range(D_MODEL // col_chunk):
        cols = slice(c * col_chunk, (c + 1) * col_chunk)
        o_f = jnp.dot(fm_ref[...], wf_ref[:, cols], preferred_element_type=F32)
        o_n = jnp.dot(na_ref[...], wn_ref[:, cols], preferred_element_type=F32)
        o_m = jnp.dot(mo_ref[...], wm_ref[:, cols], preferred_element_type=F32)
        acc = jax.nn.sigmoid(g0_ref[:, cols].astype(F32)) * o_f
        acc = acc + jax.nn.sigmoid(g1_ref[:, cols].astype(F32)) * o_n
        acc = acc + jax.nn.sigmoid(g2_ref[:, cols].astype(F32)) * o_m
        o_ref[:, cols] = acc.astype(o_ref.dtype)


def _merge(fm, o_na, o_mem, proj, wf, wn, wm):
    tm = 512
    gb = GATE_OFF // D_MODEL
    return pl.pallas_call(
        functools.partial(_merge_kernel, col_chunk=512),
        grid=(SEQ // tm,),
        in_specs=[pl.BlockSpec((tm, FOURIER_W), lambda i: (i, 0)),
                  pl.BlockSpec((tm, NA_W), lambda i: (i, 0)),
                  pl.BlockSpec((tm, MEM_W), lambda i: (i, 0)),
                  pl.BlockSpec((tm, D_MODEL), lambda i: (i, gb)),
                  pl.BlockSpec((tm, D_MODEL), lambda i: (i, gb + 1)),
                  pl.BlockSpec((tm, D_MODEL), lambda i: (i, gb + 2)),
                  pl.BlockSpec((FOURIER_W, D_MODEL), lambda i: (0, 0)),
                  pl.BlockSpec((NA_W, D_MODEL), lambda i: (0, 0)),
                  pl.BlockSpec((MEM_W, D_MODEL), lambda i: (0, 0))],
        out_specs=pl.BlockSpec((tm, D_MODEL), lambda i: (i, 0)),
        out_shape=jax.ShapeDtypeStruct((SEQ, D_MODEL), BF16),
        compiler_params=_params(("parallel",), 48),
        name="gated_merge",
    )(fm, o_na, o_mem, proj, proj, proj, wf, wn, wm)


ROUTER_LANES = 128


def _route_top2(h, wr_ref):
    h_hi = h.astype(BF16)
    h_lo = (h - h_hi.astype(F32)).astype(BF16)
    r_hi = jnp.dot(h_hi, wr_ref[...], preferred_element_type=F32)
    r_lo = jnp.dot(h_lo, wr_ref[:, :ROUTER_LANES], preferred_element_type=F32)
    logits = r_hi[:, :ROUTER_LANES] + r_hi[:, ROUTER_LANES:] + r_lo
    lane = lax.broadcasted_iota(jnp.int32, logits.shape, 1).astype(F32)
    logits = jnp.where(lane < N_EXPERTS, logits, -jnp.inf)
    m1 = jnp.max(logits, axis=-1, keepdims=True)
    i1 = jnp.min(jnp.where(logits == m1, lane, float(ROUTER_LANES)), axis=-1, keepdims=True)
    rest = jnp.where(lane == i1, -jnp.inf, logits)
    m2 = jnp.max(rest, axis=-1, keepdims=True)
    i2 = jnp.min(jnp.where(rest == m2, lane, float(ROUTER_LANES)), axis=-1, keepdims=True)
    e21 = jnp.exp(m2 - m1)
    g1 = 1.0 / (1.0 + e21)
    g2 = e21 * g1
    return jnp.where(lane == 0.0, i1, i2).astype(jnp.int32), jnp.where(lane == 0.0, g1, g2)


def _oproj_kernel(a_ref, w_ref, x_ref, g_ref, *rest, row_chunk, route):
    if route:
        wr_ref, xo_ref, h_ref, idx_ref, gate_ref = rest
    else:
        xo_ref, h_ref = rest
    xo_ref[...] = jnp.dot(a_ref[...], w_ref[...], preferred_element_type=F32) + x_ref[...]

    def body(c, carry):
        rows = pl.ds(pl.multiple_of(c * row_chunk, row_chunk), row_chunk)
        h = _rms_rows(xo_ref[rows, :], g_ref[...])
        if route:
            h_ref[rows, :] = _pack_halves(h)
            idx_ref[rows, :], gate_ref[rows, :] = _route_top2(h, wr_ref)
        else:
            h_ref[rows, :] = h.astype(h_ref.dtype)
        return carry
    lax.fori_loop(0, xo_ref.shape[0] // row_chunk, body, 0)


def _out_proj(merged, w_o, x, g, w_router=None):
    tm = 512
    route = w_router is not None
    in_specs = [pl.BlockSpec((tm, D_MODEL), lambda i: (i, 0)),
                pl.BlockSpec((D_MODEL, D_MODEL), lambda i: (0, 0)),
                pl.BlockSpec((tm, D_MODEL), lambda i: (i, 0)),
                pl.BlockSpec((1, D_MODEL), lambda i: (0, 0))]
    args = [merged, w_o, x, g.reshape(1, D_MODEL)]
    out_specs = [pl.BlockSpec((tm, D_MODEL), lambda i: (i, 0))]
    out_shape = [jax.ShapeDtypeStruct((SEQ, D_MODEL), F32)]
    if route:
        w_pad = jnp.zeros((D_MODEL, ROUTER_LANES), F32).at[:, :N_EXPERTS].set(w_router.astype(F32))
        w_hi = w_pad.astype(BF16)
        w_lo = (w_pad - w_hi.astype(F32)).astype(BF16)
        in_specs.append(pl.BlockSpec((D_MODEL, 2 * ROUTER_LANES), lambda i: (0, 0)))
        args.append(jnp.concatenate([w_hi, w_lo], axis=1))
        out_specs += [pl.BlockSpec((tm, HALF_D), lambda i: (i, 0)),
                      pl.BlockSpec((tm, ROUTER_LANES), lambda i: (i, 0)),
                      pl.BlockSpec((tm, ROUTER_LANES), lambda i: (i, 0))]
        out_shape += [jax.ShapeDtypeStruct((SEQ, HALF_D), jnp.uint32),
                      jax.ShapeDtypeStruct((SEQ, ROUTER_LANES), jnp.int32),
                      jax.ShapeDtypeStruct((SEQ, ROUTER_LANES), F32)]
    else:
        out_specs.append(pl.BlockSpec((tm, D_MODEL), lambda i: (i, 0)))
        out_shape.append(jax.ShapeDtypeStruct((SEQ, D_MODEL), BF16))
    return pl.pallas_call(
        functools.partial(_oproj_kernel, row_chunk=128, route=route),
        grid=(SEQ // tm,),
        in_specs=in_specs,
        out_specs=out_specs,
        out_shape=out_shape,
        compiler_params=_params(("parallel",), 48),
        name="out_proj",
    )(*args)


def _swiglu_step(x_ref, wg_ref, wu_ref, wd_ref, acc_ref, rows=None):
    x = x_ref[:rows, :]
    g = jnp.dot(x, wg_ref[0].astype(BF16), preferred_element_type=F32)
    u = jnp.dot(x, wu_ref[0].astype(BF16), preferred_element_type=F32)
    a = (g * jax.nn.sigmoid(g) * u).astype(BF16)
    acc_ref[:rows, :] += jnp.dot(a, wd_ref[0].astype(BF16), preferred_element_type=F32)


def _dense_ffn_kernel(x_ref, wg_ref, wu_ref, wd_ref, resid_ref, o_ref, acc_ref, *, nf, row_chunk):
    f = pl.program_id(1)

    @pl.when(f == 0)
    def _():
        acc_ref[...] = jnp.zeros(acc_ref.shape, acc_ref.dtype)

    _swiglu_step(x_ref, wg_ref, wu_ref, wd_ref, acc_ref)

    @pl.when(f == nf - 1)
    def _():
        def finish_rows(c, carry):
            rows = pl.ds(pl.multiple_of(c * row_chunk, row_chunk), row_chunk)
            o_ref[rows, :] = acc_ref[rows, :] + resid_ref[rows, :]
            return carry
        lax.fori_loop(0, acc_ref.shape[0] // row_chunk, finish_rows, 0)


def _dense_ffn(h, wg, wu, wd, layer, resid):
    tm, tf = 512, 512
    m, d = h.shape
    nf = D_FF // tf
    return pl.pallas_call(
        functools.partial(_dense_ffn_kernel, nf=nf, row_chunk=64),
        grid=(m // tm, nf),
        in_specs=[pl.BlockSpec((tm, d), lambda i, f: (i, 0)),
                  pl.BlockSpec((1, d, tf), lambda i, f: (layer, 0, f)),
                  pl.BlockSpec((1, d, tf), lambda i, f: (layer, 0, f)),
                  pl.BlockSpec((1, tf, d), lambda i, f: (layer, f, 0)),
                  pl.BlockSpec((tm, d), lambda i, f: (i, 0))],
        out_specs=pl.BlockSpec((tm, d), lambda i, f: (i, 0)),
        out_shape=jax.ShapeDtypeStruct((m, d), F32),
        scratch_shapes=[pltpu.VMEM((tm, d), F32)],
        compiler_params=_params(("parallel", "arbitrary"), 48),
        name="dense_ffn",
    )(h, wg, wu, wd, resid)


MOE_TF = 256
MOE_NF = D_FF // MOE_TF
MOE_ROWS_PER_STEP = 48
MOE_TM = MOE_NF * MOE_ROWS_PER_STEP
MOE_ROW_CHUNK = 96
MOE_N_TILES = -(-SEQ * TOP_K // MOE_TM) + N_EXPERTS
MOE_THIRDS = 3
MOE_THIRD_ROWS = MOE_TM // MOE_THIRDS


def _tile_row_copy(tok_ref, h_ref, xg_ref, sems, tile, slot, r):
    row = tok_ref[tile * MOE_TM + r]
    return pltpu.make_async_copy(h_ref.at[pl.ds(row, 1), :], xg_ref.at[slot, pl.ds(r, 1), :],
                                 sems.at[slot])


def _wait_tile_rows(h_ref, xg_ref, sems, slot):
    pltpu.make_async_copy(h_ref.at[pl.ds(0, MOE_TM), :], xg_ref.at[slot], sems.at[slot]).wait()


def _moe_expert_kernel(be_ref, nu_ref, nt_ref, tok_ref, h_ref, wg_ref, wu_ref, wd_ref, o_ref,
                       xg_ref, xb_ref, acc_ref, sems):
    i = pl.program_id(0)
    f = pl.program_id(1)
    n_used = nu_ref[0]
    active = i < n_used
    slot = i % 2
    n_chunks = MOE_TM // MOE_ROW_CHUNK

    @pl.when(jnp.logical_and(i == 0, f == 0))
    def _():
        def start_row(r, carry):
            _tile_row_copy(tok_ref, h_ref, xg_ref, sems, 0, 0, r).start()
            return carry
        lax.fori_loop(0, MOE_TM, start_row, 0, unroll=8)

    @pl.when(jnp.logical_and(active, f == 0))
    def _():
        _wait_tile_rows(h_ref, xg_ref, sems, slot)
        acc_ref[...] = jnp.zeros(acc_ref.shape, acc_ref.dtype)

        def unpack_rows(c, carry):
            rows = pl.ds(pl.multiple_of(c * MOE_ROW_CHUNK, MOE_ROW_CHUNK), MOE_ROW_CHUNK)
            lo, hi = _unpack_halves(xg_ref[slot, rows, :])
            xb_ref[rows, :HALF_D] = lo.astype(BF16)
            xb_ref[rows, HALF_D:] = hi.astype(BF16)
            return carry
        lax.fori_loop(0, n_chunks, unpack_rows, 0)

    for thirds in range(1, MOE_THIRDS + 1):
        @pl.when(jnp.logical_and(active, nt_ref[i] == thirds))
        def _(thirds=thirds):
            for j in range(MOE_ROWS_PER_STEP):
                _tile_row_copy(tok_ref, h_ref, xg_ref, sems, i + 1, 1 - slot,
                               f * MOE_ROWS_PER_STEP + j).start()
            _swiglu_step(xb_ref, wg_ref, wu_ref, wd_ref, acc_ref, rows=thirds * MOE_THIRD_ROWS)

    @pl.when(jnp.logical_and(i == n_used, f == 0))
    def _():
        _wait_tile_rows(h_ref, xg_ref, sems, slot)

    @pl.when(f == MOE_NF - 1)
    def _():
        @pl.when(active)
        def _():
            def finish_rows(c, carry):
                rows = pl.ds(pl.multiple_of(c * MOE_ROW_CHUNK, MOE_ROW_CHUNK), MOE_ROW_CHUNK)
                o_ref[rows, :] = _pack_halves(acc_ref[rows, :])
                return carry
            lax.fori_loop(0, n_chunks, finish_rows, 0)

        @pl.when(jnp.logical_not(active))
        def _():
            o_ref[...] = jnp.zeros(o_ref.shape, o_ref.dtype)


def _moe_experts(h_packed, tok, blk_e, n_used, n_thirds, wg, wu, wd):
    d, tf = D_MODEL, MOE_TF

    def f_idx(i, f, nu):
        return jnp.where(i < nu[0], f, MOE_NF - 1)

    def w_gu(i, f, be, nu, nt, tok):
        return (be[i], 0, f_idx(i, f, nu))

    def w_d(i, f, be, nu, nt, tok):
        return (be[i], f_idx(i, f, nu), 0)

    grid_spec = pltpu.PrefetchScalarGridSpec(
        num_scalar_prefetch=4,
        grid=(MOE_N_TILES, MOE_NF),
        in_specs=[pl.BlockSpec(memory_space=pl.ANY),
                  pl.BlockSpec((1, d, tf), w_gu),
                  pl.BlockSpec((1, d, tf), w_gu),
                  pl.BlockSpec((1, tf, d), w_d)],
        out_specs=pl.BlockSpec((MOE_TM, HALF_D), lambda i, f, be, nu, nt, tok: (i, 0)),
        scratch_shapes=[pltpu.VMEM((2, MOE_TM, HALF_D), jnp.uint32),
                        pltpu.VMEM((MOE_TM, d), BF16),
                        pltpu.VMEM((MOE_TM, d), F32),
                        pltpu.SemaphoreType.DMA((2,))],
    )
    return pl.pallas_call(
        _moe_expert_kernel,
        grid_spec=grid_spec,
        out_shape=jax.ShapeDtypeStruct((MOE_N_TILES * MOE_TM, HALF_D), jnp.uint32),
        compiler_params=_params(("arbitrary", "arbitrary"), 56),
        name="moe_experts",
    )(blk_e, n_used, n_thirds, tok, h_packed, wg, wu, wd)


def _start_row_gather(idx_ref, idx_base, src_ref, dst_ref, sem, n_rows):
    def body(r, carry):
        row = idx_ref[idx_base + r]
        pltpu.make_async_copy(src_ref.at[pl.ds(row, 1), :], dst_ref.at[pl.ds(r, 1), :], sem).start()
        return carry
    lax.fori_loop(0, n_rows, body, 0, unroll=8)


def _wait_row_gather(src_ref, dst_ref, sem, n_rows):
    pltpu.make_async_copy(src_ref.at[pl.ds(0, n_rows), :], dst_ref, sem).wait()


def _moe_combine_kernel(pos_ref, y_ref, x_ref, gate_ref, o_ref, buf_ref, sems, *, tokens):
    i = pl.program_id(0)
    slot = i % 2
    rows = TOP_K * tokens

    @pl.when(i == 0)
    def _():
        _start_row_gather(pos_ref, 0, y_ref, buf_ref.at[0], sems.at[0], rows)

    @pl.when(i + 1 < pl.num_programs(0))
    def _():
        _start_row_gather(pos_ref, (i + 1) * rows, y_ref, buf_ref.at[1 - slot],
                          sems.at[1 - slot], rows)

    _wait_row_gather(y_ref, buf_ref.at[slot], sems.at[slot], rows)
    g0 = gate_ref[:, 0:1]
    g1 = gate_ref[:, 1:2]
    lo0, hi0 = _unpack_halves(buf_ref[slot, :tokens, :])
    lo1, hi1 = _unpack_halves(buf_ref[slot, tokens:, :])
    o_ref[:, :HALF_D] = x_ref[:, :HALF_D] + (g0 * lo0 + g1 * lo1)
    o_ref[:, HALF_D:] = x_ref[:, HALF_D:] + (g0 * hi0 + g1 * hi1)


def _moe_combine(y_packed, pos, x, gate):
    tokens = 512
    n = x.shape[0]
    steps = n // tokens
    pos_tiled = pos.reshape(steps, tokens, TOP_K).transpose(0, 2, 1).reshape(-1)
    grid_spec = pltpu.PrefetchScalarGridSpec(
        num_scalar_prefetch=1,
        grid=(steps,),
        in_specs=[pl.BlockSpec(memory_space=pl.ANY),
                  pl.BlockSpec((tokens, D_MODEL), lambda i, pos_ref: (i, 0)),
                  pl.BlockSpec((tokens, ROUTER_LANES), lambda i, pos_ref: (i, 0))],
        out_specs=pl.BlockSpec((tokens, D_MODEL), lambda i, pos_ref: (i, 0)),
        scratch_shapes=[pltpu.VMEM((2, TOP_K * tokens, HALF_D), jnp.uint32),
                        pltpu.SemaphoreType.DMA((2,))],
    )
    return pl.pallas_call(
        functools.partial(_moe_combine_kernel, tokens=tokens),
        grid_spec=grid_spec,
        out_shape=jax.ShapeDtypeStruct((n, D_MODEL), F32),
        compiler_params=_params(("arbitrary",), 40),
        name="moe_combine",
    )(pos_tiled, y_packed, x, gate)


def _moe(x_new, h_packed, idx, gate, wg, wu, wd):
    n = SEQ
    n_assign = n * TOP_K
    cap = MOE_N_TILES * MOE_TM
    e_flat = idx[:, :TOP_K].reshape(-1)
    onehot = (e_flat[:, None] == jnp.arange(N_EXPERTS, dtype=jnp.int32)[None, :]).astype(jnp.int32)
    csum = jnp.cumsum(onehot, axis=0)
    counts = csum[-1]
    rank = jnp.sum((csum - onehot) * onehot, axis=1)
    padded = (counts + MOE_TM - 1) // MOE_TM * MOE_TM
    pad_ends = jnp.cumsum(padded)
    pad_starts = pad_ends - padded
    dest = jnp.sum(onehot * pad_starts[None, :], axis=1) + rank
    tok = jnp.zeros((cap,), jnp.int32).at[dest].set(
        jnp.arange(n_assign, dtype=jnp.int32) // TOP_K)
    n_used = (pad_ends[-1] // MOE_TM).astype(jnp.int32).reshape(1)
    tile_start = jnp.arange(MOE_N_TILES, dtype=jnp.int32) * MOE_TM
    tile_start = jnp.minimum(tile_start, pad_ends[-1] - 1)
    blk_e = jnp.sum((tile_start[:, None] >= pad_ends[None, :]).astype(jnp.int32), axis=1)
    blk_e = jnp.clip(blk_e, 0, N_EXPERTS - 1)
    real_rows = counts[blk_e] - (tile_start - pad_starts[blk_e])
    n_thirds = jnp.clip((real_rows + MOE_THIRD_ROWS - 1) // MOE_THIRD_ROWS, 1, MOE_THIRDS)
    yb = _moe_experts(h_packed, tok, blk_e, n_used, n_thirds.astype(jnp.int32), wg, wu, wd)
    return _moe_combine(yb, dest.reshape(n, TOP_K), x_new, gate)


def kernel(x, mem, ln_mix_g, w_in, na_q_g, na_k_g, na_rpb, mem_ln_g, w_mem_kv, mem_q_g, mem_k_g,
           w_fourier_out, w_na_out, w_mem_out, w_o, ln_ffn_g, ffn_w_gate, ffn_w_up, ffn_w_down,
           moe_router, moe_w_gate, moe_w_up, moe_w_down):
    assert x.shape == (1, SEQ, D_MODEL) and mem.shape == (1, N_MEM, D_MODEL)
    xs = x.reshape(SEQ, D_MODEL)
    mem2 = mem.reshape(N_MEM, D_MODEL)
    tables = _dft_tables()
    for l in range(DEPTH):
        proj = _norm_mm(xs, ln_mix_g[l], w_in, l, tm=1024, tn=1024, out_dtype=BF16,
                        vmem_mib=56, name="in_proj")
        fm = _fourier_mix(proj[:, :FOURIER_W], tables)
        o_na = _na_attention(proj, na_q_g[l], na_k_g[l], _na_col_blocks(na_rpb[l]))
        kv = _norm_mm(mem2, mem_ln_g[l], w_mem_kv, l, tm=N_MEM, tn=2 * MEM_W, out_dtype=F32,
                      vmem_mib=40, name="mem_kv_proj")
        o_mem = _mem_attention(proj, kv, mem_q_g[l], mem_k_g[l])
        merged = _merge(fm, o_na, o_mem, proj, w_fourier_out[l].astype(BF16),
                        w_na_out[l].astype(BF16), w_mem_out[l].astype(BF16))
        i = l // 2
        if l % 2 == 0:
            x_new, h2 = _out_proj(merged, w_o[l].astype(BF16), xs, ln_ffn_g[l])
            xs = _dense_ffn(h2, ffn_w_gate.astype(BF16), ffn_w_up.astype(BF16),
                            ffn_w_down.astype(BF16), i, x_new)
        else:
            x_new, h2, idx, gate = _out_proj(merged, w_o[l].astype(BF16), xs, ln_ffn_g[l],
                                             moe_router[i])
            xs = _moe(x_new, h2, idx, gate, moe_w_gate[i], moe_w_up[i], moe_w_down[i])
    return xs.reshape(1, SEQ, D_MODEL)
```

```python
import functools
import math

import numpy as np
import jax
import jax.numpy as jnp
from jax import lax
from jax.experimental import pallas as pl
from jax.experimental.pallas import tpu as pltpu

F32 = jnp.float32
BF16 = jnp.bfloat16

D_MODEL = 2048
SEQ = 16384
DEPTH = 2
GRID_W = 64
ROWS = SEQ // GRID_W
HEAD_DIM = 128
N_FOURIER_GROUPS = 4
FOURIER_W = N_FOURIER_GROUPS * HEAD_DIM
N_NA_HEADS = 8
NA_W = N_NA_HEADS * HEAD_DIM
WIN_H = 8
WIN_W = 16
N_MEM = 256
N_MEM_HEADS = 4
MEM_W = N_MEM_HEADS * HEAD_DIM
IN_W = FOURIER_W + 3 * NA_W + MEM_W + 3 * D_MODEL
D_FF = 5632
N_EXPERTS = 8
TOP_K = 2
EPS = 1e-6
HALF_D = D_MODEL // 2

Q_NA_OFF = FOURIER_W
K_NA_OFF = Q_NA_OFF + NA_W
V_NA_OFF = K_NA_OFF + NA_W
Q_MEM_OFF = V_NA_OFF + NA_W
GATE_OFF = Q_MEM_OFF + MEM_W

MIB = 1024 * 1024
NEG_BIG = -1e30

FFT_A = 128
FFT_B = SEQ // FFT_A


def _params(semantics, vmem_mib):
    return pltpu.CompilerParams(dimension_semantics=semantics,
                                vmem_limit_bytes=int(vmem_mib * MIB))


def _rms_rows(x, gain):
    ms = jnp.mean(x * x, axis=-1, keepdims=True)
    return x * lax.rsqrt(ms + EPS) * gain


def _bf16_bits_rtne(v):
    bits = pltpu.bitcast(v, jnp.uint32)
    lsb = lax.shift_right_logical(bits, jnp.uint32(16)) & jnp.uint32(1)
    return bits + jnp.uint32(0x7FFF) + lsb


def _pack_halves(v):
    half = v.shape[1] // 2
    lo = lax.shift_right_logical(_bf16_bits_rtne(v[:, :half]), jnp.uint32(16))
    hi = _bf16_bits_rtne(v[:, half:]) & jnp.uint32(0xFFFF0000)
    return lo | hi


def _unpack_halves(w):
    lo = pltpu.bitcast(lax.shift_left(w, jnp.uint32(16)), F32)
    hi = pltpu.bitcast(w & jnp.uint32(0xFFFF0000), F32)
    return lo, hi


def _norm_mm_kernel(x_ref, g_ref, w_ref, o_ref, h_ref, *, row_chunk):
    @pl.when(pl.program_id(1) == 0)
    def _():
        def body(c, carry):
            rows = pl.ds(pl.multiple_of(c * row_chunk, row_chunk), row_chunk)
            h_ref[rows, :] = _rms_rows(x_ref[rows, :].astype(F32), g_ref[...]).astype(BF16)
            return carry
        lax.fori_loop(0, h_ref.shape[0] // row_chunk, body, 0)

    o_ref[...] = jnp.dot(h_ref[...], w_ref[0].astype(BF16),
                         preferred_element_type=F32).astype(o_ref.dtype)


def _norm_mm(x, g, w, layer, *, tm, tn, out_dtype, vmem_mib, name):
    m, k = x.shape
    n = w.shape[2]
    row_chunk = min(tm, 128)
    return pl.pallas_call(
        functools.partial(_norm_mm_kernel, row_chunk=row_chunk),
        grid=(m // tm, n // tn),
        in_specs=[pl.BlockSpec((tm, k), lambda i, j: (i, 0)),
                  pl.BlockSpec((1, k), lambda i, j: (0, 0)),
                  pl.BlockSpec((1, k, tn), lambda i, j: (layer, 0, j))],
        out_specs=pl.BlockSpec((tm, tn), lambda i, j: (i, j)),
        out_shape=jax.ShapeDtypeStruct((m, n), out_dtype),
        scratch_shapes=[pltpu.VMEM((tm, k), BF16)],
        compiler_params=_params(("parallel", "arbitrary"), vmem_mib),
        name=name,
    )(x, g.reshape(1, k), w)


def _dft_tables():
    a = np.arange(FFT_A)
    ang1 = 2.0 * np.pi * np.outer(a, a) / FFT_A
    w1 = np.concatenate([np.cos(ang1), -np.sin(ang1)], axis=0)
    k1 = np.arange(FFT_A)[:, None, None]
    k2 = np.arange(FFT_B)[None, :, None]
    s2 = np.arange(FFT_B)[None, None, :]
    ang2 = 2.0 * np.pi * ((s2 * (k1 + FFT_A * k2)) % SEQ) / SEQ
    g2 = np.concatenate([np.cos(ang2), np.sin(ang2)], axis=2)
    c = np.arange(HEAD_DIM)
    angc = 2.0 * np.pi * np.outer(c, c) / HEAD_DIM
    eye = np.eye(N_FOURIER_GROUPS)
    bd = np.concatenate([np.kron(eye, np.cos(angc)), np.kron(eye, np.sin(angc))], axis=0)
    return (jnp.asarray(w1, dtype=BF16), jnp.asarray(g2, dtype=BF16),
            jnp.asarray(bd, dtype=BF16))


FFT_BLK = 16


def _dft1_kernel(w_ref, x_ref, br_ref, bi_ref):
    xt = pltpu.einshape("abc->bac", x_ref[...])
    for j in range(FFT_BLK):
        r = jnp.dot(w_ref[...], xt[j], preferred_element_type=F32)
        br_ref[j] = r[:FFT_A].astype(BF16)
        bi_ref[j] = r[FFT_A:].astype(BF16)


def _dft2_kernel(br_ref, bi_ref, g_ref, bd_ref, o_ref, *, norm):
    brt = pltpu.einshape("abc->bac", br_ref[...])
    bit = pltpu.einshape("abc->bac", bi_ref[...])
    ys = []
    for i in range(FFT_BLK):
        br = brt[i]
        bi = bit[i]
        g = g_ref[i]
        zr = jnp.dot(g, jnp.concatenate([br, bi], axis=0), preferred_element_type=F32)
        zi = jnp.dot(g, jnp.concatenate([bi, -br], axis=0), preferred_element_type=F32)
        z = jnp.concatenate([zr, zi], axis=1).astype(BF16)
        y = jnp.dot(z, bd_ref[...], preferred_element_type=F32) * norm
        ys.append(y.astype(o_ref.dtype))
    o_ref[...] = pltpu.einshape("abc->bac", jnp.stack(ys, axis=0))


def _fourier_mix(proj, tables):
    w1, g2, bd = tables
    x3 = proj.reshape(FFT_A, FFT_B, IN_W)
    blk = (FFT_A, FFT_BLK, FOURIER_W)
    br, bi = pl.pallas_call(
        _dft1_kernel,
        grid=(FFT_B // FFT_BLK,),
        in_specs=[pl.BlockSpec((2 * FFT_A, FFT_A), lambda j: (0, 0)),
                  pl.BlockSpec(blk, lambda j: (0, j, 0))],
        out_specs=[pl.BlockSpec((FFT_BLK, FFT_A, FOURIER_W), lambda j: (j, 0, 0)),
                   pl.BlockSpec((FFT_BLK, FFT_A, FOURIER_W), lambda j: (j, 0, 0))],
        out_shape=[jax.ShapeDtypeStruct((FFT_B, FFT_A, FOURIER_W), BF16)] * 2,
        compiler_params=_params(("parallel",), 40),
        name="fourier_stage1",
    )(w1, x3)
    norm = 1.0 / math.sqrt(SEQ * HEAD_DIM)
    y3 = pl.pallas_call(
        functools.partial(_dft2_kernel, norm=norm),
        grid=(FFT_A // FFT_BLK,),
        in_specs=[pl.BlockSpec((FFT_B, FFT_BLK, FOURIER_W), lambda j: (0, j, 0)),
                  pl.BlockSpec((FFT_B, FFT_BLK, FOURIER_W), lambda j: (0, j, 0)),
                  pl.BlockSpec((FFT_BLK, FFT_B, 2 * FFT_B), lambda j: (j, 0, 0)),
                  pl.BlockSpec((2 * FOURIER_W, FOURIER_W), lambda j: (0, 0))],
        out_specs=pl.BlockSpec((FFT_B, FFT_BLK, FOURIER_W), lambda j: (0, j, 0)),
        out_shape=jax.ShapeDtypeStruct((FFT_B, FFT_A, FOURIER_W), BF16),
        compiler_params=_params(("parallel",), 40),
        name="fourier_stage2",
    )(br, bi, g2, bd)
    return y3.reshape(SEQ, FOURIER_W)


NA_SUB_ROWS = 4
NA_KEY_ROWS = NA_SUB_ROWS + WIN_H
NA_SUBS_PER_STEP = 4
NA_N_SUB = ROWS // NA_SUB_ROWS
NA_MAX_KEY_START = ROWS - NA_KEY_ROWS
NA_N_ROW_OFFSETS = 2 * WIN_H - 1
NA_MASKED_BLOCK = NA_N_ROW_OFFSETS


def _na_row_offsets():
    out = np.full((3, NA_SUB_ROWS, NA_KEY_ROWS), NA_MASKED_BLOCK, np.int32)
    for case, r0 in enumerate((0, 2 * NA_SUB_ROWS, ROWS - NA_SUB_ROWS)):
        ks = int(np.clip(r0 - WIN_H // 2, 0, NA_MAX_KEY_START))
        for qi in range(NA_SUB_ROWS):
            r = r0 + qi
            r_start = int(np.clip(r - WIN_H // 2, 0, ROWS - WIN_H))
            for kj in range(NA_KEY_ROWS):
                if r_start <= ks + kj < r_start + WIN_H:
                    out[case, qi, kj] = ks + kj - r + (WIN_H - 1)
    return out


def _na_col_blocks(rpb):
    qc = np.arange(GRID_W)[:, None]
    kc = np.arange(GRID_W)[None, :]
    col_start = np.clip(qc - WIN_W // 2, 0, GRID_W - WIN_W)
    col_valid = (kc >= col_start) & (kc < col_start + WIN_W)
    dc = kc - qc + (WIN_W - 1)
    sel = (np.arange(2 * WIN_W - 1)[:, None, None] == dc[None]) & col_valid[None]
    sel = jnp.asarray(sel.astype(np.float32))
    t = jnp.sum(rpb.astype(F32)[:, :, :, None, None] * sel[None, None], axis=2)
    t = jnp.where(col_valid[None, None], t, NEG_BIG)
    masked = jnp.full((N_NA_HEADS, 1, GRID_W, GRID_W), NEG_BIG, F32)
    t = jnp.concatenate([t, masked], axis=1)
    return jnp.concatenate([t, t], axis=-1)


def _na_kernel(q_ref, k_ref, v_ref, qg_ref, kg_ref, cb_ref, o_ref, kn_ref, bias_ref):
    step = pl.program_id(1)
    nq = NA_SUB_ROWS * GRID_W
    nk = NA_KEY_ROWS * GRID_W

    @pl.when(step == 0)
    def _():
        chunk = 1024

        def body(c, carry):
            rows = pl.ds(pl.multiple_of(c * chunk, chunk), chunk)
            kn_ref[rows, :] = _rms_rows(k_ref[rows, :].astype(F32), kg_ref[...]).astype(BF16)
            return carry
        lax.fori_loop(0, SEQ // chunk, body, 0)

        row_off = _na_row_offsets()
        for case in range(3):
            for qi in range(NA_SUB_ROWS):
                for kj in range(NA_KEY_ROWS):
                    lanes = slice(kj * GRID_W, (kj + 1) * GRID_W)
                    src = slice((kj % 2) * GRID_W, (kj % 2 + 1) * GRID_W)
                    bias_ref[case, qi * GRID_W:(qi + 1) * GRID_W, lanes] = (
                        cb_ref[0, int(row_off[case, qi, kj]), :, src])

    scale = HEAD_DIM ** -0.5
    for sb in range(NA_SUBS_PER_STEP):
        sub = step * NA_SUBS_PER_STEP + sb
        key_row0 = jnp.clip(sub * NA_SUB_ROWS - WIN_H // 2, 0, NA_MAX_KEY_START)
        case = jnp.where(sub == 0, 0, jnp.where(sub == NA_N_SUB - 1, 2, 1))
        q = q_ref[sb * nq:(sb + 1) * nq, :].astype(F32)
        qn = (_rms_rows(q, qg_ref[...]) * scale).astype(BF16)
        kstart = pl.multiple_of(key_row0 * GRID_W, GRID_W)
        kw = kn_ref[pl.ds(kstart, nk), :]
        vw = v_ref[pl.ds(kstart, nk), :]
        s = lax.dot_general(qn, kw, (((1,), (1,)), ((), ())), preferred_element_type=F32)
        s = s + bias_ref[case]
        m = jnp.max(s, axis=-1, keepdims=True)
        p = jnp.exp(s - m)
        l = jnp.sum(p, axis=-1, keepdims=True)
        o = jnp.dot(p.astype(BF16), vw, preferred_element_type=F32) / l
        o_ref[sb * nq:(sb + 1) * nq, :] = o.astype(o_ref.dtype)


def _na_attention(proj, q_g, k_g, col_blocks):
    nq = NA_SUB_ROWS * GRID_W
    nk = NA_KEY_ROWS * GRID_W
    tq = NA_SUBS_PER_STEP * nq
    n_blk = NA_N_ROW_OFFSETS + 1
    qb, kb_, vb = Q_NA_OFF // HEAD_DIM, K_NA_OFF // HEAD_DIM, V_NA_OFF // HEAD_DIM
    return pl.pallas_call(
        _na_kernel,
        grid=(N_NA_HEADS, SEQ // tq),
        in_specs=[pl.BlockSpec((tq, HEAD_DIM), lambda h, i: (i, qb + h)),
                  pl.BlockSpec((SEQ, HEAD_DIM), lambda h, i: (0, kb_ + h)),
                  pl.BlockSpec((SEQ, HEAD_DIM), lambda h, i: (0, vb + h)),
                  pl.BlockSpec((1, HEAD_DIM), lambda h, i: (0, 0)),
                  pl.BlockSpec((1, HEAD_DIM), lambda h, i: (0, 0)),
                  pl.BlockSpec((1, n_blk, GRID_W, 2 * GRID_W), lambda h, i: (h, 0, 0, 0))],
        out_specs=pl.BlockSpec((tq, HEAD_DIM), lambda h, i: (i, h)),
        out_shape=jax.ShapeDtypeStruct((SEQ, NA_W), BF16),
        scratch_shapes=[pltpu.VMEM((SEQ, HEAD_DIM), BF16),
                        pltpu.VMEM((3, nq, nk), F32)],
        compiler_params=_params(("parallel", "arbitrary"), 48),
        name="na_attention",
    )(proj, proj, proj, q_g.reshape(1, HEAD_DIM), k_g.reshape(1, HEAD_DIM), col_blocks)


def _mem_attn_kernel(q_ref, kv_ref, qg_ref, kg_ref, o_ref):
    scale = HEAD_DIM ** -0.5
    for h in range(N_MEM_HEADS):
        cols = slice(h * HEAD_DIM, (h + 1) * HEAD_DIM)
        qn = (_rms_rows(q_ref[:, cols].astype(F32), qg_ref[...]) * scale).astype(BF16)
        kn = _rms_rows(kv_ref[:, cols].astype(F32), kg_ref[...]).astype(BF16)
        v = kv_ref[:, MEM_W + h * HEAD_DIM:MEM_W + (h + 1) * HEAD_DIM].astype(BF16)
        s = lax.dot_general(qn, kn, (((1,), (1,)), ((), ())), preferred_element_type=F32)
        m = jnp.max(s, axis=-1, keepdims=True)
        p = jnp.exp(s - m)
        l = jnp.sum(p, axis=-1, keepdims=True)
        o = jnp.dot(p.astype(BF16), v, preferred_element_type=F32) / l
        o_ref[:, cols] = o.astype(o_ref.dtype)


def _mem_attention(proj, kv, q_g, k_g):
    tm = 512
    return pl.pallas_call(
        _mem_attn_kernel,
        grid=(SEQ // tm,),
        in_specs=[pl.BlockSpec((tm, MEM_W), lambda i: (i, Q_MEM_OFF // MEM_W)),
                  pl.BlockSpec((N_MEM, 2 * MEM_W), lambda i: (0, 0)),
                  pl.BlockSpec((1, HEAD_DIM), lambda i: (0, 0)),
                  pl.BlockSpec((1, HEAD_DIM), lambda i: (0, 0))],
        out_specs=pl.BlockSpec((tm, MEM_W), lambda i: (i, 0)),
        out_shape=jax.ShapeDtypeStruct((SEQ, MEM_W), BF16),
        compiler_params=_params(("parallel",), 32),
        name="mem_attention",
    )(proj, kv, q_g.reshape(1, HEAD_DIM), k_g.reshape(1, HEAD_DIM))


def _merge_kernel(fm_ref, na_ref, mo_ref, g0_ref, g1_ref, g2_ref, wf_ref, wn_ref, wm_ref, o_ref,
                  *, col_chunk):
    for c in range(D_MODEL // col_chunk):
        cols = slice(c * col_chunk, (c + 1) * col_chunk)
        o_f = jnp.dot(fm_ref[...], wf_ref[:, cols], preferred_element_type=F32)
        o_n = jnp.dot(na_ref[...], wn_ref[:, cols], preferred_element_type=F32)
        o_m = jnp.dot(mo_ref[...], wm_ref[:, cols], preferred_element_type=F32)
        acc = jax.nn.sigmoid(g0_ref[:, cols].astype(F32)) * o_f
        acc = acc + jax.nn.sigmoid(g1_ref[:, cols].astype(F32)) * o_n
        acc = acc + jax.nn.sigmoid(g2_ref[:, cols].astype(F32)) * o_m
        o_ref[:, cols] = acc.astype(o_ref.dtype)


def _merge(fm, o_na, o_mem, proj, wf, wn, wm):
    tm = 512
    gb = GATE_OFF // D_MODEL
    return pl.pallas_call(
        functools.partial(_merge_kernel, col_chunk=512),
        grid=(SEQ // tm,),
        in_specs=[pl.BlockSpec((tm, FOURIER_W), lambda i: (i, 0)),
                  pl.BlockSpec((tm, NA_W), lambda i: (i, 0)),
                  pl.BlockSpec((tm, MEM_W), lambda i: (i, 0)),
                  pl.BlockSpec((tm, D_MODEL), lambda i: (i, gb)),
                  pl.BlockSpec((tm, D_MODEL), lambda i: (i, gb + 1)),
                  pl.BlockSpec((tm, D_MODEL), lambda i: (i, gb + 2)),
                  pl.BlockSpec((FOURIER_W, D_MODEL), lambda i: (0, 0)),
                  pl.BlockSpec((NA_W, D_MODEL), lambda i: (0, 0)),
                  pl.BlockSpec((MEM_W, D_MODEL), lambda i: (0, 0))],
        out_specs=pl.BlockSpec((tm, D_MODEL), lambda i: (i, 0)),
        out_shape=jax.ShapeDtypeStruct((SEQ, D_MODEL), BF16),
        compiler_params=_params(("parallel",), 48),
        name="gated_merge",
    )(fm, o_na, o_mem, proj, proj, proj, wf, wn, wm)


ROUTER_LANES = 128


def _route_top2(h, wr_ref):
    h_hi = h.astype(BF16)
    h_lo = (h - h_hi.astype(F32)).astype(BF16)
    r_hi = jnp.dot(h_hi, wr_ref[...], preferred_element_type=F32)
    r_lo = jnp.dot(h_lo, wr_ref[:, :ROUTER_LANES], preferred_element_type=F32)
    logits = r_hi[:, :ROUTER_LANES] + r_hi[:, ROUTER_LANES:] + r_lo
    lane = lax.broadcasted_iota(jnp.int32, logits.shape, 1).astype(F32)
    logits = jnp.where(lane < N_EXPERTS, logits, -jnp.inf)
    m1 = jnp.max(logits, axis=-1, keepdims=True)
    i1 = jnp.min(jnp.where(logits == m1, lane, float(ROUTER_LANES)), axis=-1, keepdims=True)
    rest = jnp.where(lane == i1, -jnp.inf, logits)
    m2 = jnp.max(rest, axis=-1, keepdims=True)
    i2 = jnp.min(jnp.where(rest == m2, lane, float(ROUTER_LANES)), axis=-1, keepdims=True)
    e21 = jnp.exp(m2 - m1)
    g1 = 1.0 / (1.0 + e21)
    g2 = e21 * g1
    return jnp.where(lane == 0.0, i1, i2).astype(jnp.int32), jnp.where(lane == 0.0, g1, g2)


def _oproj_kernel(a_ref, w_ref, x_ref, g_ref, *rest, row_chunk, route):
    if route:
        wr_ref, xo_ref, h_ref, idx_ref, gate_ref = rest
    else:
        xo_ref, h_ref = rest
    xo_ref[...] = jnp.dot(a_ref[...], w_ref[...], preferred_element_type=F32) + x_ref[...]

    def body(c, carry):
        rows = pl.ds(pl.multiple_of(c * row_chunk, row_chunk), row_chunk)
        h = _rms_rows(xo_ref[rows, :], g_ref[...])
        if route:
            h_ref[rows, :] = _pack_halves(h)
            idx_ref[rows, :], gate_ref[rows, :] = _route_top2(h, wr_ref)
        else:
            h_ref[rows, :] = h.astype(h_ref.dtype)
        return carry
    lax.fori_loop(0, xo_ref.shape[0] // row_chunk, body, 0)


def _out_proj(merged, w_o, x, g, w_router=None):
    tm = 512
    route = w_router is not None
    in_specs = [pl.BlockSpec((tm, D_MODEL), lambda i: (i, 0)),
                pl.BlockSpec((D_MODEL, D_MODEL), lambda i: (0, 0)),
                pl.BlockSpec((tm, D_MODEL), lambda i: (i, 0)),
                pl.BlockSpec((1, D_MODEL), lambda i: (0, 0))]
    args = [merged, w_o, x, g.reshape(1, D_MODEL)]
    out_specs = [pl.BlockSpec((tm, D_MODEL), lambda i: (i, 0))]
    out_shape = [jax.ShapeDtypeStruct((SEQ, D_MODEL), F32)]
    if route:
        w_pad = jnp.zeros((D_MODEL, ROUTER_LANES), F32).at[:, :N_EXPERTS].set(w_router.astype(F32))
        w_hi = w_pad.astype(BF16)
        w_lo = (w_pad - w_hi.astype(F32)).astype(BF16)
        in_specs.append(pl.BlockSpec((D_MODEL, 2 * ROUTER_LANES), lambda i: (0, 0)))
        args.append(jnp.concatenate([w_hi, w_lo], axis=1))
        out_specs += [pl.BlockSpec((tm, HALF_D), lambda i: (i, 0)),
                      pl.BlockSpec((tm, ROUTER_LANES), lambda i: (i, 0)),
                      pl.BlockSpec((tm, ROUTER_LANES), lambda i: (i, 0))]
        out_shape += [jax.ShapeDtypeStruct((SEQ, HALF_D), jnp.uint32),
                      jax.ShapeDtypeStruct((SEQ, ROUTER_LANES), jnp.int32),
                      jax.ShapeDtypeStruct((SEQ, ROUTER_LANES), F32)]
    else:
        out_specs.append(pl.BlockSpec((tm, D_MODEL), lambda i: (i, 0)))
        out_shape.append(jax.ShapeDtypeStruct((SEQ, D_MODEL), BF16))
    return pl.pallas_call(
        functools.partial(_oproj_kernel, row_chunk=128, route=route),
        grid=(SEQ // tm,),
        in_specs=in_specs,
        out_specs=out_specs,
        out_shape=out_shape,
        compiler_params=_params(("parallel",), 48),
        name="out_proj",
    )(*args)


def _swiglu_step(x_ref, wg_ref, wu_ref, wd_ref, acc_ref, rows=None):
    x = x_ref[:rows, :]
    g = jnp.dot(x, wg_ref[0].astype(BF16), preferred_element_type=F32)
    u = jnp.dot(x, wu_ref[0].astype(BF16), preferred_element_type=F32)
    a = (g * jax.nn.sigmoid(g) * u).astype(BF16)
    acc_ref[:rows, :] += jnp.dot(a, wd_ref[0].astype(BF16), preferred_element_type=F32)


def _dense_ffn_kernel(x_ref, wg_ref, wu_ref, wd_ref, resid_ref, o_ref, acc_ref, *, nf, row_chunk):
    f = pl.program_id(1)

    @pl.when(f == 0)
    def _():
        acc_ref[...] = jnp.zeros(acc_ref.shape, acc_ref.dtype)

    _swiglu_step(x_ref, wg_ref, wu_ref, wd_ref, acc_ref)

    @pl.when(f == nf - 1)
    def _():
        def finish_rows(c, carry):
            rows = pl.ds(pl.multiple_of(c * row_chunk, row_chunk), row_chunk)
            o_ref[rows, :] = acc_ref[rows, :] + resid_ref[rows, :]
            return carry
        lax.fori_loop(0, acc_ref.shape[0] // row_chunk, finish_rows, 0)


def _dense_ffn(h, wg, wu, wd, layer, resid):
    tm, tf = 512, 512
    m, d = h.shape
    nf = D_FF // tf
    return pl.pallas_call(
        functools.partial(_dense_ffn_kernel, nf=nf, row_chunk=64),
        grid=(m // tm, nf),
        in_specs=[pl.BlockSpec((tm, d), lambda i, f: (i, 0)),
                  pl.BlockSpec((1, d, tf), lambda i, f: (layer, 0, f)),
                  pl.BlockSpec((1, d, tf), lambda i, f: (layer, 0, f)),
                  pl.BlockSpec((1, tf, d), lambda i, f: (layer, f, 0)),
                  pl.BlockSpec((tm, d), lambda i, f: (i, 0))],
        out_specs=pl.BlockSpec((tm, d), lambda i, f: (i, 0)),
        out_shape=jax.ShapeDtypeStruct((m, d), F32),
        scratch_shapes=[pltpu.VMEM((tm, d), F32)],
        compiler_params=_params(("parallel", "arbitrary"), 48),
        name="dense_ffn",
    )(h, wg, wu, wd, resid)


MOE_TF = 512
MOE_NF = D_FF // MOE_TF
MOE_ROWS_PER_STEP = 96
MOE_TM = MOE_NF * MOE_ROWS_PER_STEP
MOE_ROW_CHUNK = 96
MOE_N_TILES = -(-SEQ * TOP_K // MOE_TM) + N_EXPERTS
MOE_THIRDS = 3
MOE_THIRD_ROWS = MOE_TM // MOE_THIRDS


def _tile_row_copy(tok_ref, h_ref, xg_ref, sems, tile, slot, r):
    row = tok_ref[tile * MOE_TM + r]
    return pltpu.make_async_copy(h_ref.at[pl.ds(row, 1), :], xg_ref.at[slot, pl.ds(r, 1), :],
                                 sems.at[slot])


def _wait_tile_rows(h_ref, xg_ref, sems, slot):
    pltpu.make_async_copy(h_ref.at[pl.ds(0, MOE_TM), :], xg_ref.at[slot], sems.at[slot]).wait()


def _moe_expert_kernel(be_ref, nu_ref, nt_ref, tok_ref, h_ref, wg_ref, wu_ref, wd_ref, o_ref,
                       xg_ref, xb_ref, acc_ref, sems):
    i = pl.program_id(0)
    f = pl.program_id(1)
    n_used = nu_ref[0]
    active = i < n_used
    slot = i % 2
    n_chunks = MOE_TM // MOE_ROW_CHUNK

    @pl.when(jnp.logical_and(i == 0, f == 0))
    def _():
        def start_row(r, carry):
            _tile_row_copy(tok_ref, h_ref, xg_ref, sems, 0, 0, r).start()
            return carry
        lax.fori_loop(0, MOE_TM, start_row, 0, unroll=8)

    @pl.when(jnp.logical_and(active, f == 0))
    def _():
        _wait_tile_rows(h_ref, xg_ref, sems, slot)
        acc_ref[...] = jnp.zeros(acc_ref.shape, acc_ref.dtype)

        def unpack_rows(c, carry):
            rows = pl.ds(pl.multiple_of(c * MOE_ROW_CHUNK, MOE_ROW_CHUNK), MOE_ROW_CHUNK)
            lo, hi = _unpack_halves(xg_ref[slot, rows, :])
            xb_ref[rows, :HALF_D] = lo.astype(BF16)
            xb_ref[rows, HALF_D:] = hi.astype(BF16)
            return carry
        lax.fori_loop(0, n_chunks, unpack_rows, 0)

    for thirds in range(1, MOE_THIRDS + 1):
        @pl.when(jnp.logical_and(active, nt_ref[i] == thirds))
        def _(thirds=thirds):
            for j in range(MOE_ROWS_PER_STEP):
                _tile_row_copy(tok_ref, h_ref, xg_ref, sems, i + 1, 1 - slot,
                               f * MOE_ROWS_PER_STEP + j).start()
            _swiglu_step(xb_ref, wg_ref, wu_ref, wd_ref, acc_ref, rows=thirds * MOE_THIRD_ROWS)

    @pl.when(jnp.logical_and(i == n_used, f == 0))
    def _():
        _wait_tile_rows(h_ref, xg_ref, sems, slot)

    @pl.when(f == MOE_NF - 1)
    def _():
        @pl.when(active)
        def _():
            def finish_rows(c, carry):
                rows = pl.ds(pl.multiple_of(c * MOE_ROW_CHUNK, MOE_ROW_CHUNK), MOE_ROW_CHUNK)
                o_ref[rows, :] = _pack_halves(acc_ref[rows, :])
                return carry
            lax.fori_loop(0, n_chunks, finish_rows, 0)

        @pl.when(jnp.logical_not(active))
        def _():
            o_ref[...] = jnp.zeros(o_ref.shape, o_ref.dtype)


def _moe_experts(h_packed, tok, blk_e, n_used, n_thirds, wg, wu, wd):
    d, tf = D_MODEL, MOE_TF

    def f_idx(i, f, nu):
        return jnp.where(i < nu[0], f, MOE_NF - 1)

    def w_gu(i, f, be, nu, nt, tok):
        return (be[i], 0, f_idx(i, f, nu))

    def w_d(i, f, be, nu, nt, tok):
        return (be[i], f_idx(i, f, nu), 0)

    grid_spec = pltpu.PrefetchScalarGridSpec(
        num_scalar_prefetch=4,
        grid=(MOE_N_TILES, MOE_NF),
        in_specs=[pl.BlockSpec(memory_space=pl.ANY),
                  pl.BlockSpec((1, d, tf), w_gu),
                  pl.BlockSpec((1, d, tf), w_gu),
                  pl.BlockSpec((1, tf, d), w_d)],
        out_specs=pl.BlockSpec((MOE_TM, HALF_D), lambda i, f, be, nu, nt, tok: (i, 0)),
        scratch_shapes=[pltpu.VMEM((2, MOE_TM, HALF_D), jnp.uint32),
                        pltpu.VMEM((MOE_TM, d), BF16),
                        pltpu.VMEM((MOE_TM, d), F32),
                        pltpu.SemaphoreType.DMA((2,))],
    )
    return pl.pallas_call(
        _moe_expert_kernel,
        grid_spec=grid_spec,
        out_shape=jax.ShapeDtypeStruct((MOE_N_TILES * MOE_TM, HALF_D), jnp.uint32),
        compiler_params=_params(("arbitrary", "arbitrary"), 60),
        name="moe_experts",
    )(blk_e, n_used, n_thirds, tok, h_packed, wg, wu, wd)


def _start_row_gather(idx_ref, idx_base, src_ref, dst_ref, sem, n_rows):
    def body(r, carry):
        row = idx_ref[idx_base + r]
        pltpu.make_async_copy(src_ref.at[pl.ds(row, 1), :], dst_ref.at[pl.ds(r, 1), :], sem).start()
        return carry
    lax.fori_loop(0, n_rows, body, 0, unroll=8)


def _wait_row_gather(src_ref, dst_ref, sem, n_rows):
    pltpu.make_async_copy(src_ref.at[pl.ds(0, n_rows), :], dst_ref, sem).wait()


def _moe_combine_kernel(pos_ref, y_ref, x_ref, gate_ref, o_ref, buf_ref, sems, *, tokens):
    i = pl.program_id(0)
    slot = i % 2
    rows = TOP_K * tokens

    @pl.when(i == 0)
    def _():
        _start_row_gather(pos_ref, 0, y_ref, buf_ref.at[0], sems.at[0], rows)

    @pl.when(i + 1 < pl.num_programs(0))
    def _():
        _start_row_gather(pos_ref, (i + 1) * rows, y_ref, buf_ref.at[1 - slot],
                          sems.at[1 - slot], rows)

    _wait_row_gather(y_ref, buf_ref.at[slot], sems.at[slot], rows)
    g0 = gate_ref[:, 0:1]
    g1 = gate_ref[:, 1:2]
    lo0, hi0 = _unpack_halves(buf_ref[slot, :tokens, :])
    lo1, hi1 = _unpack_halves(buf_ref[slot, tokens:, :])
    o_ref[:, :HALF_D] = x_ref[:, :HALF_D] + (g0 * lo0 + g1 * lo1)
    o_ref[:, HALF_D:] = x_ref[:, HALF_D:] + (g0 * hi0 + g1 * hi1)


def _moe_combine(y_packed, pos, x, gate):
    tokens = 512
    n = x.shape[0]
    steps = n // tokens
    pos_tiled = pos.reshape(steps, tokens, TOP_K).transpose(0, 2, 1).reshape(-1)
    grid_spec = pltpu.PrefetchScalarGridSpec(
        num_scalar_prefetch=1,
        grid=(steps,),
        in_specs=[pl.BlockSpec(memory_space=pl.ANY),
                  pl.BlockSpec((tokens, D_MODEL), lambda i, pos_ref: (i, 0)),
                  pl.BlockSpec((tokens, ROUTER_LANES), lambda i, pos_ref: (i, 0))],
        out_specs=pl.BlockSpec((tokens, D_MODEL), lambda i, pos_ref: (i, 0)),
        scratch_shapes=[pltpu.VMEM((2, TOP_K * tokens, HALF_D), jnp.uint32),
                        pltpu.SemaphoreType.DMA((2,))],
    )
    return pl.pallas_call(
        functools.partial(_moe_combine_kernel, tokens=tokens),
        grid_spec=grid_spec,
        out_shape=jax.ShapeDtypeStruct((n, D_MODEL), F32),
        compiler_params=_params(("arbitrary",), 40),
        name="moe_combine",
    )(pos_tiled, y_packed, x, gate)


def _moe(x_new, h_packed, idx, gate, wg, wu, wd):
    n = SEQ
    n_assign = n * TOP_K
    cap = MOE_N_TILES * MOE_TM
    e_flat = idx[:, :TOP_K].reshape(-1)
    onehot = (e_flat[:, None] == jnp.arange(N_EXPERTS, dtype=jnp.int32)[None, :]).astype(jnp.int32)
    csum = jnp.cumsum(onehot, axis=0)
    counts = csum[-1]
    rank = jnp.sum((csum - onehot) * onehot, axis=1)
    padded = (counts + MOE_TM - 1) // MOE_TM * MOE_TM
    pad_ends = jnp.cumsum(padded)
    pad_starts = pad_ends - padded
    dest = jnp.sum(onehot * pad_starts[None, :], axis=1) + rank
    tok = jnp.zeros((cap,), jnp.int32).at[dest].set(
        jnp.arange(n_assign, dtype=jnp.int32) // TOP_K)
    n_used = (pad_ends[-1] // MOE_TM).astype(jnp.int32).reshape(1)
    tile_start = jnp.arange(MOE_N_TILES, dtype=jnp.int32) * MOE_TM
    tile_start = jnp.minimum(tile_start, pad_ends[-1] - 1)
    blk_e = jnp.sum((tile_start[:, None] >= pad_ends[None, :]).astype(jnp.int32), axis=1)
    blk_e = jnp.clip(blk_e, 0, N_EXPERTS - 1)
    real_rows = counts[blk_e] - (tile_start - pad_starts[blk_e])
    n_thirds = jnp.clip((real_rows + MOE_THIRD_ROWS - 1) // MOE_THIRD_ROWS, 1, MOE_THIRDS)
    yb = _moe_experts(h_packed, tok, blk_e, n_used, n_thirds.astype(jnp.int32), wg, wu, wd)
    return _moe_combine(yb, dest.reshape(n, TOP_K), x_new, gate)


def kernel(x, mem, ln_mix_g, w_in, na_q_g, na_k_g, na_rpb, mem_ln_g, w_mem_kv, mem_q_g, mem_k_g,
           w_fourier_out, w_na_out, w_mem_out, w_o, ln_ffn_g, ffn_w_gate, ffn_w_up, ffn_w_down,
           moe_router, moe_w_gate, moe_w_up, moe_w_down):
    assert x.shape == (1, SEQ, D_MODEL) and mem.shape == (1, N_MEM, D_MODEL)
    xs = x.reshape(SEQ, D_MODEL)
    mem2 = mem.reshape(N_MEM, D_MODEL)
    tables = _dft_tables()
    for l in range(DEPTH):
        proj = _norm_mm(xs, ln_mix_g[l], w_in, l, tm=1024, tn=1024, out_dtype=BF16,
                        vmem_mib=56, name="in_proj")
        fm = _fourier_mix(proj, tables)
        o_na = _na_attention(proj, na_q_g[l], na_k_g[l], _na_col_blocks(na_rpb[l]))
        kv = _norm_mm(mem2, mem_ln_g[l], w_mem_kv, l, tm=N_MEM, tn=2 * MEM_W, out_dtype=F32,
                      vmem_mib=40, name="mem_kv_proj")
        o_mem = _mem_attention(proj, kv, mem_q_g[l], mem_k_g[l])
        merged = _merge(fm, o_na, o_mem, proj, w_fourier_out[l].astype(BF16),
                        w_na_out[l].astype(BF16), w_mem_out[l].astype(BF16))
        i = l // 2
        if l % 2 == 0:
            x_new, h2 = _out_proj(merged, w_o[l].astype(BF16), xs, ln_ffn_g[l])
            xs = _dense_ffn(h2, ffn_w_gate.astype(BF16), ffn_w_up.astype(BF16),
                            ffn_w_down.astype(BF16), i, x_new)
        else:
            x_new, h2, idx, gate = _out_proj(merged, w_o[l].astype(BF16), xs, ln_ffn_g[l],
                                             moe_router[i])
            xs = _moe(x_new, h2, idx, gate, moe_w_gate[i], moe_w_up[i], moe_w_down[i])
    return xs.reshape(1, SEQ, D_MODEL)
```

```python
import functools
import math

import numpy as np
import jax
import jax.numpy as jnp
from jax import lax
from jax.experimental import pallas as pl
from jax.experimental.pallas import tpu as pltpu

F32 = jnp.float32
BF16 = jnp.bfloat16

D_MODEL = 2048
SEQ = 16384
DEPTH = 2
GRID_W = 64
ROWS = SEQ // GRID_W
HEAD_DIM = 128
N_FOURIER_GROUPS = 4
FOURIER_W = N_FOURIER_GROUPS * HEAD_DIM
N_NA_HEADS = 8
NA_W = N_NA_HEADS * HEAD_DIM
WIN_H = 8
WIN_W = 16
N_MEM = 256
N_MEM_HEADS = 4
MEM_W = N_MEM_HEADS * HEAD_DIM
IN_W = FOURIER_W + 3 * NA_W + MEM_W + 3 * D_MODEL
D_FF = 5632
N_EXPERTS = 8
TOP_K = 2
EPS = 1e-6
HALF_D = D_MODEL // 2

Q_NA_OFF = FOURIER_W
K_NA_OFF = Q_NA_OFF + NA_W
V_NA_OFF = K_NA_OFF + NA_W
Q_MEM_OFF = V_NA_OFF + NA_W
GATE_OFF = Q_MEM_OFF + MEM_W

MIB = 1024 * 1024
NEG_BIG = -1e30

FFT_A = 128
FFT_B = SEQ // FFT_A


def _params(semantics, vmem_mib):
    return pltpu.CompilerParams(dimension_semantics=semantics,
                                vmem_limit_bytes=int(vmem_mib * MIB))


def _rms_rows(x, gain):
    ms = jnp.mean(x * x, axis=-1, keepdims=True)
    return x * lax.rsqrt(ms + EPS) * gain


def _bf16_bits_rtne(v):
    bits = pltpu.bitcast(v, jnp.uint32)
    lsb = lax.shift_right_logical(bits, jnp.uint32(16)) & jnp.uint32(1)
    return bits + jnp.uint32(0x7FFF) + lsb


def _pack_halves(v):
    half = v.shape[1] // 2
    lo = lax.shift_right_logical(_bf16_bits_rtne(v[:, :half]), jnp.uint32(16))
    hi = _bf16_bits_rtne(v[:, half:]) & jnp.uint32(0xFFFF0000)
    return lo | hi


def _unpack_halves(w):
    lo = pltpu.bitcast(lax.shift_left(w, jnp.uint32(16)), F32)
    hi = pltpu.bitcast(w & jnp.uint32(0xFFFF0000), F32)
    return lo, hi


def _norm_mm_kernel(x_ref, g_ref, w_ref, o_ref, h_ref, *, row_chunk):
    @pl.when(pl.program_id(1) == 0)
    def _():
        def body(c, carry):
            rows = pl.ds(pl.multiple_of(c * row_chunk, row_chunk), row_chunk)
            h_ref[rows, :] = _rms_rows(x_ref[rows, :].astype(F32), g_ref[...]).astype(BF16)
            return carry
        lax.fori_loop(0, h_ref.shape[0] // row_chunk, body, 0)

    o_ref[...] = jnp.dot(h_ref[...], w_ref[0].astype(BF16),
                         preferred_element_type=F32).astype(o_ref.dtype)


def _norm_mm(x, g, w, layer, *, tm, tn, out_dtype, vmem_mib, name):
    m, k = x.shape
    n = w.shape[2]
    row_chunk = min(tm, 128)
    return pl.pallas_call(
        functools.partial(_norm_mm_kernel, row_chunk=row_chunk),
        grid=(m // tm, n // tn),
        in_specs=[pl.BlockSpec((tm, k), lambda i, j: (i, 0)),
                  pl.BlockSpec((1, k), lambda i, j: (0, 0)),
                  pl.BlockSpec((1, k, tn), lambda i, j: (layer, 0, j))],
        out_specs=pl.BlockSpec((tm, tn), lambda i, j: (i, j)),
        out_shape=jax.ShapeDtypeStruct((m, n), out_dtype),
        scratch_shapes=[pltpu.VMEM((tm, k), BF16)],
        compiler_params=_params(("parallel", "arbitrary"), vmem_mib),
        name=name,
    )(x, g.reshape(1, k), w)


def _dft_tables():
    a = np.arange(FFT_A)
    ang1 = 2.0 * np.pi * np.outer(a, a) / FFT_A
    w1 = np.concatenate([np.cos(ang1), -np.sin(ang1)], axis=0)
    k1 = np.arange(FFT_A)[:, None, None]
    k2 = np.arange(FFT_B)[None, :, None]
    s2 = np.arange(FFT_B)[None, None, :]
    ang2 = 2.0 * np.pi * ((s2 * (k1 + FFT_A * k2)) % SEQ) / SEQ
    g2 = np.concatenate([np.cos(ang2), np.sin(ang2)], axis=2)
    c = np.arange(HEAD_DIM)
    angc = 2.0 * np.pi * np.outer(c, c) / HEAD_DIM
    eye = np.eye(N_FOURIER_GROUPS)
    bd = np.concatenate([np.kron(eye, np.cos(angc)), np.kron(eye, np.sin(angc))], axis=0)
    return (jnp.asarray(w1, dtype=BF16), jnp.asarray(g2, dtype=BF16),
            jnp.asarray(bd, dtype=BF16))


FFT_BLK = 16


def _dft1_kernel(w_ref, x_ref, br_ref, bi_ref):
    xt = pltpu.einshape("abc->bac", x_ref[...])
    for j in range(FFT_BLK):
        r = jnp.dot(w_ref[...], xt[j], preferred_element_type=F32)
        br_ref[j] = r[:FFT_A].astype(BF16)
        bi_ref[j] = r[FFT_A:].astype(BF16)


def _dft2_kernel(br_ref, bi_ref, g_ref, bd_ref, o_ref, *, norm):
    brt = pltpu.einshape("abc->bac", br_ref[...])
    bit = pltpu.einshape("abc->bac", bi_ref[...])
    ys = []
    for i in range(FFT_BLK):
        br = brt[i]
        bi = bit[i]
        g = g_ref[i]
        zr = jnp.dot(g, jnp.concatenate([br, bi], axis=0), preferred_element_type=F32)
        zi = jnp.dot(g, jnp.concatenate([bi, -br], axis=0), preferred_element_type=F32)
        z = jnp.concatenate([zr, zi], axis=1).astype(BF16)
        y = jnp.dot(z, bd_ref[...], preferred_element_type=F32) * norm
        ys.append(y.astype(o_ref.dtype))
    o_ref[...] = pltpu.einshape("abc->bac", jnp.stack(ys, axis=0))


def _fourier_mix(proj, tables):
    w1, g2, bd = tables
    x3 = proj.reshape(FFT_A, FFT_B, IN_W)
    blk = (FFT_A, FFT_BLK, FOURIER_W)
    br, bi = pl.pallas_call(
        _dft1_kernel,
        grid=(FFT_B // FFT_BLK,),
        in_specs=[pl.BlockSpec((2 * FFT_A, FFT_A), lambda j: (0, 0)),
                  pl.BlockSpec(blk, lambda j: (0, j, 0))],
        out_specs=[pl.BlockSpec((FFT_BLK, FFT_A, FOURIER_W), lambda j: (j, 0, 0)),
                   pl.BlockSpec((FFT_BLK, FFT_A, FOURIER_W), lambda j: (j, 0, 0))],
        out_shape=[jax.ShapeDtypeStruct((FFT_B, FFT_A, FOURIER_W), BF16)] * 2,
        compiler_params=_params(("parallel",), 40),
        name="fourier_stage1",
    )(w1, x3)
    norm = 1.0 / math.sqrt(SEQ * HEAD_DIM)
    y3 = pl.pallas_call(
        functools.partial(_dft2_kernel, norm=norm),
        grid=(FFT_A // FFT_BLK,),
        in_specs=[pl.BlockSpec((FFT_B, FFT_BLK, FOURIER_W), lambda j: (0, j, 0)),
                  pl.BlockSpec((FFT_B, FFT_BLK, FOURIER_W), lambda j: (0, j, 0)),
                  pl.BlockSpec((FFT_BLK, FFT_B, 2 * FFT_B), lambda j: (j, 0, 0)),
                  pl.BlockSpec((2 * FOURIER_W, FOURIER_W), lambda j: (0, 0))],
        out_specs=pl.BlockSpec((FFT_B, FFT_BLK, FOURIER_W), lambda j: (0, j, 0)),
        out_shape=jax.ShapeDtypeStruct((FFT_B, FFT_A, FOURIER_W), BF16),
        compiler_params=_params(("parallel",), 40),
        name="fourier_stage2",
    )(br, bi, g2, bd)
    return y3.reshape(SEQ, FOURIER_W)


NA_SUB_ROWS = 4
NA_KEY_ROWS = NA_SUB_ROWS + WIN_H
NA_SUBS_PER_STEP = 4
NA_N_SUB = ROWS // NA_SUB_ROWS
NA_MAX_KEY_START = ROWS - NA_KEY_ROWS
NA_N_ROW_OFFSETS = 2 * WIN_H - 1
NA_MASKED_BLOCK = NA_N_ROW_OFFSETS


def _na_row_offsets():
    out = np.full((3, NA_SUB_ROWS, NA_KEY_ROWS), NA_MASKED_BLOCK, np.int32)
    for case, r0 in enumerate((0, 2 * NA_SUB_ROWS, ROWS - NA_SUB_ROWS)):
        ks = int(np.clip(r0 - WIN_H // 2, 0, NA_MAX_KEY_START))
        for qi in range(NA_SUB_ROWS):
            r = r0 + qi
            r_start = int(np.clip(r - WIN_H // 2, 0, ROWS - WIN_H))
            for kj in range(NA_KEY_ROWS):
                if r_start <= ks + kj < r_start + WIN_H:
                    out[case, qi, kj] = ks + kj - r + (WIN_H - 1)
    return out


def _na_col_blocks(rpb):
    qc = np.arange(GRID_W)[:, None]
    kc = np.arange(GRID_W)[None, :]
    col_start = np.clip(qc - WIN_W // 2, 0, GRID_W - WIN_W)
    col_valid = (kc >= col_start) & (kc < col_start + WIN_W)
    dc = kc - qc + (WIN_W - 1)
    sel = (np.arange(2 * WIN_W - 1)[:, None, None] == dc[None]) & col_valid[None]
    sel = jnp.asarray(sel.astype(np.float32))
    t = jnp.sum(rpb.astype(F32)[:, :, :, None, None] * sel[None, None], axis=2)
    t = jnp.where(col_valid[None, None], t, NEG_BIG)
    masked = jnp.full((N_NA_HEADS, 1, GRID_W, GRID_W), NEG_BIG, F32)
    t = jnp.concatenate([t, masked], axis=1)
    return jnp.concatenate([t, t], axis=-1)


def _na_kernel(q_ref, k_ref, v_ref, qg_ref, kg_ref, cb_ref, o_ref, kn_ref, bias_ref):
    step = pl.program_id(1)
    nq = NA_SUB_ROWS * GRID_W
    nk = NA_KEY_ROWS * GRID_W

    @pl.when(step == 0)
    def _():
        chunk = 1024

        def body(c, carry):
            rows = pl.ds(pl.multiple_of(c * chunk, chunk), chunk)
            kn_ref[rows, :] = _rms_rows(k_ref[rows, :].astype(F32), kg_ref[...]).astype(BF16)
            return carry
        lax.fori_loop(0, SEQ // chunk, body, 0)

        row_off = _na_row_offsets()
        for case in range(3):
            for qi in range(NA_SUB_ROWS):
                for kj in range(NA_KEY_ROWS):
                    lanes = slice(kj * GRID_W, (kj + 1) * GRID_W)
                    src = slice((kj % 2) * GRID_W, (kj % 2 + 1) * GRID_W)
                    bias_ref[case, qi * GRID_W:(qi + 1) * GRID_W, lanes] = (
                        cb_ref[0, int(row_off[case, qi, kj]), :, src])

    scale = HEAD_DIM ** -0.5
    for sb in range(NA_SUBS_PER_STEP):
        sub = step * NA_SUBS_PER_STEP + sb
        key_row0 = jnp.clip(sub * NA_SUB_ROWS - WIN_H // 2, 0, NA_MAX_KEY_START)
        case = jnp.where(sub == 0, 0, jnp.where(sub == NA_N_SUB - 1, 2, 1))
        q = q_ref[sb * nq:(sb + 1) * nq, :].astype(F32)
        qn = (_rms_rows(q, qg_ref[...]) * scale).astype(BF16)
        kstart = pl.multiple_of(key_row0 * GRID_W, GRID_W)
        kw = kn_ref[pl.ds(kstart, nk), :]
        vw = v_ref[pl.ds(kstart, nk), :]
        s = lax.dot_general(qn, kw, (((1,), (1,)), ((), ())), preferred_element_type=F32)
        s = s + bias_ref[case]
        m = jnp.max(s, axis=-1, keepdims=True)
        p = jnp.exp(s - m)
        l = jnp.sum(p, axis=-1, keepdims=True)
        o = jnp.dot(p.astype(BF16), vw, preferred_element_type=F32) / l
        o_ref[sb * nq:(sb + 1) * nq, :] = o.astype(o_ref.dtype)


def _na_attention(proj, q_g, k_g, col_blocks):
    nq = NA_SUB_ROWS * GRID_W
    nk = NA_KEY_ROWS * GRID_W
    tq = NA_SUBS_PER_STEP * nq
    n_blk = NA_N_ROW_OFFSETS + 1
    qb, kb_, vb = Q_NA_OFF // HEAD_DIM, K_NA_OFF // HEAD_DIM, V_NA_OFF // HEAD_DIM
    return pl.pallas_call(
        _na_kernel,
        grid=(N_NA_HEADS, SEQ // tq),
        in_specs=[pl.BlockSpec((tq, HEAD_DIM), lambda h, i: (i, qb + h)),
                  pl.BlockSpec((SEQ, HEAD_DIM), lambda h, i: (0, kb_ + h)),
                  pl.BlockSpec((SEQ, HEAD_DIM), lambda h, i: (0, vb + h)),
                  pl.BlockSpec((1, HEAD_DIM), lambda h, i: (0, 0)),
                  pl.BlockSpec((1, HEAD_DIM), lambda h, i: (0, 0)),
                  pl.BlockSpec((1, n_blk, GRID_W, 2 * GRID_W), lambda h, i: (h, 0, 0, 0))],
        out_specs=pl.BlockSpec((tq, HEAD_DIM), lambda h, i: (i, h)),
        out_shape=jax.ShapeDtypeStruct((SEQ, NA_W), BF16),
        scratch_shapes=[pltpu.VMEM((SEQ, HEAD_DIM), BF16),
                        pltpu.VMEM((3, nq, nk), F32)],
        compiler_params=_params(("parallel", "arbitrary"), 48),
        name="na_attention",
    )(proj, proj, proj, q_g.reshape(1, HEAD_DIM), k_g.reshape(1, HEAD_DIM), col_blocks)


def _mem_attn_kernel(q_ref, kv_ref, qg_ref, kg_ref, o_ref):
    scale = HEAD_DIM ** -0.5
    for h in range(N_MEM_HEADS):
        cols = slice(h * HEAD_DIM, (h + 1) * HEAD_DIM)
        qn = (_rms_rows(q_ref[:, cols].astype(F32), qg_ref[...]) * scale).astype(BF16)
        kn = _rms_rows(kv_ref[:, cols].astype(F32), kg_ref[...]).astype(BF16)
        v = kv_ref[:, MEM_W + h * HEAD_DIM:MEM_W + (h + 1) * HEAD_DIM].astype(BF16)
        s = lax.dot_general(qn, kn, (((1,), (1,)), ((), ())), preferred_element_type=F32)
        m = jnp.max(s, axis=-1, keepdims=True)
        p = jnp.exp(s - m)
        l = jnp.sum(p, axis=-1, keepdims=True)
        o = jnp.dot(p.astype(BF16), v, preferred_element_type=F32) / l
        o_ref[:, cols] = o.astype(o_ref.dtype)


def _mem_attention(proj, kv, q_g, k_g):
    tm = 512
    return pl.pallas_call(
        _mem_attn_kernel,
        grid=(SEQ // tm,),
        in_specs=[pl.BlockSpec((tm, MEM_W), lambda i: (i, Q_MEM_OFF // MEM_W)),
                  pl.BlockSpec((N_MEM, 2 * MEM_W), lambda i: (0, 0)),
                  pl.BlockSpec((1, HEAD_DIM), lambda i: (0, 0)),
                  pl.BlockSpec((1, HEAD_DIM), lambda i: (0, 0))],
        out_specs=pl.BlockSpec((tm, MEM_W), lambda i: (i, 0)),
        out_shape=jax.ShapeDtypeStruct((SEQ, MEM_W), BF16),
        compiler_params=_params(("parallel",), 32),
        name="mem_attention",
    )(proj, kv, q_g.reshape(1, HEAD_DIM), k_g.reshape(1, HEAD_DIM))


def _merge_kernel(fm_ref, na_ref, mo_ref, g0_ref, g1_ref, g2_ref, wf_ref, wn_ref, wm_ref, o_ref,
                  *, col_chunk):
    for c in range(D_MODEL // col_chunk):
        cols = slice(c * col_chunk, (c + 1) * col_chunk)
        o_f = jnp.dot(fm_ref[...], wf_ref[:, cols], preferred_element_type=F32)
        o_n = jnp.dot(na_ref[...], wn_ref[:, cols], preferred_element_type=F32)
        o_m = jnp.dot(mo_ref[...], wm_ref[:, cols], preferred_element_type=F32)
        acc = jax.nn.sigmoid(g0_ref[:, cols].astype(F32)) * o_f
        acc = acc + jax.nn.sigmoid(g1_ref[:, cols].astype(F32)) * o_n
        acc = acc + jax.nn.sigmoid(g2_ref[:, cols].astype(F32)) * o_m
        o_ref[:, cols] = acc.astype(o_ref.dtype)


def _merge(fm, o_na, o_mem, proj, wf, wn, wm):
    tm = 512
    gb = GATE_OFF // D_MODEL
    return pl.pallas_call(
        functools.partial(_merge_kernel, col_chunk=512),
        grid=(SEQ // tm,),
        in_specs=[pl.BlockSpec((tm, FOURIER_W), lambda i: (i, 0)),
                  pl.BlockSpec((tm, NA_W), lambda i: (i, 0)),
                  pl.BlockSpec((tm, MEM_W), lambda i: (i, 0)),
                  pl.BlockSpec((tm, D_MODEL), lambda i: (i, gb)),
                  pl.BlockSpec((tm, D_MODEL), lambda i: (i, gb + 1)),
                  pl.BlockSpec((tm, D_MODEL), lambda i: (i, gb + 2)),
                  pl.BlockSpec((FOURIER_W, D_MODEL), lambda i: (0, 0)),
                  pl.BlockSpec((NA_W, D_MODEL), lambda i: (0, 0)),
                  pl.BlockSpec((MEM_W, D_MODEL), lambda i: (0, 0))],
        out_specs=pl.BlockSpec((tm, D_MODEL), lambda i: (i, 0)),
        out_shape=jax.ShapeDtypeStruct((SEQ, D_MODEL), BF16),
        compiler_params=_params(("parallel",), 48),
        name="gated_merge",
    )(fm, o_na, o_mem, proj, proj, proj, wf, wn, wm)


ROUTER_LANES = 128


def _route_top2(h, wr_ref):
    h_hi = h.astype(BF16)
    h_lo = (h - h_hi.astype(F32)).astype(BF16)
    r_hi = jnp.dot(h_hi, wr_ref[...], preferred_element_type=F32)
    r_lo = jnp.dot(h_lo, wr_ref[:, :ROUTER_LANES], preferred_element_type=F32)
    logits = r_hi[:, :ROUTER_LANES] + r_hi[:, ROUTER_LANES:] + r_lo
    lane = lax.broadcasted_iota(jnp.int32, logits.shape, 1).astype(F32)
    logits = jnp.where(lane < N_EXPERTS, logits, -jnp.inf)
    m1 = jnp.max(logits, axis=-1, keepdims=True)
    i1 = jnp.min(jnp.where(logits == m1, lane, float(ROUTER_LANES)), axis=-1, keepdims=True)
    rest = jnp.where(lane == i1, -jnp.inf, logits)
    m2 = jnp.max(rest, axis=-1, keepdims=True)
    i2 = jnp.min(jnp.where(rest == m2, lane, float(ROUTER_LANES)), axis=-1, keepdims=True)
    e21 = jnp.exp(m2 - m1)
    g1 = 1.0 / (1.0 + e21)
    g2 = e21 * g1
    return jnp.where(lane == 0.0, i1, i2).astype(jnp.int32), jnp.where(lane == 0.0, g1, g2)


def _oproj_kernel(a_ref, w_ref, x_ref, g_ref, *rest, row_chunk, route):
    if route:
        wr_ref, xo_ref, h_ref, idx_ref, gate_ref = rest
    else:
        xo_ref, h_ref = rest
    xo_ref[...] = jnp.dot(a_ref[...], w_ref[...], preferred_element_type=F32) + x_ref[...]

    def body(c, carry):
        rows = pl.ds(pl.multiple_of(c * row_chunk, row_chunk), row_chunk)
        h = _rms_rows(xo_ref[rows, :], g_ref[...])
        if route:
            h_ref[rows, :] = _pack_halves(h)
            idx_ref[rows, :], gate_ref[rows, :] = _route_top2(h, wr_ref)
        else:
            h_ref[rows, :] = h.astype(h_ref.dtype)
        return carry
    lax.fori_loop(0, xo_ref.shape[0] // row_chunk, body, 0, unroll=route)


def _out_proj(merged, w_o, x, g, w_router=None):
    tm = 512
    route = w_router is not None
    in_specs = [pl.BlockSpec((tm, D_MODEL), lambda i: (i, 0)),
                pl.BlockSpec((D_MODEL, D_MODEL), lambda i: (0, 0)),
                pl.BlockSpec((tm, D_MODEL), lambda i: (i, 0)),
                pl.BlockSpec((1, D_MODEL), lambda i: (0, 0))]
    args = [merged, w_o, x, g.reshape(1, D_MODEL)]
    out_specs = [pl.BlockSpec((tm, D_MODEL), lambda i: (i, 0))]
    out_shape = [jax.ShapeDtypeStruct((SEQ, D_MODEL), F32)]
    if route:
        w_pad = jnp.zeros((D_MODEL, ROUTER_LANES), F32).at[:, :N_EXPERTS].set(w_router.astype(F32))
        w_hi = w_pad.astype(BF16)
        w_lo = (w_pad - w_hi.astype(F32)).astype(BF16)
        in_specs.append(pl.BlockSpec((D_MODEL, 2 * ROUTER_LANES), lambda i: (0, 0)))
        args.append(jnp.concatenate([w_hi, w_lo], axis=1))
        out_specs += [pl.BlockSpec((tm, HALF_D), lambda i: (i, 0)),
                      pl.BlockSpec((tm, ROUTER_LANES), lambda i: (i, 0)),
                      pl.BlockSpec((tm, ROUTER_LANES), lambda i: (i, 0))]
        out_shape += [jax.ShapeDtypeStruct((SEQ, HALF_D), jnp.uint32),
                      jax.ShapeDtypeStruct((SEQ, ROUTER_LANES), jnp.int32),
                      jax.ShapeDtypeStruct((SEQ, ROUTER_LANES), F32)]
    else:
        out_specs.append(pl.BlockSpec((tm, D_MODEL), lambda i: (i, 0)))
        out_shape.append(jax.ShapeDtypeStruct((SEQ, D_MODEL), BF16))
    return pl.pallas_call(
        functools.partial(_oproj_kernel, row_chunk=128, route=route),
        grid=(SEQ // tm,),
        in_specs=in_specs,
        out_specs=out_specs,
        out_shape=out_shape,
        compiler_params=_params(("parallel",), 48),
        name="out_proj",
    )(*args)


def _swiglu_step(x_ref, wg_ref, wu_ref, wd_ref, acc_ref, rows=None):
    x = x_ref[:rows, :]
    g = jnp.dot(x, wg_ref[0].astype(BF16), preferred_element_type=F32)
    u = jnp.dot(x, wu_ref[0].astype(BF16), preferred_element_type=F32)
    a = (g * jax.nn.sigmoid(g) * u).astype(BF16)
    acc_ref[:rows, :] += jnp.dot(a, wd_ref[0].astype(BF16), preferred_element_type=F32)


def _dense_ffn_kernel(x_ref, wg_ref, wu_ref, wd_ref, resid_ref, o_ref):
    @pl.when(pl.program_id(1) == 0)
    def _():
        o_ref[...] = resid_ref[...]

    _swiglu_step(x_ref, wg_ref, wu_ref, wd_ref, o_ref)


def _dense_ffn(h, wg, wu, wd, layer, resid):
    tm, tf = 1024, 512
    m, d = h.shape
    nf = D_FF // tf
    return pl.pallas_call(
        _dense_ffn_kernel,
        grid=(m // tm, nf),
        in_specs=[pl.BlockSpec((tm, d), lambda i, f: (i, 0)),
                  pl.BlockSpec((1, d, tf), lambda i, f: (layer, 0, f)),
                  pl.BlockSpec((1, d, tf), lambda i, f: (layer, 0, f)),
                  pl.BlockSpec((1, tf, d), lambda i, f: (layer, f, 0)),
                  pl.BlockSpec((tm, d), lambda i, f: (i, 0))],
        out_specs=pl.BlockSpec((tm, d), lambda i, f: (i, 0)),
        out_shape=jax.ShapeDtypeStruct((m, d), F32),
        compiler_params=_params(("parallel", "arbitrary"), 60),
        name="dense_ffn",
    )(h, wg, wu, wd, resid)


MOE_TF = 512
MOE_NF = D_FF // MOE_TF
MOE_ROWS_PER_STEP = 96
MOE_TM = MOE_NF * MOE_ROWS_PER_STEP
MOE_ROW_CHUNK = 96
MOE_N_TILES = -(-SEQ * TOP_K // MOE_TM) + N_EXPERTS
MOE_THIRDS = 3
MOE_THIRD_ROWS = MOE_TM // MOE_THIRDS


def _tile_row_copy(tok_ref, h_ref, xg_ref, sems, tile, slot, r):
    row = tok_ref[tile * MOE_TM + r]
    return pltpu.make_async_copy(h_ref.at[pl.ds(row, 1), :], xg_ref.at[slot, pl.ds(r, 1), :],
                                 sems.at[slot])


def _wait_tile_rows(h_ref, xg_ref, sems, slot):
    pltpu.make_async_copy(h_ref.at[pl.ds(0, MOE_TM), :], xg_ref.at[slot], sems.at[slot]).wait()


def _moe_expert_kernel(be_ref, nu_ref, nt_ref, tok_ref, h_ref, wg_ref, wu_ref, wd_ref, o_ref,
                       xg_ref, xb_ref, acc_ref, sems):
    i = pl.program_id(0)
    f = pl.program_id(1)
    n_used = nu_ref[0]
    active = i < n_used
    slot = i % 2
    n_chunks = MOE_TM // MOE_ROW_CHUNK

    @pl.when(jnp.logical_and(i == 0, f == 0))
    def _():
        def start_row(r, carry):
            _tile_row_copy(tok_ref, h_ref, xg_ref, sems, 0, 0, r).start()
            return carry
        lax.fori_loop(0, MOE_TM, start_row, 0, unroll=8)

    @pl.when(jnp.logical_and(active, f == 0))
    def _():
        _wait_tile_rows(h_ref, xg_ref, sems, slot)
        acc_ref[...] = jnp.zeros(acc_ref.shape, acc_ref.dtype)

        def unpack_rows(c, carry):
            rows = pl.ds(pl.multiple_of(c * MOE_ROW_CHUNK, MOE_ROW_CHUNK), MOE_ROW_CHUNK)
            lo, hi = _unpack_halves(xg_ref[slot, rows, :])
            xb_ref[rows, :HALF_D] = lo.astype(BF16)
            xb_ref[rows, HALF_D:] = hi.astype(BF16)
            return carry
        lax.fori_loop(0, n_chunks, unpack_rows, 0)

    for thirds in range(1, MOE_THIRDS + 1):
        @pl.when(jnp.logical_and(active, nt_ref[i] == thirds))
        def _(thirds=thirds):
            for j in range(MOE_ROWS_PER_STEP):
                _tile_row_copy(tok_ref, h_ref, xg_ref, sems, i + 1, 1 - slot,
                               f * MOE_ROWS_PER_STEP + j).start()
            _swiglu_step(xb_ref, wg_ref, wu_ref, wd_ref, acc_ref, rows=thirds * MOE_THIRD_ROWS)

    @pl.when(jnp.logical_and(i == n_used, f == 0))
    def _():
        _wait_tile_rows(h_ref, xg_ref, sems, slot)

    @pl.when(f == MOE_NF - 1)
    def _():
        @pl.when(active)
        def _():
            def finish_rows(c, carry):
                rows = pl.ds(pl.multiple_of(c * MOE_ROW_CHUNK, MOE_ROW_CHUNK), MOE_ROW_CHUNK)
                o_ref[rows, :] = _pack_halves(acc_ref[rows, :])
                return carry
            lax.fori_loop(0, n_chunks, finish_rows, 0)

        @pl.when(jnp.logical_not(active))
        def _():
            o_ref[...] = jnp.zeros(o_ref.shape, o_ref.dtype)


def _moe_experts(h_packed, tok, blk_e, n_used, n_thirds, wg, wu, wd):
    d, tf = D_MODEL, MOE_TF

    def f_idx(i, f, nu):
        return jnp.where(i < nu[0], f, MOE_NF - 1)

    def w_gu(i, f, be, nu, nt, tok):
        return (be[i], 0, f_idx(i, f, nu))

    def w_d(i, f, be, nu, nt, tok):
        return (be[i], f_idx(i, f, nu), 0)

    grid_spec = pltpu.PrefetchScalarGridSpec(
        num_scalar_prefetch=4,
        grid=(MOE_N_TILES, MOE_NF),
        in_specs=[pl.BlockSpec(memory_space=pl.ANY),
                  pl.BlockSpec((1, d, tf), w_gu),
                  pl.BlockSpec((1, d, tf), w_gu),
                  pl.BlockSpec((1, tf, d), w_d)],
        out_specs=pl.BlockSpec((MOE_TM, HALF_D), lambda i, f, be, nu, nt, tok: (i, 0)),
        scratch_shapes=[pltpu.VMEM((2, MOE_TM, HALF_D), jnp.uint32),
                        pltpu.VMEM((MOE_TM, d), BF16),
                        pltpu.VMEM((MOE_TM, d), F32),
                        pltpu.SemaphoreType.DMA((2,))],
    )
    return pl.pallas_call(
        _moe_expert_kernel,
        grid_spec=grid_spec,
        out_shape=jax.ShapeDtypeStruct((MOE_N_TILES * MOE_TM, HALF_D), jnp.uint32),
        compiler_params=_params(("arbitrary", "arbitrary"), 60),
        name="moe_experts",
    )(blk_e, n_used, n_thirds, tok, h_packed, wg, wu, wd)


def _start_row_gather(idx_ref, idx_base, src_ref, dst_ref, sem, n_rows):
    def body(r, carry):
        row = idx_ref[idx_base + r]
        pltpu.make_async_copy(src_ref.at[pl.ds(row, 1), :], dst_ref.at[pl.ds(r, 1), :], sem).start()
        return carry
    lax.fori_loop(0, n_rows, body, 0, unroll=8)


def _wait_row_gather(src_ref, dst_ref, sem, n_rows):
    pltpu.make_async_copy(src_ref.at[pl.ds(0, n_rows), :], dst_ref, sem).wait()


def _moe_combine_kernel(pos_ref, y_ref, x_ref, gate_ref, o_ref, buf_ref, sems, *, tokens):
    i = pl.program_id(0)
    slot = i % 2
    rows = TOP_K * tokens

    @pl.when(i == 0)
    def _():
        _start_row_gather(pos_ref, 0, y_ref, buf_ref.at[0], sems.at[0], rows)

    @pl.when(i + 1 < pl.num_programs(0))
    def _():
        _start_row_gather(pos_ref, (i + 1) * rows, y_ref, buf_ref.at[1 - slot],
                          sems.at[1 - slot], rows)

    _wait_row_gather(y_ref, buf_ref.at[slot], sems.at[slot], rows)
    g0 = gate_ref[:, 0:1]
    g1 = gate_ref[:, 1:2]
    lo0, hi0 = _unpack_halves(buf_ref[slot, :tokens, :])
    lo1, hi1 = _unpack_halves(buf_ref[slot, tokens:, :])
    o_ref[:, :HALF_D] = x_ref[:, :HALF_D] + (g0 * lo0 + g1 * lo1)
    o_ref[:, HALF_D:] = x_ref[:, HALF_D:] + (g0 * hi0 + g1 * hi1)


def _moe_combine(y_packed, pos, x, gate):
    tokens = 512
    n = x.shape[0]
    steps = n // tokens
    pos_tiled = pos.reshape(steps, tokens, TOP_K).transpose(0, 2, 1).reshape(-1)
    grid_spec = pltpu.PrefetchScalarGridSpec(
        num_scalar_prefetch=1,
        grid=(steps,),
        in_specs=[pl.BlockSpec(memory_space=pl.ANY),
                  pl.BlockSpec((tokens, D_MODEL), lambda i, pos_ref: (i, 0)),
                  pl.BlockSpec((tokens, ROUTER_LANES), lambda i, pos_ref: (i, 0))],
        out_specs=pl.BlockSpec((tokens, D_MODEL), lambda i, pos_ref: (i, 0)),
        scratch_shapes=[pltpu.VMEM((2, TOP_K * tokens, HALF_D), jnp.uint32),
                        pltpu.SemaphoreType.DMA((2,))],
    )
    return pl.pallas_call(
        functools.partial(_moe_combine_kernel, tokens=tokens),
        grid_spec=grid_spec,
        out_shape=jax.ShapeDtypeStruct((n, D_MODEL), F32),
        compiler_params=_params(("arbitrary",), 40),
        name="moe_combine",
    )(pos_tiled, y_packed, x, gate)


def _moe(x_new, h_packed, idx, gate, wg, wu, wd):
    n = SEQ
    n_assign = n * TOP_K
    cap = MOE_N_TILES * MOE_TM
    e_flat = idx[:, :TOP_K].reshape(-1)
    onehot = (e_flat[:, None] == jnp.arange(N_EXPERTS, dtype=jnp.int32)[None, :]).astype(jnp.int32)
    csum = jnp.cumsum(onehot, axis=0)
    counts = csum[-1]
    rank = jnp.sum((csum - onehot) * onehot, axis=1)
    padded = (counts + MOE_TM - 1) // MOE_TM * MOE_TM
    pad_ends = jnp.cumsum(padded)
    pad_starts = pad_ends - padded
    dest = jnp.sum(onehot * pad_starts[None, :], axis=1) + rank
    tok = jnp.zeros((cap,), jnp.int32).at[dest].set(
        jnp.arange(n_assign, dtype=jnp.int32) // TOP_K)
    n_used = (pad_ends[-1] // MOE_TM).astype(jnp.int32).reshape(1)
    tile_start = jnp.arange(MOE_N_TILES, dtype=jnp.int32) * MOE_TM
    tile_start = jnp.minimum(tile_start, pad_ends[-1] - 1)
    blk_e = jnp.sum((tile_start[:, None] >= pad_ends[None, :]).astype(jnp.int32), axis=1)
    blk_e = jnp.clip(blk_e, 0, N_EXPERTS - 1)
    real_rows = counts[blk_e] - (tile_start - pad_starts[blk_e])
    n_thirds = jnp.clip((real_rows + MOE_THIRD_ROWS - 1) // MOE_THIRD_ROWS, 1, MOE_THIRDS)
    yb = _moe_experts(h_packed, tok, blk_e, n_used, n_thirds.astype(jnp.int32), wg, wu, wd)
    return _moe_combine(yb, dest.reshape(n, TOP_K), x_new, gate)


def kernel(x, mem, ln_mix_g, w_in, na_q_g, na_k_g, na_rpb, mem_ln_g, w_mem_kv, mem_q_g, mem_k_g,
           w_fourier_out, w_na_out, w_mem_out, w_o, ln_ffn_g, ffn_w_gate, ffn_w_up, ffn_w_down,
           moe_router, moe_w_gate, moe_w_up, moe_w_down):
    assert x.shape == (1, SEQ, D_MODEL) and mem.shape == (1, N_MEM, D_MODEL)
    xs = x.reshape(SEQ, D_MODEL)
    mem2 = mem.reshape(N_MEM, D_MODEL)
    tables = _dft_tables()
    for l in range(DEPTH):
        proj = _norm_mm(xs, ln_mix_g[l], w_in, l, tm=1024, tn=1024, out_dtype=BF16,
                        vmem_mib=56, name="in_proj")
        fm = _fourier_mix(proj, tables)
        o_na = _na_attention(proj, na_q_g[l], na_k_g[l], _na_col_blocks(na_rpb[l]))
        kv = _norm_mm(mem2, mem_ln_g[l], w_mem_kv, l, tm=N_MEM, tn=2 * MEM_W, out_dtype=F32,
                      vmem_mib=40, name="mem_kv_proj")
        o_mem = _mem_attention(proj, kv, mem_q_g[l], mem_k_g[l])
        merged = _merge(fm, o_na, o_mem, proj, w_fourier_out[l].astype(BF16),
                        w_na_out[l].astype(BF16), w_mem_out[l].astype(BF16))
        i = l // 2
        if l % 2 == 0:
            x_new, h2 = _out_proj(merged, w_o[l].astype(BF16), xs, ln_ffn_g[l])
            xs = _dense_ffn(h2, ffn_w_gate.astype(BF16), ffn_w_up.astype(BF16),
                            ffn_w_down.astype(BF16), i, x_new)
        else:
            x_new, h2, idx, gate = _out_proj(merged, w_o[l].astype(BF16), xs, ln_ffn_g[l],
                                             moe_router[i])
            xs = _moe(x_new, h2, idx, gate, moe_w_gate[i], moe_w_up[i], moe_w_down[i])
    return xs.reshape(1, SEQ, D_MODEL)
```

```python
import functools
import math

import numpy as np
import jax
import jax.numpy as jnp
from jax import lax
from jax.experimental import pallas as pl
from jax.experimental.pallas import tpu as pltpu

F32 = jnp.float32
BF16 = jnp.bfloat16

D_MODEL = 2048
SEQ = 16384
DEPTH = 2
GRID_W = 64
ROWS = SEQ // GRID_W
HEAD_DIM = 128
N_FOURIER_GROUPS = 4
FOURIER_W = N_FOURIER_GROUPS * HEAD_DIM
N_NA_HEADS = 8
NA_W = N_NA_HEADS * HEAD_DIM
WIN_H = 8
WIN_W = 16
N_MEM = 256
N_MEM_HEADS = 4
MEM_W = N_MEM_HEADS * HEAD_DIM
IN_W = FOURIER_W + 3 * NA_W + MEM_W + 3 * D_MODEL
D_FF = 5632
N_EXPERTS = 8
TOP_K = 2
EPS = 1e-6
HALF_D = D_MODEL // 2

Q_NA_OFF = FOURIER_W
K_NA_OFF = Q_NA_OFF + NA_W
V_NA_OFF = K_NA_OFF + NA_W
Q_MEM_OFF = V_NA_OFF + NA_W
GATE_OFF = Q_MEM_OFF + MEM_W

MIB = 1024 * 1024
NEG_BIG = -1e30

FFT_A = 128
FFT_B = SEQ // FFT_A


def _params(semantics, vmem_mib):
    return pltpu.CompilerParams(dimension_semantics=semantics,
                                vmem_limit_bytes=int(vmem_mib * MIB))


def _rms_rows(x, gain):
    ms = jnp.mean(x * x, axis=-1, keepdims=True)
    return x * lax.rsqrt(ms + EPS) * gain


def _bf16_bits_rtne(v):
    bits = pltpu.bitcast(v, jnp.uint32)
    lsb = lax.shift_right_logical(bits, jnp.uint32(16)) & jnp.uint32(1)
    return bits + jnp.uint32(0x7FFF) + lsb


def _pack_halves(v):
    half = v.shape[1] // 2
    lo = lax.shift_right_logical(_bf16_bits_rtne(v[:, :half]), jnp.uint32(16))
    hi = _bf16_bits_rtne(v[:, half:]) & jnp.uint32(0xFFFF0000)
    return lo | hi


def _unpack_halves(w):
    lo = pltpu.bitcast(lax.shift_left(w, jnp.uint32(16)), F32)
    hi = pltpu.bitcast(w & jnp.uint32(0xFFFF0000), F32)
    return lo, hi


def _norm_mm_kernel(x_ref, g_ref, w_ref, o_ref, h_ref, *, row_chunk):
    @pl.when(pl.program_id(1) == 0)
    def _():
        def body(c, carry):
            rows = pl.ds(pl.multiple_of(c * row_chunk, row_chunk), row_chunk)
            h_ref[rows, :] = _rms_rows(x_ref[rows, :].astype(F32), g_ref[...]).astype(BF16)
            return carry
        lax.fori_loop(0, h_ref.shape[0] // row_chunk, body, 0)

    o_ref[...] = jnp.dot(h_ref[...], w_ref[0].astype(BF16),
                         preferred_element_type=F32).astype(o_ref.dtype)


def _norm_mm(x, g, w, layer, *, tm, tn, out_dtype, vmem_mib, name):
    m, k = x.shape
    n = w.shape[2]
    row_chunk = min(tm, 128)
    return pl.pallas_call(
        functools.partial(_norm_mm_kernel, row_chunk=row_chunk),
        grid=(m // tm, n // tn),
        in_specs=[pl.BlockSpec((tm, k), lambda i, j: (i, 0)),
                  pl.BlockSpec((1, k), lambda i, j: (0, 0)),
                  pl.BlockSpec((1, k, tn), lambda i, j: (layer, 0, j))],
        out_specs=pl.BlockSpec((tm, tn), lambda i, j: (i, j)),
        out_shape=jax.ShapeDtypeStruct((m, n), out_dtype),
        scratch_shapes=[pltpu.VMEM((tm, k), BF16)],
        compiler_params=_params(("parallel", "arbitrary"), vmem_mib),
        name=name,
    )(x, g.reshape(1, k), w)


def _dft_tables():
    a = np.arange(FFT_A)
    ang1 = 2.0 * np.pi * np.outer(a, a) / FFT_A
    w1 = np.concatenate([np.cos(ang1), -np.sin(ang1)], axis=0)
    k1 = np.arange(FFT_A)[:, None, None]
    k2 = np.arange(FFT_B)[None, :, None]
    s2 = np.arange(FFT_B)[None, None, :]
    ang2 = 2.0 * np.pi * ((s2 * (k1 + FFT_A * k2)) % SEQ) / SEQ
    g2 = np.concatenate([np.cos(ang2), np.sin(ang2)], axis=2)
    c = np.arange(HEAD_DIM)
    angc = 2.0 * np.pi * np.outer(c, c) / HEAD_DIM
    eye = np.eye(N_FOURIER_GROUPS)
    bd = np.concatenate([np.kron(eye, np.cos(angc)), np.kron(eye, np.sin(angc))], axis=0)
    return (jnp.asarray(w1, dtype=BF16), jnp.asarray(g2, dtype=BF16),
            jnp.asarray(bd, dtype=BF16))


FFT_BLK = 16


def _dft1_kernel(w_ref, x_ref, br_ref, bi_ref):
    xt = pltpu.einshape("abc->bac", x_ref[...])
    for j in range(FFT_BLK):
        r = jnp.dot(w_ref[...], xt[j], preferred_element_type=F32)
        br_ref[j] = r[:FFT_A].astype(BF16)
        bi_ref[j] = r[FFT_A:].astype(BF16)


def _dft2_kernel(br_ref, bi_ref, g_ref, bd_ref, o_ref, *, norm):
    brt = pltpu.einshape("abc->bac", br_ref[...])
    bit = pltpu.einshape("abc->bac", bi_ref[...])
    ys = []
    for i in range(FFT_BLK):
        br = brt[i]
        bi = bit[i]
        g = g_ref[i]
        zr = jnp.dot(g, jnp.concatenate([br, bi], axis=0), preferred_element_type=F32)
        zi = jnp.dot(g, jnp.concatenate([bi, -br], axis=0), preferred_element_type=F32)
        z = jnp.concatenate([zr, zi], axis=1).astype(BF16)
        y = jnp.dot(z, bd_ref[...], preferred_element_type=F32) * norm
        ys.append(y.astype(o_ref.dtype))
    o_ref[...] = pltpu.einshape("abc->bac", jnp.stack(ys, axis=0))


def _fourier_mix(proj, tables):
    w1, g2, bd = tables
    x3 = proj.reshape(FFT_A, FFT_B, IN_W)
    blk = (FFT_A, FFT_BLK, FOURIER_W)
    br, bi = pl.pallas_call(
        _dft1_kernel,
        grid=(FFT_B // FFT_BLK,),
        in_specs=[pl.BlockSpec((2 * FFT_A, FFT_A), lambda j: (0, 0)),
                  pl.BlockSpec(blk, lambda j: (0, j, 0))],
        out_specs=[pl.BlockSpec((FFT_BLK, FFT_A, FOURIER_W), lambda j: (j, 0, 0)),
                   pl.BlockSpec((FFT_BLK, FFT_A, FOURIER_W), lambda j: (j, 0, 0))],
        out_shape=[jax.ShapeDtypeStruct((FFT_B, FFT_A, FOURIER_W), BF16)] * 2,
        compiler_params=_params(("parallel",), 40),
        name="fourier_stage1",
    )(w1, x3)
    norm = 1.0 / math.sqrt(SEQ * HEAD_DIM)
    y3 = pl.pallas_call(
        functools.partial(_dft2_kernel, norm=norm),
        grid=(FFT_A // FFT_BLK,),
        in_specs=[pl.BlockSpec((FFT_B, FFT_BLK, FOURIER_W), lambda j: (0, j, 0)),
                  pl.BlockSpec((FFT_B, FFT_BLK, FOURIER_W), lambda j: (0, j, 0)),
                  pl.BlockSpec((FFT_BLK, FFT_B, 2 * FFT_B), lambda j: (j, 0, 0)),
                  pl.BlockSpec((2 * FOURIER_W, FOURIER_W), lambda j: (0, 0))],
        out_specs=pl.BlockSpec((FFT_B, FFT_BLK, FOURIER_W), lambda j: (0, j, 0)),
        out_shape=jax.ShapeDtypeStruct((FFT_B, FFT_A, FOURIER_W), BF16),
        compiler_params=_params(("parallel",), 40),
        name="fourier_stage2",
    )(br, bi, g2, bd)
    return y3.reshape(SEQ, FOURIER_W)


NA_SUB_ROWS = 4
NA_KEY_ROWS = NA_SUB_ROWS + WIN_H
NA_SUBS_PER_STEP = 4
NA_N_SUB = ROWS // NA_SUB_ROWS
NA_MAX_KEY_START = ROWS - NA_KEY_ROWS
NA_N_ROW_OFFSETS = 2 * WIN_H - 1
NA_MASKED_BLOCK = NA_N_ROW_OFFSETS


def _na_row_offsets():
    out = np.full((3, NA_SUB_ROWS, NA_KEY_ROWS), NA_MASKED_BLOCK, np.int32)
    for case, r0 in enumerate((0, 2 * NA_SUB_ROWS, ROWS - NA_SUB_ROWS)):
        ks = int(np.clip(r0 - WIN_H // 2, 0, NA_MAX_KEY_START))
        for qi in range(NA_SUB_ROWS):
            r = r0 + qi
            r_start = int(np.clip(r - WIN_H // 2, 0, ROWS - WIN_H))
            for kj in range(NA_KEY_ROWS):
                if r_start <= ks + kj < r_start + WIN_H:
                    out[case, qi, kj] = ks + kj - r + (WIN_H - 1)
    return out


def _na_col_blocks(rpb):
    qc = np.arange(GRID_W)[:, None]
    kc = np.arange(GRID_W)[None, :]
    col_start = np.clip(qc - WIN_W // 2, 0, GRID_W - WIN_W)
    col_valid = (kc >= col_start) & (kc < col_start + WIN_W)
    dc = kc - qc + (WIN_W - 1)
    sel = (np.arange(2 * WIN_W - 1)[:, None, None] == dc[None]) & col_valid[None]
    sel = jnp.asarray(sel.astype(np.float32))
    t = jnp.sum(rpb.astype(F32)[:, :, :, None, None] * sel[None, None], axis=2)
    t = jnp.where(col_valid[None, None], t, NEG_BIG)
    masked = jnp.full((N_NA_HEADS, 1, GRID_W, GRID_W), NEG_BIG, F32)
    t = jnp.concatenate([t, masked], axis=1)
    return jnp.concatenate([t, t], axis=-1)


def _na_kernel(q_ref, k_ref, v_ref, qg_ref, kg_ref, cb_ref, o_ref, kn_ref, bias_ref):
    step = pl.program_id(1)
    nq = NA_SUB_ROWS * GRID_W
    nk = NA_KEY_ROWS * GRID_W

    @pl.when(step == 0)
    def _():
        chunk = 1024

        def body(c, carry):
            rows = pl.ds(pl.multiple_of(c * chunk, chunk), chunk)
            kn_ref[rows, :] = _rms_rows(k_ref[rows, :].astype(F32), kg_ref[...]).astype(BF16)
            return carry
        lax.fori_loop(0, SEQ // chunk, body, 0)

        row_off = _na_row_offsets()
        for case in range(3):
            for qi in range(NA_SUB_ROWS):
                for kj in range(NA_KEY_ROWS):
                    lanes = slice(kj * GRID_W, (kj + 1) * GRID_W)
                    src = slice((kj % 2) * GRID_W, (kj % 2 + 1) * GRID_W)
                    bias_ref[case, qi * GRID_W:(qi + 1) * GRID_W, lanes] = (
                        cb_ref[0, int(row_off[case, qi, kj]), :, src])

    scale = HEAD_DIM ** -0.5
    for sb in range(NA_SUBS_PER_STEP):
        sub = step * NA_SUBS_PER_STEP + sb
        key_row0 = jnp.clip(sub * NA_SUB_ROWS - WIN_H // 2, 0, NA_MAX_KEY_START)
        case = jnp.where(sub == 0, 0, jnp.where(sub == NA_N_SUB - 1, 2, 1))
        q = q_ref[sb * nq:(sb + 1) * nq, :].astype(F32)
        qn = (_rms_rows(q, qg_ref[...]) * scale).astype(BF16)
        kstart = pl.multiple_of(key_row0 * GRID_W, GRID_W)
        kw = kn_ref[pl.ds(kstart, nk), :]
        vw = v_ref[pl.ds(kstart, nk), :]
        s = lax.dot_general(qn, kw, (((1,), (1,)), ((), ())), preferred_element_type=F32)
        s = s + bias_ref[case]
        m = jnp.max(s, axis=-1, keepdims=True)
        p = jnp.exp(s - m)
        l = jnp.sum(p, axis=-1, keepdims=True)
        o = jnp.dot(p.astype(BF16), vw, preferred_element_type=F32) / l
        o_ref[sb * nq:(sb + 1) * nq, :] = o.astype(o_ref.dtype)


def _na_attention(proj, q_g, k_g, col_blocks):
    nq = NA_SUB_ROWS * GRID_W
    nk = NA_KEY_ROWS * GRID_W
    tq = NA_SUBS_PER_STEP * nq
    n_blk = NA_N_ROW_OFFSETS + 1
    qb, kb_, vb = Q_NA_OFF // HEAD_DIM, K_NA_OFF // HEAD_DIM, V_NA_OFF // HEAD_DIM
    return pl.pallas_call(
        _na_kernel,
        grid=(N_NA_HEADS, SEQ // tq),
        in_specs=[pl.BlockSpec((tq, HEAD_DIM), lambda h, i: (i, qb + h)),
                  pl.BlockSpec((SEQ, HEAD_DIM), lambda h, i: (0, kb_ + h)),
                  pl.BlockSpec((SEQ, HEAD_DIM), lambda h, i: (0, vb + h)),
                  pl.BlockSpec((1, HEAD_DIM), lambda h, i: (0, 0)),
                  pl.BlockSpec((1, HEAD_DIM), lambda h, i: (0, 0)),
                  pl.BlockSpec((1, n_blk, GRID_W, 2 * GRID_W), lambda h, i: (h, 0, 0, 0))],
        out_specs=pl.BlockSpec((tq, HEAD_DIM), lambda h, i: (i, h)),
        out_shape=jax.ShapeDtypeStruct((SEQ, NA_W), BF16),
        scratch_shapes=[pltpu.VMEM((SEQ, HEAD_DIM), BF16),
                        pltpu.VMEM((3, nq, nk), F32)],
        compiler_params=_params(("parallel", "arbitrary"), 48),
        name="na_attention",
    )(proj, proj, proj, q_g.reshape(1, HEAD_DIM), k_g.reshape(1, HEAD_DIM), col_blocks)


def _mem_attn_kernel(q_ref, kv_ref, qg_ref, kg_ref, o_ref):
    scale = HEAD_DIM ** -0.5
    for h in range(N_MEM_HEADS):
        cols = slice(h * HEAD_DIM, (h + 1) * HEAD_DIM)
        qn = (_rms_rows(q_ref[:, cols].astype(F32), qg_ref[...]) * scale).astype(BF16)
        kn = _rms_rows(kv_ref[:, cols].astype(F32), kg_ref[...]).astype(BF16)
        v = kv_ref[:, MEM_W + h * HEAD_DIM:MEM_W + (h + 1) * HEAD_DIM].astype(BF16)
        s = lax.dot_general(qn, kn, (((1,), (1,)), ((), ())), preferred_element_type=F32)
        m = jnp.max(s, axis=-1, keepdims=True)
        p = jnp.exp(s - m)
        l = jnp.sum(p, axis=-1, keepdims=True)
        o = jnp.dot(p.astype(BF16), v, preferred_element_type=F32) / l
        o_ref[:, cols] = o.astype(o_ref.dtype)


def _mem_attention(proj, kv, q_g, k_g):
    tm = 512
    return pl.pallas_call(
        _mem_attn_kernel,
        grid=(SEQ // tm,),
        in_specs=[pl.BlockSpec((tm, MEM_W), lambda i: (i, Q_MEM_OFF // MEM_W)),
                  pl.BlockSpec((N_MEM, 2 * MEM_W), lambda i: (0, 0)),
                  pl.BlockSpec((1, HEAD_DIM), lambda i: (0, 0)),
                  pl.BlockSpec((1, HEAD_DIM), lambda i: (0, 0))],
        out_specs=pl.BlockSpec((tm, MEM_W), lambda i: (i, 0)),
        out_shape=jax.ShapeDtypeStruct((SEQ, MEM_W), BF16),
        compiler_params=_params(("parallel",), 32),
        name="mem_attention",
    )(proj, kv, q_g.reshape(1, HEAD_DIM), k_g.reshape(1, HEAD_DIM))


def _merge_kernel(fm_ref, na_ref, mo_ref, g0_ref, g1_ref, g2_ref, wf_ref, wn_ref, wm_ref, o_ref,
                  *, col_chunk):
    for c in range(D_MODEL // col_chunk):
        cols = slice(c * col_chunk, (c + 1) * col_chunk)
        o_f = jnp.dot(fm_ref[...], wf_ref[:, cols], preferred_element_type=F32)
        o_n = jnp.dot(na_ref[...], wn_ref[:, cols], preferred_element_type=F32)
        o_m = jnp.dot(mo_ref[...], wm_ref[:, cols], preferred_element_type=F32)
        acc = jax.nn.sigmoid(g0_ref[:, cols].astype(F32)) * o_f
        acc = acc + jax.nn.sigmoid(g1_ref[:, cols].astype(F32)) * o_n
        acc = acc + jax.nn.sigmoid(g2_ref[:, cols].astype(F32)) * o_m
        o_ref[:, cols] = acc.astype(o_ref.dtype)


def _merge(fm, o_na, o_mem, proj, wf, wn, wm):
    tm = 512
    gb = GATE_OFF // D_MODEL
    return pl.pallas_call(
        functools.partial(_merge_kernel, col_chunk=512),
        grid=(SEQ // tm,),
        in_specs=[pl.BlockSpec((tm, FOURIER_W), lambda i: (i, 0)),
                  pl.BlockSpec((tm, NA_W), lambda i: (i, 0)),
                  pl.BlockSpec((tm, MEM_W), lambda i: (i, 0)),
                  pl.BlockSpec((tm, D_MODEL), lambda i: (i, gb)),
                  pl.BlockSpec((tm, D_MODEL), lambda i: (i, gb + 1)),
                  pl.BlockSpec((tm, D_MODEL), lambda i: (i, gb + 2)),
                  pl.BlockSpec((FOURIER_W, D_MODEL), lambda i: (0, 0)),
                  pl.BlockSpec((NA_W, D_MODEL), lambda i: (0, 0)),
                  pl.BlockSpec((MEM_W, D_MODEL), lambda i: (0, 0))],
        out_specs=pl.BlockSpec((tm, D_MODEL), lambda i: (i, 0)),
        out_shape=jax.ShapeDtypeStruct((SEQ, D_MODEL), BF16),
        compiler_params=_params(("parallel",), 48),
        name="gated_merge",
    )(fm, o_na, o_mem, proj, proj, proj, wf, wn, wm)


ROUTER_LANES = 128


def _route_top2(h, wr_ref):
    h_hi = h.astype(BF16)
    h_lo = (h - h_hi.astype(F32)).astype(BF16)
    r_hi = jnp.dot(h_hi, wr_ref[...], preferred_element_type=F32)
    r_lo = jnp.dot(h_lo, wr_ref[:, :ROUTER_LANES], preferred_element_type=F32)
    logits = r_hi[:, :ROUTER_LANES] + r_hi[:, ROUTER_LANES:] + r_lo
    lane = lax.broadcasted_iota(jnp.int32, logits.shape, 1).astype(F32)
    logits = jnp.where(lane < N_EXPERTS, logits, -jnp.inf)
    m1 = jnp.max(logits, axis=-1, keepdims=True)
    i1 = jnp.min(jnp.where(logits == m1, lane, float(ROUTER_LANES)), axis=-1, keepdims=True)
    rest = jnp.where(lane == i1, -jnp.inf, logits)
    m2 = jnp.max(rest, axis=-1, keepdims=True)
    i2 = jnp.min(jnp.where(rest == m2, lane, float(ROUTER_LANES)), axis=-1, keepdims=True)
    e21 = jnp.exp(m2 - m1)
    g1 = 1.0 / (1.0 + e21)
    g2 = e21 * g1
    return jnp.where(lane == 0.0, i1, i2).astype(jnp.int32), jnp.where(lane == 0.0, g1, g2)


def _oproj_kernel(a_ref, w_ref, x_ref, g_ref, *rest, row_chunk, route):
    if route:
        wr_ref, xo_ref, h_ref, idx_ref, gate_ref = rest
    else:
        xo_ref, h_ref = rest
    xo_ref[...] = jnp.dot(a_ref[...], w_ref[...], preferred_element_type=F32) + x_ref[...]

    def body(c, carry):
        rows = pl.ds(pl.multiple_of(c * row_chunk, row_chunk), row_chunk)
        h = _rms_rows(xo_ref[rows, :], g_ref[...])
        if route:
            h_ref[rows, :] = _pack_halves(h)
            idx_ref[rows, :], gate_ref[rows, :] = _route_top2(h, wr_ref)
        else:
            h_ref[rows, :] = h.astype(h_ref.dtype)
        return carry
    lax.fori_loop(0, xo_ref.shape[0] // row_chunk, body, 0, unroll=route)


def _out_proj(merged, w_o, x, g, w_router=None):
    tm = 512
    route = w_router is not None
    in_specs = [pl.BlockSpec((tm, D_MODEL), lambda i: (i, 0)),
                pl.BlockSpec((D_MODEL, D_MODEL), lambda i: (0, 0)),
                pl.BlockSpec((tm, D_MODEL), lambda i: (i, 0)),
                pl.BlockSpec((1, D_MODEL), lambda i: (0, 0))]
    args = [merged, w_o, x, g.reshape(1, D_MODEL)]
    out_specs = [pl.BlockSpec((tm, D_MODEL), lambda i: (i, 0))]
    out_shape = [jax.ShapeDtypeStruct((SEQ, D_MODEL), F32)]
    if route:
        w_pad = jnp.zeros((D_MODEL, ROUTER_LANES), F32).at[:, :N_EXPERTS].set(w_router.astype(F32))
        w_hi = w_pad.astype(BF16)
        w_lo = (w_pad - w_hi.astype(F32)).astype(BF16)
        in_specs.append(pl.BlockSpec((D_MODEL, 2 * ROUTER_LANES), lambda i: (0, 0)))
        args.append(jnp.concatenate([w_hi, w_lo], axis=1))
        out_specs += [pl.BlockSpec((tm, HALF_D), lambda i: (i, 0)),
                      pl.BlockSpec((tm, ROUTER_LANES), lambda i: (i, 0)),
                      pl.BlockSpec((tm, ROUTER_LANES), lambda i: (i, 0))]
        out_shape += [jax.ShapeDtypeStruct((SEQ, HALF_D), jnp.uint32),
                      jax.ShapeDtypeStruct((SEQ, ROUTER_LANES), jnp.int32),
                      jax.ShapeDtypeStruct((SEQ, ROUTER_LANES), F32)]
    else:
        out_specs.append(pl.BlockSpec((tm, D_MODEL), lambda i: (i, 0)))
        out_shape.append(jax.ShapeDtypeStruct((SEQ, D_MODEL), BF16))
    return pl.pallas_call(
        functools.partial(_oproj_kernel, row_chunk=128, route=route),
        grid=(SEQ // tm,),
        in_specs=in_specs,
        out_specs=out_specs,
        out_shape=out_shape,
        compiler_params=_params(("parallel",), 48),
        name="out_proj",
    )(*args)


def _swiglu_step(x_ref, wg_ref, wu_ref, wd_ref, acc_ref, rows=None):
    x = x_ref[:rows, :]
    g = jnp.dot(x, wg_ref[0].astype(BF16), preferred_element_type=F32)
    u = jnp.dot(x, wu_ref[0].astype(BF16), preferred_element_type=F32)
    a = (g * jax.nn.sigmoid(g) * u).astype(BF16)
    acc_ref[:rows, :] += jnp.dot(a, wd_ref[0].astype(BF16), preferred_element_type=F32)


def _dense_ffn_kernel(x_ref, wg_ref, wu_ref, wd_ref, resid_ref, o_ref):
    @pl.when(pl.program_id(1) == 0)
    def _():
        o_ref[...] = resid_ref[...]

    _swiglu_step(x_ref, wg_ref, wu_ref, wd_ref, o_ref)


def _dense_ffn(h, wg, wu, wd, layer, resid):
    tm, tf = 1024, 512
    m, d = h.shape
    nf = D_FF // tf
    return pl.pallas_call(
        _dense_ffn_kernel,
        grid=(m // tm, nf),
        in_specs=[pl.BlockSpec((tm, d), lambda i, f: (i, 0)),
                  pl.BlockSpec((1, d, tf), lambda i, f: (layer, 0, f)),
                  pl.BlockSpec((1, d, tf), lambda i, f: (layer, 0, f)),
                  pl.BlockSpec((1, tf, d), lambda i, f: (layer, f, 0)),
                  pl.BlockSpec((tm, d), lambda i, f: (i, 0))],
        out_specs=pl.BlockSpec((tm, d), lambda i, f: (i, 0)),
        out_shape=jax.ShapeDtypeStruct((m, d), F32),
        compiler_params=_params(("parallel", "arbitrary"), 60),
        name="dense_ffn",
    )(h, wg, wu, wd, resid)


MOE_TF = 512
MOE_NF = D_FF // MOE_TF
MOE_ROWS_PER_STEP = 96
MOE_TM = MOE_NF * MOE_ROWS_PER_STEP
MOE_ROW_CHUNK = 96
MOE_N_TILES = -(-SEQ * TOP_K // MOE_TM) + N_EXPERTS
MOE_THIRDS = 3
MOE_THIRD_ROWS = MOE_TM // MOE_THIRDS


def _tile_row_copy(tok_ref, h_ref, xg_ref, sems, tile, slot, r):
    row = tok_ref[tile * MOE_TM + r]
    return pltpu.make_async_copy(h_ref.at[pl.ds(row, 1), :], xg_ref.at[slot, pl.ds(r, 1), :],
                                 sems.at[slot])


def _wait_tile_rows(h_ref, xg_ref, sems, slot):
    pltpu.make_async_copy(h_ref.at[pl.ds(0, MOE_TM), :], xg_ref.at[slot], sems.at[slot]).wait()


def _moe_expert_kernel(be_ref, nu_ref, nt_ref, tok_ref, h_ref, wg_ref, wu_ref, wd_ref, o_ref,
                       xg_ref, xb_ref, acc_ref, sems):
    i = pl.program_id(0)
    f = pl.program_id(1)
    n_used = nu_ref[0]
    active = i < n_used
    slot = i % 2
    n_chunks = MOE_TM // MOE_ROW_CHUNK

    @pl.when(jnp.logical_and(i == 0, f == 0))
    def _():
        def start_row(r, carry):
            _tile_row_copy(tok_ref, h_ref, xg_ref, sems, 0, 0, r).start()
            return carry
        lax.fori_loop(0, MOE_TM, start_row, 0, unroll=8)

    @pl.when(jnp.logical_and(active, f == 0))
    def _():
        _wait_tile_rows(h_ref, xg_ref, sems, slot)
        acc_ref[...] = jnp.zeros(acc_ref.shape, acc_ref.dtype)

        def unpack_rows(c, carry):
            rows = pl.ds(pl.multiple_of(c * MOE_ROW_CHUNK, MOE_ROW_CHUNK), MOE_ROW_CHUNK)
            lo, hi = _unpack_halves(xg_ref[slot, rows, :])
            xb_ref[rows, :HALF_D] = lo.astype(BF16)
            xb_ref[rows, HALF_D:] = hi.astype(BF16)
            return carry
        lax.fori_loop(0, n_chunks, unpack_rows, 0)

    for thirds in range(1, MOE_THIRDS + 1):
        @pl.when(jnp.logical_and(active, nt_ref[i] == thirds))
        def _(thirds=thirds):
            for j in range(MOE_ROWS_PER_STEP):
                _tile_row_copy(tok_ref, h_ref, xg_ref, sems, i + 1, 1 - slot,
                               f * MOE_ROWS_PER_STEP + j).start()
            _swiglu_step(xb_ref, wg_ref, wu_ref, wd_ref, acc_ref, rows=thirds * MOE_THIRD_ROWS)

    @pl.when(jnp.logical_and(i == n_used, f == 0))
    def _():
        _wait_tile_rows(h_ref, xg_ref, sems, slot)

    @pl.when(f == MOE_NF - 1)
    def _():
        @pl.when(active)
        def _():
            def finish_rows(c, carry):
                rows = pl.ds(pl.multiple_of(c * MOE_ROW_CHUNK, MOE_ROW_CHUNK), MOE_ROW_CHUNK)
                o_ref[rows, :] = _pack_halves(acc_ref[rows, :])
                return carry
            lax.fori_loop(0, n_chunks, finish_rows, 0)

        @pl.when(jnp.logical_not(active))
        def _():
            o_ref[...] = jnp.zeros(o_ref.shape, o_ref.dtype)


def _moe_experts(h_packed, tok, blk_e, n_used, n_thirds, wg, wu, wd):
    d, tf = D_MODEL, MOE_TF

    def f_idx(i, f, nu):
        return jnp.where(i < nu[0], f, MOE_NF - 1)

    def w_gu(i, f, be, nu, nt, tok):
        return (be[i], 0, f_idx(i, f, nu))

    def w_d(i, f, be, nu, nt, tok):
        return (be[i], f_idx(i, f, nu), 0)

    grid_spec = pltpu.PrefetchScalarGridSpec(
        num_scalar_prefetch=4,
        grid=(MOE_N_TILES, MOE_NF),
        in_specs=[pl.BlockSpec(memory_space=pl.ANY),
                  pl.BlockSpec((1, d, tf), w_gu),
                  pl.BlockSpec((1, d, tf), w_gu),
                  pl.BlockSpec((1, tf, d), w_d)],
        out_specs=pl.BlockSpec((MOE_TM, HALF_D), lambda i, f, be, nu, nt, tok: (i, 0)),
        scratch_shapes=[pltpu.VMEM((2, MOE_TM, HALF_D), jnp.uint32),
                        pltpu.VMEM((MOE_TM, d), BF16),
                        pltpu.VMEM((MOE_TM, d), F32),
                        pltpu.SemaphoreType.DMA((2,))],
    )
    return pl.pallas_call(
        _moe_expert_kernel,
        grid_spec=grid_spec,
        out_shape=jax.ShapeDtypeStruct((MOE_N_TILES * MOE_TM, HALF_D), jnp.uint32),
        compiler_params=_params(("arbitrary", "arbitrary"), 60),
        name="moe_experts",
    )(blk_e, n_used, n_thirds, tok, h_packed, wg, wu, wd)


def _start_row_gather(idx_ref, idx_base, src_ref, dst_ref, sem, n_rows):
    def body(r, carry):
        row = idx_ref[idx_base + r]
        pltpu.make_async_copy(src_ref.at[pl.ds(row, 1), :], dst_ref.at[pl.ds(r, 1), :], sem).start()
        return carry
    lax.fori_loop(0, n_rows, body, 0, unroll=8)


def _wait_row_gather(src_ref, dst_ref, sem, n_rows):
    pltpu.make_async_copy(src_ref.at[pl.ds(0, n_rows), :], dst_ref, sem).wait()


def _moe_combine_kernel(pos_ref, y_ref, x_ref, gate_ref, o_ref, buf_ref, sems, *, tokens):
    i = pl.program_id(0)
    slot = i % 2
    rows = TOP_K * tokens

    @pl.when(i == 0)
    def _():
        _start_row_gather(pos_ref, 0, y_ref, buf_ref.at[0], sems.at[0], rows)

    @pl.when(i + 1 < pl.num_programs(0))
    def _():
        _start_row_gather(pos_ref, (i + 1) * rows, y_ref, buf_ref.at[1 - slot],
                          sems.at[1 - slot], rows)

    _wait_row_gather(y_ref, buf_ref.at[slot], sems.at[slot], rows)
    g0 = gate_ref[:, 0:1]
    g1 = gate_ref[:, 1:2]
    lo0, hi0 = _unpack_halves(buf_ref[slot, :tokens, :])
    lo1, hi1 = _unpack_halves(buf_ref[slot, tokens:, :])
    o_ref[:, :HALF_D] = x_ref[:, :HALF_D] + (g0 * lo0 + g1 * lo1)
    o_ref[:, HALF_D:] = x_ref[:, HALF_D:] + (g0 * hi0 + g1 * hi1)


def _moe_combine(y_packed, pos, x, gate):
    tokens = 512
    n = x.shape[0]
    steps = n // tokens
    pos_tiled = pos.reshape(steps, tokens, TOP_K).transpose(0, 2, 1).reshape(-1)
    grid_spec = pltpu.PrefetchScalarGridSpec(
        num_scalar_prefetch=1,
        grid=(steps,),
        in_specs=[pl.BlockSpec(memory_space=pl.ANY),
                  pl.BlockSpec((tokens, D_MODEL), lambda i, pos_ref: (i, 0)),
                  pl.BlockSpec((tokens, ROUTER_LANES), lambda i, pos_ref: (i, 0))],
        out_specs=pl.BlockSpec((tokens, D_MODEL), lambda i, pos_ref: (i, 0)),
        scratch_shapes=[pltpu.VMEM((2, TOP_K * tokens, HALF_D), jnp.uint32),
                        pltpu.SemaphoreType.DMA((2,))],
    )
    return pl.pallas_call(
        functools.partial(_moe_combine_kernel, tokens=tokens),
        grid_spec=grid_spec,
        out_shape=jax.ShapeDtypeStruct((n, D_MODEL), F32),
        compiler_params=_params(("arbitrary",), 40),
        name="moe_combine",
    )(pos_tiled, y_packed, x, gate)


def _moe(x_new, h_packed, idx, gate, wg, wu, wd):
    n = SEQ
    n_assign = n * TOP_K
    cap = MOE_N_TILES * MOE_TM
    e_flat = idx[:, :TOP_K].reshape(-1)
    onehot = (e_flat[:, None] == jnp.arange(N_EXPERTS, dtype=jnp.int32)[None, :]).astype(jnp.int32)
    csum = jnp.cumsum(onehot, axis=0)
    counts = csum[-1]
    rank = jnp.sum((csum - onehot) * onehot, axis=1)
    padded = (counts + MOE_TM - 1) // MOE_TM * MOE_TM
    pad_ends = jnp.cumsum(padded)
    pad_starts = pad_ends - padded
    dest = jnp.sum(onehot * pad_starts[None, :], axis=1) + rank
    tok = jnp.zeros((cap,), jnp.int32).at[dest].set(
        jnp.arange(n_assign, dtype=jnp.int32) // TOP_K)
    n_used = (pad_ends[-1] // MOE_TM).astype(jnp.int32).reshape(1)
    tile_start = jnp.arange(MOE_N_TILES, dtype=jnp.int32) * MOE_TM
    tile_start = jnp.minimum(tile_start, pad_ends[-1] - 1)
    blk_e = jnp.sum((tile_start[:, None] >= pad_ends[None, :]).astype(jnp.int32), axis=1)
    blk_e = jnp.clip(blk_e, 0, N_EXPERTS - 1)
    real_rows = counts[blk_e] - (tile_start - pad_starts[blk_e])
    n_thirds = jnp.clip((real_rows + MOE_THIRD_ROWS - 1) // MOE_THIRD_ROWS, 1, MOE_THIRDS)
    yb = _moe_experts(h_packed, tok, blk_e, n_used, n_thirds.astype(jnp.int32), wg, wu, wd)
    return _moe_combine(yb, dest.reshape(n, TOP_K), x_new, gate)


def kernel(x, mem, ln_mix_g, w_in, na_q_g, na_k_g, na_rpb, mem_ln_g, w_mem_kv, mem_q_g, mem_k_g,
           w_fourier_out, w_na_out, w_mem_out, w_o, ln_ffn_g, ffn_w_gate, ffn_w_up, ffn_w_down,
           moe_router, moe_w_gate, moe_w_up, moe_w_down):
    assert x.shape == (1, SEQ, D_MODEL) and mem.shape == (1, N_MEM, D_MODEL)
    xs = x.reshape(SEQ, D_MODEL)
    mem2 = mem.reshape(N_MEM, D_MODEL)
    tables = _dft_tables()
    w_in_bf16 = w_in.astype(BF16)
    for l in range(DEPTH):
        proj = _norm_mm(xs, ln_mix_g[l], w_in_bf16, l, tm=1024, tn=2048, out_dtype=BF16,
                        vmem_mib=58, name="in_proj")
        fm = _fourier_mix(proj, tables)
        o_na = _na_attention(proj, na_q_g[l], na_k_g[l], _na_col_blocks(na_rpb[l]))
        kv = _norm_mm(mem2, mem_ln_g[l], w_mem_kv, l, tm=N_MEM, tn=2 * MEM_W, out_dtype=F32,
                      vmem_mib=40, name="mem_kv_proj")
        o_mem = _mem_attention(proj, kv, mem_q_g[l], mem_k_g[l])
        merged = _merge(fm, o_na, o_mem, proj, w_fourier_out[l].astype(BF16),
                        w_na_out[l].astype(BF16), w_mem_out[l].astype(BF16))
        i = l // 2
        if l % 2 == 0:
            x_new, h2 = _out_proj(merged, w_o[l].astype(BF16), xs, ln_ffn_g[l])
            xs = _dense_ffn(h2, ffn_w_gate.astype(BF16), ffn_w_up.astype(BF16),
                            ffn_w_down.astype(BF16), i, x_new)
        else:
            x_new, h2, idx, gate = _out_proj(merged, w_o[l].astype(BF16), xs, ln_ffn_g[l],
                                             moe_router[i])
            xs = _moe(x_new, h2, idx, gate, moe_w_gate[i], moe_w_up[i], moe_w_down[i])
    return xs.reshape(1, SEQ, D_MODEL)
```

```python
import functools
import math

import numpy as np
import jax
import jax.numpy as jnp
from jax import lax
from jax.experimental import pallas as pl
from jax.experimental.pallas import tpu as pltpu

F32 = jnp.float32
BF16 = jnp.bfloat16

D_MODEL = 2048
SEQ = 16384
DEPTH = 2
GRID_W = 64
ROWS = SEQ // GRID_W
HEAD_DIM = 128
N_FOURIER_GROUPS = 4
FOURIER_W = N_FOURIER_GROUPS * HEAD_DIM
N_NA_HEADS = 8
NA_W = N_NA_HEADS * HEAD_DIM
WIN_H = 8
WIN_W = 16
N_MEM = 256
N_MEM_HEADS = 4
MEM_W = N_MEM_HEADS * HEAD_DIM
IN_W = FOURIER_W + 3 * NA_W + MEM_W + 3 * D_MODEL
D_FF = 5632
N_EXPERTS = 8
TOP_K = 2
EPS = 1e-6
HALF_D = D_MODEL // 2

Q_NA_OFF = FOURIER_W
K_NA_OFF = Q_NA_OFF + NA_W
V_NA_OFF = K_NA_OFF + NA_W
Q_MEM_OFF = V_NA_OFF + NA_W
GATE_OFF = Q_MEM_OFF + MEM_W

MIB = 1024 * 1024
NEG_BIG = -1e30

FFT_A = 128
FFT_B = SEQ // FFT_A


def _params(semantics, vmem_mib):
    return pltpu.CompilerParams(dimension_semantics=semantics,
                                vmem_limit_bytes=int(vmem_mib * MIB))


def _rms_rows(x, gain):
    ms = jnp.mean(x * x, axis=-1, keepdims=True)
    return x * lax.rsqrt(ms + EPS) * gain


def _bf16_bits_rtne(v):
    bits = pltpu.bitcast(v, jnp.uint32)
    lsb = lax.shift_right_logical(bits, jnp.uint32(16)) & jnp.uint32(1)
    return bits + jnp.uint32(0x7FFF) + lsb


def _pack_halves(v):
    half = v.shape[1] // 2
    lo = lax.shift_right_logical(_bf16_bits_rtne(v[:, :half]), jnp.uint32(16))
    hi = _bf16_bits_rtne(v[:, half:]) & jnp.uint32(0xFFFF0000)
    return lo | hi


def _unpack_halves(w):
    lo = pltpu.bitcast(lax.shift_left(w, jnp.uint32(16)), F32)
    hi = pltpu.bitcast(w & jnp.uint32(0xFFFF0000), F32)
    return lo, hi


def _norm_mm_kernel(x_ref, g_ref, w_ref, o_ref, h_ref, *, row_chunk):
    @pl.when(pl.program_id(1) == 0)
    def _():
        def body(c, carry):
            rows = pl.ds(pl.multiple_of(c * row_chunk, row_chunk), row_chunk)
            h_ref[rows, :] = _rms_rows(x_ref[rows, :].astype(F32), g_ref[...]).astype(BF16)
            return carry
        lax.fori_loop(0, h_ref.shape[0] // row_chunk, body, 0)

    o_ref[...] = jnp.dot(h_ref[...], w_ref[0].astype(BF16),
                         preferred_element_type=F32).astype(o_ref.dtype)


def _norm_mm(x, g, w, layer, *, tm, tn, out_dtype, vmem_mib, name):
    m, k = x.shape
    n = w.shape[2]
    row_chunk = min(tm, 128)
    return pl.pallas_call(
        functools.partial(_norm_mm_kernel, row_chunk=row_chunk),
        grid=(m // tm, n // tn),
        in_specs=[pl.BlockSpec((tm, k), lambda i, j: (i, 0)),
                  pl.BlockSpec((1, k), lambda i, j: (0, 0)),
                  pl.BlockSpec((1, k, tn), lambda i, j: (layer, 0, j))],
        out_specs=pl.BlockSpec((tm, tn), lambda i, j: (i, j)),
        out_shape=jax.ShapeDtypeStruct((m, n), out_dtype),
        scratch_shapes=[pltpu.VMEM((tm, k), BF16)],
        compiler_params=_params(("parallel", "arbitrary"), vmem_mib),
        name=name,
    )(x, g.reshape(1, k), w)


def _dft_tables():
    a = np.arange(FFT_A)
    ang1 = 2.0 * np.pi * np.outer(a, a) / FFT_A
    w1 = np.concatenate([np.cos(ang1), -np.sin(ang1)], axis=0)
    k1 = np.arange(FFT_A)[:, None, None]
    k2 = np.arange(FFT_B)[None, :, None]
    s2 = np.arange(FFT_B)[None, None, :]
    ang2 = 2.0 * np.pi * ((s2 * (k1 + FFT_A * k2)) % SEQ) / SEQ
    g2 = np.concatenate([np.cos(ang2), np.sin(ang2)], axis=2)
    c = np.arange(HEAD_DIM)
    angc = 2.0 * np.pi * np.outer(c, c) / HEAD_DIM
    eye = np.eye(N_FOURIER_GROUPS)
    bd = np.concatenate([np.kron(eye, np.cos(angc)), np.kron(eye, np.sin(angc))], axis=0)
    return (jnp.asarray(w1, dtype=BF16), jnp.asarray(g2, dtype=BF16),
            jnp.asarray(bd, dtype=BF16))


FFT_BLK = 16


def _dft1_kernel(w_ref, x_ref, br_ref, bi_ref):
    xt = pltpu.einshape("abc->bac", x_ref[...])
    for j in range(FFT_BLK):
        r = jnp.dot(w_ref[...], xt[j], preferred_element_type=F32)
        br_ref[j] = r[:FFT_A].astype(BF16)
        bi_ref[j] = r[FFT_A:].astype(BF16)


def _dft2_kernel(br_ref, bi_ref, g_ref, bd_ref, o_ref, *, norm):
    brt = pltpu.einshape("abc->bac", br_ref[...])
    bit = pltpu.einshape("abc->bac", bi_ref[...])
    ys = []
    for i in range(FFT_BLK):
        br = brt[i]
        bi = bit[i]
        g = g_ref[i]
        zr = jnp.dot(g, jnp.concatenate([br, bi], axis=0), preferred_element_type=F32)
        zi = jnp.dot(g, jnp.concatenate([bi, -br], axis=0), preferred_element_type=F32)
        z = jnp.concatenate([zr, zi], axis=1).astype(BF16)
        y = jnp.dot(z, bd_ref[...], preferred_element_type=F32) * norm
        ys.append(y.astype(o_ref.dtype))
    o_ref[...] = pltpu.einshape("abc->bac", jnp.stack(ys, axis=0))


def _fourier_mix(proj, tables):
    w1, g2, bd = tables
    x3 = proj.reshape(FFT_A, FFT_B, IN_W)
    blk = (FFT_A, FFT_BLK, FOURIER_W)
    br, bi = pl.pallas_call(
        _dft1_kernel,
        grid=(FFT_B // FFT_BLK,),
        in_specs=[pl.BlockSpec((2 * FFT_A, FFT_A), lambda j: (0, 0)),
                  pl.BlockSpec(blk, lambda j: (0, j, 0))],
        out_specs=[pl.BlockSpec((FFT_BLK, FFT_A, FOURIER_W), lambda j: (j, 0, 0)),
                   pl.BlockSpec((FFT_BLK, FFT_A, FOURIER_W), lambda j: (j, 0, 0))],
        out_shape=[jax.ShapeDtypeStruct((FFT_B, FFT_A, FOURIER_W), BF16)] * 2,
        compiler_params=_params(("parallel",), 40),
        name="fourier_stage1",
    )(w1, x3)
    norm = 1.0 / math.sqrt(SEQ * HEAD_DIM)
    y3 = pl.pallas_call(
        functools.partial(_dft2_kernel, norm=norm),
        grid=(FFT_A // FFT_BLK,),
        in_specs=[pl.BlockSpec((FFT_B, FFT_BLK, FOURIER_W), lambda j: (0, j, 0)),
                  pl.BlockSpec((FFT_B, FFT_BLK, FOURIER_W), lambda j: (0, j, 0)),
                  pl.BlockSpec((FFT_BLK, FFT_B, 2 * FFT_B), lambda j: (j, 0, 0)),
                  pl.BlockSpec((2 * FOURIER_W, FOURIER_W), lambda j: (0, 0))],
        out_specs=pl.BlockSpec((FFT_B, FFT_BLK, FOURIER_W), lambda j: (0, j, 0)),
        out_shape=jax.ShapeDtypeStruct((FFT_B, FFT_A, FOURIER_W), BF16),
        compiler_params=_params(("parallel",), 40),
        name="fourier_stage2",
    )(br, bi, g2, bd)
    return y3.reshape(SEQ, FOURIER_W)


NA_SUB_ROWS = 2
NA_KEY_ROWS = NA_SUB_ROWS + WIN_H
NA_SUBS_PER_STEP = 32
NA_N_SUB = ROWS // NA_SUB_ROWS
NA_MAX_KEY_START = ROWS - NA_KEY_ROWS
NA_N_ROW_OFFSETS = 2 * WIN_H - 1
NA_MASKED_BLOCK = NA_N_ROW_OFFSETS
LOG2E = math.log2(math.e)


def _na_key_row0(sub):
    return jnp.clip(sub * NA_SUB_ROWS - WIN_H // 2, 0, NA_MAX_KEY_START)


def _na_cases():
    patterns, case_of_sub = [], []
    for sub in range(NA_N_SUB):
        r0 = sub * NA_SUB_ROWS
        ks = int(np.clip(r0 - WIN_H // 2, 0, NA_MAX_KEY_START))
        pat = np.full((NA_SUB_ROWS, NA_KEY_ROWS), NA_MASKED_BLOCK, np.int32)
        for qi in range(NA_SUB_ROWS):
            r = r0 + qi
            r_start = int(np.clip(r - WIN_H // 2, 0, ROWS - WIN_H))
            for kj in range(NA_KEY_ROWS):
                if r_start <= ks + kj < r_start + WIN_H:
                    pat[qi, kj] = ks + kj - r + (WIN_H - 1)
        for c, p in enumerate(patterns):
            if np.array_equal(p, pat):
                case_of_sub.append(c)
                break
        else:
            case_of_sub.append(len(patterns))
            patterns.append(pat)
    return np.stack(patterns), np.asarray(case_of_sub, np.int32)


def _na_case_of(sub, case_of_sub):
    common = int(np.bincount(case_of_sub).argmax())
    case = jnp.int32(common)
    for s in np.nonzero(case_of_sub != common)[0]:
        case = jnp.where(sub == int(s), int(case_of_sub[s]), case)
    return case


def _na_col_blocks(rpb):
    qc = np.arange(GRID_W)[:, None]
    kc = np.arange(GRID_W)[None, :]
    col_start = np.clip(qc - WIN_W // 2, 0, GRID_W - WIN_W)
    col_valid = (kc >= col_start) & (kc < col_start + WIN_W)
    dc = kc - qc + (WIN_W - 1)
    sel = (np.arange(2 * WIN_W - 1)[:, None, None] == dc[None]) & col_valid[None]
    sel = jnp.asarray(sel.astype(np.float32))
    t = jnp.sum(rpb.astype(F32)[:, :, :, None, None] * sel[None, None], axis=2)
    t = jnp.where(col_valid[None, None], t * LOG2E, NEG_BIG)
    masked = jnp.full((N_NA_HEADS, 1, GRID_W, GRID_W), NEG_BIG, F32)
    t = jnp.concatenate([t, masked], axis=1)
    t = jnp.swapaxes(t, -1, -2)
    return jnp.concatenate([t, t], axis=-1)


def _na_kernel(q_ref, k_ref, v_ref, qg_ref, kg_ref, cb_ref, o_ref, kn_ref, bias_ref):
    step = pl.program_id(1)
    nq = NA_SUB_ROWS * GRID_W
    nk = NA_KEY_ROWS * GRID_W
    row_off, case_of_sub = _na_cases()

    @pl.when(step == 0)
    def _():
        chunk = 1024

        def body(c, carry):
            rows = pl.ds(pl.multiple_of(c * chunk, chunk), chunk)
            kn_ref[rows, :] = _rms_rows(k_ref[rows, :].astype(F32), kg_ref[...]).astype(BF16)
            return carry
        lax.fori_loop(0, SEQ // chunk, body, 0)

        for case in range(row_off.shape[0]):
            for qi in range(NA_SUB_ROWS):
                for kj in range(NA_KEY_ROWS):
                    lanes = slice(qi * GRID_W, (qi + 1) * GRID_W)
                    src = slice((qi % 2) * GRID_W, (qi % 2 + 1) * GRID_W)
                    bias_ref[case, kj * GRID_W:(kj + 1) * GRID_W, lanes] = (
                        cb_ref[0, int(row_off[case, qi, kj]), :, src])

    scale = HEAD_DIM ** -0.5 * LOG2E
    for sb in range(NA_SUBS_PER_STEP):
        sub = step * NA_SUBS_PER_STEP + sb
        case = _na_case_of(sub, case_of_sub)
        q = q_ref[sb * nq:(sb + 1) * nq, :].astype(F32)
        qn = (_rms_rows(q, qg_ref[...]) * scale).astype(BF16)
        kstart = pl.multiple_of(_na_key_row0(sub) * GRID_W, 2 * GRID_W)
        kw = kn_ref[pl.ds(kstart, nk), :]
        vw = v_ref[pl.ds(kstart, nk), :]
        st = lax.dot_general(kw, qn, (((1,), (1,)), ((), ())), preferred_element_type=F32)
        st = st + bias_ref[case]
        m = jnp.max(st, axis=0, keepdims=True)
        pt = jnp.exp2(st - m)
        l = jnp.sum(pt, axis=0, keepdims=True)
        pt = (pt * (1.0 / l)).astype(BF16)
        o = lax.dot_general(pt, vw, (((0,), (0,)), ((), ())), preferred_element_type=F32)
        o_ref[sb * nq:(sb + 1) * nq, :] = o.astype(o_ref.dtype)


def _na_attention(proj, q_g, k_g, col_blocks):
    nq = NA_SUB_ROWS * GRID_W
    nk = NA_KEY_ROWS * GRID_W
    tq = NA_SUBS_PER_STEP * nq
    n_blk = NA_N_ROW_OFFSETS + 1
    qb, kb_, vb = Q_NA_OFF // HEAD_DIM, K_NA_OFF // HEAD_DIM, V_NA_OFF // HEAD_DIM
    return pl.pallas_call(
        _na_kernel,
        grid=(N_NA_HEADS, SEQ // tq),
        in_specs=[pl.BlockSpec((tq, HEAD_DIM), lambda h, i: (i, qb + h)),
                  pl.BlockSpec((SEQ, HEAD_DIM), lambda h, i: (0, kb_ + h)),
                  pl.BlockSpec((SEQ, HEAD_DIM), lambda h, i: (0, vb + h)),
                  pl.BlockSpec((1, HEAD_DIM), lambda h, i: (0, 0)),
                  pl.BlockSpec((1, HEAD_DIM), lambda h, i: (0, 0)),
                  pl.BlockSpec((1, n_blk, GRID_W, 2 * GRID_W), lambda h, i: (h, 0, 0, 0))],
        out_specs=pl.BlockSpec((tq, HEAD_DIM), lambda h, i: (i, h)),
        out_shape=jax.ShapeDtypeStruct((SEQ, NA_W), BF16),
        scratch_shapes=[pltpu.VMEM((SEQ, HEAD_DIM), BF16),
                        pltpu.VMEM((_na_cases()[0].shape[0], nk, nq), F32)],
        compiler_params=_params(("parallel", "arbitrary"), 48),
        name="na_attention",
    )(proj, proj, proj, q_g.reshape(1, HEAD_DIM), k_g.reshape(1, HEAD_DIM), col_blocks)


def _mem_attn_kernel(q_ref, kv_ref, qg_ref, kg_ref, o_ref):
    scale = HEAD_DIM ** -0.5
    for h in range(N_MEM_HEADS):
        cols = slice(h * HEAD_DIM, (h + 1) * HEAD_DIM)
        qn = (_rms_rows(q_ref[:, cols].astype(F32), qg_ref[...]) * scale).astype(BF16)
        kn = _rms_rows(kv_ref[:, cols].astype(F32), kg_ref[...]).astype(BF16)
        v = kv_ref[:, MEM_W + h * HEAD_DIM:MEM_W + (h + 1) * HEAD_DIM].astype(BF16)
        s = lax.dot_general(qn, kn, (((1,), (1,)), ((), ())), preferred_element_type=F32)
        m = jnp.max(s, axis=-1, keepdims=True)
        p = jnp.exp(s - m)
        l = jnp.sum(p, axis=-1, keepdims=True)
        o = jnp.dot(p.astype(BF16), v, preferred_element_type=F32) / l
        o_ref[:, cols] = o.astype(o_ref.dtype)


def _mem_attention(proj, kv, q_g, k_g):
    tm = 512
    return pl.pallas_call(
        _mem_attn_kernel,
        grid=(SEQ // tm,),
        in_specs=[pl.BlockSpec((tm, MEM_W), lambda i: (i, Q_MEM_OFF // MEM_W)),
                  pl.BlockSpec((N_MEM, 2 * MEM_W), lambda i: (0, 0)),
                  pl.BlockSpec((1, HEAD_DIM), lambda i: (0, 0)),
                  pl.BlockSpec((1, HEAD_DIM), lambda i: (0, 0))],
        out_specs=pl.BlockSpec((tm, MEM_W), lambda i: (i, 0)),
        out_shape=jax.ShapeDtypeStruct((SEQ, MEM_W), BF16),
        compiler_params=_params(("parallel",), 32),
        name="mem_attention",
    )(proj, kv, q_g.reshape(1, HEAD_DIM), k_g.reshape(1, HEAD_DIM))


def _merge_kernel(fm_ref, na_ref, mo_ref, g0_ref, g1_ref, g2_ref, wf_ref, wn_ref, wm_ref, o_ref,
                  *, col_chunk):
    for c in range(D_MODEL // col_chunk):
        cols = slice(c * col_chunk, (c + 1) * col_chunk)
        o_f = jnp.dot(fm_ref[...], wf_ref[:, cols], preferred_element_type=F32)
        o_n = jnp.dot(na_ref[...], wn_ref[:, cols], preferred_element_type=F32)
        o_m = jnp.dot(mo_ref[...], wm_ref[:, cols], preferred_element_type=F32)
        acc = jax.nn.sigmoid(g0_ref[:, cols].astype(F32)) * o_f
        acc = acc + jax.nn.sigmoid(g1_ref[:, cols].astype(F32)) * o_n
        acc = acc + jax.nn.sigmoid(g2_ref[:, cols].astype(F32)) * o_m
        o_ref[:, cols] = acc.astype(o_ref.dtype)


def _merge(fm, o_na, o_mem, proj, wf, wn, wm):
    tm = 512
    gb = GATE_OFF // D_MODEL
    return pl.pallas_call(
        functools.partial(_merge_kernel, col_chunk=512),
        grid=(SEQ // tm,),
        in_specs=[pl.BlockSpec((tm, FOURIER_W), lambda i: (i, 0)),
                  pl.BlockSpec((tm, NA_W), lambda i: (i, 0)),
                  pl.BlockSpec((tm, MEM_W), lambda i: (i, 0)),
                  pl.BlockSpec((tm, D_MODEL), lambda i: (i, gb)),
                  pl.BlockSpec((tm, D_MODEL), lambda i: (i, gb + 1)),
                  pl.BlockSpec((tm, D_MODEL), lambda i: (i, gb + 2)),
                  pl.BlockSpec((FOURIER_W, D_MODEL), lambda i: (0, 0)),
                  pl.BlockSpec((NA_W, D_MODEL), lambda i: (0, 0)),
                  pl.BlockSpec((MEM_W, D_MODEL), lambda i: (0, 0))],
        out_specs=pl.BlockSpec((tm, D_MODEL), lambda i: (i, 0)),
        out_shape=jax.ShapeDtypeStruct((SEQ, D_MODEL), BF16),
        compiler_params=_params(("parallel",), 48),
        name="gated_merge",
    )(fm, o_na, o_mem, proj, proj, proj, wf, wn, wm)


ROUTER_LANES = 128


def _route_top2(h, wr_ref):
    h_hi = h.astype(BF16)
    h_lo = (h - h_hi.astype(F32)).astype(BF16)
    r_hi = jnp.dot(h_hi, wr_ref[...], preferred_element_type=F32)
    r_lo = jnp.dot(h_lo, wr_ref[:, :ROUTER_LANES], preferred_element_type=F32)
    logits = r_hi[:, :ROUTER_LANES] + r_hi[:, ROUTER_LANES:] + r_lo
    lane = lax.broadcasted_iota(jnp.int32, logits.shape, 1).astype(F32)
    logits = jnp.where(lane < N_EXPERTS, logits, -jnp.inf)
    m1 = jnp.max(logits, axis=-1, keepdims=True)
    i1 = jnp.min(jnp.where(logits == m1, lane, float(ROUTER_LANES)), axis=-1, keepdims=True)
    rest = jnp.where(lane == i1, -jnp.inf, logits)
    m2 = jnp.max(rest, axis=-1, keepdims=True)
    i2 = jnp.min(jnp.where(rest == m2, lane, float(ROUTER_LANES)), axis=-1, keepdims=True)
    e21 = jnp.exp(m2 - m1)
    g1 = 1.0 / (1.0 + e21)
    g2 = e21 * g1
    return jnp.where(lane == 0.0, i1, i2).astype(jnp.int32), jnp.where(lane == 0.0, g1, g2)


def _oproj_kernel(a_ref, w_ref, x_ref, g_ref, *rest, row_chunk, route):
    if route:
        wr_ref, xo_ref, h_ref, idx_ref, gate_ref = rest
    else:
        xo_ref, h_ref = rest
    xo_ref[...] = jnp.dot(a_ref[...], w_ref[...], preferred_element_type=F32) + x_ref[...]

    def body(c, carry):
        rows = pl.ds(pl.multiple_of(c * row_chunk, row_chunk), row_chunk)
        h = _rms_rows(xo_ref[rows, :], g_ref[...])
        if route:
            h_ref[rows, :] = _pack_halves(h)
            idx_ref[rows, :], gate_ref[rows, :] = _route_top2(h, wr_ref)
        else:
            h_ref[rows, :] = h.astype(h_ref.dtype)
        return carry
    lax.fori_loop(0, xo_ref.shape[0] // row_chunk, body, 0, unroll=route)


def _out_proj(merged, w_o, x, g, w_router=None):
    tm = 512
    route = w_router is not None
    in_specs = [pl.BlockSpec((tm, D_MODEL), lambda i: (i, 0)),
                pl.BlockSpec((D_MODEL, D_MODEL), lambda i: (0, 0)),
                pl.BlockSpec((tm, D_MODEL), lambda i: (i, 0)),
                pl.BlockSpec((1, D_MODEL), lambda i: (0, 0))]
    args = [merged, w_o, x, g.reshape(1, D_MODEL)]
    out_specs = [pl.BlockSpec((tm, D_MODEL), lambda i: (i, 0))]
    out_shape = [jax.ShapeDtypeStruct((SEQ, D_MODEL), F32)]
    if route:
        w_pad = jnp.zeros((D_MODEL, ROUTER_LANES), F32).at[:, :N_EXPERTS].set(w_router.astype(F32))
        w_hi = w_pad.astype(BF16)
        w_lo = (w_pad - w_hi.astype(F32)).astype(BF16)
        in_specs.append(pl.BlockSpec((D_MODEL, 2 * ROUTER_LANES), lambda i: (0, 0)))
        args.append(jnp.concatenate([w_hi, w_lo], axis=1))
        out_specs += [pl.BlockSpec((tm, HALF_D), lambda i: (i, 0)),
                      pl.BlockSpec((tm, ROUTER_LANES), lambda i: (i, 0)),
                      pl.BlockSpec((tm, ROUTER_LANES), lambda i: (i, 0))]
        out_shape += [jax.ShapeDtypeStruct((SEQ, HALF_D), jnp.uint32),
                      jax.ShapeDtypeStruct((SEQ, ROUTER_LANES), jnp.int32),
                      jax.ShapeDtypeStruct((SEQ, ROUTER_LANES), F32)]
    else:
        out_specs.append(pl.BlockSpec((tm, D_MODEL), lambda i: (i, 0)))
        out_shape.append(jax.ShapeDtypeStruct((SEQ, D_MODEL), BF16))
    return pl.pallas_call(
        functools.partial(_oproj_kernel, row_chunk=128, route=route),
        grid=(SEQ // tm,),
        in_specs=in_specs,
        out_specs=out_specs,
        out_shape=out_shape,
        compiler_params=_params(("parallel",), 48),
        name="out_proj",
    )(*args)


def _swiglu_step(x_ref, wg_ref, wu_ref, wd_ref, acc_ref, rows=None):
    x = x_ref[:rows, :]
    g = jnp.dot(x, wg_ref[0].astype(BF16), preferred_element_type=F32)
    u = jnp.dot(x, wu_ref[0].astype(BF16), preferred_element_type=F32)
    a = (g * jax.nn.sigmoid(g) * u).astype(BF16)
    acc_ref[:rows, :] += jnp.dot(a, wd_ref[0].astype(BF16), preferred_element_type=F32)


def _dense_ffn_kernel(x_ref, wg_ref, wu_ref, wd_ref, resid_ref, o_ref):
    @pl.when(pl.program_id(1) == 0)
    def _():
        o_ref[...] = resid_ref[...]

    _swiglu_step(x_ref, wg_ref, wu_ref, wd_ref, o_ref)


def _dense_ffn(h, wg, wu, wd, layer, resid):
    tm, tf = 1024, 512
    m, d = h.shape
    nf = D_FF // tf
    return pl.pallas_call(
        _dense_ffn_kernel,
        grid=(m // tm, nf),
        in_specs=[pl.BlockSpec((tm, d), lambda i, f: (i, 0)),
                  pl.BlockSpec((1, d, tf), lambda i, f: (layer, 0, f)),
                  pl.BlockSpec((1, d, tf), lambda i, f: (layer, 0, f)),
                  pl.BlockSpec((1, tf, d), lambda i, f: (layer, f, 0)),
                  pl.BlockSpec((tm, d), lambda i, f: (i, 0))],
        out_specs=pl.BlockSpec((tm, d), lambda i, f: (i, 0)),
        out_shape=jax.ShapeDtypeStruct((m, d), F32),
        compiler_params=_params(("parallel", "arbitrary"), 60),
        name="dense_ffn",
    )(h, wg, wu, wd, resid)


MOE_TF = 512
MOE_NF = D_FF // MOE_TF
MOE_ROWS_PER_STEP = 96
MOE_TM = MOE_NF * MOE_ROWS_PER_STEP
MOE_ROW_CHUNK = 96
MOE_N_TILES = -(-SEQ * TOP_K // MOE_TM) + N_EXPERTS
MOE_THIRDS = 3
MOE_THIRD_ROWS = MOE_TM // MOE_THIRDS


def _tile_row_copy(tok_ref, h_ref, xg_ref, sems, tile, slot, r):
    row = tok_ref[tile * MOE_TM + r]
    return pltpu.make_async_copy(h_ref.at[pl.ds(row, 1), :], xg_ref.at[slot, pl.ds(r, 1), :],
                                 sems.at[slot])


def _wait_tile_rows(h_ref, xg_ref, sems, slot):
    pltpu.make_async_copy(h_ref.at[pl.ds(0, MOE_TM), :], xg_ref.at[slot], sems.at[slot]).wait()


def _moe_expert_kernel(be_ref, nu_ref, nt_ref, tok_ref, h_ref, wg_ref, wu_ref, wd_ref, o_ref,
                       xg_ref, xb_ref, acc_ref, sems):
    i = pl.program_id(0)
    f = pl.program_id(1)
    n_used = nu_ref[0]
    active = i < n_used
    slot = i % 2
    n_chunks = MOE_TM // MOE_ROW_CHUNK

    @pl.when(jnp.logical_and(i == 0, f == 0))
    def _():
        def start_row(r, carry):
            _tile_row_copy(tok_ref, h_ref, xg_ref, sems, 0, 0, r).start()
            return carry
        lax.fori_loop(0, MOE_TM, start_row, 0, unroll=8)

    @pl.when(jnp.logical_and(active, f == 0))
    def _():
        _wait_tile_rows(h_ref, xg_ref, sems, slot)
        acc_ref[...] = jnp.zeros(acc_ref.shape, acc_ref.dtype)

        def unpack_rows(c, carry):
            rows = pl.ds(pl.multiple_of(c * MOE_ROW_CHUNK, MOE_ROW_CHUNK), MOE_ROW_CHUNK)
            lo, hi = _unpack_halves(xg_ref[slot, rows, :])
            xb_ref[rows, :HALF_D] = lo.astype(BF16)
            xb_ref[rows, HALF_D:] = hi.astype(BF16)
            return carry
        lax.fori_loop(0, n_chunks, unpack_rows, 0)

    for thirds in range(1, MOE_THIRDS + 1):
        @pl.when(jnp.logical_and(active, nt_ref[i] == thirds))
        def _(thirds=thirds):
            for j in range(MOE_ROWS_PER_STEP):
                _tile_row_copy(tok_ref, h_ref, xg_ref, sems, i + 1, 1 - slot,
                               f * MOE_ROWS_PER_STEP + j).start()
            _swiglu_step(xb_ref, wg_ref, wu_ref, wd_ref, acc_ref, rows=thirds * MOE_THIRD_ROWS)

    @pl.when(jnp.logical_and(i == n_used, f == 0))
    def _():
        _wait_tile_rows(h_ref, xg_ref, sems, slot)

    @pl.when(f == MOE_NF - 1)
    def _():
        @pl.when(active)
        def _():
            def finish_rows(c, carry):
                rows = pl.ds(pl.multiple_of(c * MOE_ROW_CHUNK, MOE_ROW_CHUNK), MOE_ROW_CHUNK)
                o_ref[rows, :] = _pack_halves(acc_ref[rows, :])
                return carry
            lax.fori_loop(0, n_chunks, finish_rows, 0)

        @pl.when(jnp.logical_not(active))
        def _():
            o_ref[...] = jnp.zeros(o_ref.shape, o_ref.dtype)


def _moe_experts(h_packed, tok, blk_e, n_used, n_thirds, wg, wu, wd):
    d, tf = D_MODEL, MOE_TF

    def f_idx(i, f, nu):
        return jnp.where(i < nu[0], f, MOE_NF - 1)

    def w_gu(i, f, be, nu, nt, tok):
        return (be[i], 0, f_idx(i, f, nu))

    def w_d(i, f, be, nu, nt, tok):
        return (be[i], f_idx(i, f, nu), 0)

    grid_spec = pltpu.PrefetchScalarGridSpec(
        num_scalar_prefetch=4,
        grid=(MOE_N_TILES, MOE_NF),
        in_specs=[pl.BlockSpec(memory_space=pl.ANY),
                  pl.BlockSpec((1, d, tf), w_gu),
                  pl.BlockSpec((1, d, tf), w_gu),
                  pl.BlockSpec((1, tf, d), w_d)],
        out_specs=pl.BlockSpec((MOE_TM, HALF_D), lambda i, f, be, nu, nt, tok: (i, 0)),
        scratch_shapes=[pltpu.VMEM((2, MOE_TM, HALF_D), jnp.uint32),
                        pltpu.VMEM((MOE_TM, d), BF16),
                        pltpu.VMEM((MOE_TM, d), F32),
                        pltpu.SemaphoreType.DMA((2,))],
    )
    return pl.pallas_call(
        _moe_expert_kernel,
        grid_spec=grid_spec,
        out_shape=jax.ShapeDtypeStruct((MOE_N_TILES * MOE_TM, HALF_D), jnp.uint32),
        compiler_params=_params(("arbitrary", "arbitrary"), 60),
        name="moe_experts",
    )(blk_e, n_used, n_thirds, tok, h_packed, wg, wu, wd)


def _start_row_gather(idx_ref, idx_base, src_ref, dst_ref, sem, n_rows):
    def body(r, carry):
        row = idx_ref[idx_base + r]
        pltpu.make_async_copy(src_ref.at[pl.ds(row, 1), :], dst_ref.at[pl.ds(r, 1), :], sem).start()
        return carry
    lax.fori_loop(0, n_rows, body, 0, unroll=8)


def _wait_row_gather(src_ref, dst_ref, sem, n_rows):
    pltpu.make_async_copy(src_ref.at[pl.ds(0, n_rows), :], dst_ref, sem).wait()


def _moe_combine_kernel(pos_ref, y_ref, x_ref, gate_ref, o_ref, buf_ref, sems, *, tokens):
    i = pl.program_id(0)
    slot = i % 2
    rows = TOP_K * tokens

    @pl.when(i == 0)
    def _():
        _start_row_gather(pos_ref, 0, y_ref, buf_ref.at[0], sems.at[0], rows)

    @pl.when(i + 1 < pl.num_programs(0))
    def _():
        _start_row_gather(pos_ref, (i + 1) * rows, y_ref, buf_ref.at[1 - slot],
                          sems.at[1 - slot], rows)

    _wait_row_gather(y_ref, buf_ref.at[slot], sems.at[slot], rows)
    g0 = gate_ref[:, 0:1]
    g1 = gate_ref[:, 1:2]
    lo0, hi0 = _unpack_halves(buf_ref[slot, :tokens, :])
    lo1, hi1 = _unpack_halves(buf_ref[slot, tokens:, :])
    o_ref[:, :HALF_D] = x_ref[:, :HALF_D] + (g0 * lo0 + g1 * lo1)
    o_ref[:, HALF_D:] = x_ref[:, HALF_D:] + (g0 * hi0 + g1 * hi1)


def _moe_combine(y_packed, pos, x, gate):
    tokens = 512
    n = x.shape[0]
    steps = n // tokens
    pos_tiled = pos.reshape(steps, tokens, TOP_K).transpose(0, 2, 1).reshape(-1)
    grid_spec = pltpu.PrefetchScalarGridSpec(
        num_scalar_prefetch=1,
        grid=(steps,),
        in_specs=[pl.BlockSpec(memory_space=pl.ANY),
                  pl.BlockSpec((tokens, D_MODEL), lambda i, pos_ref: (i, 0)),
                  pl.BlockSpec((tokens, ROUTER_LANES), lambda i, pos_ref: (i, 0))],
        out_specs=pl.BlockSpec((tokens, D_MODEL), lambda i, pos_ref: (i, 0)),
        scratch_shapes=[pltpu.VMEM((2, TOP_K * tokens, HALF_D), jnp.uint32),
                        pltpu.SemaphoreType.DMA((2,))],
    )
    return pl.pallas_call(
        functools.partial(_moe_combine_kernel, tokens=tokens),
        grid_spec=grid_spec,
        out_shape=jax.ShapeDtypeStruct((n, D_MODEL), F32),
        compiler_params=_params(("arbitrary",), 40),
        name="moe_combine",
    )(pos_tiled, y_packed, x, gate)


def _moe(x_new, h_packed, idx, gate, wg, wu, wd):
    n = SEQ
    n_assign = n * TOP_K
    cap = MOE_N_TILES * MOE_TM
    e_flat = idx[:, :TOP_K].reshape(-1)
    onehot = (e_flat[:, None] == jnp.arange(N_EXPERTS, dtype=jnp.int32)[None, :]).astype(jnp.int32)
    csum = jnp.cumsum(onehot, axis=0)
    counts = csum[-1]
    rank = jnp.sum((csum - onehot) * onehot, axis=1)
    padded = (counts + MOE_TM - 1) // MOE_TM * MOE_TM
    pad_ends = jnp.cumsum(padded)
    pad_starts = pad_ends - padded
    dest = jnp.sum(onehot * pad_starts[None, :], axis=1) + rank
    tok = jnp.zeros((cap,), jnp.int32).at[dest].set(
        jnp.arange(n_assign, dtype=jnp.int32) // TOP_K)
    n_used = (pad_ends[-1] // MOE_TM).astype(jnp.int32).reshape(1)
    tile_start = jnp.arange(MOE_N_TILES, dtype=jnp.int32) * MOE_TM
    tile_start = jnp.minimum(tile_start, pad_ends[-1] - 1)
    blk_e = jnp.sum((tile_start[:, None] >= pad_ends[None, :]).astype(jnp.int32), axis=1)
    blk_e = jnp.clip(blk_e, 0, N_EXPERTS - 1)
    real_rows = counts[blk_e] - (tile_start - pad_starts[blk_e])
    n_thirds = jnp.clip((real_rows + MOE_THIRD_ROWS - 1) // MOE_THIRD_ROWS, 1, MOE_THIRDS)
    yb = _moe_experts(h_packed, tok, blk_e, n_used, n_thirds.astype(jnp.int32), wg, wu, wd)
    return _moe_combine(yb, dest.reshape(n, TOP_K), x_new, gate)


def kernel(x, mem, ln_mix_g, w_in, na_q_g, na_k_g, na_rpb, mem_ln_g, w_mem_kv, mem_q_g, mem_k_g,
           w_fourier_out, w_na_out, w_mem_out, w_o, ln_ffn_g, ffn_w_gate, ffn_w_up, ffn_w_down,
           moe_router, moe_w_gate, moe_w_up, moe_w_down):
    assert x.shape == (1, SEQ, D_MODEL) and mem.shape == (1, N_MEM, D_MODEL)
    xs = x.reshape(SEQ, D_MODEL)
    mem2 = mem.reshape(N_MEM, D_MODEL)
    tables = _dft_tables()
    w_in_bf16 = w_in.astype(BF16)
    for l in range(DEPTH):
        proj = _norm_mm(xs, ln_mix_g[l], w_in_bf16, l, tm=1024, tn=2048, out_dtype=BF16,
                        vmem_mib=58, name="in_proj")
        fm = _fourier_mix(proj, tables)
        o_na = _na_attention(proj, na_q_g[l], na_k_g[l], _na_col_blocks(na_rpb[l]))
        kv = _norm_mm(mem2, mem_ln_g[l], w_mem_kv, l, tm=N_MEM, tn=2 * MEM_W, out_dtype=F32,
                      vmem_mib=40, name="mem_kv_proj")
        o_mem = _mem_attention(proj, kv, mem_q_g[l], mem_k_g[l])
        merged = _merge(fm, o_na, o_mem, proj, w_fourier_out[l].astype(BF16),
                        w_na_out[l].astype(BF16), w_mem_out[l].astype(BF16))
        i = l // 2
        if l % 2 == 0:
            x_new, h2 = _out_proj(merged, w_o[l].astype(BF16), xs, ln_ffn_g[l])
            xs = _dense_ffn(h2, ffn_w_gate.astype(BF16), ffn_w_up.astype(BF16),
                            ffn_w_down.astype(BF16), i, x_new)
        else:
            x_new, h2, idx, gate = _out_proj(merged, w_o[l].astype(BF16), xs, ln_ffn_g[l],
                                             moe_router[i])
            xs = _moe(x_new, h2, idx, gate, moe_w_gate[i], moe_w_up[i], moe_w_down[i])
    return xs.reshape(1, SEQ, D_MODEL)
```

```python
import functools
import math

import numpy as np
import jax
import jax.numpy as jnp
from jax import lax
from jax.experimental import pallas as pl
from jax.experimental.pallas import tpu as pltpu

F32 = jnp.float32
BF16 = jnp.bfloat16

D_MODEL = 2048
SEQ = 16384
DEPTH = 2
GRID_W = 64
ROWS = SEQ // GRID_W
HEAD_DIM = 128
N_FOURIER_GROUPS = 4
FOURIER_W = N_FOURIER_GROUPS * HEAD_DIM
N_NA_HEADS = 8
NA_W = N_NA_HEADS * HEAD_DIM
WIN_H = 8
WIN_W = 16
N_MEM = 256
N_MEM_HEADS = 4
MEM_W = N_MEM_HEADS * HEAD_DIM
IN_W = FOURIER_W + 3 * NA_W + MEM_W + 3 * D_MODEL
D_FF = 5632
N_EXPERTS = 8
TOP_K = 2
EPS = 1e-6
HALF_D = D_MODEL // 2

Q_NA_OFF = FOURIER_W
K_NA_OFF = Q_NA_OFF + NA_W
V_NA_OFF = K_NA_OFF + NA_W
Q_MEM_OFF = V_NA_OFF + NA_W
GATE_OFF = Q_MEM_OFF + MEM_W

MIB = 1024 * 1024
NEG_BIG = -1e30

FFT_A = 128
FFT_B = SEQ // FFT_A


def _params(semantics, vmem_mib):
    return pltpu.CompilerParams(dimension_semantics=semantics,
                                vmem_limit_bytes=int(vmem_mib * MIB))


def _rms_rows(x, gain):
    ms = jnp.mean(x * x, axis=-1, keepdims=True)
    return x * lax.rsqrt(ms + EPS) * gain


def _bf16_bits(v):
    return pltpu.bitcast(v.astype(BF16).astype(F32), jnp.uint32)


def _pack_halves(v):
    half = v.shape[1] // 2
    lo = lax.shift_right_logical(_bf16_bits(v[:, :half]), jnp.uint32(16))
    return lo | _bf16_bits(v[:, half:])


def _unpack_halves(w):
    lo = pltpu.bitcast(lax.shift_left(w, jnp.uint32(16)), F32)
    hi = pltpu.bitcast(w & jnp.uint32(0xFFFF0000), F32)
    return lo, hi


def _norm_mm_kernel(x_ref, g_ref, w_ref, o_ref, h_ref, *, row_chunk):
    @pl.when(pl.program_id(1) == 0)
    def _():
        def body(c, carry):
            rows = pl.ds(pl.multiple_of(c * row_chunk, row_chunk), row_chunk)
            h_ref[rows, :] = _rms_rows(x_ref[rows, :].astype(F32), g_ref[...]).astype(BF16)
            return carry
        lax.fori_loop(0, h_ref.shape[0] // row_chunk, body, 0)

    o_ref[...] = jnp.dot(h_ref[...], w_ref[0].astype(BF16),
                         preferred_element_type=F32).astype(o_ref.dtype)


def _norm_mm(x, g, w, layer, *, tm, tn, out_dtype, vmem_mib, name):
    m, k = x.shape
    n = w.shape[2]
    row_chunk = min(tm, 128)
    return pl.pallas_call(
        functools.partial(_norm_mm_kernel, row_chunk=row_chunk),
        grid=(m // tm, n // tn),
        in_specs=[pl.BlockSpec((tm, k), lambda i, j: (i, 0)),
                  pl.BlockSpec((1, k), lambda i, j: (0, 0)),
                  pl.BlockSpec((1, k, tn), lambda i, j: (layer, 0, j))],
        out_specs=pl.BlockSpec((tm, tn), lambda i, j: (i, j)),
        out_shape=jax.ShapeDtypeStruct((m, n), out_dtype),
        scratch_shapes=[pltpu.VMEM((tm, k), BF16)],
        compiler_params=_params(("parallel", "arbitrary"), vmem_mib),
        name=name,
    )(x, g.reshape(1, k), w)


def _dft_tables():
    a = np.arange(FFT_A)
    ang1 = 2.0 * np.pi * np.outer(a, a) / FFT_A
    w1 = np.concatenate([np.cos(ang1), -np.sin(ang1)], axis=0)
    k1 = np.arange(FFT_A)[:, None, None]
    k2 = np.arange(FFT_B)[None, :, None]
    s2 = np.arange(FFT_B)[None, None, :]
    ang2 = 2.0 * np.pi * ((s2 * (k1 + FFT_A * k2)) % SEQ) / SEQ
    g2 = np.concatenate([np.cos(ang2), np.sin(ang2)], axis=2)
    c = np.arange(HEAD_DIM)
    angc = 2.0 * np.pi * np.outer(c, c) / HEAD_DIM
    eye = np.eye(N_FOURIER_GROUPS)
    bd = np.concatenate([np.kron(eye, np.cos(angc)), np.kron(eye, np.sin(angc))], axis=0)
    return (jnp.asarray(w1, dtype=BF16), jnp.asarray(g2, dtype=BF16),
            jnp.asarray(bd, dtype=BF16))


FFT_BLK = 16


def _dft1_kernel(w_ref, x_ref, br_ref, bi_ref):
    xt = pltpu.einshape("abc->bac", x_ref[...])
    for j in range(FFT_BLK):
        r = jnp.dot(w_ref[...], xt[j], preferred_element_type=F32)
        br_ref[j] = r[:FFT_A].astype(BF16)
        bi_ref[j] = r[FFT_A:].astype(BF16)


def _dft2_kernel(br_ref, bi_ref, g_ref, bd_ref, o_ref, *, norm):
    brt = pltpu.einshape("abc->bac", br_ref[...])
    bit = pltpu.einshape("abc->bac", bi_ref[...])
    ys = []
    for i in range(FFT_BLK):
        br = brt[i]
        bi = bit[i]
        g = g_ref[i]
        zr = jnp.dot(g, jnp.concatenate([br, bi], axis=0), preferred_element_type=F32)
        zi = jnp.dot(g, jnp.concatenate([bi, -br], axis=0), preferred_element_type=F32)
        z = jnp.concatenate([zr, zi], axis=1).astype(BF16)
        y = jnp.dot(z, bd_ref[...], preferred_element_type=F32) * norm
        ys.append(y.astype(o_ref.dtype))
    o_ref[...] = pltpu.einshape("abc->bac", jnp.stack(ys, axis=0))


def _fourier_mix(proj, tables):
    w1, g2, bd = tables
    x3 = proj.reshape(FFT_A, FFT_B, IN_W)
    blk = (FFT_A, FFT_BLK, FOURIER_W)
    br, bi = pl.pallas_call(
        _dft1_kernel,
        grid=(FFT_B // FFT_BLK,),
        in_specs=[pl.BlockSpec((2 * FFT_A, FFT_A), lambda j: (0, 0)),
                  pl.BlockSpec(blk, lambda j: (0, j, 0))],
        out_specs=[pl.BlockSpec((FFT_BLK, FFT_A, FOURIER_W), lambda j: (j, 0, 0)),
                   pl.BlockSpec((FFT_BLK, FFT_A, FOURIER_W), lambda j: (j, 0, 0))],
        out_shape=[jax.ShapeDtypeStruct((FFT_B, FFT_A, FOURIER_W), BF16)] * 2,
        compiler_params=_params(("parallel",), 40),
        name="fourier_stage1",
    )(w1, x3)
    norm = 1.0 / math.sqrt(SEQ * HEAD_DIM)
    y3 = pl.pallas_call(
        functools.partial(_dft2_kernel, norm=norm),
        grid=(FFT_A // FFT_BLK,),
        in_specs=[pl.BlockSpec((FFT_B, FFT_BLK, FOURIER_W), lambda j: (0, j, 0)),
                  pl.BlockSpec((FFT_B, FFT_BLK, FOURIER_W), lambda j: (0, j, 0)),
                  pl.BlockSpec((FFT_BLK, FFT_B, 2 * FFT_B), lambda j: (j, 0, 0)),
                  pl.BlockSpec((2 * FOURIER_W, FOURIER_W), lambda j: (0, 0))],
        out_specs=pl.BlockSpec((FFT_B, FFT_BLK, FOURIER_W), lambda j: (0, j, 0)),
        out_shape=jax.ShapeDtypeStruct((FFT_B, FFT_A, FOURIER_W), BF16),
        compiler_params=_params(("parallel",), 40),
        name="fourier_stage2",
    )(br, bi, g2, bd)
    return y3.reshape(SEQ, FOURIER_W)


NA_SUB_ROWS = 2
NA_KEY_ROWS = NA_SUB_ROWS + WIN_H
NA_SUBS_PER_STEP = 32
NA_N_SUB = ROWS // NA_SUB_ROWS
NA_MAX_KEY_START = ROWS - NA_KEY_ROWS
NA_N_ROW_OFFSETS = 2 * WIN_H - 1
NA_MASKED_BLOCK = NA_N_ROW_OFFSETS
LOG2E = math.log2(math.e)


def _na_key_row0(sub):
    return jnp.clip(sub * NA_SUB_ROWS - WIN_H // 2, 0, NA_MAX_KEY_START)


def _na_cases():
    patterns, case_of_sub = [], []
    for sub in range(NA_N_SUB):
        r0 = sub * NA_SUB_ROWS
        ks = int(np.clip(r0 - WIN_H // 2, 0, NA_MAX_KEY_START))
        pat = np.full((NA_SUB_ROWS, NA_KEY_ROWS), NA_MASKED_BLOCK, np.int32)
        for qi in range(NA_SUB_ROWS):
            r = r0 + qi
            r_start = int(np.clip(r - WIN_H // 2, 0, ROWS - WIN_H))
            for kj in range(NA_KEY_ROWS):
                if r_start <= ks + kj < r_start + WIN_H:
                    pat[qi, kj] = ks + kj - r + (WIN_H - 1)
        for c, p in enumerate(patterns):
            if np.array_equal(p, pat):
                case_of_sub.append(c)
                break
        else:
            case_of_sub.append(len(patterns))
            patterns.append(pat)
    return np.stack(patterns), np.asarray(case_of_sub, np.int32)


def _na_case_of(sub, case_of_sub):
    common = int(np.bincount(case_of_sub).argmax())
    case = jnp.int32(common)
    for s in np.nonzero(case_of_sub != common)[0]:
        case = jnp.where(sub == int(s), int(case_of_sub[s]), case)
    return case


def _na_col_blocks(rpb):
    qc = np.arange(GRID_W)[:, None]
    kc = np.arange(GRID_W)[None, :]
    col_start = np.clip(qc - WIN_W // 2, 0, GRID_W - WIN_W)
    col_valid = (kc >= col_start) & (kc < col_start + WIN_W)
    dc = kc - qc + (WIN_W - 1)
    sel = (np.arange(2 * WIN_W - 1)[:, None, None] == dc[None]) & col_valid[None]
    sel = jnp.asarray(sel.astype(np.float32))
    t = jnp.einsum("hrd,dqk->hrqk", rpb.astype(F32), sel, precision=lax.Precision.HIGHEST)
    t = jnp.where(col_valid[None, None], t * LOG2E, NEG_BIG)
    masked = jnp.full((N_NA_HEADS, 1, GRID_W, GRID_W), NEG_BIG, F32)
    t = jnp.concatenate([t, masked], axis=1)
    t = jnp.swapaxes(t, -1, -2)
    return jnp.concatenate([t, t], axis=-1)


def _na_kernel(q_ref, k_ref, v_ref, qg_ref, kg_ref, cb_ref, o_ref, kn_ref, bias_ref):
    step = pl.program_id(1)
    nq = NA_SUB_ROWS * GRID_W
    nk = NA_KEY_ROWS * GRID_W
    row_off, case_of_sub = _na_cases()

    @pl.when(step == 0)
    def _():
        chunk = 1024

        def body(c, carry):
            rows = pl.ds(pl.multiple_of(c * chunk, chunk), chunk)
            kn_ref[rows, :] = _rms_rows(k_ref[rows, :].astype(F32), kg_ref[...]).astype(BF16)
            return carry
        lax.fori_loop(0, SEQ // chunk, body, 0)

        for case in range(row_off.shape[0]):
            for qi in range(NA_SUB_ROWS):
                for kj in range(NA_KEY_ROWS):
                    lanes = slice(qi * GRID_W, (qi + 1) * GRID_W)
                    src = slice((qi % 2) * GRID_W, (qi % 2 + 1) * GRID_W)
                    bias_ref[case, kj * GRID_W:(kj + 1) * GRID_W, lanes] = (
                        cb_ref[0, int(row_off[case, qi, kj]), :, src])

    scale = HEAD_DIM ** -0.5 * LOG2E
    for sb in range(NA_SUBS_PER_STEP):
        sub = step * NA_SUBS_PER_STEP + sb
        case = _na_case_of(sub, case_of_sub)
        q = q_ref[sb * nq:(sb + 1) * nq, :].astype(F32)
        qn = (_rms_rows(q, qg_ref[...]) * scale).astype(BF16)
        kstart = pl.multiple_of(_na_key_row0(sub) * GRID_W, 2 * GRID_W)
        kw = kn_ref[pl.ds(kstart, nk), :]
        vw = v_ref[pl.ds(kstart, nk), :]
        st = lax.dot_general(kw, qn, (((1,), (1,)), ((), ())), preferred_element_type=F32)
        st = st + bias_ref[case]
        m = jnp.max(st, axis=0, keepdims=True)
        pt = jnp.exp2(st - m)
        l = jnp.sum(pt, axis=0, keepdims=True)
        pt = (pt * (1.0 / l)).astype(BF16)
        o = lax.dot_general(pt, vw, (((0,), (0,)), ((), ())), preferred_element_type=F32)
        o_ref[sb * nq:(sb + 1) * nq, :] = o.astype(o_ref.dtype)


def _na_attention(proj, q_g, k_g, col_blocks):
    nq = NA_SUB_ROWS * GRID_W
    nk = NA_KEY_ROWS * GRID_W
    tq = NA_SUBS_PER_STEP * nq
    n_blk = NA_N_ROW_OFFSETS + 1
    qb, kb_, vb = Q_NA_OFF // HEAD_DIM, K_NA_OFF // HEAD_DIM, V_NA_OFF // HEAD_DIM
    return pl.pallas_call(
        _na_kernel,
        grid=(N_NA_HEADS, SEQ // tq),
        in_specs=[pl.BlockSpec((tq, HEAD_DIM), lambda h, i: (i, qb + h)),
                  pl.BlockSpec((SEQ, HEAD_DIM), lambda h, i: (0, kb_ + h)),
                  pl.BlockSpec((SEQ, HEAD_DIM), lambda h, i: (0, vb + h)),
                  pl.BlockSpec((1, HEAD_DIM), lambda h, i: (0, 0)),
                  pl.BlockSpec((1, HEAD_DIM), lambda h, i: (0, 0)),
                  pl.BlockSpec((1, n_blk, GRID_W, 2 * GRID_W), lambda h, i: (h, 0, 0, 0))],
        out_specs=pl.BlockSpec((tq, HEAD_DIM), lambda h, i: (i, h)),
        out_shape=jax.ShapeDtypeStruct((SEQ, NA_W), BF16),
        scratch_shapes=[pltpu.VMEM((SEQ, HEAD_DIM), BF16),
                        pltpu.VMEM((_na_cases()[0].shape[0], nk, nq), F32)],
        compiler_params=_params(("parallel", "arbitrary"), 48),
        name="na_attention",
    )(proj, proj, proj, q_g.reshape(1, HEAD_DIM), k_g.reshape(1, HEAD_DIM), col_blocks)


def _mem_attn_kernel(q_ref, kv_ref, qg_ref, kg_ref, o_ref):
    scale = HEAD_DIM ** -0.5 * LOG2E
    for h in range(N_MEM_HEADS):
        cols = slice(h * HEAD_DIM, (h + 1) * HEAD_DIM)
        qn = (_rms_rows(q_ref[:, cols].astype(F32), qg_ref[...]) * scale).astype(BF16)
        kn = _rms_rows(kv_ref[:, cols].astype(F32), kg_ref[...]).astype(BF16)
        v = kv_ref[:, MEM_W + h * HEAD_DIM:MEM_W + (h + 1) * HEAD_DIM].astype(BF16)
        s = lax.dot_general(qn, kn, (((1,), (1,)), ((), ())), preferred_element_type=F32)
        m = jnp.max(s, axis=-1, keepdims=True)
        p = jnp.exp2(s - m)
        l = jnp.sum(p, axis=-1, keepdims=True)
        o = jnp.dot(p.astype(BF16), v, preferred_element_type=F32) / l
        o_ref[:, cols] = o.astype(o_ref.dtype)


def _mem_attention(proj, kv, q_g, k_g):
    tm = 512
    return pl.pallas_call(
        _mem_attn_kernel,
        grid=(SEQ // tm,),
        in_specs=[pl.BlockSpec((tm, MEM_W), lambda i: (i, Q_MEM_OFF // MEM_W)),
                  pl.BlockSpec((N_MEM, 2 * MEM_W), lambda i: (0, 0)),
                  pl.BlockSpec((1, HEAD_DIM), lambda i: (0, 0)),
                  pl.BlockSpec((1, HEAD_DIM), lambda i: (0, 0))],
        out_specs=pl.BlockSpec((tm, MEM_W), lambda i: (i, 0)),
        out_shape=jax.ShapeDtypeStruct((SEQ, MEM_W), BF16),
        compiler_params=_params(("parallel",), 32),
        name="mem_attention",
    )(proj, kv, q_g.reshape(1, HEAD_DIM), k_g.reshape(1, HEAD_DIM))


def _merge_kernel(fm_ref, na_ref, mo_ref, g0_ref, g1_ref, g2_ref, wf_ref, wn_ref, wm_ref, o_ref,
                  *, col_chunk):
    for c in range(D_MODEL // col_chunk):
        cols = slice(c * col_chunk, (c + 1) * col_chunk)
        o_f = jnp.dot(fm_ref[...], wf_ref[:, cols], preferred_element_type=F32)
        o_n = jnp.dot(na_ref[...], wn_ref[:, cols], preferred_element_type=F32)
        o_m = jnp.dot(mo_ref[...], wm_ref[:, cols], preferred_element_type=F32)
        acc = jax.nn.sigmoid(g0_ref[:, cols].astype(F32)) * o_f
        acc = acc + jax.nn.sigmoid(g1_ref[:, cols].astype(F32)) * o_n
        acc = acc + jax.nn.sigmoid(g2_ref[:, cols].astype(F32)) * o_m
        o_ref[:, cols] = acc.astype(o_ref.dtype)


def _merge(fm, o_na, o_mem, proj, wf, wn, wm):
    tm = 512
    gb = GATE_OFF // D_MODEL
    return pl.pallas_call(
        functools.partial(_merge_kernel, col_chunk=512),
        grid=(SEQ // tm,),
        in_specs=[pl.BlockSpec((tm, FOURIER_W), lambda i: (i, 0)),
                  pl.BlockSpec((tm, NA_W), lambda i: (i, 0)),
                  pl.BlockSpec((tm, MEM_W), lambda i: (i, 0)),
                  pl.BlockSpec((tm, D_MODEL), lambda i: (i, gb)),
                  pl.BlockSpec((tm, D_MODEL), lambda i: (i, gb + 1)),
                  pl.BlockSpec((tm, D_MODEL), lambda i: (i, gb + 2)),
                  pl.BlockSpec((FOURIER_W, D_MODEL), lambda i: (0, 0)),
                  pl.BlockSpec((NA_W, D_MODEL), lambda i: (0, 0)),
                  pl.BlockSpec((MEM_W, D_MODEL), lambda i: (0, 0))],
        out_specs=pl.BlockSpec((tm, D_MODEL), lambda i: (i, 0)),
        out_shape=jax.ShapeDtypeStruct((SEQ, D_MODEL), BF16),
        compiler_params=_params(("parallel",), 48),
        name="gated_merge",
    )(fm, o_na, o_mem, proj, proj, proj, wf, wn, wm)


ROUTER_LANES = 128


def _route_top2(h, wr_ref):
    h_hi = h.astype(BF16)
    h_lo = (h - h_hi.astype(F32)).astype(BF16)
    r_hi = jnp.dot(h_hi, wr_ref[...], preferred_element_type=F32)
    r_lo = jnp.dot(h_lo, wr_ref[:, :ROUTER_LANES], preferred_element_type=F32)
    logits = r_hi[:, :ROUTER_LANES] + r_hi[:, ROUTER_LANES:] + r_lo
    lane = lax.broadcasted_iota(jnp.int32, logits.shape, 1).astype(F32)
    logits = jnp.where(lane < N_EXPERTS, logits, -jnp.inf)
    m1 = jnp.max(logits, axis=-1, keepdims=True)
    i1 = jnp.min(jnp.where(logits == m1, lane, float(ROUTER_LANES)), axis=-1, keepdims=True)
    rest = jnp.where(lane == i1, -jnp.inf, logits)
    m2 = jnp.max(rest, axis=-1, keepdims=True)
    i2 = jnp.min(jnp.where(rest == m2, lane, float(ROUTER_LANES)), axis=-1, keepdims=True)
    e21 = jnp.exp(m2 - m1)
    g1 = 1.0 / (1.0 + e21)
    g2 = e21 * g1
    return jnp.where(lane == 0.0, i1, i2).astype(jnp.int32), jnp.where(lane == 0.0, g1, g2)


def _oproj_kernel(a_ref, w_ref, x_ref, g_ref, *rest, row_chunk, route):
    if route:
        wr_ref, xo_ref, h_ref, idx_ref, gate_ref = rest
    else:
        xo_ref, h_ref = rest
    half = xo_ref.shape[0] // 2
    for r in range(2):
        xo_ref[r * half:(r + 1) * half, :] = (
            jnp.dot(a_ref[r * half:(r + 1) * half, :], w_ref[...], preferred_element_type=F32)
            + x_ref[r * half:(r + 1) * half, :])
        for c in range(half // row_chunk):
            rows = slice(r * half + c * row_chunk, r * half + (c + 1) * row_chunk)
            h = _rms_rows(xo_ref[rows, :], g_ref[...])
            if route:
                h_ref[rows, :] = _pack_halves(h)
                idx_ref[rows, :], gate_ref[rows, :] = _route_top2(h, wr_ref)
            else:
                h_ref[rows, :] = h.astype(h_ref.dtype)


def _out_proj(merged, w_o, x, g, w_router=None):
    tm = 512
    route = w_router is not None
    in_specs = [pl.BlockSpec((tm, D_MODEL), lambda i: (i, 0)),
                pl.BlockSpec((D_MODEL, D_MODEL), lambda i: (0, 0)),
                pl.BlockSpec((tm, D_MODEL), lambda i: (i, 0)),
                pl.BlockSpec((1, D_MODEL), lambda i: (0, 0))]
    args = [merged, w_o, x, g.reshape(1, D_MODEL)]
    out_specs = [pl.BlockSpec((tm, D_MODEL), lambda i: (i, 0))]
    out_shape = [jax.ShapeDtypeStruct((SEQ, D_MODEL), F32)]
    if route:
        w_pad = jnp.zeros((D_MODEL, ROUTER_LANES), F32).at[:, :N_EXPERTS].set(w_router.astype(F32))
        w_hi = w_pad.astype(BF16)
        w_lo = (w_pad - w_hi.astype(F32)).astype(BF16)
        in_specs.append(pl.BlockSpec((D_MODEL, 2 * ROUTER_LANES), lambda i: (0, 0)))
        args.append(jnp.concatenate([w_hi, w_lo], axis=1))
        out_specs += [pl.BlockSpec((tm, HALF_D), lambda i: (i, 0)),
                      pl.BlockSpec((tm, ROUTER_LANES), lambda i: (i, 0)),
                      pl.BlockSpec((tm, ROUTER_LANES), lambda i: (i, 0))]
        out_shape += [jax.ShapeDtypeStruct((SEQ, HALF_D), jnp.uint32),
                      jax.ShapeDtypeStruct((SEQ, ROUTER_LANES), jnp.int32),
                      jax.ShapeDtypeStruct((SEQ, ROUTER_LANES), F32)]
    else:
        out_specs.append(pl.BlockSpec((tm, D_MODEL), lambda i: (i, 0)))
        out_shape.append(jax.ShapeDtypeStruct((SEQ, D_MODEL), BF16))
    return pl.pallas_call(
        functools.partial(_oproj_kernel, row_chunk=128, route=route),
        grid=(SEQ // tm,),
        in_specs=in_specs,
        out_specs=out_specs,
        out_shape=out_shape,
        compiler_params=_params(("parallel",), 48),
        name="out_proj",
    )(*args)


def _swiglu_step(x_ref, wg_ref, wu_ref, wd_ref, acc_ref, rows=None):
    x = x_ref[:rows, :]
    g = jnp.dot(x, wg_ref[0].astype(BF16), preferred_element_type=F32)
    u = jnp.dot(x, wu_ref[0].astype(BF16), preferred_element_type=F32)
    a = (g * jax.nn.sigmoid(g) * u).astype(BF16)
    acc_ref[:rows, :] += jnp.dot(a, wd_ref[0].astype(BF16), preferred_element_type=F32)


def _dense_ffn_kernel(x_ref, wg_ref, wu_ref, wd_ref, resid_ref, o_ref):
    @pl.when(pl.program_id(1) == 0)
    def _():
        o_ref[...] = resid_ref[...]

    _swiglu_step(x_ref, wg_ref, wu_ref, wd_ref, o_ref)


def _dense_ffn(h, wg, wu, wd, layer, resid):
    tm, tf = 1024, 512
    m, d = h.shape
    nf = D_FF // tf
    return pl.pallas_call(
        _dense_ffn_kernel,
        grid=(m // tm, nf),
        in_specs=[pl.BlockSpec((tm, d), lambda i, f: (i, 0)),
                  pl.BlockSpec((1, d, tf), lambda i, f: (layer, 0, f)),
                  pl.BlockSpec((1, d, tf), lambda i, f: (layer, 0, f)),
                  pl.BlockSpec((1, tf, d), lambda i, f: (layer, f, 0)),
                  pl.BlockSpec((tm, d), lambda i, f: (i, 0))],
        out_specs=pl.BlockSpec((tm, d), lambda i, f: (i, 0)),
        out_shape=jax.ShapeDtypeStruct((m, d), F32),
        compiler_params=_params(("parallel", "arbitrary"), 60),
        name="dense_ffn",
    )(h, wg, wu, wd, resid)


MOE_TF = 512
MOE_NF = D_FF // MOE_TF
MOE_ROWS_PER_STEP = 96
MOE_TM = MOE_NF * MOE_ROWS_PER_STEP
MOE_ROW_CHUNK = 96
MOE_N_TILES = -(-SEQ * TOP_K // MOE_TM) + N_EXPERTS
MOE_THIRDS = 3
MOE_THIRD_ROWS = MOE_TM // MOE_THIRDS


def _tile_row_copy(tok_ref, h_ref, xg_ref, sems, tile, slot, r):
    row = tok_ref[tile * MOE_TM + r]
    return pltpu.make_async_copy(h_ref.at[pl.ds(row, 1), :], xg_ref.at[slot, pl.ds(r, 1), :],
                                 sems.at[slot])


def _wait_tile_rows(h_ref, xg_ref, sems, slot):
    pltpu.make_async_copy(h_ref.at[pl.ds(0, MOE_TM), :], xg_ref.at[slot], sems.at[slot]).wait()


def _moe_expert_kernel(be_ref, nu_ref, nt_ref, tok_ref, h_ref, wg_ref, wu_ref, wd_ref, o_ref,
                       xg_ref, xb_ref, acc_ref, sems):
    i = pl.program_id(0)
    f = pl.program_id(1)
    n_used = nu_ref[0]
    active = i < n_used
    slot = i % 2
    n_chunks = MOE_TM // MOE_ROW_CHUNK

    @pl.when(jnp.logical_and(i == 0, f == 0))
    def _():
        def start_row(r, carry):
            _tile_row_copy(tok_ref, h_ref, xg_ref, sems, 0, 0, r).start()
            return carry
        lax.fori_loop(0, MOE_TM, start_row, 0, unroll=8)

    @pl.when(jnp.logical_and(active, f == 0))
    def _():
        _wait_tile_rows(h_ref, xg_ref, sems, slot)
        acc_ref[...] = jnp.zeros(acc_ref.shape, acc_ref.dtype)

        def unpack_rows(c, carry):
            rows = pl.ds(pl.multiple_of(c * MOE_ROW_CHUNK, MOE_ROW_CHUNK), MOE_ROW_CHUNK)
            lo, hi = _unpack_halves(xg_ref[slot, rows, :])
            xb_ref[rows, :HALF_D] = lo.astype(BF16)
            xb_ref[rows, HALF_D:] = hi.astype(BF16)
            return carry
        lax.fori_loop(0, n_chunks, unpack_rows, 0)

    for thirds in range(1, MOE_THIRDS + 1):
        @pl.when(jnp.logical_and(active, nt_ref[i] == thirds))
        def _(thirds=thirds):
            for j in range(MOE_ROWS_PER_STEP):
                _tile_row_copy(tok_ref, h_ref, xg_ref, sems, i + 1, 1 - slot,
                               f * MOE_ROWS_PER_STEP + j).start()
            _swiglu_step(xb_ref, wg_ref, wu_ref, wd_ref, acc_ref, rows=thirds * MOE_THIRD_ROWS)

    @pl.when(jnp.logical_and(i == n_used, f == 0))
    def _():
        _wait_tile_rows(h_ref, xg_ref, sems, slot)

    @pl.when(f == MOE_NF - 1)
    def _():
        @pl.when(active)
        def _():
            def finish_rows(c, carry):
                rows = pl.ds(pl.multiple_of(c * MOE_ROW_CHUNK, MOE_ROW_CHUNK), MOE_ROW_CHUNK)
                o_ref[rows, :] = _pack_halves(acc_ref[rows, :])
                return carry
            lax.fori_loop(0, n_chunks, finish_rows, 0)

        @pl.when(jnp.logical_not(active))
        def _():
            o_ref[...] = jnp.zeros(o_ref.shape, o_ref.dtype)


def _moe_experts(h_packed, tok, blk_e, n_used, n_thirds, wg, wu, wd):
    d, tf = D_MODEL, MOE_TF

    def f_idx(i, f, nu):
        return jnp.where(i < nu[0], f, MOE_NF - 1)

    def w_gu(i, f, be, nu, nt, tok):
        return (be[i], 0, f_idx(i, f, nu))

    def w_d(i, f, be, nu, nt, tok):
        return (be[i], f_idx(i, f, nu), 0)

    grid_spec = pltpu.PrefetchScalarGridSpec(
        num_scalar_prefetch=4,
        grid=(MOE_N_TILES, MOE_NF),
        in_specs=[pl.BlockSpec(memory_space=pl.ANY),
                  pl.BlockSpec((1, d, tf), w_gu),
                  pl.BlockSpec((1, d, tf), w_gu),
                  pl.BlockSpec((1, tf, d), w_d)],
        out_specs=pl.BlockSpec((MOE_TM, HALF_D), lambda i, f, be, nu, nt, tok: (i, 0)),
        scratch_shapes=[pltpu.VMEM((2, MOE_TM, HALF_D), jnp.uint32),
                        pltpu.VMEM((MOE_TM, d), BF16),
                        pltpu.VMEM((MOE_TM, d), F32),
                        pltpu.SemaphoreType.DMA((2,))],
    )
    return pl.pallas_call(
        _moe_expert_kernel,
        grid_spec=grid_spec,
        out_shape=jax.ShapeDtypeStruct((MOE_N_TILES * MOE_TM, HALF_D), jnp.uint32),
        compiler_params=_params(("arbitrary", "arbitrary"), 60),
        name="moe_experts",
    )(blk_e, n_used, n_thirds, tok, h_packed, wg, wu, wd)


def _start_row_gather(idx_ref, idx_base, src_ref, dst_ref, sem, n_rows):
    def body(r, carry):
        row = idx_ref[idx_base + r]
        pltpu.make_async_copy(src_ref.at[pl.ds(row, 1), :], dst_ref.at[pl.ds(r, 1), :], sem).start()
        return carry
    lax.fori_loop(0, n_rows, body, 0, unroll=8)


def _wait_row_gather(src_ref, dst_ref, sem, n_rows):
    pltpu.make_async_copy(src_ref.at[pl.ds(0, n_rows), :], dst_ref, sem).wait()


def _moe_combine_kernel(pos_ref, y_ref, x_ref, gate_ref, o_ref, buf_ref, sems, *, tokens):
    i = pl.program_id(0)
    slot = i % 2
    rows = TOP_K * tokens

    @pl.when(i == 0)
    def _():
        _start_row_gather(pos_ref, 0, y_ref, buf_ref.at[0], sems.at[0], rows)

    @pl.when(i + 1 < pl.num_programs(0))
    def _():
        _start_row_gather(pos_ref, (i + 1) * rows, y_ref, buf_ref.at[1 - slot],
                          sems.at[1 - slot], rows)

    _wait_row_gather(y_ref, buf_ref.at[slot], sems.at[slot], rows)
    g0 = gate_ref[:, 0:1]
    g1 = gate_ref[:, 1:2]
    lo0, hi0 = _unpack_halves(buf_ref[slot, :tokens, :])
    lo1, hi1 = _unpack_halves(buf_ref[slot, tokens:, :])
    o_ref[:, :HALF_D] = x_ref[:, :HALF_D] + (g0 * lo0 + g1 * lo1)
    o_ref[:, HALF_D:] = x_ref[:, HALF_D:] + (g0 * hi0 + g1 * hi1)


def _moe_combine(y_packed, pos, x, gate):
    tokens = 512
    n = x.shape[0]
    steps = n // tokens
    pos_tiled = pos.reshape(steps, tokens, TOP_K).transpose(0, 2, 1).reshape(-1)
    grid_spec = pltpu.PrefetchScalarGridSpec(
        num_scalar_prefetch=1,
        grid=(steps,),
        in_specs=[pl.BlockSpec(memory_space=pl.ANY),
                  pl.BlockSpec((tokens, D_MODEL), lambda i, pos_ref: (i, 0)),
                  pl.BlockSpec((tokens, ROUTER_LANES), lambda i, pos_ref: (i, 0))],
        out_specs=pl.BlockSpec((tokens, D_MODEL), lambda i, pos_ref: (i, 0)),
        scratch_shapes=[pltpu.VMEM((2, TOP_K * tokens, HALF_D), jnp.uint32),
                        pltpu.SemaphoreType.DMA((2,))],
    )
    return pl.pallas_call(
        functools.partial(_moe_combine_kernel, tokens=tokens),
        grid_spec=grid_spec,
        out_shape=jax.ShapeDtypeStruct((n, D_MODEL), F32),
        compiler_params=_params(("arbitrary",), 40),
        name="moe_combine",
    )(pos_tiled, y_packed, x, gate)


def _moe(x_new, h_packed, idx, gate, wg, wu, wd):
    n = SEQ
    n_assign = n * TOP_K
    cap = MOE_N_TILES * MOE_TM
    e_flat = idx[:, :TOP_K].reshape(-1)
    onehot = (e_flat[:, None] == jnp.arange(N_EXPERTS, dtype=jnp.int32)[None, :]).astype(jnp.int32)
    csum = jnp.cumsum(onehot, axis=0)
    counts = csum[-1]
    rank = jnp.sum((csum - onehot) * onehot, axis=1)
    padded = (counts + MOE_TM - 1) // MOE_TM * MOE_TM
    pad_ends = jnp.cumsum(padded)
    pad_starts = pad_ends - padded
    dest = jnp.sum(onehot * pad_starts[None, :], axis=1) + rank
    tok = jnp.zeros((cap,), jnp.int32).at[dest].set(
        jnp.arange(n_assign, dtype=jnp.int32) // TOP_K)
    n_used = (pad_ends[-1] // MOE_TM).astype(jnp.int32).reshape(1)
    tile_start = jnp.arange(MOE_N_TILES, dtype=jnp.int32) * MOE_TM
    tile_start = jnp.minimum(tile_start, pad_ends[-1] - 1)
    blk_e = jnp.sum((tile_start[:, None] >= pad_ends[None, :]).astype(jnp.int32), axis=1)
    blk_e = jnp.clip(blk_e, 0, N_EXPERTS - 1)
    real_rows = counts[blk_e] - (tile_start - pad_starts[blk_e])
    n_thirds = jnp.clip((real_rows + MOE_THIRD_ROWS - 1) // MOE_THIRD_ROWS, 1, MOE_THIRDS)
    yb = _moe_experts(h_packed, tok, blk_e, n_used, n_thirds.astype(jnp.int32), wg, wu, wd)
    return _moe_combine(yb, dest.reshape(n, TOP_K), x_new, gate)


def kernel(x, mem, ln_mix_g, w_in, na_q_g, na_k_g, na_rpb, mem_ln_g, w_mem_kv, mem_q_g, mem_k_g,
           w_fourier_out, w_na_out, w_mem_out, w_o, ln_ffn_g, ffn_w_gate, ffn_w_up, ffn_w_down,
           moe_router, moe_w_gate, moe_w_up, moe_w_down):
    assert x.shape == (1, SEQ, D_MODEL) and mem.shape == (1, N_MEM, D_MODEL)
    xs = x.reshape(SEQ, D_MODEL)
    mem2 = mem.reshape(N_MEM, D_MODEL)
    tables = _dft_tables()
    w_in_bf16 = w_in.astype(BF16)
    for l in range(DEPTH):
        proj = _norm_mm(xs, ln_mix_g[l], w_in_bf16, l, tm=1024, tn=2048, out_dtype=BF16,
                        vmem_mib=58, name="in_proj")
        fm = _fourier_mix(proj, tables)
        o_na = _na_attention(proj, na_q_g[l], na_k_g[l], _na_col_blocks(na_rpb[l]))
        kv = _norm_mm(mem2, mem_ln_g[l], w_mem_kv, l, tm=N_MEM, tn=2 * MEM_W, out_dtype=F32,
                      vmem_mib=40, name="mem_kv_proj")
        o_mem = _mem_attention(proj, kv, mem_q_g[l], mem_k_g[l])
        merged = _merge(fm, o_na, o_mem, proj, w_fourier_out[l].astype(BF16),
                        w_na_out[l].astype(BF16), w_mem_out[l].astype(BF16))
        i = l // 2
        if l % 2 == 0:
            x_new, h2 = _out_proj(merged, w_o[l].astype(BF16), xs, ln_ffn_g[l])
            xs = _dense_ffn(h2, ffn_w_gate.astype(BF16), ffn_w_up.astype(BF16),
                            ffn_w_down.astype(BF16), i, x_new)
        else:
            x_new, h2, idx, gate = _out_proj(merged, w_o[l].astype(BF16), xs, ln_ffn_g[l],
                                             moe_router[i])
            xs = _moe(x_new, h2, idx, gate, moe_w_gate[i], moe_w_up[i], moe_w_down[i])
    return xs.reshape(1, SEQ, D_MODEL)
```

```python
import functools
import math

import numpy as np
import jax
import jax.numpy as jnp
from jax import lax
from jax.experimental import pallas as pl
from jax.experimental.pallas import tpu as pltpu

F32 = jnp.float32
BF16 = jnp.bfloat16

D_MODEL = 2048
SEQ = 16384
DEPTH = 2
GRID_W = 64
ROWS = SEQ // GRID_W
HEAD_DIM = 128
N_FOURIER_GROUPS = 4
FOURIER_W = N_FOURIER_GROUPS * HEAD_DIM
N_NA_HEADS = 8
NA_W = N_NA_HEADS * HEAD_DIM
WIN_H = 8
WIN_W = 16
N_MEM = 256
N_MEM_HEADS = 4
MEM_W = N_MEM_HEADS * HEAD_DIM
IN_W = FOURIER_W + 3 * NA_W + MEM_W + 3 * D_MODEL
D_FF = 5632
N_EXPERTS = 8
TOP_K = 2
EPS = 1e-6
HALF_D = D_MODEL // 2

Q_NA_OFF = FOURIER_W
K_NA_OFF = Q_NA_OFF + NA_W
V_NA_OFF = K_NA_OFF + NA_W
Q_MEM_OFF = V_NA_OFF + NA_W
GATE_OFF = Q_MEM_OFF + MEM_W

MIB = 1024 * 1024
NEG_BIG = -1e30

FFT_A = 128
FFT_B = SEQ // FFT_A


def _params(semantics, vmem_mib):
    return pltpu.CompilerParams(dimension_semantics=semantics,
                                vmem_limit_bytes=int(vmem_mib * MIB))


def _rms_rows(x, gain):
    ms = jnp.mean(x * x, axis=-1, keepdims=True)
    return x * lax.rsqrt(ms + EPS) * gain


def _bf16_bits(v):
    return pltpu.bitcast(v.astype(BF16).astype(F32), jnp.uint32)


def _pack_halves(v):
    half = v.shape[1] // 2
    lo = lax.shift_right_logical(_bf16_bits(v[:, :half]), jnp.uint32(16))
    return lo | _bf16_bits(v[:, half:])


def _unpack_halves(w):
    lo = pltpu.bitcast(lax.shift_left(w, jnp.uint32(16)), F32)
    hi = pltpu.bitcast(w & jnp.uint32(0xFFFF0000), F32)
    return lo, hi


def _norm_mm_kernel(x_ref, g_ref, w_ref, o_ref, h_ref, *, row_chunk):
    @pl.when(pl.program_id(1) == 0)
    def _():
        def body(c, carry):
            rows = pl.ds(pl.multiple_of(c * row_chunk, row_chunk), row_chunk)
            h_ref[rows, :] = _rms_rows(x_ref[rows, :].astype(F32), g_ref[...]).astype(BF16)
            return carry
        lax.fori_loop(0, h_ref.shape[0] // row_chunk, body, 0)

    o_ref[...] = jnp.dot(h_ref[...], w_ref[0].astype(BF16),
                         preferred_element_type=F32).astype(o_ref.dtype)


def _norm_mm(x, g, w, layer, *, tm, tn, out_dtype, vmem_mib, name):
    m, k = x.shape
    n = w.shape[2]
    row_chunk = min(tm, 128)
    return pl.pallas_call(
        functools.partial(_norm_mm_kernel, row_chunk=row_chunk),
        grid=(m // tm, n // tn),
        in_specs=[pl.BlockSpec((tm, k), lambda i, j: (i, 0)),
                  pl.BlockSpec((1, k), lambda i, j: (0, 0)),
                  pl.BlockSpec((1, k, tn), lambda i, j: (layer, 0, j))],
        out_specs=pl.BlockSpec((tm, tn), lambda i, j: (i, j)),
        out_shape=jax.ShapeDtypeStruct((m, n), out_dtype),
        scratch_shapes=[pltpu.VMEM((tm, k), BF16)],
        compiler_params=_params(("parallel", "arbitrary"), vmem_mib),
        name=name,
    )(x, g.reshape(1, k), w)


def _dft_tables():
    a = np.arange(FFT_A)
    ang1 = 2.0 * np.pi * np.outer(a, a) / FFT_A
    w1 = np.concatenate([np.cos(ang1), -np.sin(ang1)], axis=0)
    k1 = np.arange(FFT_A)[:, None, None]
    k2 = np.arange(FFT_B)[None, :, None]
    s2 = np.arange(FFT_B)[None, None, :]
    ang2 = 2.0 * np.pi * ((s2 * (k1 + FFT_A * k2)) % SEQ) / SEQ
    g2 = np.concatenate([np.cos(ang2), np.sin(ang2)], axis=2)
    c = np.arange(HEAD_DIM)
    angc = 2.0 * np.pi * np.outer(c, c) / HEAD_DIM
    eye = np.eye(N_FOURIER_GROUPS)
    bd = np.concatenate([np.kron(eye, np.cos(angc)), np.kron(eye, np.sin(angc))], axis=0)
    return (jnp.asarray(w1, dtype=BF16), jnp.asarray(g2, dtype=BF16),
            jnp.asarray(bd, dtype=BF16))


FFT_BLK = 16


def _dft1_kernel(w_ref, x_ref, br_ref, bi_ref):
    xt = pltpu.einshape("abc->bac", x_ref[...])
    for j in range(FFT_BLK):
        r = jnp.dot(w_ref[...], xt[j], preferred_element_type=F32)
        br_ref[j] = r[:FFT_A].astype(BF16)
        bi_ref[j] = r[FFT_A:].astype(BF16)


def _dft2_kernel(br_ref, bi_ref, g_ref, bd_ref, o_ref, *, norm):
    brt = pltpu.einshape("abc->bac", br_ref[...])
    bit = pltpu.einshape("abc->bac", bi_ref[...])
    ys = []
    for i in range(FFT_BLK):
        br = brt[i]
        bi = bit[i]
        g = g_ref[i]
        zr = jnp.dot(g, jnp.concatenate([br, bi], axis=0), preferred_element_type=F32)
        zi = jnp.dot(g, jnp.concatenate([bi, -br], axis=0), preferred_element_type=F32)
        z = jnp.concatenate([zr, zi], axis=1).astype(BF16)
        y = jnp.dot(z, bd_ref[...], preferred_element_type=F32) * norm
        ys.append(y.astype(o_ref.dtype))
    o_ref[...] = pltpu.einshape("abc->bac", jnp.stack(ys, axis=0))


def _fourier_mix(proj, tables):
    w1, g2, bd = tables
    x3 = proj.reshape(FFT_A, FFT_B, IN_W)
    blk = (FFT_A, FFT_BLK, FOURIER_W)
    br, bi = pl.pallas_call(
        _dft1_kernel,
        grid=(FFT_B // FFT_BLK,),
        in_specs=[pl.BlockSpec((2 * FFT_A, FFT_A), lambda j: (0, 0)),
                  pl.BlockSpec(blk, lambda j: (0, j, 0))],
        out_specs=[pl.BlockSpec((FFT_BLK, FFT_A, FOURIER_W), lambda j: (j, 0, 0)),
                   pl.BlockSpec((FFT_BLK, FFT_A, FOURIER_W), lambda j: (j, 0, 0))],
        out_shape=[jax.ShapeDtypeStruct((FFT_B, FFT_A, FOURIER_W), BF16)] * 2,
        compiler_params=_params(("parallel",), 40),
        name="fourier_stage1",
    )(w1, x3)
    norm = 1.0 / math.sqrt(SEQ * HEAD_DIM)
    y3 = pl.pallas_call(
        functools.partial(_dft2_kernel, norm=norm),
        grid=(FFT_A // FFT_BLK,),
        in_specs=[pl.BlockSpec((FFT_B, FFT_BLK, FOURIER_W), lambda j: (0, j, 0)),
                  pl.BlockSpec((FFT_B, FFT_BLK, FOURIER_W), lambda j: (0, j, 0)),
                  pl.BlockSpec((FFT_BLK, FFT_B, 2 * FFT_B), lambda j: (j, 0, 0)),
                  pl.BlockSpec((2 * FOURIER_W, FOURIER_W), lambda j: (0, 0))],
        out_specs=pl.BlockSpec((FFT_B, FFT_BLK, FOURIER_W), lambda j: (0, j, 0)),
        out_shape=jax.ShapeDtypeStruct((FFT_B, FFT_A, FOURIER_W), BF16),
        compiler_params=_params(("parallel",), 40),
        name="fourier_stage2",
    )(br, bi, g2, bd)
    return y3.reshape(SEQ, FOURIER_W)


NA_SUB_ROWS = 2
NA_KEY_ROWS = NA_SUB_ROWS + WIN_H
NA_SUBS_PER_STEP = 32
NA_N_SUB = ROWS // NA_SUB_ROWS
NA_MAX_KEY_START = ROWS - NA_KEY_ROWS
NA_N_ROW_OFFSETS = 2 * WIN_H - 1
NA_MASKED_BLOCK = NA_N_ROW_OFFSETS
LOG2E = math.log2(math.e)


def _na_key_row0(sub):
    return jnp.clip(sub * NA_SUB_ROWS - WIN_H // 2, 0, NA_MAX_KEY_START)


def _na_cases():
    patterns, case_of_sub = [], []
    for sub in range(NA_N_SUB):
        r0 = sub * NA_SUB_ROWS
        ks = int(np.clip(r0 - WIN_H // 2, 0, NA_MAX_KEY_START))
        pat = np.full((NA_SUB_ROWS, NA_KEY_ROWS), NA_MASKED_BLOCK, np.int32)
        for qi in range(NA_SUB_ROWS):
            r = r0 + qi
            r_start = int(np.clip(r - WIN_H // 2, 0, ROWS - WIN_H))
            for kj in range(NA_KEY_ROWS):
                if r_start <= ks + kj < r_start + WIN_H:
                    pat[qi, kj] = ks + kj - r + (WIN_H - 1)
        for c, p in enumerate(patterns):
            if np.array_equal(p, pat):
                case_of_sub.append(c)
                break
        else:
            case_of_sub.append(len(patterns))
            patterns.append(pat)
    return np.stack(patterns), np.asarray(case_of_sub, np.int32)


def _na_case_of(sub, case_of_sub):
    common = int(np.bincount(case_of_sub).argmax())
    case = jnp.int32(common)
    for s in np.nonzero(case_of_sub != common)[0]:
        case = jnp.where(sub == int(s), int(case_of_sub[s]), case)
    return case


def _na_col_blocks(rpb):
    qc = np.arange(GRID_W)[:, None]
    kc = np.arange(GRID_W)[None, :]
    col_start = np.clip(qc - WIN_W // 2, 0, GRID_W - WIN_W)
    col_valid = (kc >= col_start) & (kc < col_start + WIN_W)
    dc = kc - qc + (WIN_W - 1)
    sel = (np.arange(2 * WIN_W - 1)[:, None, None] == dc[None]) & col_valid[None]
    sel = jnp.asarray(sel.astype(np.float32))
    t = jnp.einsum("hrd,dqk->hrqk", rpb.astype(F32), sel, precision=lax.Precision.HIGHEST)
    t = jnp.where(col_valid[None, None], t * LOG2E, NEG_BIG)
    masked = jnp.full((N_NA_HEADS, 1, GRID_W, GRID_W), NEG_BIG, F32)
    t = jnp.concatenate([t, masked], axis=1)
    t = jnp.swapaxes(t, -1, -2)
    return jnp.concatenate([t, t], axis=-1)


def _na_kernel(q_ref, k_ref, v_ref, qg_ref, kg_ref, cb_ref, o_ref, kn_ref, bias_ref):
    step = pl.program_id(1)
    nq = NA_SUB_ROWS * GRID_W
    nk = NA_KEY_ROWS * GRID_W
    row_off, case_of_sub = _na_cases()

    @pl.when(step == 0)
    def _():
        chunk = 1024

        def body(c, carry):
            rows = pl.ds(pl.multiple_of(c * chunk, chunk), chunk)
            kn_ref[rows, :] = _rms_rows(k_ref[rows, :].astype(F32), kg_ref[...]).astype(BF16)
            return carry
        lax.fori_loop(0, SEQ // chunk, body, 0)

        for case in range(row_off.shape[0]):
            for qi in range(NA_SUB_ROWS):
                for kj in range(NA_KEY_ROWS):
                    lanes = slice(qi * GRID_W, (qi + 1) * GRID_W)
                    src = slice((qi % 2) * GRID_W, (qi % 2 + 1) * GRID_W)
                    bias_ref[case, kj * GRID_W:(kj + 1) * GRID_W, lanes] = (
                        cb_ref[0, int(row_off[case, qi, kj]), :, src])

    scale = HEAD_DIM ** -0.5 * LOG2E
    for sb in range(NA_SUBS_PER_STEP):
        sub = step * NA_SUBS_PER_STEP + sb
        case = _na_case_of(sub, case_of_sub)
        q = q_ref[sb * nq:(sb + 1) * nq, :].astype(F32)
        qn = (_rms_rows(q, qg_ref[...]) * scale).astype(BF16)
        kstart = pl.multiple_of(_na_key_row0(sub) * GRID_W, 2 * GRID_W)
        kw = kn_ref[pl.ds(kstart, nk), :]
        vw = v_ref[pl.ds(kstart, nk), :]
        st = lax.dot_general(kw, qn, (((1,), (1,)), ((), ())), preferred_element_type=F32)
        st = st + bias_ref[case]
        m = jnp.max(st, axis=0, keepdims=True)
        pt = jnp.exp2(st - m)
        l = jnp.sum(pt, axis=0, keepdims=True)
        pt = (pt * (1.0 / l)).astype(BF16)
        o = lax.dot_general(pt, vw, (((0,), (0,)), ((), ())), preferred_element_type=F32)
        o_ref[sb * nq:(sb + 1) * nq, :] = o.astype(o_ref.dtype)


def _na_attention(proj, q_g, k_g, col_blocks):
    nq = NA_SUB_ROWS * GRID_W
    nk = NA_KEY_ROWS * GRID_W
    tq = NA_SUBS_PER_STEP * nq
    n_blk = NA_N_ROW_OFFSETS + 1
    qb, kb_, vb = Q_NA_OFF // HEAD_DIM, K_NA_OFF // HEAD_DIM, V_NA_OFF // HEAD_DIM
    return pl.pallas_call(
        _na_kernel,
        grid=(N_NA_HEADS, SEQ // tq),
        in_specs=[pl.BlockSpec((tq, HEAD_DIM), lambda h, i: (i, qb + h)),
                  pl.BlockSpec((SEQ, HEAD_DIM), lambda h, i: (0, kb_ + h)),
                  pl.BlockSpec((SEQ, HEAD_DIM), lambda h, i: (0, vb + h)),
                  pl.BlockSpec((1, HEAD_DIM), lambda h, i: (0, 0)),
                  pl.BlockSpec((1, HEAD_DIM), lambda h, i: (0, 0)),
                  pl.BlockSpec((1, n_blk, GRID_W, 2 * GRID_W), lambda h, i: (h, 0, 0, 0))],
        out_specs=pl.BlockSpec((tq, HEAD_DIM), lambda h, i: (i, h)),
        out_shape=jax.ShapeDtypeStruct((SEQ, NA_W), BF16),
        scratch_shapes=[pltpu.VMEM((SEQ, HEAD_DIM), BF16),
                        pltpu.VMEM((_na_cases()[0].shape[0], nk, nq), F32)],
        compiler_params=_params(("parallel", "arbitrary"), 48),
        name="na_attention",
    )(proj, proj, proj, q_g.reshape(1, HEAD_DIM), k_g.reshape(1, HEAD_DIM), col_blocks)


def _mem_attn_kernel(q_ref, kv_ref, qg_ref, kg_ref, o_ref):
    scale = HEAD_DIM ** -0.5 * LOG2E
    for h in range(N_MEM_HEADS):
        cols = slice(h * HEAD_DIM, (h + 1) * HEAD_DIM)
        qn = (_rms_rows(q_ref[:, cols].astype(F32), qg_ref[...]) * scale).astype(BF16)
        kn = _rms_rows(kv_ref[:, cols].astype(F32), kg_ref[...]).astype(BF16)
        v = kv_ref[:, MEM_W + h * HEAD_DIM:MEM_W + (h + 1) * HEAD_DIM].astype(BF16)
        s = lax.dot_general(qn, kn, (((1,), (1,)), ((), ())), preferred_element_type=F32)
        m = jnp.max(s, axis=-1, keepdims=True)
        p = jnp.exp2(s - m)
        l = jnp.sum(p, axis=-1, keepdims=True)
        o = jnp.dot(p.astype(BF16), v, preferred_element_type=F32) / l
        o_ref[:, cols] = o.astype(o_ref.dtype)


def _mem_attention(proj, kv, q_g, k_g):
    tm = 512
    return pl.pallas_call(
        _mem_attn_kernel,
        grid=(SEQ // tm,),
        in_specs=[pl.BlockSpec((tm, MEM_W), lambda i: (i, Q_MEM_OFF // MEM_W)),
                  pl.BlockSpec((N_MEM, 2 * MEM_W), lambda i: (0, 0)),
                  pl.BlockSpec((1, HEAD_DIM), lambda i: (0, 0)),
                  pl.BlockSpec((1, HEAD_DIM), lambda i: (0, 0))],
        out_specs=pl.BlockSpec((tm, MEM_W), lambda i: (i, 0)),
        out_shape=jax.ShapeDtypeStruct((SEQ, MEM_W), BF16),
        compiler_params=_params(("parallel",), 32),
        name="mem_attention",
    )(proj, kv, q_g.reshape(1, HEAD_DIM), k_g.reshape(1, HEAD_DIM))


def _merge_kernel(fm_ref, na_ref, mo_ref, g0_ref, g1_ref, g2_ref, wf_ref, wn_ref, wm_ref, o_ref,
                  *, col_chunk):
    for c in range(D_MODEL // col_chunk):
        cols = slice(c * col_chunk, (c + 1) * col_chunk)
        o_f = jnp.dot(fm_ref[...], wf_ref[:, cols], preferred_element_type=F32)
        o_n = jnp.dot(na_ref[...], wn_ref[:, cols], preferred_element_type=F32)
        o_m = jnp.dot(mo_ref[...], wm_ref[:, cols], preferred_element_type=F32)
        acc = jax.nn.sigmoid(g0_ref[:, cols].astype(F32)) * o_f
        acc = acc + jax.nn.sigmoid(g1_ref[:, cols].astype(F32)) * o_n
        acc = acc + jax.nn.sigmoid(g2_ref[:, cols].astype(F32)) * o_m
        o_ref[:, cols] = acc.astype(o_ref.dtype)


def _merge(fm, o_na, o_mem, proj, wf, wn, wm):
    tm = 512
    gb = GATE_OFF // D_MODEL
    return pl.pallas_call(
        functools.partial(_merge_kernel, col_chunk=512),
        grid=(SEQ // tm,),
        in_specs=[pl.BlockSpec((tm, FOURIER_W), lambda i: (i, 0)),
                  pl.BlockSpec((tm, NA_W), lambda i: (i, 0)),
                  pl.BlockSpec((tm, MEM_W), lambda i: (i, 0)),
                  pl.BlockSpec((tm, D_MODEL), lambda i: (i, gb)),
                  pl.BlockSpec((tm, D_MODEL), lambda i: (i, gb + 1)),
                  pl.BlockSpec((tm, D_MODEL), lambda i: (i, gb + 2)),
                  pl.BlockSpec((FOURIER_W, D_MODEL), lambda i: (0, 0)),
                  pl.BlockSpec((NA_W, D_MODEL), lambda i: (0, 0)),
                  pl.BlockSpec((MEM_W, D_MODEL), lambda i: (0, 0))],
        out_specs=pl.BlockSpec((tm, D_MODEL), lambda i: (i, 0)),
        out_shape=jax.ShapeDtypeStruct((SEQ, D_MODEL), BF16),
        compiler_params=_params(("parallel",), 48),
        name="gated_merge",
    )(fm, o_na, o_mem, proj, proj, proj, wf, wn, wm)


ROUTER_LANES = 128


def _route_top2(h, wr_ref):
    h_hi = h.astype(BF16)
    h_lo = (h - h_hi.astype(F32)).astype(BF16)
    r_hi = jnp.dot(h_hi, wr_ref[...], preferred_element_type=F32)
    r_lo = jnp.dot(h_lo, wr_ref[:, :ROUTER_LANES], preferred_element_type=F32)
    logits = r_hi[:, :ROUTER_LANES] + r_hi[:, ROUTER_LANES:] + r_lo
    lane = lax.broadcasted_iota(jnp.int32, logits.shape, 1).astype(F32)
    logits = jnp.where(lane < N_EXPERTS, logits, -jnp.inf)
    m1 = jnp.max(logits, axis=-1, keepdims=True)
    i1 = jnp.min(jnp.where(logits == m1, lane, float(ROUTER_LANES)), axis=-1, keepdims=True)
    rest = jnp.where(lane == i1, -jnp.inf, logits)
    m2 = jnp.max(rest, axis=-1, keepdims=True)
    i2 = jnp.min(jnp.where(rest == m2, lane, float(ROUTER_LANES)), axis=-1, keepdims=True)
    e21 = jnp.exp(m2 - m1)
    g1 = 1.0 / (1.0 + e21)
    g2 = e21 * g1
    return jnp.where(lane == 0.0, i1, i2).astype(jnp.int32), jnp.where(lane == 0.0, g1, g2)


def _oproj_kernel(a_ref, w_ref, x_ref, g_ref, *rest, row_chunk, route):
    if route:
        wr_ref, xo_ref, h_ref, idx_ref, gate_ref = rest
    else:
        xo_ref, h_ref = rest
    half = xo_ref.shape[0] // 2
    for r in range(2):
        xo_ref[r * half:(r + 1) * half, :] = (
            jnp.dot(a_ref[r * half:(r + 1) * half, :], w_ref[...], preferred_element_type=F32)
            + x_ref[r * half:(r + 1) * half, :])
        for c in range(half // row_chunk):
            rows = slice(r * half + c * row_chunk, r * half + (c + 1) * row_chunk)
            h = _rms_rows(xo_ref[rows, :], g_ref[...])
            if route:
                h_ref[rows, :] = _pack_halves(h)
                idx_ref[rows, :], gate_ref[rows, :] = _route_top2(h, wr_ref)
            else:
                h_ref[rows, :] = h.astype(h_ref.dtype)


def _out_proj(merged, w_o, x, g, w_router=None):
    tm = 512
    route = w_router is not None
    in_specs = [pl.BlockSpec((tm, D_MODEL), lambda i: (i, 0)),
                pl.BlockSpec((D_MODEL, D_MODEL), lambda i: (0, 0)),
                pl.BlockSpec((tm, D_MODEL), lambda i: (i, 0)),
                pl.BlockSpec((1, D_MODEL), lambda i: (0, 0))]
    args = [merged, w_o, x, g.reshape(1, D_MODEL)]
    out_specs = [pl.BlockSpec((tm, D_MODEL), lambda i: (i, 0))]
    out_shape = [jax.ShapeDtypeStruct((SEQ, D_MODEL), F32)]
    if route:
        w_pad = jnp.zeros((D_MODEL, ROUTER_LANES), F32).at[:, :N_EXPERTS].set(w_router.astype(F32))
        w_hi = w_pad.astype(BF16)
        w_lo = (w_pad - w_hi.astype(F32)).astype(BF16)
        in_specs.append(pl.BlockSpec((D_MODEL, 2 * ROUTER_LANES), lambda i: (0, 0)))
        args.append(jnp.concatenate([w_hi, w_lo], axis=1))
        out_specs += [pl.BlockSpec((tm, HALF_D), lambda i: (i, 0)),
                      pl.BlockSpec((tm, ROUTER_LANES), lambda i: (i, 0)),
                      pl.BlockSpec((tm, ROUTER_LANES), lambda i: (i, 0))]
        out_shape += [jax.ShapeDtypeStruct((SEQ, HALF_D), jnp.uint32),
                      jax.ShapeDtypeStruct((SEQ, ROUTER_LANES), jnp.int32),
                      jax.ShapeDtypeStruct((SEQ, ROUTER_LANES), F32)]
    else:
        out_specs.append(pl.BlockSpec((tm, D_MODEL), lambda i: (i, 0)))
        out_shape.append(jax.ShapeDtypeStruct((SEQ, D_MODEL), BF16))
    return pl.pallas_call(
        functools.partial(_oproj_kernel, row_chunk=128, route=route),
        grid=(SEQ // tm,),
        in_specs=in_specs,
        out_specs=out_specs,
        out_shape=out_shape,
        compiler_params=_params(("parallel",), 48),
        name="out_proj",
    )(*args)


def _swiglu_step(x_ref, wg_ref, wu_ref, wd_ref, acc_ref, rows=None):
    x = x_ref[:rows, :]
    g = jnp.dot(x, wg_ref[0].astype(BF16), preferred_element_type=F32)
    u = jnp.dot(x, wu_ref[0].astype(BF16), preferred_element_type=F32)
    a = (g * jax.nn.sigmoid(g) * u).astype(BF16)
    acc_ref[:rows, :] += jnp.dot(a, wd_ref[0].astype(BF16), preferred_element_type=F32)


def _dense_ffn_kernel(x_ref, wg_ref, wu_ref, wd_ref, resid_ref, o_ref):
    @pl.when(pl.program_id(1) == 0)
    def _():
        o_ref[...] = resid_ref[...]

    _swiglu_step(x_ref, wg_ref, wu_ref, wd_ref, o_ref)


def _dense_ffn(h, wg, wu, wd, layer, resid):
    tm, tf = 1024, 512
    m, d = h.shape
    nf = D_FF // tf
    return pl.pallas_call(
        _dense_ffn_kernel,
        grid=(m // tm, nf),
        in_specs=[pl.BlockSpec((tm, d), lambda i, f: (i, 0)),
                  pl.BlockSpec((1, d, tf), lambda i, f: (layer, 0, f)),
                  pl.BlockSpec((1, d, tf), lambda i, f: (layer, 0, f)),
                  pl.BlockSpec((1, tf, d), lambda i, f: (layer, f, 0)),
                  pl.BlockSpec((tm, d), lambda i, f: (i, 0))],
        out_specs=pl.BlockSpec((tm, d), lambda i, f: (i, 0)),
        out_shape=jax.ShapeDtypeStruct((m, d), F32),
        compiler_params=_params(("parallel", "arbitrary"), 60),
        name="dense_ffn",
    )(h, wg, wu, wd, resid)


MOE_TF = 512
MOE_NF = D_FF // MOE_TF
MOE_ROWS_PER_STEP = 96
MOE_TM = MOE_NF * MOE_ROWS_PER_STEP
MOE_ROW_CHUNK = 96
MOE_N_TILES = -(-SEQ * TOP_K // MOE_TM) + N_EXPERTS
MOE_THIRDS = 3
MOE_THIRD_ROWS = MOE_TM // MOE_THIRDS


def _tile_row_copy(tok_ref, h_ref, xg_ref, sems, tile, slot, r):
    row = tok_ref[tile * MOE_TM + r]
    return pltpu.make_async_copy(h_ref.at[pl.ds(row, 1), :], xg_ref.at[slot, pl.ds(r, 1), :],
                                 sems.at[slot])


def _wait_tile_rows(h_ref, xg_ref, sems, slot):
    pltpu.make_async_copy(h_ref.at[pl.ds(0, MOE_TM), :], xg_ref.at[slot], sems.at[slot]).wait()


def _moe_expert_kernel(be_ref, nu_ref, nt_ref, tok_ref, h_ref, wg_ref, wu_ref, wd_ref, o_ref,
                       xg_ref, xb_ref, acc_ref, sems):
    i = pl.program_id(0)
    f = pl.program_id(1)
    n_used = nu_ref[0]
    active = i < n_used
    slot = i % 2
    n_chunks = MOE_TM // MOE_ROW_CHUNK

    @pl.when(jnp.logical_and(i == 0, f == 0))
    def _():
        def start_row(r, carry):
            _tile_row_copy(tok_ref, h_ref, xg_ref, sems, 0, 0, r).start()
            return carry
        lax.fori_loop(0, MOE_TM, start_row, 0, unroll=8)

    @pl.when(jnp.logical_and(active, f == 0))
    def _():
        _wait_tile_rows(h_ref, xg_ref, sems, slot)
        acc_ref[...] = jnp.zeros(acc_ref.shape, acc_ref.dtype)

        def unpack_rows(c, carry):
            rows = pl.ds(pl.multiple_of(c * MOE_ROW_CHUNK, MOE_ROW_CHUNK), MOE_ROW_CHUNK)
            lo, hi = _unpack_halves(xg_ref[slot, rows, :])
            xb_ref[rows, :HALF_D] = lo.astype(BF16)
            xb_ref[rows, HALF_D:] = hi.astype(BF16)
            return carry
        lax.fori_loop(0, n_chunks, unpack_rows, 0)

    for thirds in range(1, MOE_THIRDS + 1):
        @pl.when(jnp.logical_and(active, nt_ref[i] == thirds))
        def _(thirds=thirds):
            for j in range(MOE_ROWS_PER_STEP):
                _tile_row_copy(tok_ref, h_ref, xg_ref, sems, i + 1, 1 - slot,
                               f * MOE_ROWS_PER_STEP + j).start(priority=1)
            _swiglu_step(xb_ref, wg_ref, wu_ref, wd_ref, acc_ref, rows=thirds * MOE_THIRD_ROWS)

    @pl.when(jnp.logical_and(i == n_used, f == 0))
    def _():
        _wait_tile_rows(h_ref, xg_ref, sems, slot)

    @pl.when(f == MOE_NF - 1)
    def _():
        @pl.when(active)
        def _():
            def finish_rows(c, carry):
                rows = pl.ds(pl.multiple_of(c * MOE_ROW_CHUNK, MOE_ROW_CHUNK), MOE_ROW_CHUNK)
                o_ref[rows, :] = _pack_halves(acc_ref[rows, :])
                return carry
            lax.fori_loop(0, n_chunks, finish_rows, 0)

        @pl.when(jnp.logical_not(active))
        def _():
            o_ref[...] = jnp.zeros(o_ref.shape, o_ref.dtype)


def _moe_experts(h_packed, tok, blk_e, n_used, n_thirds, wg, wu, wd):
    d, tf = D_MODEL, MOE_TF

    def f_idx(i, f, nu):
        return jnp.where(i < nu[0], f, MOE_NF - 1)

    def w_gu(i, f, be, nu, nt, tok):
        return (be[i], 0, f_idx(i, f, nu))

    def w_d(i, f, be, nu, nt, tok):
        return (be[i], f_idx(i, f, nu), 0)

    grid_spec = pltpu.PrefetchScalarGridSpec(
        num_scalar_prefetch=4,
        grid=(MOE_N_TILES, MOE_NF),
        in_specs=[pl.BlockSpec(memory_space=pl.ANY),
                  pl.BlockSpec((1, d, tf), w_gu),
                  pl.BlockSpec((1, d, tf), w_gu),
                  pl.BlockSpec((1, tf, d), w_d)],
        out_specs=pl.BlockSpec((MOE_TM, HALF_D), lambda i, f, be, nu, nt, tok: (i, 0)),
        scratch_shapes=[pltpu.VMEM((2, MOE_TM, HALF_D), jnp.uint32),
                        pltpu.VMEM((MOE_TM, d), BF16),
                        pltpu.VMEM((MOE_TM, d), F32),
                        pltpu.SemaphoreType.DMA((2,))],
    )
    return pl.pallas_call(
        _moe_expert_kernel,
        grid_spec=grid_spec,
        out_shape=jax.ShapeDtypeStruct((MOE_N_TILES * MOE_TM, HALF_D), jnp.uint32),
        compiler_params=_params(("arbitrary", "arbitrary"), 60),
        name="moe_experts",
    )(blk_e, n_used, n_thirds, tok, h_packed, wg, wu, wd)


def _start_row_gather(idx_ref, idx_base, src_ref, dst_ref, sem, n_rows):
    def body(r, carry):
        row = idx_ref[idx_base + r]
        pltpu.make_async_copy(src_ref.at[pl.ds(row, 1), :], dst_ref.at[pl.ds(r, 1), :], sem).start()
        return carry
    lax.fori_loop(0, n_rows, body, 0, unroll=8)


def _wait_row_gather(src_ref, dst_ref, sem, n_rows):
    pltpu.make_async_copy(src_ref.at[pl.ds(0, n_rows), :], dst_ref, sem).wait()


def _moe_combine_kernel(pos_ref, y_ref, x_ref, gate_ref, o_ref, buf_ref, sems, *, tokens):
    i = pl.program_id(0)
    slot = i % 2
    rows = TOP_K * tokens

    @pl.when(i == 0)
    def _():
        _start_row_gather(pos_ref, 0, y_ref, buf_ref.at[0], sems.at[0], rows)

    @pl.when(i + 1 < pl.num_programs(0))
    def _():
        _start_row_gather(pos_ref, (i + 1) * rows, y_ref, buf_ref.at[1 - slot],
                          sems.at[1 - slot], rows)

    _wait_row_gather(y_ref, buf_ref.at[slot], sems.at[slot], rows)
    g0 = gate_ref[:, 0:1]
    g1 = gate_ref[:, 1:2]
    lo0, hi0 = _unpack_halves(buf_ref[slot, :tokens, :])
    lo1, hi1 = _unpack_halves(buf_ref[slot, tokens:, :])
    o_ref[:, :HALF_D] = x_ref[:, :HALF_D] + (g0 * lo0 + g1 * lo1)
    o_ref[:, HALF_D:] = x_ref[:, HALF_D:] + (g0 * hi0 + g1 * hi1)


def _moe_combine(y_packed, pos, x, gate):
    tokens = 512
    n = x.shape[0]
    steps = n // tokens
    pos_tiled = pos.reshape(steps, tokens, TOP_K).transpose(0, 2, 1).reshape(-1)
    grid_spec = pltpu.PrefetchScalarGridSpec(
        num_scalar_prefetch=1,
        grid=(steps,),
        in_specs=[pl.BlockSpec(memory_space=pl.ANY),
                  pl.BlockSpec((tokens, D_MODEL), lambda i, pos_ref: (i, 0)),
                  pl.BlockSpec((tokens, ROUTER_LANES), lambda i, pos_ref: (i, 0))],
        out_specs=pl.BlockSpec((tokens, D_MODEL), lambda i, pos_ref: (i, 0)),
        scratch_shapes=[pltpu.VMEM((2, TOP_K * tokens, HALF_D), jnp.uint32),
                        pltpu.SemaphoreType.DMA((2,))],
    )
    return pl.pallas_call(
        functools.partial(_moe_combine_kernel, tokens=tokens),
        grid_spec=grid_spec,
        out_shape=jax.ShapeDtypeStruct((n, D_MODEL), F32),
        compiler_params=_params(("arbitrary",), 40),
        name="moe_combine",
    )(pos_tiled, y_packed, x, gate)


def _slot_tokens_kernel(dest_ref, fill_ref, tok_ref):
    for e in range(N_EXPERTS + 1):
        def fill(s, carry):
            tok_ref[s] = 0
            return carry
        lax.fori_loop(fill_ref[2 * e], fill_ref[2 * e + 1], fill, 0)

    def place(a, carry):
        tok_ref[dest_ref[a]] = a >> 1
        return carry
    lax.fori_loop(0, SEQ * TOP_K, place, 0, unroll=8)


def _slot_tokens(dest, fill_ranges):
    assert TOP_K == 2
    cap = MOE_N_TILES * MOE_TM
    grid_spec = pltpu.PrefetchScalarGridSpec(
        num_scalar_prefetch=2,
        grid=(1,),
        in_specs=[],
        out_specs=pl.BlockSpec(memory_space=pltpu.SMEM),
    )
    return pl.pallas_call(
        _slot_tokens_kernel,
        grid_spec=grid_spec,
        out_shape=jax.ShapeDtypeStruct((cap,), jnp.int32),
        name="moe_slot_tokens",
    )(dest, fill_ranges)


def _moe(x_new, h_packed, idx, gate, wg, wu, wd):
    n = SEQ
    n_assign = n * TOP_K
    cap = MOE_N_TILES * MOE_TM
    e_flat = idx[:, :TOP_K].reshape(-1)
    onehot = (e_flat[:, None] == jnp.arange(N_EXPERTS, dtype=jnp.int32)[None, :]).astype(jnp.int32)
    csum = jnp.cumsum(onehot, axis=0)
    counts = csum[-1]
    rank = jnp.sum((csum - onehot) * onehot, axis=1)
    padded = (counts + MOE_TM - 1) // MOE_TM * MOE_TM
    pad_ends = jnp.cumsum(padded)
    pad_starts = pad_ends - padded
    dest = jnp.sum(onehot * pad_starts[None, :], axis=1) + rank
    fill_lo = jnp.concatenate([pad_starts + counts, pad_ends[-1:]])
    fill_hi = jnp.concatenate([pad_ends, pad_ends[-1:] + MOE_TM])
    tok = _slot_tokens(dest.astype(jnp.int32),
                       jnp.stack([fill_lo, fill_hi], axis=1).reshape(-1).astype(jnp.int32))
    n_used = (pad_ends[-1] // MOE_TM).astype(jnp.int32).reshape(1)
    tile_start = jnp.arange(MOE_N_TILES, dtype=jnp.int32) * MOE_TM
    tile_start = jnp.minimum(tile_start, pad_ends[-1] - 1)
    blk_e = jnp.sum((tile_start[:, None] >= pad_ends[None, :]).astype(jnp.int32), axis=1)
    blk_e = jnp.clip(blk_e, 0, N_EXPERTS - 1)
    real_rows = counts[blk_e] - (tile_start - pad_starts[blk_e])
    n_thirds = jnp.clip((real_rows + MOE_THIRD_ROWS - 1) // MOE_THIRD_ROWS, 1, MOE_THIRDS)
    yb = _moe_experts(h_packed, tok, blk_e, n_used, n_thirds.astype(jnp.int32), wg, wu, wd)
    return _moe_combine(yb, dest.reshape(n, TOP_K), x_new, gate)


def kernel(x, mem, ln_mix_g, w_in, na_q_g, na_k_g, na_rpb, mem_ln_g, w_mem_kv, mem_q_g, mem_k_g,
           w_fourier_out, w_na_out, w_mem_out, w_o, ln_ffn_g, ffn_w_gate, ffn_w_up, ffn_w_down,
           moe_router, moe_w_gate, moe_w_up, moe_w_down):
    assert x.shape == (1, SEQ, D_MODEL) and mem.shape == (1, N_MEM, D_MODEL)
    xs = x.reshape(SEQ, D_MODEL)
    mem2 = mem.reshape(N_MEM, D_MODEL)
    tables = _dft_tables()
    w_in_bf16 = w_in.astype(BF16)
    for l in range(DEPTH):
        proj = _norm_mm(xs, ln_mix_g[l], w_in_bf16, l, tm=1024, tn=2048, out_dtype=BF16,
                        vmem_mib=58, name="in_proj")
        fm = _fourier_mix(proj, tables)
        o_na = _na_attention(proj, na_q_g[l], na_k_g[l], _na_col_blocks(na_rpb[l]))
        kv = _norm_mm(mem2, mem_ln_g[l], w_mem_kv, l, tm=N_MEM, tn=2 * MEM_W, out_dtype=F32,
                      vmem_mib=40, name="mem_kv_proj")
        o_mem = _mem_attention(proj, kv, mem_q_g[l], mem_k_g[l])
        merged = _merge(fm, o_na, o_mem, proj, w_fourier_out[l].astype(BF16),
                        w_na_out[l].astype(BF16), w_mem_out[l].astype(BF16))
        i = l // 2
        if l % 2 == 0:
            x_new, h2 = _out_proj(merged, w_o[l].astype(BF16), xs, ln_ffn_g[l])
            xs = _dense_ffn(h2, ffn_w_gate.astype(BF16), ffn_w_up.astype(BF16),
                            ffn_w_down.astype(BF16), i, x_new)
        else:
            x_new, h2, idx, gate = _out_proj(merged, w_o[l].astype(BF16), xs, ln_ffn_g[l],
                                             moe_router[i])
            xs = _moe(x_new, h2, idx, gate, moe_w_gate[i], moe_w_up[i], moe_w_down[i])
    return xs.reshape(1, SEQ, D_MODEL)
```

```python
import functools
import math

import numpy as np
import jax
import jax.numpy as jnp
from jax import lax
from jax.experimental import pallas as pl
from jax.experimental.pallas import tpu as pltpu

F32 = jnp.float32
BF16 = jnp.bfloat16

D_MODEL = 2048
SEQ = 16384
DEPTH = 2
GRID_W = 64
ROWS = SEQ // GRID_W
HEAD_DIM = 128
N_FOURIER_GROUPS = 4
FOURIER_W = N_FOURIER_GROUPS * HEAD_DIM
N_NA_HEADS = 8
NA_W = N_NA_HEADS * HEAD_DIM
WIN_H = 8
WIN_W = 16
N_MEM = 256
N_MEM_HEADS = 4
MEM_W = N_MEM_HEADS * HEAD_DIM
IN_W = FOURIER_W + 3 * NA_W + MEM_W + 3 * D_MODEL
D_FF = 5632
N_EXPERTS = 8
TOP_K = 2
EPS = 1e-6
HALF_D = D_MODEL // 2

Q_NA_OFF = FOURIER_W
K_NA_OFF = Q_NA_OFF + NA_W
V_NA_OFF = K_NA_OFF + NA_W
Q_MEM_OFF = V_NA_OFF + NA_W
GATE_OFF = Q_MEM_OFF + MEM_W

MIB = 1024 * 1024
NEG_BIG = -1e30

FFT_A = 128
FFT_B = SEQ // FFT_A


def _params(semantics, vmem_mib):
    return pltpu.CompilerParams(dimension_semantics=semantics,
                                vmem_limit_bytes=int(vmem_mib * MIB))


def _rms_rows(x, gain):
    ms = jnp.mean(x * x, axis=-1, keepdims=True)
    return x * lax.rsqrt(ms + EPS) * gain


def _sigmoid(x):
    return 0.5 * jnp.tanh(0.5 * x) + 0.5


def _bf16_bits(v):
    return pltpu.bitcast(v.astype(BF16).astype(F32), jnp.uint32)


def _pack_halves(v):
    half = v.shape[1] // 2
    lo = lax.shift_right_logical(_bf16_bits(v[:, :half]), jnp.uint32(16))
    return lo | _bf16_bits(v[:, half:])


def _unpack_halves(w):
    lo = pltpu.bitcast(lax.shift_left(w, jnp.uint32(16)), F32)
    hi = pltpu.bitcast(w & jnp.uint32(0xFFFF0000), F32)
    return lo, hi


def _norm_mm_kernel(x_ref, g_ref, w_ref, o_ref, h_ref, *, row_chunk):
    @pl.when(pl.program_id(1) == 0)
    def _():
        def body(c, carry):
            rows = pl.ds(pl.multiple_of(c * row_chunk, row_chunk), row_chunk)
            h_ref[rows, :] = _rms_rows(x_ref[rows, :].astype(F32), g_ref[...]).astype(BF16)
            return carry
        lax.fori_loop(0, h_ref.shape[0] // row_chunk, body, 0)

    o_ref[...] = jnp.dot(h_ref[...], w_ref[0].astype(BF16),
                         preferred_element_type=F32).astype(o_ref.dtype)


def _norm_mm(x, g, w, layer, *, tm, tn, out_dtype, vmem_mib, name):
    m, k = x.shape
    n = w.shape[2]
    row_chunk = min(tm, 128)
    return pl.pallas_call(
        functools.partial(_norm_mm_kernel, row_chunk=row_chunk),
        grid=(m // tm, n // tn),
        in_specs=[pl.BlockSpec((tm, k), lambda i, j: (i, 0)),
                  pl.BlockSpec((1, k), lambda i, j: (0, 0)),
                  pl.BlockSpec((1, k, tn), lambda i, j: (layer, 0, j))],
        out_specs=pl.BlockSpec((tm, tn), lambda i, j: (i, j)),
        out_shape=jax.ShapeDtypeStruct((m, n), out_dtype),
        scratch_shapes=[pltpu.VMEM((tm, k), BF16)],
        compiler_params=_params(("parallel", "arbitrary"), vmem_mib),
        name=name,
    )(x, g.reshape(1, k), w)


def _dft_tables():
    a = np.arange(FFT_A)
    ang1 = 2.0 * np.pi * np.outer(a, a) / FFT_A
    w1 = np.concatenate([np.cos(ang1), -np.sin(ang1)], axis=0)
    k1 = np.arange(FFT_A)[:, None, None]
    k2 = np.arange(FFT_B)[None, :, None]
    s2 = np.arange(FFT_B)[None, None, :]
    ang2 = 2.0 * np.pi * ((s2 * (k1 + FFT_A * k2)) % SEQ) / SEQ
    g2 = np.concatenate([np.cos(ang2), np.sin(ang2)], axis=2)
    c = np.arange(HEAD_DIM)
    angc = 2.0 * np.pi * np.outer(c, c) / HEAD_DIM
    eye = np.eye(N_FOURIER_GROUPS)
    bd = np.concatenate([np.kron(eye, np.cos(angc)), np.kron(eye, np.sin(angc))], axis=0)
    return (jnp.asarray(w1, dtype=BF16), jnp.asarray(g2, dtype=BF16),
            jnp.asarray(bd, dtype=BF16))


FFT_BLK = 16


def _dft1_kernel(w_ref, x_ref, br_ref, bi_ref):
    xt = pltpu.einshape("abc->bac", x_ref[...])
    for j in range(FFT_BLK):
        r = jnp.dot(w_ref[...], xt[j], preferred_element_type=F32)
        br_ref[j] = r[:FFT_A].astype(BF16)
        bi_ref[j] = r[FFT_A:].astype(BF16)


def _dft2_kernel(br_ref, bi_ref, g_ref, bd_ref, o_ref, *, norm):
    brt = pltpu.einshape("abc->bac", br_ref[...])
    bit = pltpu.einshape("abc->bac", bi_ref[...])
    ys = []
    for i in range(FFT_BLK):
        br = brt[i]
        bi = bit[i]
        g = g_ref[i]
        zr = jnp.dot(g, jnp.concatenate([br, bi], axis=0), preferred_element_type=F32)
        zi = jnp.dot(g, jnp.concatenate([bi, -br], axis=0), preferred_element_type=F32)
        z = jnp.concatenate([zr, zi], axis=1).astype(BF16)
        y = jnp.dot(z, bd_ref[...], preferred_element_type=F32) * norm
        ys.append(y.astype(o_ref.dtype))
    o_ref[...] = pltpu.einshape("abc->bac", jnp.stack(ys, axis=0))


def _fourier_mix(proj, tables):
    w1, g2, bd = tables
    x3 = proj.reshape(FFT_A, FFT_B, IN_W)
    blk = (FFT_A, FFT_BLK, FOURIER_W)
    br, bi = pl.pallas_call(
        _dft1_kernel,
        grid=(FFT_B // FFT_BLK,),
        in_specs=[pl.BlockSpec((2 * FFT_A, FFT_A), lambda j: (0, 0)),
                  pl.BlockSpec(blk, lambda j: (0, j, 0))],
        out_specs=[pl.BlockSpec((FFT_BLK, FFT_A, FOURIER_W), lambda j: (j, 0, 0)),
                   pl.BlockSpec((FFT_BLK, FFT_A, FOURIER_W), lambda j: (j, 0, 0))],
        out_shape=[jax.ShapeDtypeStruct((FFT_B, FFT_A, FOURIER_W), BF16)] * 2,
        compiler_params=_params(("parallel",), 40),
        name="fourier_stage1",
    )(w1, x3)
    norm = 1.0 / math.sqrt(SEQ * HEAD_DIM)
    y3 = pl.pallas_call(
        functools.partial(_dft2_kernel, norm=norm),
        grid=(FFT_A // FFT_BLK,),
        in_specs=[pl.BlockSpec((FFT_B, FFT_BLK, FOURIER_W), lambda j: (0, j, 0)),
                  pl.BlockSpec((FFT_B, FFT_BLK, FOURIER_W), lambda j: (0, j, 0)),
                  pl.BlockSpec((FFT_BLK, FFT_B, 2 * FFT_B), lambda j: (j, 0, 0)),
                  pl.BlockSpec((2 * FOURIER_W, FOURIER_W), lambda j: (0, 0))],
        out_specs=pl.BlockSpec((FFT_B, FFT_BLK, FOURIER_W), lambda j: (0, j, 0)),
        out_shape=jax.ShapeDtypeStruct((FFT_B, FFT_A, FOURIER_W), BF16),
        compiler_params=_params(("parallel",), 40),
        name="fourier_stage2",
    )(br, bi, g2, bd)
    return y3.reshape(SEQ, FOURIER_W)


NA_SUB_ROWS = 2
NA_KEY_ROWS = NA_SUB_ROWS + WIN_H
NA_SUBS_PER_STEP = 32
NA_N_SUB = ROWS // NA_SUB_ROWS
NA_MAX_KEY_START = ROWS - NA_KEY_ROWS
NA_N_ROW_OFFSETS = 2 * WIN_H - 1
NA_MASKED_BLOCK = NA_N_ROW_OFFSETS
LOG2E = math.log2(math.e)


def _na_key_row0(sub):
    return jnp.clip(sub * NA_SUB_ROWS - WIN_H // 2, 0, NA_MAX_KEY_START)


def _na_cases():
    patterns, case_of_sub = [], []
    for sub in range(NA_N_SUB):
        r0 = sub * NA_SUB_ROWS
        ks = int(np.clip(r0 - WIN_H // 2, 0, NA_MAX_KEY_START))
        pat = np.full((NA_SUB_ROWS, NA_KEY_ROWS), NA_MASKED_BLOCK, np.int32)
        for qi in range(NA_SUB_ROWS):
            r = r0 + qi
            r_start = int(np.clip(r - WIN_H // 2, 0, ROWS - WIN_H))
            for kj in range(NA_KEY_ROWS):
                if r_start <= ks + kj < r_start + WIN_H:
                    pat[qi, kj] = ks + kj - r + (WIN_H - 1)
        for c, p in enumerate(patterns):
            if np.array_equal(p, pat):
                case_of_sub.append(c)
                break
        else:
            case_of_sub.append(len(patterns))
            patterns.append(pat)
    return np.stack(patterns), np.asarray(case_of_sub, np.int32)


def _na_case_of(sub, case_of_sub):
    common = int(np.bincount(case_of_sub).argmax())
    case = jnp.int32(common)
    for s in np.nonzero(case_of_sub != common)[0]:
        case = jnp.where(sub == int(s), int(case_of_sub[s]), case)
    return case


def _na_col_blocks(rpb):
    qc = np.arange(GRID_W)[:, None]
    kc = np.arange(GRID_W)[None, :]
    col_start = np.clip(qc - WIN_W // 2, 0, GRID_W - WIN_W)
    col_valid = (kc >= col_start) & (kc < col_start + WIN_W)
    dc = kc - qc + (WIN_W - 1)
    sel = (np.arange(2 * WIN_W - 1)[:, None, None] == dc[None]) & col_valid[None]
    sel = jnp.asarray(sel.astype(np.float32))
    t = jnp.einsum("hrd,dqk->hrqk", rpb.astype(F32), sel, precision=lax.Precision.HIGHEST)
    t = jnp.where(col_valid[None, None], t * LOG2E, NEG_BIG)
    masked = jnp.full((N_NA_HEADS, 1, GRID_W, GRID_W), NEG_BIG, F32)
    t = jnp.concatenate([t, masked], axis=1)
    t = jnp.swapaxes(t, -1, -2)
    return jnp.concatenate([t, t], axis=-1)


def _na_kernel(q_ref, k_ref, v_ref, qg_ref, kg_ref, cb_ref, o_ref, kn_ref, bias_ref):
    step = pl.program_id(1)
    nq = NA_SUB_ROWS * GRID_W
    nk = NA_KEY_ROWS * GRID_W
    row_off, case_of_sub = _na_cases()

    @pl.when(step == 0)
    def _():
        chunk = 1024

        def body(c, carry):
            rows = pl.ds(pl.multiple_of(c * chunk, chunk), chunk)
            kn_ref[rows, :] = _rms_rows(k_ref[rows, :].astype(F32), kg_ref[...]).astype(BF16)
            return carry
        lax.fori_loop(0, SEQ // chunk, body, 0)

        for case in range(row_off.shape[0]):
            for qi in range(NA_SUB_ROWS):
                for kj in range(NA_KEY_ROWS):
                    lanes = slice(qi * GRID_W, (qi + 1) * GRID_W)
                    src = slice((qi % 2) * GRID_W, (qi % 2 + 1) * GRID_W)
                    bias_ref[case, kj * GRID_W:(kj + 1) * GRID_W, lanes] = (
                        cb_ref[0, int(row_off[case, qi, kj]), :, src])

    scale = HEAD_DIM ** -0.5 * LOG2E
    for sb in range(NA_SUBS_PER_STEP):
        sub = step * NA_SUBS_PER_STEP + sb
        case = _na_case_of(sub, case_of_sub)
        q = q_ref[sb * nq:(sb + 1) * nq, :].astype(F32)
        qn = (_rms_rows(q, qg_ref[...]) * scale).astype(BF16)
        kstart = pl.multiple_of(_na_key_row0(sub) * GRID_W, 2 * GRID_W)
        kw = kn_ref[pl.ds(kstart, nk), :]
        vw = v_ref[pl.ds(kstart, nk), :]
        st = lax.dot_general(kw, qn, (((1,), (1,)), ((), ())), preferred_element_type=F32)
        st = st + bias_ref[case]
        m = jnp.max(st, axis=0, keepdims=True)
        pt = jnp.exp2(st - m)
        l = jnp.sum(pt, axis=0, keepdims=True)
        pt = (pt * (1.0 / l)).astype(BF16)
        o = lax.dot_general(pt, vw, (((0,), (0,)), ((), ())), preferred_element_type=F32)
        o_ref[sb * nq:(sb + 1) * nq, :] = o.astype(o_ref.dtype)


def _na_attention(proj, q_g, k_g, col_blocks):
    nq = NA_SUB_ROWS * GRID_W
    nk = NA_KEY_ROWS * GRID_W
    tq = NA_SUBS_PER_STEP * nq
    n_blk = NA_N_ROW_OFFSETS + 1
    qb, kb_, vb = Q_NA_OFF // HEAD_DIM, K_NA_OFF // HEAD_DIM, V_NA_OFF // HEAD_DIM
    return pl.pallas_call(
        _na_kernel,
        grid=(N_NA_HEADS, SEQ // tq),
        in_specs=[pl.BlockSpec((tq, HEAD_DIM), lambda h, i: (i, qb + h)),
                  pl.BlockSpec((SEQ, HEAD_DIM), lambda h, i: (0, kb_ + h)),
                  pl.BlockSpec((SEQ, HEAD_DIM), lambda h, i: (0, vb + h)),
                  pl.BlockSpec((1, HEAD_DIM), lambda h, i: (0, 0)),
                  pl.BlockSpec((1, HEAD_DIM), lambda h, i: (0, 0)),
                  pl.BlockSpec((1, n_blk, GRID_W, 2 * GRID_W), lambda h, i: (h, 0, 0, 0))],
        out_specs=pl.BlockSpec((tq, HEAD_DIM), lambda h, i: (i, h)),
        out_shape=jax.ShapeDtypeStruct((SEQ, NA_W), BF16),
        scratch_shapes=[pltpu.VMEM((SEQ, HEAD_DIM), BF16),
                        pltpu.VMEM((_na_cases()[0].shape[0], nk, nq), F32)],
        compiler_params=_params(("parallel", "arbitrary"), 48),
        name="na_attention",
    )(proj, proj, proj, q_g.reshape(1, HEAD_DIM), k_g.reshape(1, HEAD_DIM), col_blocks)


def _mem_attn_kernel(q_ref, kv_ref, qg_ref, kg_ref, o_ref):
    scale = HEAD_DIM ** -0.5 * LOG2E
    for h in range(N_MEM_HEADS):
        cols = slice(h * HEAD_DIM, (h + 1) * HEAD_DIM)
        qn = (_rms_rows(q_ref[:, cols].astype(F32), qg_ref[...]) * scale).astype(BF16)
        kn = _rms_rows(kv_ref[:, cols].astype(F32), kg_ref[...]).astype(BF16)
        v = kv_ref[:, MEM_W + h * HEAD_DIM:MEM_W + (h + 1) * HEAD_DIM].astype(BF16)
        s = lax.dot_general(qn, kn, (((1,), (1,)), ((), ())), preferred_element_type=F32)
        m = jnp.max(s, axis=-1, keepdims=True)
        p = jnp.exp2(s - m)
        l = jnp.sum(p, axis=-1, keepdims=True)
        o = jnp.dot(p.astype(BF16), v, preferred_element_type=F32) / l
        o_ref[:, cols] = o.astype(o_ref.dtype)


def _mem_attention(proj, kv, q_g, k_g):
    tm = 512
    return pl.pallas_call(
        _mem_attn_kernel,
        grid=(SEQ // tm,),
        in_specs=[pl.BlockSpec((tm, MEM_W), lambda i: (i, Q_MEM_OFF // MEM_W)),
                  pl.BlockSpec((N_MEM, 2 * MEM_W), lambda i: (0, 0)),
                  pl.BlockSpec((1, HEAD_DIM), lambda i: (0, 0)),
                  pl.BlockSpec((1, HEAD_DIM), lambda i: (0, 0))],
        out_specs=pl.BlockSpec((tm, MEM_W), lambda i: (i, 0)),
        out_shape=jax.ShapeDtypeStruct((SEQ, MEM_W), BF16),
        compiler_params=_params(("parallel",), 32),
        name="mem_attention",
    )(proj, kv, q_g.reshape(1, HEAD_DIM), k_g.reshape(1, HEAD_DIM))


def _merge_kernel(fm_ref, na_ref, mo_ref, g0_ref, g1_ref, g2_ref, wf_ref, wn_ref, wm_ref, o_ref,
                  *, col_chunk):
    for c in range(D_MODEL // col_chunk):
        cols = slice(c * col_chunk, (c + 1) * col_chunk)
        o_f = jnp.dot(fm_ref[...], wf_ref[:, cols], preferred_element_type=F32)
        o_n = jnp.dot(na_ref[...], wn_ref[:, cols], preferred_element_type=F32)
        o_m = jnp.dot(mo_ref[...], wm_ref[:, cols], preferred_element_type=F32)
        acc = _sigmoid(g0_ref[:, cols].astype(F32)) * o_f
        acc = acc + _sigmoid(g1_ref[:, cols].astype(F32)) * o_n
        acc = acc + _sigmoid(g2_ref[:, cols].astype(F32)) * o_m
        o_ref[:, cols] = acc.astype(o_ref.dtype)


def _merge(fm, o_na, o_mem, proj, wf, wn, wm):
    tm = 512
    gb = GATE_OFF // D_MODEL
    return pl.pallas_call(
        functools.partial(_merge_kernel, col_chunk=512),
        grid=(SEQ // tm,),
        in_specs=[pl.BlockSpec((tm, FOURIER_W), lambda i: (i, 0)),
                  pl.BlockSpec((tm, NA_W), lambda i: (i, 0)),
                  pl.BlockSpec((tm, MEM_W), lambda i: (i, 0)),
                  pl.BlockSpec((tm, D_MODEL), lambda i: (i, gb)),
                  pl.BlockSpec((tm, D_MODEL), lambda i: (i, gb + 1)),
                  pl.BlockSpec((tm, D_MODEL), lambda i: (i, gb + 2)),
                  pl.BlockSpec((FOURIER_W, D_MODEL), lambda i: (0, 0)),
                  pl.BlockSpec((NA_W, D_MODEL), lambda i: (0, 0)),
                  pl.BlockSpec((MEM_W, D_MODEL), lambda i: (0, 0))],
        out_specs=pl.BlockSpec((tm, D_MODEL), lambda i: (i, 0)),
        out_shape=jax.ShapeDtypeStruct((SEQ, D_MODEL), BF16),
        compiler_params=_params(("parallel",), 48),
        name="gated_merge",
    )(fm, o_na, o_mem, proj, proj, proj, wf, wn, wm)


ROUTER_LANES = 128


def _route_top2(h, wr_ref):
    h_hi = h.astype(BF16)
    h_lo = (h - h_hi.astype(F32)).astype(BF16)
    r_hi = jnp.dot(h_hi, wr_ref[...], preferred_element_type=F32)
    r_lo = jnp.dot(h_lo, wr_ref[:, :ROUTER_LANES], preferred_element_type=F32)
    logits = r_hi[:, :ROUTER_LANES] + r_hi[:, ROUTER_LANES:] + r_lo
    lane = lax.broadcasted_iota(jnp.int32, logits.shape, 1).astype(F32)
    logits = jnp.where(lane < N_EXPERTS, logits, -jnp.inf)
    m1 = jnp.max(logits, axis=-1, keepdims=True)
    i1 = jnp.min(jnp.where(logits == m1, lane, float(ROUTER_LANES)), axis=-1, keepdims=True)
    rest = jnp.where(lane == i1, -jnp.inf, logits)
    m2 = jnp.max(rest, axis=-1, keepdims=True)
    i2 = jnp.min(jnp.where(rest == m2, lane, float(ROUTER_LANES)), axis=-1, keepdims=True)
    e21 = jnp.exp(m2 - m1)
    g1 = 1.0 / (1.0 + e21)
    g2 = e21 * g1
    return jnp.where(lane == 0.0, i1, i2).astype(jnp.int32), jnp.where(lane == 0.0, g1, g2)


def _oproj_kernel(a_ref, w_ref, x_ref, g_ref, *rest, row_chunk, route):
    if route:
        wr_ref, xo_ref, h_ref, idx_ref, gate_ref = rest
    else:
        xo_ref, h_ref = rest
    half = xo_ref.shape[0] // 2
    for r in range(2):
        xo_ref[r * half:(r + 1) * half, :] = (
            jnp.dot(a_ref[r * half:(r + 1) * half, :], w_ref[...], preferred_element_type=F32)
            + x_ref[r * half:(r + 1) * half, :])
        for c in range(half // row_chunk):
            rows = slice(r * half + c * row_chunk, r * half + (c + 1) * row_chunk)
            h = _rms_rows(xo_ref[rows, :], g_ref[...])
            if route:
                h_ref[rows, :] = _pack_halves(h)
                idx_ref[rows, :], gate_ref[rows, :] = _route_top2(h, wr_ref)
            else:
                h_ref[rows, :] = h.astype(h_ref.dtype)


def _out_proj(merged, w_o, x, g, w_router=None):
    tm = 512
    route = w_router is not None
    in_specs = [pl.BlockSpec((tm, D_MODEL), lambda i: (i, 0)),
                pl.BlockSpec((D_MODEL, D_MODEL), lambda i: (0, 0)),
                pl.BlockSpec((tm, D_MODEL), lambda i: (i, 0)),
                pl.BlockSpec((1, D_MODEL), lambda i: (0, 0))]
    args = [merged, w_o, x, g.reshape(1, D_MODEL)]
    out_specs = [pl.BlockSpec((tm, D_MODEL), lambda i: (i, 0))]
    out_shape = [jax.ShapeDtypeStruct((SEQ, D_MODEL), F32)]
    if route:
        w_pad = jnp.zeros((D_MODEL, ROUTER_LANES), F32).at[:, :N_EXPERTS].set(w_router.astype(F32))
        w_hi = w_pad.astype(BF16)
        w_lo = (w_pad - w_hi.astype(F32)).astype(BF16)
        in_specs.append(pl.BlockSpec((D_MODEL, 2 * ROUTER_LANES), lambda i: (0, 0)))
        args.append(jnp.concatenate([w_hi, w_lo], axis=1))
        out_specs += [pl.BlockSpec((tm, HALF_D), lambda i: (i, 0)),
                      pl.BlockSpec((tm, ROUTER_LANES), lambda i: (i, 0)),
                      pl.BlockSpec((tm, ROUTER_LANES), lambda i: (i, 0))]
        out_shape += [jax.ShapeDtypeStruct((SEQ, HALF_D), jnp.uint32),
                      jax.ShapeDtypeStruct((SEQ, ROUTER_LANES), jnp.int32),
                      jax.ShapeDtypeStruct((SEQ, ROUTER_LANES), F32)]
    else:
        out_specs.append(pl.BlockSpec((tm, D_MODEL), lambda i: (i, 0)))
        out_shape.append(jax.ShapeDtypeStruct((SEQ, D_MODEL), BF16))
    return pl.pallas_call(
        functools.partial(_oproj_kernel, row_chunk=128, route=route),
        grid=(SEQ // tm,),
        in_specs=in_specs,
        out_specs=out_specs,
        out_shape=out_shape,
        compiler_params=_params(("parallel",), 48),
        name="out_proj",
    )(*args)


def _swiglu_step(x_ref, wg_ref, wu_ref, wd_ref, acc_ref, rows=None):
    x = x_ref[:rows, :]
    g = jnp.dot(x, wg_ref[0].astype(BF16), preferred_element_type=F32)
    u = jnp.dot(x, wu_ref[0].astype(BF16), preferred_element_type=F32)
    a = (g * _sigmoid(g) * u).astype(BF16)
    acc_ref[:rows, :] += jnp.dot(a, wd_ref[0].astype(BF16), preferred_element_type=F32)


def _dense_ffn_kernel(x_ref, wg_ref, wu_ref, wd_ref, resid_ref, o_ref):
    @pl.when(pl.program_id(1) == 0)
    def _():
        o_ref[...] = resid_ref[...]

    _swiglu_step(x_ref, wg_ref, wu_ref, wd_ref, o_ref)


def _dense_ffn(h, wg, wu, wd, layer, resid):
    tm, tf = 1024, 512
    m, d = h.shape
    nf = D_FF // tf
    return pl.pallas_call(
        _dense_ffn_kernel,
        grid=(m // tm, nf),
        in_specs=[pl.BlockSpec((tm, d), lambda i, f: (i, 0)),
                  pl.BlockSpec((1, d, tf), lambda i, f: (layer, 0, f)),
                  pl.BlockSpec((1, d, tf), lambda i, f: (layer, 0, f)),
                  pl.BlockSpec((1, tf, d), lambda i, f: (layer, f, 0)),
                  pl.BlockSpec((tm, d), lambda i, f: (i, 0))],
        out_specs=pl.BlockSpec((tm, d), lambda i, f: (i, 0)),
        out_shape=jax.ShapeDtypeStruct((m, d), F32),
        compiler_params=_params(("parallel", "arbitrary"), 60),
        name="dense_ffn",
    )(h, wg, wu, wd, resid)


MOE_TF = 512
MOE_NF = D_FF // MOE_TF
MOE_ROWS_PER_STEP = 96
MOE_TM = MOE_NF * MOE_ROWS_PER_STEP
MOE_ROW_CHUNK = 96
MOE_N_TILES = -(-SEQ * TOP_K // MOE_TM) + N_EXPERTS
MOE_THIRDS = 3
MOE_THIRD_ROWS = MOE_TM // MOE_THIRDS


def _tile_row_copy(tok_ref, h_ref, xg_ref, sems, tile, slot, r):
    row = tok_ref[tile * MOE_TM + r]
    return pltpu.make_async_copy(h_ref.at[pl.ds(row, 1), :], xg_ref.at[slot, pl.ds(r, 1), :],
                                 sems.at[slot])


def _wait_tile_rows(h_ref, xg_ref, sems, slot):
    pltpu.make_async_copy(h_ref.at[pl.ds(0, MOE_TM), :], xg_ref.at[slot], sems.at[slot]).wait()


def _moe_expert_kernel(be_ref, nu_ref, nt_ref, tok_ref, h_ref, wg_ref, wu_ref, wd_ref, o_ref,
                       xg_ref, xb_ref, acc_ref, sems):
    i = pl.program_id(0)
    f = pl.program_id(1)
    n_used = nu_ref[0]
    active = i < n_used
    slot = i % 2
    n_chunks = MOE_TM // MOE_ROW_CHUNK

    @pl.when(jnp.logical_and(i == 0, f == 0))
    def _():
        def start_row(r, carry):
            _tile_row_copy(tok_ref, h_ref, xg_ref, sems, 0, 0, r).start()
            return carry
        lax.fori_loop(0, MOE_TM, start_row, 0, unroll=8)

    @pl.when(jnp.logical_and(active, f == 0))
    def _():
        _wait_tile_rows(h_ref, xg_ref, sems, slot)
        acc_ref[...] = jnp.zeros(acc_ref.shape, acc_ref.dtype)

        def unpack_rows(c, carry):
            rows = pl.ds(pl.multiple_of(c * MOE_ROW_CHUNK, MOE_ROW_CHUNK), MOE_ROW_CHUNK)
            lo, hi = _unpack_halves(xg_ref[slot, rows, :])
            xb_ref[rows, :HALF_D] = lo.astype(BF16)
            xb_ref[rows, HALF_D:] = hi.astype(BF16)
            return carry
        lax.fori_loop(0, n_chunks, unpack_rows, 0)

    for thirds in range(1, MOE_THIRDS + 1):
        @pl.when(jnp.logical_and(active, nt_ref[i] == thirds))
        def _(thirds=thirds):
            for j in range(MOE_ROWS_PER_STEP):
                _tile_row_copy(tok_ref, h_ref, xg_ref, sems, i + 1, 1 - slot,
                               f * MOE_ROWS_PER_STEP + j).start()
            _swiglu_step(xb_ref, wg_ref, wu_ref, wd_ref, acc_ref, rows=thirds * MOE_THIRD_ROWS)

    @pl.when(jnp.logical_and(i == n_used, f == 0))
    def _():
        _wait_tile_rows(h_ref, xg_ref, sems, slot)

    @pl.when(f == MOE_NF - 1)
    def _():
        @pl.when(active)
        def _():
            def finish_rows(c, carry):
                rows = pl.ds(pl.multiple_of(c * MOE_ROW_CHUNK, MOE_ROW_CHUNK), MOE_ROW_CHUNK)
                o_ref[rows, :] = _pack_halves(acc_ref[rows, :])
                return carry
            lax.fori_loop(0, n_chunks, finish_rows, 0)

        @pl.when(jnp.logical_not(active))
        def _():
            o_ref[...] = jnp.zeros(o_ref.shape, o_ref.dtype)


def _moe_experts(h_packed, tok, blk_e, n_used, n_thirds, wg, wu, wd):
    d, tf = D_MODEL, MOE_TF

    def f_idx(i, f, nu):
        return jnp.where(i < nu[0], f, MOE_NF - 1)

    def w_gu(i, f, be, nu, nt, tok):
        return (be[i], 0, f_idx(i, f, nu))

    def w_d(i, f, be, nu, nt, tok):
        return (be[i], f_idx(i, f, nu), 0)

    grid_spec = pltpu.PrefetchScalarGridSpec(
        num_scalar_prefetch=4,
        grid=(MOE_N_TILES, MOE_NF),
        in_specs=[pl.BlockSpec(memory_space=pl.ANY),
                  pl.BlockSpec((1, d, tf), w_gu),
                  pl.BlockSpec((1, d, tf), w_gu),
                  pl.BlockSpec((1, tf, d), w_d)],
        out_specs=pl.BlockSpec((MOE_TM, HALF_D), lambda i, f, be, nu, nt, tok: (i, 0)),
        scratch_shapes=[pltpu.VMEM((2, MOE_TM, HALF_D), jnp.uint32),
                        pltpu.VMEM((MOE_TM, d), BF16),
                        pltpu.VMEM((MOE_TM, d), F32),
                        pltpu.SemaphoreType.DMA((2,))],
    )
    return pl.pallas_call(
        _moe_expert_kernel,
        grid_spec=grid_spec,
        out_shape=jax.ShapeDtypeStruct((MOE_N_TILES * MOE_TM, HALF_D), jnp.uint32),
        compiler_params=_params(("arbitrary", "arbitrary"), 60),
        name="moe_experts",
    )(blk_e, n_used, n_thirds, tok, h_packed, wg, wu, wd)


def _start_row_gather(idx_ref, idx_base, src_ref, dst_ref, sem, n_rows):
    def body(r, carry):
        row = idx_ref[idx_base + r]
        pltpu.make_async_copy(src_ref.at[pl.ds(row, 1), :], dst_ref.at[pl.ds(r, 1), :], sem).start()
        return carry
    lax.fori_loop(0, n_rows, body, 0, unroll=8)


def _wait_row_gather(src_ref, dst_ref, sem, n_rows):
    pltpu.make_async_copy(src_ref.at[pl.ds(0, n_rows), :], dst_ref, sem).wait()


def _moe_combine_kernel(pos_ref, y_ref, x_ref, gate_ref, o_ref, buf_ref, sems, *, tokens):
    i = pl.program_id(0)
    slot = i % 2
    rows = TOP_K * tokens

    @pl.when(i == 0)
    def _():
        _start_row_gather(pos_ref, 0, y_ref, buf_ref.at[0], sems.at[0], rows)

    @pl.when(i + 1 < pl.num_programs(0))
    def _():
        _start_row_gather(pos_ref, (i + 1) * rows, y_ref, buf_ref.at[1 - slot],
                          sems.at[1 - slot], rows)

    _wait_row_gather(y_ref, buf_ref.at[slot], sems.at[slot], rows)
    g0 = gate_ref[:, 0:1]
    g1 = gate_ref[:, 1:2]
    lo0, hi0 = _unpack_halves(buf_ref[slot, :tokens, :])
    lo1, hi1 = _unpack_halves(buf_ref[slot, tokens:, :])
    o_ref[:, :HALF_D] = x_ref[:, :HALF_D] + (g0 * lo0 + g1 * lo1)
    o_ref[:, HALF_D:] = x_ref[:, HALF_D:] + (g0 * hi0 + g1 * hi1)


def _moe_combine(y_packed, pos, x, gate):
    tokens = 512
    n = x.shape[0]
    steps = n // tokens
    pos_tiled = pos.reshape(steps, tokens, TOP_K).transpose(0, 2, 1).reshape(-1)
    grid_spec = pltpu.PrefetchScalarGridSpec(
        num_scalar_prefetch=1,
        grid=(steps,),
        in_specs=[pl.BlockSpec(memory_space=pl.ANY),
                  pl.BlockSpec((tokens, D_MODEL), lambda i, pos_ref: (i, 0)),
                  pl.BlockSpec((tokens, ROUTER_LANES), lambda i, pos_ref: (i, 0))],
        out_specs=pl.BlockSpec((tokens, D_MODEL), lambda i, pos_ref: (i, 0)),
        scratch_shapes=[pltpu.VMEM((2, TOP_K * tokens, HALF_D), jnp.uint32),
                        pltpu.SemaphoreType.DMA((2,))],
    )
    return pl.pallas_call(
        functools.partial(_moe_combine_kernel, tokens=tokens),
        grid_spec=grid_spec,
        out_shape=jax.ShapeDtypeStruct((n, D_MODEL), F32),
        compiler_params=_params(("arbitrary",), 40),
        name="moe_combine",
    )(pos_tiled, y_packed, x, gate)


def _slot_tokens_kernel(dest_ref, fill_ref, tok_ref):
    for e in range(N_EXPERTS + 1):
        def fill(b, carry):
            for j in range(8):
                tok_ref[b * 8 + j] = 0
            return carry
        lax.fori_loop(fill_ref[2 * e], fill_ref[2 * e + 1], fill, 0)

    def place(a, carry):
        tok_ref[dest_ref[a]] = a >> 1
        return carry
    lax.fori_loop(0, SEQ * TOP_K, place, 0, unroll=8)


def _slot_tokens(dest, fill_ranges):
    assert TOP_K == 2
    cap = MOE_N_TILES * MOE_TM
    grid_spec = pltpu.PrefetchScalarGridSpec(
        num_scalar_prefetch=2,
        grid=(1,),
        in_specs=[],
        out_specs=pl.BlockSpec(memory_space=pltpu.SMEM),
    )
    return pl.pallas_call(
        _slot_tokens_kernel,
        grid_spec=grid_spec,
        out_shape=jax.ShapeDtypeStruct((cap,), jnp.int32),
        name="moe_slot_tokens",
    )(dest, fill_ranges)


def _moe(x_new, h_packed, idx, gate, wg, wu, wd):
    n = SEQ
    n_assign = n * TOP_K
    cap = MOE_N_TILES * MOE_TM
    e_flat = idx[:, :TOP_K].reshape(-1)
    onehot = (e_flat[:, None] == jnp.arange(N_EXPERTS, dtype=jnp.int32)[None, :]).astype(jnp.int32)
    csum = jnp.cumsum(onehot, axis=0)
    counts = csum[-1]
    rank = jnp.sum((csum - onehot) * onehot, axis=1)
    padded = (counts + MOE_TM - 1) // MOE_TM * MOE_TM
    pad_ends = jnp.cumsum(padded)
    pad_starts = pad_ends - padded
    dest = jnp.sum(onehot * pad_starts[None, :], axis=1) + rank
    fill_lo = jnp.concatenate([pad_starts + counts, pad_ends[-1:]]) // 8
    fill_hi = (jnp.concatenate([pad_ends, jnp.full((1,), cap, pad_ends.dtype)]) + 7) // 8
    tok = _slot_tokens(dest.astype(jnp.int32),
                       jnp.stack([fill_lo, fill_hi], axis=1).reshape(-1).astype(jnp.int32))
    n_used = (pad_ends[-1] // MOE_TM).astype(jnp.int32).reshape(1)
    tile_start = jnp.arange(MOE_N_TILES, dtype=jnp.int32) * MOE_TM
    tile_start = jnp.minimum(tile_start, pad_ends[-1] - 1)
    blk_e = jnp.sum((tile_start[:, None] >= pad_ends[None, :]).astype(jnp.int32), axis=1)
    blk_e = jnp.clip(blk_e, 0, N_EXPERTS - 1)
    real_rows = counts[blk_e] - (tile_start - pad_starts[blk_e])
    n_thirds = jnp.clip((real_rows + MOE_THIRD_ROWS - 1) // MOE_THIRD_ROWS, 1, MOE_THIRDS)
    yb = _moe_experts(h_packed, tok, blk_e, n_used, n_thirds.astype(jnp.int32), wg, wu, wd)
    return _moe_combine(yb, dest.reshape(n, TOP_K), x_new, gate)


def kernel(x, mem, ln_mix_g, w_in, na_q_g, na_k_g, na_rpb, mem_ln_g, w_mem_kv, mem_q_g, mem_k_g,
           w_fourier_out, w_na_out, w_mem_out, w_o, ln_ffn_g, ffn_w_gate, ffn_w_up, ffn_w_down,
           moe_router, moe_w_gate, moe_w_up, moe_w_down):
    assert x.shape == (1, SEQ, D_MODEL) and mem.shape == (1, N_MEM, D_MODEL)
    xs = x.reshape(SEQ, D_MODEL)
    mem2 = mem.reshape(N_MEM, D_MODEL)
    tables = _dft_tables()
    w_in_bf16 = w_in.astype(BF16)
    for l in range(DEPTH):
        proj = _norm_mm(xs, ln_mix_g[l], w_in_bf16, l, tm=1024, tn=2048, out_dtype=BF16,
                        vmem_mib=58, name="in_proj")
        fm = _fourier_mix(proj, tables)
        o_na = _na_attention(proj, na_q_g[l], na_k_g[l], _na_col_blocks(na_rpb[l]))
        kv = _norm_mm(mem2, mem_ln_g[l], w_mem_kv, l, tm=N_MEM, tn=2 * MEM_W, out_dtype=F32,
                      vmem_mib=40, name="mem_kv_proj")
        o_mem = _mem_attention(proj, kv, mem_q_g[l], mem_k_g[l])
        merged = _merge(fm, o_na, o_mem, proj, w_fourier_out[l].astype(BF16),
                        w_na_out[l].astype(BF16), w_mem_out[l].astype(BF16))
        i = l // 2
        if l % 2 == 0:
            x_new, h2 = _out_proj(merged, w_o[l].astype(BF16), xs, ln_ffn_g[l])
            xs = _dense_ffn(h2, ffn_w_gate.astype(BF16), ffn_w_up.astype(BF16),
                            ffn_w_down.astype(BF16), i, x_new)
        else:
            x_new, h2, idx, gate = _out_proj(merged, w_o[l].astype(BF16), xs, ln_ffn_g[l],
                                             moe_router[i])
            xs = _moe(x_new, h2, idx, gate, moe_w_gate[i], moe_w_up[i], moe_w_down[i])
    return xs.reshape(1, SEQ, D_MODEL)
```

```python
import functools
import math

import numpy as np
import jax
import jax.numpy as jnp
from jax import lax
from jax.experimental import pallas as pl
from jax.experimental.pallas import tpu as pltpu

F32 = jnp.float32
BF16 = jnp.bfloat16

D_MODEL = 2048
SEQ = 16384
DEPTH = 2
GRID_W = 64
ROWS = SEQ // GRID_W
HEAD_DIM = 128
N_FOURIER_GROUPS = 4
FOURIER_W = N_FOURIER_GROUPS * HEAD_DIM
N_NA_HEADS = 8
NA_W = N_NA_HEADS * HEAD_DIM
WIN_H = 8
WIN_W = 16
N_MEM = 256
N_MEM_HEADS = 4
MEM_W = N_MEM_HEADS * HEAD_DIM
IN_W = FOURIER_W + 3 * NA_W + MEM_W + 3 * D_MODEL
D_FF = 5632
N_EXPERTS = 8
TOP_K = 2
EPS = 1e-6
HALF_D = D_MODEL // 2

Q_NA_OFF = FOURIER_W
K_NA_OFF = Q_NA_OFF + NA_W
V_NA_OFF = K_NA_OFF + NA_W
Q_MEM_OFF = V_NA_OFF + NA_W
GATE_OFF = Q_MEM_OFF + MEM_W

MIB = 1024 * 1024
NEG_BIG = -1e30

FFT_A = 128
FFT_B = SEQ // FFT_A


def _params(semantics, vmem_mib):
    return pltpu.CompilerParams(dimension_semantics=semantics,
                                vmem_limit_bytes=int(vmem_mib * MIB))


def _rms_rows(x, gain):
    ms = jnp.mean(x * x, axis=-1, keepdims=True)
    return x * lax.rsqrt(ms + EPS) * gain


def _sigmoid(x):
    return 0.5 * jnp.tanh(0.5 * x) + 0.5


def _bf16_bits(v):
    return pltpu.bitcast(v.astype(BF16).astype(F32), jnp.uint32)


def _pack_halves(v):
    half = v.shape[1] // 2
    lo = lax.shift_right_logical(_bf16_bits(v[:, :half]), jnp.uint32(16))
    return lo | _bf16_bits(v[:, half:])


def _unpack_halves(w):
    lo = pltpu.bitcast(lax.shift_left(w, jnp.uint32(16)), F32)
    hi = pltpu.bitcast(w & jnp.uint32(0xFFFF0000), F32)
    return lo, hi


def _norm_mm_kernel(x_ref, g_ref, w_ref, o_ref, h_ref, *, row_chunk):
    @pl.when(pl.program_id(1) == 0)
    def _():
        def body(c, carry):
            rows = pl.ds(pl.multiple_of(c * row_chunk, row_chunk), row_chunk)
            h_ref[rows, :] = _rms_rows(x_ref[rows, :].astype(F32), g_ref[...]).astype(BF16)
            return carry
        lax.fori_loop(0, h_ref.shape[0] // row_chunk, body, 0)

    o_ref[...] = jnp.dot(h_ref[...], w_ref[0].astype(BF16),
                         preferred_element_type=F32).astype(o_ref.dtype)


def _norm_mm(x, g, w, layer, *, tm, tn, out_dtype, vmem_mib, name):
    m, k = x.shape
    n = w.shape[2]
    row_chunk = min(tm, 128)
    return pl.pallas_call(
        functools.partial(_norm_mm_kernel, row_chunk=row_chunk),
        grid=(m // tm, n // tn),
        in_specs=[pl.BlockSpec((tm, k), lambda i, j: (i, 0)),
                  pl.BlockSpec((1, k), lambda i, j: (0, 0)),
                  pl.BlockSpec((1, k, tn), lambda i, j: (layer, 0, j))],
        out_specs=pl.BlockSpec((tm, tn), lambda i, j: (i, j)),
        out_shape=jax.ShapeDtypeStruct((m, n), out_dtype),
        scratch_shapes=[pltpu.VMEM((tm, k), BF16)],
        compiler_params=_params(("parallel", "arbitrary"), vmem_mib),
        name=name,
    )(x, g.reshape(1, k), w)


def _dft_tables():
    a = np.arange(FFT_A)
    ang1 = 2.0 * np.pi * np.outer(a, a) / FFT_A
    w1 = np.concatenate([np.cos(ang1), -np.sin(ang1)], axis=0)
    k1 = np.arange(FFT_A)[:, None, None]
    k2 = np.arange(FFT_B)[None, :, None]
    s2 = np.arange(FFT_B)[None, None, :]
    ang2 = 2.0 * np.pi * ((s2 * (k1 + FFT_A * k2)) % SEQ) / SEQ
    g2 = np.concatenate([np.cos(ang2), np.sin(ang2)], axis=2)
    c = np.arange(HEAD_DIM)
    angc = 2.0 * np.pi * np.outer(c, c) / HEAD_DIM
    eye = np.eye(N_FOURIER_GROUPS)
    bd = np.concatenate([np.kron(eye, np.cos(angc)), np.kron(eye, np.sin(angc))], axis=0)
    return (jnp.asarray(w1, dtype=BF16), jnp.asarray(g2, dtype=BF16),
            jnp.asarray(bd, dtype=BF16))


FFT_BLK = 16


def _dft1_kernel(w_ref, x_ref, br_ref, bi_ref):
    xt = pltpu.einshape("abc->bac", x_ref[...])
    for j in range(FFT_BLK):
        r = jnp.dot(w_ref[...], xt[j], preferred_element_type=F32)
        br_ref[j] = r[:FFT_A].astype(BF16)
        bi_ref[j] = r[FFT_A:].astype(BF16)


def _dft2_kernel(br_ref, bi_ref, g_ref, bd_ref, o_ref, *, norm):
    brt = pltpu.einshape("abc->bac", br_ref[...])
    bit = pltpu.einshape("abc->bac", bi_ref[...])
    ys = []
    for i in range(FFT_BLK):
        br = brt[i]
        bi = bit[i]
        g = g_ref[i]
        zr = jnp.dot(g, jnp.concatenate([br, bi], axis=0), preferred_element_type=F32)
        zi = jnp.dot(g, jnp.concatenate([bi, -br], axis=0), preferred_element_type=F32)
        z = jnp.concatenate([zr, zi], axis=1).astype(BF16)
        y = jnp.dot(z, bd_ref[...], preferred_element_type=F32) * norm
        ys.append(y.astype(o_ref.dtype))
    o_ref[...] = pltpu.einshape("abc->bac", jnp.stack(ys, axis=0))


def _fourier_mix(proj, tables):
    w1, g2, bd = tables
    x3 = proj.reshape(FFT_A, FFT_B, IN_W)
    blk = (FFT_A, FFT_BLK, FOURIER_W)
    br, bi = pl.pallas_call(
        _dft1_kernel,
        grid=(FFT_B // FFT_BLK,),
        in_specs=[pl.BlockSpec((2 * FFT_A, FFT_A), lambda j: (0, 0)),
                  pl.BlockSpec(blk, lambda j: (0, j, 0))],
        out_specs=[pl.BlockSpec((FFT_BLK, FFT_A, FOURIER_W), lambda j: (j, 0, 0)),
                   pl.BlockSpec((FFT_BLK, FFT_A, FOURIER_W), lambda j: (j, 0, 0))],
        out_shape=[jax.ShapeDtypeStruct((FFT_B, FFT_A, FOURIER_W), BF16)] * 2,
        compiler_params=_params(("parallel",), 40),
        name="fourier_stage1",
    )(w1, x3)
    norm = 1.0 / math.sqrt(SEQ * HEAD_DIM)
    y3 = pl.pallas_call(
        functools.partial(_dft2_kernel, norm=norm),
        grid=(FFT_A // FFT_BLK,),
        in_specs=[pl.BlockSpec((FFT_B, FFT_BLK, FOURIER_W), lambda j: (0, j, 0)),
                  pl.BlockSpec((FFT_B, FFT_BLK, FOURIER_W), lambda j: (0, j, 0)),
                  pl.BlockSpec((FFT_BLK, FFT_B, 2 * FFT_B), lambda j: (j, 0, 0)),
                  pl.BlockSpec((2 * FOURIER_W, FOURIER_W), lambda j: (0, 0))],
        out_specs=pl.BlockSpec((FFT_B, FFT_BLK, FOURIER_W), lambda j: (0, j, 0)),
        out_shape=jax.ShapeDtypeStruct((FFT_B, FFT_A, FOURIER_W), BF16),
        compiler_params=_params(("parallel",), 40),
        name="fourier_stage2",
    )(br, bi, g2, bd)
    return y3.reshape(SEQ, FOURIER_W)


NA_SUB_ROWS = 2
NA_KEY_ROWS = NA_SUB_ROWS + WIN_H
NA_SUBS_PER_STEP = 32
NA_N_SUB = ROWS // NA_SUB_ROWS
NA_MAX_KEY_START = ROWS - NA_KEY_ROWS
NA_N_ROW_OFFSETS = 2 * WIN_H - 1
NA_MASKED_BLOCK = NA_N_ROW_OFFSETS
LOG2E = math.log2(math.e)


def _na_key_row0(sub):
    return jnp.clip(sub * NA_SUB_ROWS - WIN_H // 2, 0, NA_MAX_KEY_START)


def _na_cases():
    patterns, case_of_sub = [], []
    for sub in range(NA_N_SUB):
        r0 = sub * NA_SUB_ROWS
        ks = int(np.clip(r0 - WIN_H // 2, 0, NA_MAX_KEY_START))
        pat = np.full((NA_SUB_ROWS, NA_KEY_ROWS), NA_MASKED_BLOCK, np.int32)
        for qi in range(NA_SUB_ROWS):
            r = r0 + qi
            r_start = int(np.clip(r - WIN_H // 2, 0, ROWS - WIN_H))
            for kj in range(NA_KEY_ROWS):
                if r_start <= ks + kj < r_start + WIN_H:
                    pat[qi, kj] = ks + kj - r + (WIN_H - 1)
        for c, p in enumerate(patterns):
            if np.array_equal(p, pat):
                case_of_sub.append(c)
                break
        else:
            case_of_sub.append(len(patterns))
            patterns.append(pat)
    return np.stack(patterns), np.asarray(case_of_sub, np.int32)


def _na_case_of(sub, case_of_sub):
    common = int(np.bincount(case_of_sub).argmax())
    case = jnp.int32(common)
    for s in np.nonzero(case_of_sub != common)[0]:
        case = jnp.where(sub == int(s), int(case_of_sub[s]), case)
    return case


def _na_col_blocks(rpb):
    qc = np.arange(GRID_W)[:, None]
    kc = np.arange(GRID_W)[None, :]
    col_start = np.clip(qc - WIN_W // 2, 0, GRID_W - WIN_W)
    col_valid = (kc >= col_start) & (kc < col_start + WIN_W)
    dc = kc - qc + (WIN_W - 1)
    sel = (np.arange(2 * WIN_W - 1)[:, None, None] == dc[None]) & col_valid[None]
    sel = jnp.asarray(sel.astype(np.float32))
    t = jnp.einsum("hrd,dqk->hrqk", rpb.astype(F32), sel, precision=lax.Precision.HIGHEST)
    t = jnp.where(col_valid[None, None], t * LOG2E, NEG_BIG)
    masked = jnp.full((N_NA_HEADS, 1, GRID_W, GRID_W), NEG_BIG, F32)
    t = jnp.concatenate([t, masked], axis=1)
    t = jnp.swapaxes(t, -1, -2)
    return jnp.concatenate([t, t], axis=-1)


def _na_kernel(q_ref, k_ref, v_ref, qg_ref, kg_ref, cb_ref, o_ref, kn_ref, bias_ref):
    step = pl.program_id(1)
    nq = NA_SUB_ROWS * GRID_W
    nk = NA_KEY_ROWS * GRID_W
    row_off, case_of_sub = _na_cases()

    @pl.when(step == 0)
    def _():
        chunk = 1024

        def body(c, carry):
            rows = pl.ds(pl.multiple_of(c * chunk, chunk), chunk)
            kn_ref[rows, :] = _rms_rows(k_ref[rows, :].astype(F32), kg_ref[...]).astype(BF16)
            return carry
        lax.fori_loop(0, SEQ // chunk, body, 0)

        for case in range(row_off.shape[0]):
            for qi in range(NA_SUB_ROWS):
                for kj in range(NA_KEY_ROWS):
                    lanes = slice(qi * GRID_W, (qi + 1) * GRID_W)
                    src = slice((qi % 2) * GRID_W, (qi % 2 + 1) * GRID_W)
                    bias_ref[case, kj * GRID_W:(kj + 1) * GRID_W, lanes] = (
                        cb_ref[0, int(row_off[case, qi, kj]), :, src])

    scale = HEAD_DIM ** -0.5 * LOG2E
    for sb in range(NA_SUBS_PER_STEP):
        sub = step * NA_SUBS_PER_STEP + sb
        case = _na_case_of(sub, case_of_sub)
        q = q_ref[sb * nq:(sb + 1) * nq, :].astype(F32)
        qn = (_rms_rows(q, qg_ref[...]) * scale).astype(BF16)
        kstart = pl.multiple_of(_na_key_row0(sub) * GRID_W, 2 * GRID_W)
        kw = kn_ref[pl.ds(kstart, nk), :]
        vw = v_ref[pl.ds(kstart, nk), :]
        st = lax.dot_general(kw, qn, (((1,), (1,)), ((), ())), preferred_element_type=F32)
        st = st + bias_ref[case]
        m = jnp.max(st, axis=0, keepdims=True)
        pt = jnp.exp2(st - m)
        l = jnp.sum(pt, axis=0, keepdims=True)
        pt = (pt * (1.0 / l)).astype(BF16)
        o = lax.dot_general(pt, vw, (((0,), (0,)), ((), ())), preferred_element_type=F32)
        o_ref[sb * nq:(sb + 1) * nq, :] = o.astype(o_ref.dtype)


def _na_attention(proj, q_g, k_g, col_blocks):
    nq = NA_SUB_ROWS * GRID_W
    nk = NA_KEY_ROWS * GRID_W
    tq = NA_SUBS_PER_STEP * nq
    n_blk = NA_N_ROW_OFFSETS + 1
    qb, kb_, vb = Q_NA_OFF // HEAD_DIM, K_NA_OFF // HEAD_DIM, V_NA_OFF // HEAD_DIM
    return pl.pallas_call(
        _na_kernel,
        grid=(N_NA_HEADS, SEQ // tq),
        in_specs=[pl.BlockSpec((tq, HEAD_DIM), lambda h, i: (i, qb + h)),
                  pl.BlockSpec((SEQ, HEAD_DIM), lambda h, i: (0, kb_ + h)),
                  pl.BlockSpec((SEQ, HEAD_DIM), lambda h, i: (0, vb + h)),
                  pl.BlockSpec((1, HEAD_DIM), lambda h, i: (0, 0)),
                  pl.BlockSpec((1, HEAD_DIM), lambda h, i: (0, 0)),
                  pl.BlockSpec((1, n_blk, GRID_W, 2 * GRID_W), lambda h, i: (h, 0, 0, 0))],
        out_specs=pl.BlockSpec((tq, HEAD_DIM), lambda h, i: (i, h)),
        out_shape=jax.ShapeDtypeStruct((SEQ, NA_W), BF16),
        scratch_shapes=[pltpu.VMEM((SEQ, HEAD_DIM), BF16),
                        pltpu.VMEM((_na_cases()[0].shape[0], nk, nq), F32)],
        compiler_params=_params(("parallel", "arbitrary"), 48),
        name="na_attention",
    )(proj, proj, proj, q_g.reshape(1, HEAD_DIM), k_g.reshape(1, HEAD_DIM), col_blocks)


def _mem_attn_kernel(q_ref, kv_ref, qg_ref, kg_ref, o_ref):
    scale = HEAD_DIM ** -0.5 * LOG2E
    for h in range(N_MEM_HEADS):
        cols = slice(h * HEAD_DIM, (h + 1) * HEAD_DIM)
        qn = (_rms_rows(q_ref[:, cols].astype(F32), qg_ref[...]) * scale).astype(BF16)
        kn = _rms_rows(kv_ref[:, cols].astype(F32), kg_ref[...]).astype(BF16)
        v = kv_ref[:, MEM_W + h * HEAD_DIM:MEM_W + (h + 1) * HEAD_DIM].astype(BF16)
        s = lax.dot_general(qn, kn, (((1,), (1,)), ((), ())), preferred_element_type=F32)
        m = jnp.max(s, axis=-1, keepdims=True)
        p = jnp.exp2(s - m)
        l = jnp.sum(p, axis=-1, keepdims=True)
        o = jnp.dot(p.astype(BF16), v, preferred_element_type=F32) / l
        o_ref[:, cols] = o.astype(o_ref.dtype)


def _mem_attention(proj, kv, q_g, k_g):
    tm = 512
    return pl.pallas_call(
        _mem_attn_kernel,
        grid=(SEQ // tm,),
        in_specs=[pl.BlockSpec((tm, MEM_W), lambda i: (i, Q_MEM_OFF // MEM_W)),
                  pl.BlockSpec((N_MEM, 2 * MEM_W), lambda i: (0, 0)),
                  pl.BlockSpec((1, HEAD_DIM), lambda i: (0, 0)),
                  pl.BlockSpec((1, HEAD_DIM), lambda i: (0, 0))],
        out_specs=pl.BlockSpec((tm, MEM_W), lambda i: (i, 0)),
        out_shape=jax.ShapeDtypeStruct((SEQ, MEM_W), BF16),
        compiler_params=_params(("parallel",), 32),
        name="mem_attention",
    )(proj, kv, q_g.reshape(1, HEAD_DIM), k_g.reshape(1, HEAD_DIM))


def _merge_kernel(fm_ref, na_ref, mo_ref, g0_ref, g1_ref, g2_ref, wf_ref, wn_ref, wm_ref, o_ref,
                  *, col_chunk):
    for c in range(D_MODEL // col_chunk):
        cols = slice(c * col_chunk, (c + 1) * col_chunk)
        o_f = jnp.dot(fm_ref[...], wf_ref[:, cols], preferred_element_type=F32)
        o_n = jnp.dot(na_ref[...], wn_ref[:, cols], preferred_element_type=F32)
        o_m = jnp.dot(mo_ref[...], wm_ref[:, cols], preferred_element_type=F32)
        acc = jax.nn.sigmoid(g0_ref[:, cols].astype(F32)) * o_f
        acc = acc + jax.nn.sigmoid(g1_ref[:, cols].astype(F32)) * o_n
        acc = acc + jax.nn.sigmoid(g2_ref[:, cols].astype(F32)) * o_m
        o_ref[:, cols] = acc.astype(o_ref.dtype)


def _merge(fm, o_na, o_mem, proj, wf, wn, wm):
    tm = 512
    gb = GATE_OFF // D_MODEL
    return pl.pallas_call(
        functools.partial(_merge_kernel, col_chunk=512),
        grid=(SEQ // tm,),
        in_specs=[pl.BlockSpec((tm, FOURIER_W), lambda i: (i, 0)),
                  pl.BlockSpec((tm, NA_W), lambda i: (i, 0)),
                  pl.BlockSpec((tm, MEM_W), lambda i: (i, 0)),
                  pl.BlockSpec((tm, D_MODEL), lambda i: (i, gb)),
                  pl.BlockSpec((tm, D_MODEL), lambda i: (i, gb + 1)),
                  pl.BlockSpec((tm, D_MODEL), lambda i: (i, gb + 2)),
                  pl.BlockSpec((FOURIER_W, D_MODEL), lambda i: (0, 0)),
                  pl.BlockSpec((NA_W, D_MODEL), lambda i: (0, 0)),
                  pl.BlockSpec((MEM_W, D_MODEL), lambda i: (0, 0))],
        out_specs=pl.BlockSpec((tm, D_MODEL), lambda i: (i, 0)),
        out_shape=jax.ShapeDtypeStruct((SEQ, D_MODEL), BF16),
        compiler_params=_params(("parallel",), 48),
        name="gated_merge",
    )(fm, o_na, o_mem, proj, proj, proj, wf, wn, wm)


ROUTER_LANES = 128


def _route_top2(h, wr_ref):
    h_hi = h.astype(BF16)
    h_lo = (h - h_hi.astype(F32)).astype(BF16)
    r_hi = jnp.dot(h_hi, wr_ref[...], preferred_element_type=F32)
    r_lo = jnp.dot(h_lo, wr_ref[:, :ROUTER_LANES], preferred_element_type=F32)
    logits = r_hi[:, :ROUTER_LANES] + r_hi[:, ROUTER_LANES:] + r_lo
    lane = lax.broadcasted_iota(jnp.int32, logits.shape, 1).astype(F32)
    logits = jnp.where(lane < N_EXPERTS, logits, -jnp.inf)
    m1 = jnp.max(logits, axis=-1, keepdims=True)
    i1 = jnp.min(jnp.where(logits == m1, lane, float(ROUTER_LANES)), axis=-1, keepdims=True)
    rest = jnp.where(lane == i1, -jnp.inf, logits)
    m2 = jnp.max(rest, axis=-1, keepdims=True)
    i2 = jnp.min(jnp.where(rest == m2, lane, float(ROUTER_LANES)), axis=-1, keepdims=True)
    e21 = jnp.exp(m2 - m1)
    g1 = 1.0 / (1.0 + e21)
    g2 = e21 * g1
    return jnp.where(lane == 0.0, i1, i2).astype(jnp.int32), jnp.where(lane == 0.0, g1, g2)


def _oproj_kernel(a_ref, w_ref, x_ref, g_ref, *rest, row_chunk, route):
    if route:
        wr_ref, xo_ref, h_ref, idx_ref, gate_ref = rest
    else:
        xo_ref, h_ref = rest
    half = xo_ref.shape[0] // 2
    for r in range(2):
        xo_ref[r * half:(r + 1) * half, :] = (
            jnp.dot(a_ref[r * half:(r + 1) * half, :], w_ref[...], preferred_element_type=F32)
            + x_ref[r * half:(r + 1) * half, :])
        for c in range(half // row_chunk):
            rows = slice(r * half + c * row_chunk, r * half + (c + 1) * row_chunk)
            h = _rms_rows(xo_ref[rows, :], g_ref[...])
            if route:
                h_ref[rows, :] = _pack_halves(h)
                idx_ref[rows, :], gate_ref[rows, :] = _route_top2(h, wr_ref)
            else:
                h_ref[rows, :] = h.astype(h_ref.dtype)


def _out_proj(merged, w_o, x, g, w_router=None):
    tm = 512
    route = w_router is not None
    in_specs = [pl.BlockSpec((tm, D_MODEL), lambda i: (i, 0)),
                pl.BlockSpec((D_MODEL, D_MODEL), lambda i: (0, 0)),
                pl.BlockSpec((tm, D_MODEL), lambda i: (i, 0)),
                pl.BlockSpec((1, D_MODEL), lambda i: (0, 0))]
    args = [merged, w_o, x, g.reshape(1, D_MODEL)]
    out_specs = [pl.BlockSpec((tm, D_MODEL), lambda i: (i, 0))]
    out_shape = [jax.ShapeDtypeStruct((SEQ, D_MODEL), F32)]
    if route:
        w_pad = jnp.zeros((D_MODEL, ROUTER_LANES), F32).at[:, :N_EXPERTS].set(w_router.astype(F32))
        w_hi = w_pad.astype(BF16)
        w_lo = (w_pad - w_hi.astype(F32)).astype(BF16)
        in_specs.append(pl.BlockSpec((D_MODEL, 2 * ROUTER_LANES), lambda i: (0, 0)))
        args.append(jnp.concatenate([w_hi, w_lo], axis=1))
        out_specs += [pl.BlockSpec((tm, HALF_D), lambda i: (i, 0)),
                      pl.BlockSpec((tm, ROUTER_LANES), lambda i: (i, 0)),
                      pl.BlockSpec((tm, ROUTER_LANES), lambda i: (i, 0))]
        out_shape += [jax.ShapeDtypeStruct((SEQ, HALF_D), jnp.uint32),
                      jax.ShapeDtypeStruct((SEQ, ROUTER_LANES), jnp.int32),
                      jax.ShapeDtypeStruct((SEQ, ROUTER_LANES), F32)]
    else:
        out_specs.append(pl.BlockSpec((tm, D_MODEL), lambda i: (i, 0)))
        out_shape.append(jax.ShapeDtypeStruct((SEQ, D_MODEL), BF16))
    return pl.pallas_call(
        functools.partial(_oproj_kernel, row_chunk=128, route=route),
        grid=(SEQ // tm,),
        in_specs=in_specs,
        out_specs=out_specs,
        out_shape=out_shape,
        compiler_params=_params(("parallel",), 48),
        name="out_proj",
    )(*args)


def _swiglu_step(x_ref, wg_ref, wu_ref, wd_ref, acc_ref, rows=None):
    x = x_ref[:rows, :]
    g = jnp.dot(x, wg_ref[0].astype(BF16), preferred_element_type=F32)
    u = jnp.dot(x, wu_ref[0].astype(BF16), preferred_element_type=F32)
    a = (g * _sigmoid(g) * u).astype(BF16)
    acc_ref[:rows, :] += jnp.dot(a, wd_ref[0].astype(BF16), preferred_element_type=F32)


def _dense_ffn_kernel(x_ref, wg_ref, wu_ref, wd_ref, resid_ref, o_ref):
    @pl.when(pl.program_id(1) == 0)
    def _():
        o_ref[...] = resid_ref[...]

    _swiglu_step(x_ref, wg_ref, wu_ref, wd_ref, o_ref)


def _dense_ffn(h, wg, wu, wd, layer, resid):
    tm, tf = 1024, 512
    m, d = h.shape
    nf = D_FF // tf
    return pl.pallas_call(
        _dense_ffn_kernel,
        grid=(m // tm, nf),
        in_specs=[pl.BlockSpec((tm, d), lambda i, f: (i, 0)),
                  pl.BlockSpec((1, d, tf), lambda i, f: (layer, 0, f)),
                  pl.BlockSpec((1, d, tf), lambda i, f: (layer, 0, f)),
                  pl.BlockSpec((1, tf, d), lambda i, f: (layer, f, 0)),
                  pl.BlockSpec((tm, d), lambda i, f: (i, 0))],
        out_specs=pl.BlockSpec((tm, d), lambda i, f: (i, 0)),
        out_shape=jax.ShapeDtypeStruct((m, d), F32),
        compiler_params=_params(("parallel", "arbitrary"), 60),
        name="dense_ffn",
    )(h, wg, wu, wd, resid)


MOE_TF = 512
MOE_NF = D_FF // MOE_TF
MOE_ROWS_PER_STEP = 96
MOE_TM = MOE_NF * MOE_ROWS_PER_STEP
MOE_ROW_CHUNK = 96
MOE_N_TILES = -(-SEQ * TOP_K // MOE_TM) + N_EXPERTS
MOE_THIRDS = 3
MOE_THIRD_ROWS = MOE_TM // MOE_THIRDS


MOE_CAP = MOE_N_TILES * MOE_TM
MOE_OUT_ROWS = SEQ * TOP_K + MOE_TM


def _slot_assignment(asg_ref, tile, r):
    return asg_ref[(tile + 1) * MOE_TM + r]


def _gather_copy(asg_ref, h_ref, xg_ref, sems, tile, slot, r):
    row = jnp.minimum(_slot_assignment(asg_ref, tile, r) >> 1, SEQ - 1)
    return pltpu.make_async_copy(h_ref.at[pl.ds(row, 1), :], xg_ref.at[slot, pl.ds(r, 1), :],
                                 sems.at[slot])


def _wait_gather(h_ref, xg_ref, sems, slot):
    pltpu.make_async_copy(h_ref.at[pl.ds(0, MOE_TM), :], xg_ref.at[slot], sems.at[slot]).wait()


def _scatter_copy(asg_ref, yo_ref, out_ref, sems, tile, slot, r):
    dst = _slot_assignment(asg_ref, tile, r)
    return pltpu.make_async_copy(yo_ref.at[slot, pl.ds(r, 1), :], out_ref.at[pl.ds(dst, 1), :],
                                 sems.at[slot])


def _wait_scatter(yo_ref, out_ref, sems, slot):
    pltpu.make_async_copy(yo_ref.at[slot], out_ref.at[pl.ds(0, MOE_TM), :], sems.at[slot]).wait()


def _moe_expert_kernel(be_ref, nu_ref, nt_ref, asg_ref, h_ref, wg_ref, wu_ref, wd_ref, out_ref,
                       xg_ref, xb_ref, acc_ref, yo_ref, gsems, ssems):
    i = pl.program_id(0)
    f = pl.program_id(1)
    n_used = nu_ref[0]
    active = i < n_used
    slot = i % 2
    n_chunks = MOE_TM // MOE_ROW_CHUNK

    @pl.when(jnp.logical_and(i == 0, f == 0))
    def _():
        def start_row(r, carry):
            _gather_copy(asg_ref, h_ref, xg_ref, gsems, 0, 0, r).start()
            return carry
        lax.fori_loop(0, MOE_TM, start_row, 0, unroll=8)
        yo_ref[1] = jnp.zeros(yo_ref.shape[1:], yo_ref.dtype)

    @pl.when(jnp.logical_and(active, f == 0))
    def _():
        _wait_gather(h_ref, xg_ref, gsems, slot)
        acc_ref[...] = jnp.zeros(acc_ref.shape, acc_ref.dtype)

        def unpack_rows(c, carry):
            rows = pl.ds(pl.multiple_of(c * MOE_ROW_CHUNK, MOE_ROW_CHUNK), MOE_ROW_CHUNK)
            lo, hi = _unpack_halves(xg_ref[slot, rows, :])
            xb_ref[rows, :HALF_D] = lo.astype(BF16)
            xb_ref[rows, HALF_D:] = hi.astype(BF16)
            return carry
        lax.fori_loop(0, n_chunks, unpack_rows, 0)

    for thirds in range(1, MOE_THIRDS + 1):
        @pl.when(jnp.logical_and(active, nt_ref[i] == thirds))
        def _(thirds=thirds):
            for j in range(MOE_ROWS_PER_STEP):
                r = f * MOE_ROWS_PER_STEP + j
                _gather_copy(asg_ref, h_ref, xg_ref, gsems, i + 1, 1 - slot, r).start()
                _scatter_copy(asg_ref, yo_ref, out_ref, ssems, i - 1, 1 - slot, r).start()
            _swiglu_step(xb_ref, wg_ref, wu_ref, wd_ref, acc_ref, rows=thirds * MOE_THIRD_ROWS)

    @pl.when(jnp.logical_and(i == n_used, f == 0))
    def _():
        _wait_gather(h_ref, xg_ref, gsems, slot)

        def start_row(r, carry):
            _scatter_copy(asg_ref, yo_ref, out_ref, ssems, i - 1, 1 - slot, r).start()
            return carry
        lax.fori_loop(0, MOE_TM, start_row, 0, unroll=8)
        _wait_scatter(yo_ref, out_ref, ssems, 1 - slot)

    @pl.when(jnp.logical_and(active, f == MOE_NF - 1))
    def _():
        _wait_scatter(yo_ref, out_ref, ssems, 1 - slot)

        def finish_rows(c, carry):
            rows = pl.ds(pl.multiple_of(c * MOE_ROW_CHUNK, MOE_ROW_CHUNK), MOE_ROW_CHUNK)
            yo_ref[slot, rows, :] = _pack_halves(acc_ref[rows, :])
            return carry
        lax.fori_loop(0, n_chunks, finish_rows, 0)


def _moe_experts(h_packed, asg, blk_e, n_used, n_thirds, wg, wu, wd):
    d, tf = D_MODEL, MOE_TF

    def f_idx(i, f, nu):
        return jnp.where(i < nu[0], f, MOE_NF - 1)

    def w_gu(i, f, be, nu, nt, asg):
        return (be[i], 0, f_idx(i, f, nu))

    def w_d(i, f, be, nu, nt, asg):
        return (be[i], f_idx(i, f, nu), 0)

    grid_spec = pltpu.PrefetchScalarGridSpec(
        num_scalar_prefetch=4,
        grid=(MOE_N_TILES, MOE_NF),
        in_specs=[pl.BlockSpec(memory_space=pl.ANY),
                  pl.BlockSpec((1, d, tf), w_gu),
                  pl.BlockSpec((1, d, tf), w_gu),
                  pl.BlockSpec((1, tf, d), w_d)],
        out_specs=pl.BlockSpec(memory_space=pl.ANY),
        scratch_shapes=[pltpu.VMEM((2, MOE_TM, HALF_D), jnp.uint32),
                        pltpu.VMEM((MOE_TM, d), BF16),
                        pltpu.VMEM((MOE_TM, d), F32),
                        pltpu.VMEM((2, MOE_TM, HALF_D), jnp.uint32),
                        pltpu.SemaphoreType.DMA((2,)),
                        pltpu.SemaphoreType.DMA((2,))],
    )
    return pl.pallas_call(
        _moe_expert_kernel,
        grid_spec=grid_spec,
        out_shape=jax.ShapeDtypeStruct((MOE_OUT_ROWS, HALF_D), jnp.uint32),
        compiler_params=_params(("arbitrary", "arbitrary"), 60),
        name="moe_experts",
    )(blk_e, n_used, n_thirds, asg, h_packed, wg, wu, wd)


def _moe_finish_kernel(y_ref, x_ref, gate_ref, o_ref):
    g0 = gate_ref[:, 0:1]
    g1 = gate_ref[:, 1:2]
    lo0, hi0 = _unpack_halves(y_ref[:, :HALF_D])
    lo1, hi1 = _unpack_halves(y_ref[:, HALF_D:])
    o_ref[:, :HALF_D] = x_ref[:, :HALF_D] + (g0 * lo0 + g1 * lo1)
    o_ref[:, HALF_D:] = x_ref[:, HALF_D:] + (g0 * hi0 + g1 * hi1)


def _moe_finish(y_packed, x, gate):
    assert TOP_K == 2 and MOE_OUT_ROWS % TOP_K == 0
    tokens = 512
    n = x.shape[0]
    y_pairs = y_packed.reshape(MOE_OUT_ROWS // TOP_K, TOP_K * HALF_D)
    return pl.pallas_call(
        _moe_finish_kernel,
        grid=(n // tokens,),
        in_specs=[pl.BlockSpec((tokens, TOP_K * HALF_D), lambda i: (i, 0)),
                  pl.BlockSpec((tokens, D_MODEL), lambda i: (i, 0)),
                  pl.BlockSpec((tokens, ROUTER_LANES), lambda i: (i, 0))],
        out_specs=pl.BlockSpec((tokens, D_MODEL), lambda i: (i, 0)),
        out_shape=jax.ShapeDtypeStruct((n, D_MODEL), F32),
        compiler_params=_params(("parallel",), 40),
        name="moe_finish",
    )(y_pairs, x, gate)


def _slot_assignments_kernel(dest_ref, fill_ref, asg_ref):
    dump0 = SEQ * TOP_K

    def fill(b, carry):
        base = dump0 + lax.rem(b * 8, jnp.int32(MOE_TM))
        for j in range(8):
            asg_ref[b * 8 + j] = base + j
        return carry
    lax.fori_loop(0, MOE_TM // 8, fill, 0)
    for e in range(N_EXPERTS + 1):
        lax.fori_loop(fill_ref[2 * e] + MOE_TM // 8, fill_ref[2 * e + 1] + MOE_TM // 8, fill, 0)

    def place(a, carry):
        asg_ref[MOE_TM + dest_ref[a]] = a
        return carry
    lax.fori_loop(0, SEQ * TOP_K, place, 0, unroll=8)


def _slot_assignments(dest, fill_ranges):
    grid_spec = pltpu.PrefetchScalarGridSpec(
        num_scalar_prefetch=2,
        grid=(1,),
        in_specs=[],
        out_specs=pl.BlockSpec(memory_space=pltpu.SMEM),
    )
    return pl.pallas_call(
        _slot_assignments_kernel,
        grid_spec=grid_spec,
        out_shape=jax.ShapeDtypeStruct((MOE_TM + MOE_CAP,), jnp.int32),
        name="moe_slot_assignments",
    )(dest, fill_ranges)


def _moe(x_new, h_packed, idx, gate, wg, wu, wd):
    e_flat = idx[:, :TOP_K].reshape(-1)
    onehot = (e_flat[:, None] == jnp.arange(N_EXPERTS, dtype=jnp.int32)[None, :]).astype(jnp.int32)
    csum = jnp.cumsum(onehot, axis=0)
    counts = csum[-1]
    rank = jnp.sum((csum - onehot) * onehot, axis=1)
    padded = (counts + MOE_TM - 1) // MOE_TM * MOE_TM
    pad_ends = jnp.cumsum(padded)
    pad_starts = pad_ends - padded
    dest = jnp.sum(onehot * pad_starts[None, :], axis=1) + rank
    fill_lo = jnp.concatenate([pad_starts + counts, pad_ends[-1:]]) // 8
    fill_hi = (jnp.concatenate([pad_ends, jnp.full((1,), MOE_CAP, pad_ends.dtype)]) + 7) // 8
    asg = _slot_assignments(dest.astype(jnp.int32),
                            jnp.stack([fill_lo, fill_hi], axis=1).reshape(-1).astype(jnp.int32))
    n_used = (pad_ends[-1] // MOE_TM).astype(jnp.int32).reshape(1)
    tile_start = jnp.arange(MOE_N_TILES, dtype=jnp.int32) * MOE_TM
    tile_start = jnp.minimum(tile_start, pad_ends[-1] - 1)
    blk_e = jnp.sum((tile_start[:, None] >= pad_ends[None, :]).astype(jnp.int32), axis=1)
    blk_e = jnp.clip(blk_e, 0, N_EXPERTS - 1)
    real_rows = counts[blk_e] - (tile_start - pad_starts[blk_e])
    n_thirds = jnp.clip((real_rows + MOE_THIRD_ROWS - 1) // MOE_THIRD_ROWS, 1, MOE_THIRDS)
    yb = _moe_experts(h_packed, asg, blk_e, n_used, n_thirds.astype(jnp.int32), wg, wu, wd)
    return _moe_finish(yb, x_new, gate)


def kernel(x, mem, ln_mix_g, w_in, na_q_g, na_k_g, na_rpb, mem_ln_g, w_mem_kv, mem_q_g, mem_k_g,
           w_fourier_out, w_na_out, w_mem_out, w_o, ln_ffn_g, ffn_w_gate, ffn_w_up, ffn_w_down,
           moe_router, moe_w_gate, moe_w_up, moe_w_down):
    assert x.shape == (1, SEQ, D_MODEL) and mem.shape == (1, N_MEM, D_MODEL)
    xs = x.reshape(SEQ, D_MODEL)
    mem2 = mem.reshape(N_MEM, D_MODEL)
    tables = _dft_tables()
    w_in_bf16 = w_in.astype(BF16)
    for l in range(DEPTH):
        proj = _norm_mm(xs, ln_mix_g[l], w_in_bf16, l, tm=1024, tn=2048, out_dtype=BF16,
                        vmem_mib=58, name="in_proj")
        fm = _fourier_mix(proj, tables)
        o_na = _na_attention(proj, na_q_g[l], na_k_g[l], _na_col_blocks(na_rpb[l]))
        kv = _norm_mm(mem2, mem_ln_g[l], w_mem_kv, l, tm=N_MEM, tn=2 * MEM_W, out_dtype=F32,
                      vmem_mib=40, name="mem_kv_proj")
        o_mem = _mem_attention(proj, kv, mem_q_g[l], mem_k_g[l])
        merged = _merge(fm, o_na, o_mem, proj, w_fourier_out[l].astype(BF16),
                        w_na_out[l].astype(BF16), w_mem_out[l].astype(BF16))
        i = l // 2
        if l % 2 == 0:
            x_new, h2 = _out_proj(merged, w_o[l].astype(BF16), xs, ln_ffn_g[l])
            xs = _dense_ffn(h2, ffn_w_gate.astype(BF16), ffn_w_up.astype(BF16),
                            ffn_w_down.astype(BF16), i, x_new)
        else:
            x_new, h2, idx, gate = _out_proj(merged, w_o[l].astype(BF16), xs, ln_ffn_g[l],
                                             moe_router[i])
            xs = _moe(x_new, h2, idx, gate, moe_w_gate[i], moe_w_up[i], moe_w_down[i])
    return xs.reshape(1, SEQ, D_MODEL)
```

```python
import functools
import math

import numpy as np
import jax
import jax.numpy as jnp
from jax import lax
from jax.experimental import pallas as pl
from jax.experimental.pallas import tpu as pltpu

F32 = jnp.float32
BF16 = jnp.bfloat16

D_MODEL = 2048
SEQ = 16384
DEPTH = 2
GRID_W = 64
ROWS = SEQ // GRID_W
HEAD_DIM = 128
N_FOURIER_GROUPS = 4
FOURIER_W = N_FOURIER_GROUPS * HEAD_DIM
N_NA_HEADS = 8
NA_W = N_NA_HEADS * HEAD_DIM
WIN_H = 8
WIN_W = 16
N_MEM = 256
N_MEM_HEADS = 4
MEM_W = N_MEM_HEADS * HEAD_DIM
IN_W = FOURIER_W + 3 * NA_W + MEM_W + 3 * D_MODEL
D_FF = 5632
N_EXPERTS = 8
TOP_K = 2
EPS = 1e-6
HALF_D = D_MODEL // 2

Q_NA_OFF = FOURIER_W
K_NA_OFF = Q_NA_OFF + NA_W
V_NA_OFF = K_NA_OFF + NA_W
Q_MEM_OFF = V_NA_OFF + NA_W
GATE_OFF = Q_MEM_OFF + MEM_W

MIB = 1024 * 1024
NEG_BIG = -1e30

FFT_A = 128
FFT_B = SEQ // FFT_A


def _params(semantics, vmem_mib):
    return pltpu.CompilerParams(dimension_semantics=semantics,
                                vmem_limit_bytes=int(vmem_mib * MIB))


def _rms_rows(x, gain):
    ms = jnp.mean(x * x, axis=-1, keepdims=True)
    return x * lax.rsqrt(ms + EPS) * gain


def _sigmoid(x):
    return 0.5 * jnp.tanh(0.5 * x) + 0.5


def _bf16_bits(v):
    return pltpu.bitcast(v.astype(BF16).astype(F32), jnp.uint32)


def _pack_halves(v):
    half = v.shape[1] // 2
    lo = lax.shift_right_logical(_bf16_bits(v[:, :half]), jnp.uint32(16))
    return lo | _bf16_bits(v[:, half:])


def _unpack_halves(w):
    lo = pltpu.bitcast(lax.shift_left(w, jnp.uint32(16)), F32)
    hi = pltpu.bitcast(w & jnp.uint32(0xFFFF0000), F32)
    return lo, hi


def _norm_mm_kernel(x_ref, g_ref, w_ref, o_ref, h_ref, *, row_chunk):
    @pl.when(pl.program_id(1) == 0)
    def _():
        def body(c, carry):
            rows = pl.ds(pl.multiple_of(c * row_chunk, row_chunk), row_chunk)
            h_ref[rows, :] = _rms_rows(x_ref[rows, :].astype(F32), g_ref[...]).astype(BF16)
            return carry
        lax.fori_loop(0, h_ref.shape[0] // row_chunk, body, 0, unroll=2)

    o_ref[...] = jnp.dot(h_ref[...], w_ref[0].astype(BF16),
                         preferred_element_type=F32).astype(o_ref.dtype)


def _norm_mm(x, g, w, layer, *, tm, tn, out_dtype, vmem_mib, name):
    m, k = x.shape
    n = w.shape[2]
    row_chunk = min(tm, 128)
    return pl.pallas_call(
        functools.partial(_norm_mm_kernel, row_chunk=row_chunk),
        grid=(m // tm, n // tn),
        in_specs=[pl.BlockSpec((tm, k), lambda i, j: (i, 0)),
                  pl.BlockSpec((1, k), lambda i, j: (0, 0)),
                  pl.BlockSpec((1, k, tn), lambda i, j: (layer, 0, j))],
        out_specs=pl.BlockSpec((tm, tn), lambda i, j: (i, j)),
        out_shape=jax.ShapeDtypeStruct((m, n), out_dtype),
        scratch_shapes=[pltpu.VMEM((tm, k), BF16)],
        compiler_params=_params(("parallel", "arbitrary"), vmem_mib),
        name=name,
    )(x, g.reshape(1, k), w)


def _dft_tables():
    a = np.arange(FFT_A)
    ang1 = 2.0 * np.pi * np.outer(a, a) / FFT_A
    w1 = np.concatenate([np.cos(ang1), -np.sin(ang1)], axis=0)
    k1 = np.arange(FFT_A)[:, None, None]
    k2 = np.arange(FFT_B)[None, :, None]
    s2 = np.arange(FFT_B)[None, None, :]
    ang2 = 2.0 * np.pi * ((s2 * (k1 + FFT_A * k2)) % SEQ) / SEQ
    g2 = np.concatenate([np.cos(ang2), np.sin(ang2)], axis=2)
    c = np.arange(HEAD_DIM)
    angc = 2.0 * np.pi * np.outer(c, c) / HEAD_DIM
    eye = np.eye(N_FOURIER_GROUPS)
    bd = np.concatenate([np.kron(eye, np.cos(angc)), np.kron(eye, np.sin(angc))], axis=0)
    return (jnp.asarray(w1, dtype=BF16), jnp.asarray(g2, dtype=BF16),
            jnp.asarray(bd, dtype=BF16))


FFT_BLK = 16


def _dft1_kernel(w_ref, x_ref, br_ref, bi_ref):
    xt = pltpu.einshape("abc->bac", x_ref[...])
    for j in range(FFT_BLK):
        r = jnp.dot(w_ref[...], xt[j], preferred_element_type=F32)
        br_ref[j] = r[:FFT_A].astype(BF16)
        bi_ref[j] = r[FFT_A:].astype(BF16)


def _dft2_kernel(br_ref, bi_ref, g_ref, bd_ref, o_ref, *, norm):
    brt = pltpu.einshape("abc->bac", br_ref[...])
    bit = pltpu.einshape("abc->bac", bi_ref[...])
    ys = []
    for i in range(FFT_BLK):
        br = brt[i]
        bi = bit[i]
        g = g_ref[i]
        zr = jnp.dot(g, jnp.concatenate([br, bi], axis=0), preferred_element_type=F32)
        zi = jnp.dot(g, jnp.concatenate([bi, -br], axis=0), preferred_element_type=F32)
        z = jnp.concatenate([zr, zi], axis=1).astype(BF16)
        y = jnp.dot(z, bd_ref[...], preferred_element_type=F32) * norm
        ys.append(y.astype(o_ref.dtype))
    o_ref[...] = pltpu.einshape("abc->bac", jnp.stack(ys, axis=0))


def _fourier_mix(proj, tables):
    w1, g2, bd = tables
    x3 = proj.reshape(FFT_A, FFT_B, IN_W)
    blk = (FFT_A, FFT_BLK, FOURIER_W)
    br, bi = pl.pallas_call(
        _dft1_kernel,
        grid=(FFT_B // FFT_BLK,),
        in_specs=[pl.BlockSpec((2 * FFT_A, FFT_A), lambda j: (0, 0)),
                  pl.BlockSpec(blk, lambda j: (0, j, 0))],
        out_specs=[pl.BlockSpec((FFT_BLK, FFT_A, FOURIER_W), lambda j: (j, 0, 0)),
                   pl.BlockSpec((FFT_BLK, FFT_A, FOURIER_W), lambda j: (j, 0, 0))],
        out_shape=[jax.ShapeDtypeStruct((FFT_B, FFT_A, FOURIER_W), BF16)] * 2,
        compiler_params=_params(("parallel",), 40),
        name="fourier_stage1",
    )(w1, x3)
    norm = 1.0 / math.sqrt(SEQ * HEAD_DIM)
    y3 = pl.pallas_call(
        functools.partial(_dft2_kernel, norm=norm),
        grid=(FFT_A // FFT_BLK,),
        in_specs=[pl.BlockSpec((FFT_B, FFT_BLK, FOURIER_W), lambda j: (0, j, 0)),
                  pl.BlockSpec((FFT_B, FFT_BLK, FOURIER_W), lambda j: (0, j, 0)),
                  pl.BlockSpec((FFT_BLK, FFT_B, 2 * FFT_B), lambda j: (j, 0, 0)),
                  pl.BlockSpec((2 * FOURIER_W, FOURIER_W), lambda j: (0, 0))],
        out_specs=pl.BlockSpec((FFT_B, FFT_BLK, FOURIER_W), lambda j: (0, j, 0)),
        out_shape=jax.ShapeDtypeStruct((FFT_B, FFT_A, FOURIER_W), BF16),
        compiler_params=_params(("parallel",), 40),
        name="fourier_stage2",
    )(br, bi, g2, bd)
    return y3.reshape(SEQ, FOURIER_W)


NA_SUB_ROWS = 2
NA_KEY_ROWS = NA_SUB_ROWS + WIN_H
NA_SUBS_PER_STEP = 32
NA_N_SUB = ROWS // NA_SUB_ROWS
NA_MAX_KEY_START = ROWS - NA_KEY_ROWS
NA_N_ROW_OFFSETS = 2 * WIN_H - 1
NA_MASKED_BLOCK = NA_N_ROW_OFFSETS
LOG2E = math.log2(math.e)


def _na_key_row0(sub):
    return jnp.clip(sub * NA_SUB_ROWS - WIN_H // 2, 0, NA_MAX_KEY_START)


def _na_cases():
    patterns, case_of_sub = [], []
    for sub in range(NA_N_SUB):
        r0 = sub * NA_SUB_ROWS
        ks = int(np.clip(r0 - WIN_H // 2, 0, NA_MAX_KEY_START))
        pat = np.full((NA_SUB_ROWS, NA_KEY_ROWS), NA_MASKED_BLOCK, np.int32)
        for qi in range(NA_SUB_ROWS):
            r = r0 + qi
            r_start = int(np.clip(r - WIN_H // 2, 0, ROWS - WIN_H))
            for kj in range(NA_KEY_ROWS):
                if r_start <= ks + kj < r_start + WIN_H:
                    pat[qi, kj] = ks + kj - r + (WIN_H - 1)
        for c, p in enumerate(patterns):
            if np.array_equal(p, pat):
                case_of_sub.append(c)
                break
        else:
            case_of_sub.append(len(patterns))
            patterns.append(pat)
    return np.stack(patterns), np.asarray(case_of_sub, np.int32)


def _na_case_of(sub, case_of_sub):
    common = int(np.bincount(case_of_sub).argmax())
    case = jnp.int32(common)
    for s in np.nonzero(case_of_sub != common)[0]:
        case = jnp.where(sub == int(s), int(case_of_sub[s]), case)
    return case


def _na_col_blocks(rpb):
    qc = np.arange(GRID_W)[:, None]
    kc = np.arange(GRID_W)[None, :]
    col_start = np.clip(qc - WIN_W // 2, 0, GRID_W - WIN_W)
    col_valid = (kc >= col_start) & (kc < col_start + WIN_W)
    dc = kc - qc + (WIN_W - 1)
    sel = (np.arange(2 * WIN_W - 1)[:, None, None] == dc[None]) & col_valid[None]
    sel = jnp.asarray(sel.astype(np.float32))
    t = jnp.einsum("hrd,dqk->hrqk", rpb.astype(F32), sel, precision=lax.Precision.HIGHEST)
    t = jnp.where(col_valid[None, None], t * LOG2E, NEG_BIG)
    masked = jnp.full((N_NA_HEADS, 1, GRID_W, GRID_W), NEG_BIG, F32)
    t = jnp.concatenate([t, masked], axis=1)
    t = jnp.swapaxes(t, -1, -2)
    return jnp.concatenate([t, t], axis=-1)


def _na_kernel(q_ref, k_ref, v_ref, qg_ref, kg_ref, cb_ref, o_ref, kn_ref, bias_ref):
    step = pl.program_id(1)
    nq = NA_SUB_ROWS * GRID_W
    nk = NA_KEY_ROWS * GRID_W
    row_off, case_of_sub = _na_cases()

    @pl.when(step == 0)
    def _():
        chunk = 1024

        def body(c, carry):
            rows = pl.ds(pl.multiple_of(c * chunk, chunk), chunk)
            kn_ref[rows, :] = _rms_rows(k_ref[rows, :].astype(F32), kg_ref[...]).astype(BF16)
            return carry
        lax.fori_loop(0, SEQ // chunk, body, 0, unroll=2)

        for case in range(row_off.shape[0]):
            for qi in range(NA_SUB_ROWS):
                for kj in range(NA_KEY_ROWS):
                    lanes = slice(qi * GRID_W, (qi + 1) * GRID_W)
                    src = slice((qi % 2) * GRID_W, (qi % 2 + 1) * GRID_W)
                    bias_ref[case, kj * GRID_W:(kj + 1) * GRID_W, lanes] = (
                        cb_ref[0, int(row_off[case, qi, kj]), :, src])

    scale = HEAD_DIM ** -0.5 * LOG2E
    for sb in range(NA_SUBS_PER_STEP):
        sub = step * NA_SUBS_PER_STEP + sb
        case = _na_case_of(sub, case_of_sub)
        q = q_ref[sb * nq:(sb + 1) * nq, :].astype(F32)
        qn = (_rms_rows(q, qg_ref[...]) * scale).astype(BF16)
        kstart = pl.multiple_of(_na_key_row0(sub) * GRID_W, 2 * GRID_W)
        kw = kn_ref[pl.ds(kstart, nk), :]
        vw = v_ref[pl.ds(kstart, nk), :]
        st = lax.dot_general(kw, qn, (((1,), (1,)), ((), ())), preferred_element_type=F32)
        st = st + bias_ref[case]
        m = jnp.max(st, axis=0, keepdims=True)
        pt = jnp.exp2(st - m)
        l = jnp.sum(pt, axis=0, keepdims=True)
        pt = (pt * (1.0 / l)).astype(BF16)
        o = lax.dot_general(pt, vw, (((0,), (0,)), ((), ())), preferred_element_type=F32)
        o_ref[sb * nq:(sb + 1) * nq, :] = o.astype(o_ref.dtype)


def _na_attention(proj, q_g, k_g, col_blocks):
    nq = NA_SUB_ROWS * GRID_W
    nk = NA_KEY_ROWS * GRID_W
    tq = NA_SUBS_PER_STEP * nq
    n_blk = NA_N_ROW_OFFSETS + 1
    qb, kb_, vb = Q_NA_OFF // HEAD_DIM, K_NA_OFF // HEAD_DIM, V_NA_OFF // HEAD_DIM
    return pl.pallas_call(
        _na_kernel,
        grid=(N_NA_HEADS, SEQ // tq),
        in_specs=[pl.BlockSpec((tq, HEAD_DIM), lambda h, i: (i, qb + h)),
                  pl.BlockSpec((SEQ, HEAD_DIM), lambda h, i: (0, kb_ + h)),
                  pl.BlockSpec((SEQ, HEAD_DIM), lambda h, i: (0, vb + h)),
                  pl.BlockSpec((1, HEAD_DIM), lambda h, i: (0, 0)),
                  pl.BlockSpec((1, HEAD_DIM), lambda h, i: (0, 0)),
                  pl.BlockSpec((1, n_blk, GRID_W, 2 * GRID_W), lambda h, i: (h, 0, 0, 0))],
        out_specs=pl.BlockSpec((tq, HEAD_DIM), lambda h, i: (i, h)),
        out_shape=jax.ShapeDtypeStruct((SEQ, NA_W), BF16),
        scratch_shapes=[pltpu.VMEM((SEQ, HEAD_DIM), BF16),
                        pltpu.VMEM((_na_cases()[0].shape[0], nk, nq), F32)],
        compiler_params=_params(("parallel", "arbitrary"), 48),
        name="na_attention",
    )(proj, proj, proj, q_g.reshape(1, HEAD_DIM), k_g.reshape(1, HEAD_DIM), col_blocks)


def _mem_attn_kernel(q_ref, kv_ref, qg_ref, kg_ref, o_ref):
    scale = HEAD_DIM ** -0.5 * LOG2E
    for h in range(N_MEM_HEADS):
        cols = slice(h * HEAD_DIM, (h + 1) * HEAD_DIM)
        qn = (_rms_rows(q_ref[:, cols].astype(F32), qg_ref[...]) * scale).astype(BF16)
        kn = _rms_rows(kv_ref[:, cols].astype(F32), kg_ref[...]).astype(BF16)
        v = kv_ref[:, MEM_W + h * HEAD_DIM:MEM_W + (h + 1) * HEAD_DIM].astype(BF16)
        s = lax.dot_general(qn, kn, (((1,), (1,)), ((), ())), preferred_element_type=F32)
        m = jnp.max(s, axis=-1, keepdims=True)
        p = jnp.exp2(s - m)
        l = jnp.sum(p, axis=-1, keepdims=True)
        o = jnp.dot(p.astype(BF16), v, preferred_element_type=F32) / l
        o_ref[:, cols] = o.astype(o_ref.dtype)


def _mem_attention(proj, kv, q_g, k_g):
    tm = 512
    return pl.pallas_call(
        _mem_attn_kernel,
        grid=(SEQ // tm,),
        in_specs=[pl.BlockSpec((tm, MEM_W), lambda i: (i, Q_MEM_OFF // MEM_W)),
                  pl.BlockSpec((N_MEM, 2 * MEM_W), lambda i: (0, 0)),
                  pl.BlockSpec((1, HEAD_DIM), lambda i: (0, 0)),
                  pl.BlockSpec((1, HEAD_DIM), lambda i: (0, 0))],
        out_specs=pl.BlockSpec((tm, MEM_W), lambda i: (i, 0)),
        out_shape=jax.ShapeDtypeStruct((SEQ, MEM_W), BF16),
        compiler_params=_params(("parallel",), 32),
        name="mem_attention",
    )(proj, kv, q_g.reshape(1, HEAD_DIM), k_g.reshape(1, HEAD_DIM))


def _merge_kernel(fm_ref, na_ref, mo_ref, g0_ref, g1_ref, g2_ref, wf_ref, wn_ref, wm_ref, o_ref,
                  *, col_chunk):
    for c in range(D_MODEL // col_chunk):
        cols = slice(c * col_chunk, (c + 1) * col_chunk)
        o_f = jnp.dot(fm_ref[...], wf_ref[:, cols], preferred_element_type=F32)
        o_n = jnp.dot(na_ref[...], wn_ref[:, cols], preferred_element_type=F32)
        o_m = jnp.dot(mo_ref[...], wm_ref[:, cols], preferred_element_type=F32)
        acc = jax.nn.sigmoid(g0_ref[:, cols].astype(F32)) * o_f
        acc = acc + jax.nn.sigmoid(g1_ref[:, cols].astype(F32)) * o_n
        acc = acc + jax.nn.sigmoid(g2_ref[:, cols].astype(F32)) * o_m
        o_ref[:, cols] = acc.astype(o_ref.dtype)


def _merge(fm, o_na, o_mem, proj, wf, wn, wm):
    tm = 512
    gb = GATE_OFF // D_MODEL
    return pl.pallas_call(
        functools.partial(_merge_kernel, col_chunk=512),
        grid=(SEQ // tm,),
        in_specs=[pl.BlockSpec((tm, FOURIER_W), lambda i: (i, 0)),
                  pl.BlockSpec((tm, NA_W), lambda i: (i, 0)),
                  pl.BlockSpec((tm, MEM_W), lambda i: (i, 0)),
                  pl.BlockSpec((tm, D_MODEL), lambda i: (i, gb)),
                  pl.BlockSpec((tm, D_MODEL), lambda i: (i, gb + 1)),
                  pl.BlockSpec((tm, D_MODEL), lambda i: (i, gb + 2)),
                  pl.BlockSpec((FOURIER_W, D_MODEL), lambda i: (0, 0)),
                  pl.BlockSpec((NA_W, D_MODEL), lambda i: (0, 0)),
                  pl.BlockSpec((MEM_W, D_MODEL), lambda i: (0, 0))],
        out_specs=pl.BlockSpec((tm, D_MODEL), lambda i: (i, 0)),
        out_shape=jax.ShapeDtypeStruct((SEQ, D_MODEL), BF16),
        compiler_params=_params(("parallel",), 48),
        name="gated_merge",
    )(fm, o_na, o_mem, proj, proj, proj, wf, wn, wm)


ROUTER_LANES = 128


def _route_top2(h, wr_ref):
    h_hi = h.astype(BF16)
    h_lo = (h - h_hi.astype(F32)).astype(BF16)
    r_hi = jnp.dot(h_hi, wr_ref[...], preferred_element_type=F32)
    r_lo = jnp.dot(h_lo, wr_ref[:, :ROUTER_LANES], preferred_element_type=F32)
    logits = r_hi[:, :ROUTER_LANES] + r_hi[:, ROUTER_LANES:] + r_lo
    lane = lax.broadcasted_iota(jnp.int32, logits.shape, 1).astype(F32)
    logits = jnp.where(lane < N_EXPERTS, logits, -jnp.inf)
    m1 = jnp.max(logits, axis=-1, keepdims=True)
    i1 = jnp.min(jnp.where(logits == m1, lane, float(ROUTER_LANES)), axis=-1, keepdims=True)
    rest = jnp.where(lane == i1, -jnp.inf, logits)
    m2 = jnp.max(rest, axis=-1, keepdims=True)
    i2 = jnp.min(jnp.where(rest == m2, lane, float(ROUTER_LANES)), axis=-1, keepdims=True)
    e21 = jnp.exp(m2 - m1)
    g1 = 1.0 / (1.0 + e21)
    g2 = e21 * g1
    return jnp.where(lane == 0.0, i1, i2).astype(jnp.int32), jnp.where(lane == 0.0, g1, g2)


def _oproj_kernel(a_ref, w_ref, x_ref, g_ref, *rest, row_chunk, route):
    if route:
        wr_ref, xo_ref, h_ref, idx_ref, gate_ref = rest
    else:
        xo_ref, h_ref = rest
    half = xo_ref.shape[0] // 2
    for r in range(2):
        xo_ref[r * half:(r + 1) * half, :] = (
            jnp.dot(a_ref[r * half:(r + 1) * half, :], w_ref[...], preferred_element_type=F32)
            + x_ref[r * half:(r + 1) * half, :])
        for c in range(half // row_chunk):
            rows = slice(r * half + c * row_chunk, r * half + (c + 1) * row_chunk)
            h = _rms_rows(xo_ref[rows, :], g_ref[...])
            if route:
                h_ref[rows, :] = _pack_halves(h)
                idx_ref[rows, :], gate_ref[rows, :] = _route_top2(h, wr_ref)
            else:
                h_ref[rows, :] = h.astype(h_ref.dtype)


def _out_proj(merged, w_o, x, g, w_router=None):
    tm = 512
    route = w_router is not None
    in_specs = [pl.BlockSpec((tm, D_MODEL), lambda i: (i, 0)),
                pl.BlockSpec((D_MODEL, D_MODEL), lambda i: (0, 0)),
                pl.BlockSpec((tm, D_MODEL), lambda i: (i, 0)),
                pl.BlockSpec((1, D_MODEL), lambda i: (0, 0))]
    args = [merged, w_o, x, g.reshape(1, D_MODEL)]
    out_specs = [pl.BlockSpec((tm, D_MODEL), lambda i: (i, 0))]
    out_shape = [jax.ShapeDtypeStruct((SEQ, D_MODEL), F32)]
    if route:
        w_pad = jnp.zeros((D_MODEL, ROUTER_LANES), F32).at[:, :N_EXPERTS].set(w_router.astype(F32))
        w_hi = w_pad.astype(BF16)
        w_lo = (w_pad - w_hi.astype(F32)).astype(BF16)
        in_specs.append(pl.BlockSpec((D_MODEL, 2 * ROUTER_LANES), lambda i: (0, 0)))
        args.append(jnp.concatenate([w_hi, w_lo], axis=1))
        out_specs += [pl.BlockSpec((tm, HALF_D), lambda i: (i, 0)),
                      pl.BlockSpec((tm, ROUTER_LANES), lambda i: (i, 0)),
                      pl.BlockSpec((tm, ROUTER_LANES), lambda i: (i, 0))]
        out_shape += [jax.ShapeDtypeStruct((SEQ, HALF_D), jnp.uint32),
                      jax.ShapeDtypeStruct((SEQ, ROUTER_LANES), jnp.int32),
                      jax.ShapeDtypeStruct((SEQ, ROUTER_LANES), F32)]
    else:
        out_specs.append(pl.BlockSpec((tm, D_MODEL), lambda i: (i, 0)))
        out_shape.append(jax.ShapeDtypeStruct((SEQ, D_MODEL), BF16))
    return pl.pallas_call(
        functools.partial(_oproj_kernel, row_chunk=128, route=route),
        grid=(SEQ // tm,),
        in_specs=in_specs,
        out_specs=out_specs,
        out_shape=out_shape,
        compiler_params=_params(("parallel",), 48),
        name="out_proj",
    )(*args)


def _swiglu_step(x_ref, wg_ref, wu_ref, wd_ref, acc_ref, rows=None):
    x = x_ref[:rows, :]
    g = jnp.dot(x, wg_ref[0].astype(BF16), preferred_element_type=F32)
    u = jnp.dot(x, wu_ref[0].astype(BF16), preferred_element_type=F32)
    a = (g * _sigmoid(g) * u).astype(BF16)
    acc_ref[:rows, :] += jnp.dot(a, wd_ref[0].astype(BF16), preferred_element_type=F32)


def _dense_ffn_kernel(x_ref, wg_ref, wu_ref, wd_ref, resid_ref, o_ref):
    @pl.when(pl.program_id(1) == 0)
    def _():
        o_ref[...] = resid_ref[...]

    _swiglu_step(x_ref, wg_ref, wu_ref, wd_ref, o_ref)


def _dense_ffn(h, wg, wu, wd, layer, resid):
    tm, tf = 1024, 512
    m, d = h.shape
    nf = D_FF // tf
    return pl.pallas_call(
        _dense_ffn_kernel,
        grid=(m // tm, nf),
        in_specs=[pl.BlockSpec((tm, d), lambda i, f: (i, 0)),
                  pl.BlockSpec((1, d, tf), lambda i, f: (layer, 0, f)),
                  pl.BlockSpec((1, d, tf), lambda i, f: (layer, 0, f)),
                  pl.BlockSpec((1, tf, d), lambda i, f: (layer, f, 0)),
                  pl.BlockSpec((tm, d), lambda i, f: (i, 0))],
        out_specs=pl.BlockSpec((tm, d), lambda i, f: (i, 0)),
        out_shape=jax.ShapeDtypeStruct((m, d), F32),
        compiler_params=_params(("parallel", "arbitrary"), 60),
        name="dense_ffn",
    )(h, wg, wu, wd, resid)


MOE_TF = 512
MOE_NF = D_FF // MOE_TF
MOE_ROWS_PER_STEP = 96
MOE_TM = MOE_NF * MOE_ROWS_PER_STEP
MOE_ROW_CHUNK = 96
MOE_N_TILES = -(-SEQ * TOP_K // MOE_TM) + N_EXPERTS
MOE_THIRDS = 3
MOE_THIRD_ROWS = MOE_TM // MOE_THIRDS


def _tile_row_copy(tok_ref, h_ref, xg_ref, sems, tile, slot, r):
    row = tok_ref[tile * MOE_TM + r]
    return pltpu.make_async_copy(h_ref.at[pl.ds(row, 1), :], xg_ref.at[slot, pl.ds(r, 1), :],
                                 sems.at[slot])


def _wait_tile_rows(h_ref, xg_ref, sems, slot):
    pltpu.make_async_copy(h_ref.at[pl.ds(0, MOE_TM), :], xg_ref.at[slot], sems.at[slot]).wait()


def _moe_expert_kernel(be_ref, nu_ref, nt_ref, tok_ref, h_ref, wg_ref, wu_ref, wd_ref, o_ref,
                       xg_ref, xb_ref, acc_ref, sems):
    i = pl.program_id(0)
    f = pl.program_id(1)
    n_used = nu_ref[0]
    active = i < n_used
    slot = i % 2
    n_chunks = MOE_TM // MOE_ROW_CHUNK

    @pl.when(jnp.logical_and(i == 0, f == 0))
    def _():
        def start_row(r, carry):
            _tile_row_copy(tok_ref, h_ref, xg_ref, sems, 0, 0, r).start()
            return carry
        lax.fori_loop(0, MOE_TM, start_row, 0, unroll=8)

    @pl.when(jnp.logical_and(active, f == 0))
    def _():
        _wait_tile_rows(h_ref, xg_ref, sems, slot)
        acc_ref[...] = jnp.zeros(acc_ref.shape, acc_ref.dtype)

        def unpack_rows(c, carry):
            rows = pl.ds(pl.multiple_of(c * MOE_ROW_CHUNK, MOE_ROW_CHUNK), MOE_ROW_CHUNK)
            lo, hi = _unpack_halves(xg_ref[slot, rows, :])
            xb_ref[rows, :HALF_D] = lo.astype(BF16)
            xb_ref[rows, HALF_D:] = hi.astype(BF16)
            return carry
        lax.fori_loop(0, n_chunks, unpack_rows, 0)

    for thirds in range(1, MOE_THIRDS + 1):
        @pl.when(jnp.logical_and(active, nt_ref[i] == thirds))
        def _(thirds=thirds):
            for j in range(MOE_ROWS_PER_STEP):
                _tile_row_copy(tok_ref, h_ref, xg_ref, sems, i + 1, 1 - slot,
                               f * MOE_ROWS_PER_STEP + j).start()
            _swiglu_step(xb_ref, wg_ref, wu_ref, wd_ref, acc_ref, rows=thirds * MOE_THIRD_ROWS)

    @pl.when(jnp.logical_and(i == n_used, f == 0))
    def _():
        _wait_tile_rows(h_ref, xg_ref, sems, slot)

    @pl.when(f == MOE_NF - 1)
    def _():
        @pl.when(active)
        def _():
            def finish_rows(c, carry):
                rows = pl.ds(pl.multiple_of(c * MOE_ROW_CHUNK, MOE_ROW_CHUNK), MOE_ROW_CHUNK)
                o_ref[rows, :] = _pack_halves(acc_ref[rows, :])
                return carry
            lax.fori_loop(0, n_chunks, finish_rows, 0)

        @pl.when(jnp.logical_not(active))
        def _():
            o_ref[...] = jnp.zeros(o_ref.shape, o_ref.dtype)


def _moe_experts(h_packed, tok, blk_e, n_used, n_thirds, wg, wu, wd):
    d, tf = D_MODEL, MOE_TF

    def f_idx(i, f, nu):
        return jnp.where(i < nu[0], f, MOE_NF - 1)

    def w_gu(i, f, be, nu, nt, tok):
        return (be[i], 0, f_idx(i, f, nu))

    def w_d(i, f, be, nu, nt, tok):
        return (be[i], f_idx(i, f, nu), 0)

    grid_spec = pltpu.PrefetchScalarGridSpec(
        num_scalar_prefetch=4,
        grid=(MOE_N_TILES, MOE_NF),
        in_specs=[pl.BlockSpec(memory_space=pl.ANY),
                  pl.BlockSpec((1, d, tf), w_gu),
                  pl.BlockSpec((1, d, tf), w_gu),
                  pl.BlockSpec((1, tf, d), w_d)],
        out_specs=pl.BlockSpec((MOE_TM, HALF_D), lambda i, f, be, nu, nt, tok: (i, 0)),
        scratch_shapes=[pltpu.VMEM((2, MOE_TM, HALF_D), jnp.uint32),
                        pltpu.VMEM((MOE_TM, d), BF16),
                        pltpu.VMEM((MOE_TM, d), F32),
                        pltpu.SemaphoreType.DMA((2,))],
    )
    return pl.pallas_call(
        _moe_expert_kernel,
        grid_spec=grid_spec,
        out_shape=jax.ShapeDtypeStruct((MOE_N_TILES * MOE_TM, HALF_D), jnp.uint32),
        compiler_params=_params(("arbitrary", "arbitrary"), 60),
        name="moe_experts",
    )(blk_e, n_used, n_thirds, tok, h_packed, wg, wu, wd)


def _start_row_gather(idx_ref, idx_base, src_ref, dst_ref, sem, n_rows):
    def body(r, carry):
        row = idx_ref[idx_base + r]
        pltpu.make_async_copy(src_ref.at[pl.ds(row, 1), :], dst_ref.at[pl.ds(r, 1), :], sem).start()
        return carry
    lax.fori_loop(0, n_rows, body, 0, unroll=8)


def _wait_row_gather(src_ref, dst_ref, sem, n_rows):
    pltpu.make_async_copy(src_ref.at[pl.ds(0, n_rows), :], dst_ref, sem).wait()


def _moe_combine_kernel(pos_ref, y_ref, x_ref, gate_ref, o_ref, buf_ref, sems, *, tokens):
    i = pl.program_id(0)
    slot = i % 2
    rows = TOP_K * tokens

    @pl.when(i == 0)
    def _():
        _start_row_gather(pos_ref, 0, y_ref, buf_ref.at[0], sems.at[0], rows)

    @pl.when(i + 1 < pl.num_programs(0))
    def _():
        _start_row_gather(pos_ref, (i + 1) * rows, y_ref, buf_ref.at[1 - slot],
                          sems.at[1 - slot], rows)

    _wait_row_gather(y_ref, buf_ref.at[slot], sems.at[slot], rows)
    g0 = gate_ref[:, 0:1]
    g1 = gate_ref[:, 1:2]
    lo0, hi0 = _unpack_halves(buf_ref[slot, :tokens, :])
    lo1, hi1 = _unpack_halves(buf_ref[slot, tokens:, :])
    o_ref[:, :HALF_D] = x_ref[:, :HALF_D] + (g0 * lo0 + g1 * lo1)
    o_ref[:, HALF_D:] = x_ref[:, HALF_D:] + (g0 * hi0 + g1 * hi1)


def _moe_combine(y_packed, pos, x, gate):
    tokens = 512
    n = x.shape[0]
    steps = n // tokens
    pos_tiled = pos.reshape(steps, tokens, TOP_K).transpose(0, 2, 1).reshape(-1)
    grid_spec = pltpu.PrefetchScalarGridSpec(
        num_scalar_prefetch=1,
        grid=(steps,),
        in_specs=[pl.BlockSpec(memory_space=pl.ANY),
                  pl.BlockSpec((tokens, D_MODEL), lambda i, pos_ref: (i, 0)),
                  pl.BlockSpec((tokens, ROUTER_LANES), lambda i, pos_ref: (i, 0))],
        out_specs=pl.BlockSpec((tokens, D_MODEL), lambda i, pos_ref: (i, 0)),
        scratch_shapes=[pltpu.VMEM((2, TOP_K * tokens, HALF_D), jnp.uint32),
                        pltpu.SemaphoreType.DMA((2,))],
    )
    return pl.pallas_call(
        functools.partial(_moe_combine_kernel, tokens=tokens),
        grid_spec=grid_spec,
        out_shape=jax.ShapeDtypeStruct((n, D_MODEL), F32),
        compiler_params=_params(("arbitrary",), 40),
        name="moe_combine",
    )(pos_tiled, y_packed, x, gate)


def _slot_tokens_kernel(dest_ref, fill_ref, tok_ref):
    for e in range(N_EXPERTS + 1):
        def fill(b, carry):
            for j in range(8):
                tok_ref[b * 8 + j] = 0
            return carry
        lax.fori_loop(fill_ref[2 * e], fill_ref[2 * e + 1], fill, 0)

    def place(a, carry):
        tok_ref[dest_ref[a]] = a >> 1
        return carry
    lax.fori_loop(0, SEQ * TOP_K, place, 0, unroll=8)


def _slot_tokens(dest, fill_ranges):
    assert TOP_K == 2
    cap = MOE_N_TILES * MOE_TM
    grid_spec = pltpu.PrefetchScalarGridSpec(
        num_scalar_prefetch=2,
        grid=(1,),
        in_specs=[],
        out_specs=pl.BlockSpec(memory_space=pltpu.SMEM),
    )
    return pl.pallas_call(
        _slot_tokens_kernel,
        grid_spec=grid_spec,
        out_shape=jax.ShapeDtypeStruct((cap,), jnp.int32),
        name="moe_slot_tokens",
    )(dest, fill_ranges)


def _moe(x_new, h_packed, idx, gate, wg, wu, wd):
    n = SEQ
    n_assign = n * TOP_K
    cap = MOE_N_TILES * MOE_TM
    e_flat = idx[:, :TOP_K].reshape(-1)
    onehot = (e_flat[:, None] == jnp.arange(N_EXPERTS, dtype=jnp.int32)[None, :]).astype(jnp.int32)
    csum = jnp.cumsum(onehot, axis=0)
    counts = csum[-1]
    rank = jnp.sum((csum - onehot) * onehot, axis=1)
    padded = (counts + MOE_TM - 1) // MOE_TM * MOE_TM
    pad_ends = jnp.cumsum(padded)
    pad_starts = pad_ends - padded
    dest = jnp.sum(onehot * pad_starts[None, :], axis=1) + rank
    fill_lo = jnp.concatenate([pad_starts + counts, pad_ends[-1:]]) // 8
    fill_hi = (jnp.concatenate([pad_ends, jnp.full((1,), cap, pad_ends.dtype)]) + 7) // 8
    tok = _slot_tokens(dest.astype(jnp.int32),
                       jnp.stack([fill_lo, fill_hi], axis=1).reshape(-1).astype(jnp.int32))
    n_used = (pad_ends[-1] // MOE_TM).astype(jnp.int32).reshape(1)
    tile_start = jnp.arange(MOE_N_TILES, dtype=jnp.int32) * MOE_TM
    tile_start = jnp.minimum(tile_start, pad_ends[-1] - 1)
    blk_e = jnp.sum((tile_start[:, None] >= pad_ends[None, :]).astype(jnp.int32), axis=1)
    blk_e = jnp.clip(blk_e, 0, N_EXPERTS - 1)
    real_rows = counts[blk_e] - (tile_start - pad_starts[blk_e])
    n_thirds = jnp.clip((real_rows + MOE_THIRD_ROWS - 1) // MOE_THIRD_ROWS, 1, MOE_THIRDS)
    yb = _moe_experts(h_packed, tok, blk_e, n_used, n_thirds.astype(jnp.int32), wg, wu, wd)
    return _moe_combine(yb, dest.reshape(n, TOP_K), x_new, gate)


def kernel(x, mem, ln_mix_g, w_in, na_q_g, na_k_g, na_rpb, mem_ln_g, w_mem_kv, mem_q_g, mem_k_g,
           w_fourier_out, w_na_out, w_mem_out, w_o, ln_ffn_g, ffn_w_gate, ffn_w_up, ffn_w_down,
           moe_router, moe_w_gate, moe_w_up, moe_w_down):
    assert x.shape == (1, SEQ, D_MODEL) and mem.shape == (1, N_MEM, D_MODEL)
    xs = x.reshape(SEQ, D_MODEL)
    mem2 = mem.reshape(N_MEM, D_MODEL)
    tables = _dft_tables()
    w_in_bf16 = w_in.astype(BF16)
    for l in range(DEPTH):
        proj = _norm_mm(xs, ln_mix_g[l], w_in_bf16, l, tm=1024, tn=2048, out_dtype=BF16,
                        vmem_mib=58, name="in_proj")
        fm = _fourier_mix(proj, tables)
        o_na = _na_attention(proj, na_q_g[l], na_k_g[l], _na_col_blocks(na_rpb[l]))
        kv = _norm_mm(mem2, mem_ln_g[l], w_mem_kv, l, tm=N_MEM, tn=2 * MEM_W, out_dtype=F32,
                      vmem_mib=40, name="mem_kv_proj")
        o_mem = _mem_attention(proj, kv, mem_q_g[l], mem_k_g[l])
        merged = _merge(fm, o_na, o_mem, proj, w_fourier_out[l].astype(BF16),
                        w_na_out[l].astype(BF16), w_mem_out[l].astype(BF16))
        i = l // 2
        if l % 2 == 0:
            x_new, h2 = _out_proj(merged, w_o[l].astype(BF16), xs, ln_ffn_g[l])
            xs = _dense_ffn(h2, ffn_w_gate.astype(BF16), ffn_w_up.astype(BF16),
                            ffn_w_down.astype(BF16), i, x_new)
        else:
            x_new, h2, idx, gate = _out_proj(merged, w_o[l].astype(BF16), xs, ln_ffn_g[l],
                                             moe_router[i])
            xs = _moe(x_new, h2, idx, gate, moe_w_gate[i], moe_w_up[i], moe_w_down[i])
    return xs.reshape(1, SEQ, D_MODEL)
```

```python
import functools
import math

import numpy as np
import jax
import jax.numpy as jnp
from jax import lax
from jax.experimental import pallas as pl
from jax.experimental.pallas import tpu as pltpu

F32 = jnp.float32
BF16 = jnp.bfloat16

D_MODEL = 2048
SEQ = 16384
DEPTH = 2
GRID_W = 64
ROWS = SEQ // GRID_W
HEAD_DIM = 128
N_FOURIER_GROUPS = 4
FOURIER_W = N_FOURIER_GROUPS * HEAD_DIM
N_NA_HEADS = 8
NA_W = N_NA_HEADS * HEAD_DIM
WIN_H = 8
WIN_W = 16
N_MEM = 256
N_MEM_HEADS = 4
MEM_W = N_MEM_HEADS * HEAD_DIM
IN_W = FOURIER_W + 3 * NA_W + MEM_W + 3 * D_MODEL
D_FF = 5632
N_EXPERTS = 8
TOP_K = 2
EPS = 1e-6
HALF_D = D_MODEL // 2

Q_NA_OFF = FOURIER_W
K_NA_OFF = Q_NA_OFF + NA_W
V_NA_OFF = K_NA_OFF + NA_W
Q_MEM_OFF = V_NA_OFF + NA_W
GATE_OFF = Q_MEM_OFF + MEM_W

MIB = 1024 * 1024
NEG_BIG = -1e30

FFT_A = 128
FFT_B = SEQ // FFT_A


def _params(semantics, vmem_mib):
    return pltpu.CompilerParams(dimension_semantics=semantics,
                                vmem_limit_bytes=int(vmem_mib * MIB))


def _rms_rows(x, gain):
    ms = jnp.mean(x * x, axis=-1, keepdims=True)
    return x * lax.rsqrt(ms + EPS) * gain


def _sigmoid(x):
    return 0.5 * jnp.tanh(0.5 * x) + 0.5


def _bf16_bits(v):
    return pltpu.bitcast(v.astype(BF16).astype(F32), jnp.uint32)


def _pack_halves(v):
    half = v.shape[1] // 2
    lo = lax.shift_right_logical(_bf16_bits(v[:, :half]), jnp.uint32(16))
    return lo | _bf16_bits(v[:, half:])


def _unpack_halves(w):
    lo = pltpu.bitcast(lax.shift_left(w, jnp.uint32(16)), F32)
    hi = pltpu.bitcast(w & jnp.uint32(0xFFFF0000), F32)
    return lo, hi


def _norm_mm_kernel(x_ref, g_ref, w_ref, o_ref, h_ref, *, row_chunk):
    @pl.when(pl.program_id(1) == 0)
    def _():
        def body(c, carry):
            rows = pl.ds(pl.multiple_of(c * row_chunk, row_chunk), row_chunk)
            h_ref[rows, :] = _rms_rows(x_ref[rows, :].astype(F32), g_ref[...]).astype(BF16)
            return carry
        lax.fori_loop(0, h_ref.shape[0] // row_chunk, body, 0, unroll=2)

    o_ref[...] = jnp.dot(h_ref[...], w_ref[0].astype(BF16),
                         preferred_element_type=F32).astype(o_ref.dtype)


def _norm_mm(x, g, w, layer, *, tm, tn, out_dtype, vmem_mib, name):
    m, k = x.shape
    n = w.shape[2]
    row_chunk = min(tm, 128)
    return pl.pallas_call(
        functools.partial(_norm_mm_kernel, row_chunk=row_chunk),
        grid=(m // tm, n // tn),
        in_specs=[pl.BlockSpec((tm, k), lambda i, j: (i, 0)),
                  pl.BlockSpec((1, k), lambda i, j: (0, 0)),
                  pl.BlockSpec((1, k, tn), lambda i, j: (layer, 0, j))],
        out_specs=pl.BlockSpec((tm, tn), lambda i, j: (i, j)),
        out_shape=jax.ShapeDtypeStruct((m, n), out_dtype),
        scratch_shapes=[pltpu.VMEM((tm, k), BF16)],
        compiler_params=_params(("parallel", "arbitrary"), vmem_mib),
        name=name,
    )(x, g.reshape(1, k), w)


def _dft_tables():
    a = np.arange(FFT_A)
    ang1 = 2.0 * np.pi * np.outer(a, a) / FFT_A
    w1 = np.concatenate([np.cos(ang1), -np.sin(ang1)], axis=0)
    k1 = np.arange(FFT_A)[:, None, None]
    k2 = np.arange(FFT_B)[None, :, None]
    s2 = np.arange(FFT_B)[None, None, :]
    ang2 = 2.0 * np.pi * ((s2 * (k1 + FFT_A * k2)) % SEQ) / SEQ
    g2 = np.concatenate([np.cos(ang2), np.sin(ang2)], axis=2)
    c = np.arange(HEAD_DIM)
    angc = 2.0 * np.pi * np.outer(c, c) / HEAD_DIM
    eye = np.eye(N_FOURIER_GROUPS)
    bd = np.concatenate([np.kron(eye, np.cos(angc)), np.kron(eye, np.sin(angc))], axis=0)
    return (jnp.asarray(w1, dtype=BF16), jnp.asarray(g2, dtype=BF16),
            jnp.asarray(bd, dtype=BF16))


FFT_BLK = 16


def _dft1_kernel(w_ref, x_ref, br_ref, bi_ref):
    xt = pltpu.einshape("abc->bac", x_ref[...])
    for j in range(FFT_BLK):
        r = jnp.dot(w_ref[...], xt[j], preferred_element_type=F32)
        br_ref[j] = r[:FFT_A].astype(BF16)
        bi_ref[j] = r[FFT_A:].astype(BF16)


def _dft2_kernel(br_ref, bi_ref, g_ref, bd_ref, o_ref, *, norm):
    brt = pltpu.einshape("abc->bac", br_ref[...])
    bit = pltpu.einshape("abc->bac", bi_ref[...])
    ys = []
    for i in range(FFT_BLK):
        br = brt[i]
        bi = bit[i]
        g = g_ref[i]
        zr = jnp.dot(g, jnp.concatenate([br, bi], axis=0), preferred_element_type=F32)
        zi = jnp.dot(g, jnp.concatenate([bi, -br], axis=0), preferred_element_type=F32)
        z = jnp.concatenate([zr, zi], axis=1).astype(BF16)
        y = jnp.dot(z, bd_ref[...], preferred_element_type=F32) * norm
        ys.append(y.astype(o_ref.dtype))
    o_ref[...] = pltpu.einshape("abc->bac", jnp.stack(ys, axis=0))


def _fourier_mix(proj, tables):
    w1, g2, bd = tables
    x3 = proj.reshape(FFT_A, FFT_B, IN_W)
    blk = (FFT_A, FFT_BLK, FOURIER_W)
    br, bi = pl.pallas_call(
        _dft1_kernel,
        grid=(FFT_B // FFT_BLK,),
        in_specs=[pl.BlockSpec((2 * FFT_A, FFT_A), lambda j: (0, 0)),
                  pl.BlockSpec(blk, lambda j: (0, j, 0))],
        out_specs=[pl.BlockSpec((FFT_BLK, FFT_A, FOURIER_W), lambda j: (j, 0, 0)),
                   pl.BlockSpec((FFT_BLK, FFT_A, FOURIER_W), lambda j: (j, 0, 0))],
        out_shape=[jax.ShapeDtypeStruct((FFT_B, FFT_A, FOURIER_W), BF16)] * 2,
        compiler_params=_params(("parallel",), 40),
        name="fourier_stage1",
    )(w1, x3)
    norm = 1.0 / math.sqrt(SEQ * HEAD_DIM)
    y3 = pl.pallas_call(
        functools.partial(_dft2_kernel, norm=norm),
        grid=(FFT_A // FFT_BLK,),
        in_specs=[pl.BlockSpec((FFT_B, FFT_BLK, FOURIER_W), lambda j: (0, j, 0)),
                  pl.BlockSpec((FFT_B, FFT_BLK, FOURIER_W), lambda j: (0, j, 0)),
                  pl.BlockSpec((FFT_BLK, FFT_B, 2 * FFT_B), lambda j: (j, 0, 0)),
                  pl.BlockSpec((2 * FOURIER_W, FOURIER_W), lambda j: (0, 0))],
        out_specs=pl.BlockSpec((FFT_B, FFT_BLK, FOURIER_W), lambda j: (0, j, 0)),
        out_shape=jax.ShapeDtypeStruct((FFT_B, FFT_A, FOURIER_W), BF16),
        compiler_params=_params(("parallel",), 40),
        name="fourier_stage2",
    )(br, bi, g2, bd)
    return y3.reshape(SEQ, FOURIER_W)


NA_SUB_ROWS = 2
NA_KEY_ROWS = NA_SUB_ROWS + WIN_H
NA_SUBS_PER_STEP = 64
NA_N_SUB = ROWS // NA_SUB_ROWS
NA_MAX_KEY_START = ROWS - NA_KEY_ROWS
NA_N_ROW_OFFSETS = 2 * WIN_H - 1
NA_MASKED_BLOCK = NA_N_ROW_OFFSETS
LOG2E = math.log2(math.e)


def _na_key_row0(sub):
    return jnp.clip(sub * NA_SUB_ROWS - WIN_H // 2, 0, NA_MAX_KEY_START)


def _na_cases():
    patterns, case_of_sub = [], []
    for sub in range(NA_N_SUB):
        r0 = sub * NA_SUB_ROWS
        ks = int(np.clip(r0 - WIN_H // 2, 0, NA_MAX_KEY_START))
        pat = np.full((NA_SUB_ROWS, NA_KEY_ROWS), NA_MASKED_BLOCK, np.int32)
        for qi in range(NA_SUB_ROWS):
            r = r0 + qi
            r_start = int(np.clip(r - WIN_H // 2, 0, ROWS - WIN_H))
            for kj in range(NA_KEY_ROWS):
                if r_start <= ks + kj < r_start + WIN_H:
                    pat[qi, kj] = ks + kj - r + (WIN_H - 1)
        for c, p in enumerate(patterns):
            if np.array_equal(p, pat):
                case_of_sub.append(c)
                break
        else:
            case_of_sub.append(len(patterns))
            patterns.append(pat)
    return np.stack(patterns), np.asarray(case_of_sub, np.int32)


def _na_case_of(sub, case_of_sub):
    common = int(np.bincount(case_of_sub).argmax())
    case = jnp.int32(common)
    for s in np.nonzero(case_of_sub != common)[0]:
        case = jnp.where(sub == int(s), int(case_of_sub[s]), case)
    return case


def _na_col_blocks(rpb):
    qc = np.arange(GRID_W)[:, None]
    kc = np.arange(GRID_W)[None, :]
    col_start = np.clip(qc - WIN_W // 2, 0, GRID_W - WIN_W)
    col_valid = (kc >= col_start) & (kc < col_start + WIN_W)
    dc = kc - qc + (WIN_W - 1)
    sel = (np.arange(2 * WIN_W - 1)[:, None, None] == dc[None]) & col_valid[None]
    sel = jnp.asarray(sel.astype(np.float32))
    t = jnp.einsum("hrd,dqk->hrqk", rpb.astype(F32), sel, precision=lax.Precision.HIGHEST)
    t = jnp.where(col_valid[None, None], t * LOG2E, NEG_BIG)
    masked = jnp.full((N_NA_HEADS, 1, GRID_W, GRID_W), NEG_BIG, F32)
    t = jnp.concatenate([t, masked], axis=1)
    t = jnp.swapaxes(t, -1, -2)
    return jnp.concatenate([t, t], axis=-1)


def _na_kernel(q_ref, k_ref, v_ref, qg_ref, kg_ref, cb_ref, o_ref, kn_ref, bias_ref):
    step = pl.program_id(1)
    nq = NA_SUB_ROWS * GRID_W
    nk = NA_KEY_ROWS * GRID_W
    row_off, case_of_sub = _na_cases()

    @pl.when(step == 0)
    def _():
        chunk = 1024

        def body(c, carry):
            rows = pl.ds(pl.multiple_of(c * chunk, chunk), chunk)
            kn_ref[rows, :] = _rms_rows(k_ref[rows, :].astype(F32), kg_ref[...]).astype(BF16)
            return carry
        lax.fori_loop(0, SEQ // chunk, body, 0, unroll=2)

        for case in range(row_off.shape[0]):
            for qi in range(NA_SUB_ROWS):
                for kj in range(NA_KEY_ROWS):
                    lanes = slice(qi * GRID_W, (qi + 1) * GRID_W)
                    src = slice((qi % 2) * GRID_W, (qi % 2 + 1) * GRID_W)
                    bias_ref[case, kj * GRID_W:(kj + 1) * GRID_W, lanes] = (
                        cb_ref[0, int(row_off[case, qi, kj]), :, src])

    scale = HEAD_DIM ** -0.5 * LOG2E
    for sb in range(NA_SUBS_PER_STEP):
        sub = step * NA_SUBS_PER_STEP + sb
        case = _na_case_of(sub, case_of_sub)
        q = q_ref[sb * nq:(sb + 1) * nq, :].astype(F32)
        qn = (_rms_rows(q, qg_ref[...]) * scale).astype(BF16)
        kstart = pl.multiple_of(_na_key_row0(sub) * GRID_W, 2 * GRID_W)
        kw = kn_ref[pl.ds(kstart, nk), :]
        vw = v_ref[pl.ds(kstart, nk), :]
        st = lax.dot_general(kw, qn, (((1,), (1,)), ((), ())), preferred_element_type=F32)
        st = st + bias_ref[case]
        m = jnp.max(st, axis=0, keepdims=True)
        pt = jnp.exp2(st - m)
        l = jnp.sum(pt, axis=0, keepdims=True)
        pt = (pt * (1.0 / l)).astype(BF16)
        o = lax.dot_general(pt, vw, (((0,), (0,)), ((), ())), preferred_element_type=F32)
        o_ref[sb * nq:(sb + 1) * nq, :] = o.astype(o_ref.dtype)


def _na_attention(proj, q_g, k_g, col_blocks):
    nq = NA_SUB_ROWS * GRID_W
    nk = NA_KEY_ROWS * GRID_W
    tq = NA_SUBS_PER_STEP * nq
    n_blk = NA_N_ROW_OFFSETS + 1
    qb, kb_, vb = Q_NA_OFF // HEAD_DIM, K_NA_OFF // HEAD_DIM, V_NA_OFF // HEAD_DIM
    return pl.pallas_call(
        _na_kernel,
        grid=(N_NA_HEADS, SEQ // tq),
        in_specs=[pl.BlockSpec((tq, HEAD_DIM), lambda h, i: (i, qb + h)),
                  pl.BlockSpec((SEQ, HEAD_DIM), lambda h, i: (0, kb_ + h)),
                  pl.BlockSpec((SEQ, HEAD_DIM), lambda h, i: (0, vb + h)),
                  pl.BlockSpec((1, HEAD_DIM), lambda h, i: (0, 0)),
                  pl.BlockSpec((1, HEAD_DIM), lambda h, i: (0, 0)),
                  pl.BlockSpec((1, n_blk, GRID_W, 2 * GRID_W), lambda h, i: (h, 0, 0, 0))],
        out_specs=pl.BlockSpec((tq, HEAD_DIM), lambda h, i: (i, h)),
        out_shape=jax.ShapeDtypeStruct((SEQ, NA_W), BF16),
        scratch_shapes=[pltpu.VMEM((SEQ, HEAD_DIM), BF16),
                        pltpu.VMEM((_na_cases()[0].shape[0], nk, nq), F32)],
        compiler_params=_params(("parallel", "arbitrary"), 48),
        name="na_attention",
    )(proj, proj, proj, q_g.reshape(1, HEAD_DIM), k_g.reshape(1, HEAD_DIM), col_blocks)


def _mem_attn_kernel(q_ref, kv_ref, qg_ref, kg_ref, o_ref):
    scale = HEAD_DIM ** -0.5 * LOG2E
    for h in range(N_MEM_HEADS):
        cols = slice(h * HEAD_DIM, (h + 1) * HEAD_DIM)
        qn = (_rms_rows(q_ref[:, cols].astype(F32), qg_ref[...]) * scale).astype(BF16)
        kn = _rms_rows(kv_ref[:, cols].astype(F32), kg_ref[...]).astype(BF16)
        v = kv_ref[:, MEM_W + h * HEAD_DIM:MEM_W + (h + 1) * HEAD_DIM].astype(BF16)
        s = lax.dot_general(qn, kn, (((1,), (1,)), ((), ())), preferred_element_type=F32)
        m = jnp.max(s, axis=-1, keepdims=True)
        p = jnp.exp2(s - m)
        l = jnp.sum(p, axis=-1, keepdims=True)
        o = jnp.dot(p.astype(BF16), v, preferred_element_type=F32) / l
        o_ref[:, cols] = o.astype(o_ref.dtype)


def _mem_attention(proj, kv, q_g, k_g):
    tm = 1024
    return pl.pallas_call(
        _mem_attn_kernel,
        grid=(SEQ // tm,),
        in_specs=[pl.BlockSpec((tm, MEM_W), lambda i: (i, Q_MEM_OFF // MEM_W)),
                  pl.BlockSpec((N_MEM, 2 * MEM_W), lambda i: (0, 0)),
                  pl.BlockSpec((1, HEAD_DIM), lambda i: (0, 0)),
                  pl.BlockSpec((1, HEAD_DIM), lambda i: (0, 0))],
        out_specs=pl.BlockSpec((tm, MEM_W), lambda i: (i, 0)),
        out_shape=jax.ShapeDtypeStruct((SEQ, MEM_W), BF16),
        compiler_params=_params(("parallel",), 32),
        name="mem_attention",
    )(proj, kv, q_g.reshape(1, HEAD_DIM), k_g.reshape(1, HEAD_DIM))


def _merge_kernel(fm_ref, na_ref, mo_ref, g0_ref, g1_ref, g2_ref, wf_ref, wn_ref, wm_ref, o_ref,
                  *, col_chunk):
    for c in range(D_MODEL // col_chunk):
        cols = slice(c * col_chunk, (c + 1) * col_chunk)
        o_f = jnp.dot(fm_ref[...], wf_ref[:, cols], preferred_element_type=F32)
        o_n = jnp.dot(na_ref[...], wn_ref[:, cols], preferred_element_type=F32)
        o_m = jnp.dot(mo_ref[...], wm_ref[:, cols], preferred_element_type=F32)
        acc = jax.nn.sigmoid(g0_ref[:, cols].astype(F32)) * o_f
        acc = acc + jax.nn.sigmoid(g1_ref[:, cols].astype(F32)) * o_n
        acc = acc + jax.nn.sigmoid(g2_ref[:, cols].astype(F32)) * o_m
        o_ref[:, cols] = acc.astype(o_ref.dtype)


def _merge(fm, o_na, o_mem, proj, wf, wn, wm):
    tm = 512
    gb = GATE_OFF // D_MODEL
    return pl.pallas_call(
        functools.partial(_merge_kernel, col_chunk=512),
        grid=(SEQ // tm,),
        in_specs=[pl.BlockSpec((tm, FOURIER_W), lambda i: (i, 0)),
                  pl.BlockSpec((tm, NA_W), lambda i: (i, 0)),
                  pl.BlockSpec((tm, MEM_W), lambda i: (i, 0)),
                  pl.BlockSpec((tm, D_MODEL), lambda i: (i, gb)),
                  pl.BlockSpec((tm, D_MODEL), lambda i: (i, gb + 1)),
                  pl.BlockSpec((tm, D_MODEL), lambda i: (i, gb + 2)),
                  pl.BlockSpec((FOURIER_W, D_MODEL), lambda i: (0, 0)),
                  pl.BlockSpec((NA_W, D_MODEL), lambda i: (0, 0)),
                  pl.BlockSpec((MEM_W, D_MODEL), lambda i: (0, 0))],
        out_specs=pl.BlockSpec((tm, D_MODEL), lambda i: (i, 0)),
        out_shape=jax.ShapeDtypeStruct((SEQ, D_MODEL), BF16),
        compiler_params=_params(("parallel",), 48),
        name="gated_merge",
    )(fm, o_na, o_mem, proj, proj, proj, wf, wn, wm)


ROUTER_LANES = 128


def _route_top2(h, wr_ref):
    h_hi = h.astype(BF16)
    h_lo = (h - h_hi.astype(F32)).astype(BF16)
    r_hi = jnp.dot(h_hi, wr_ref[...], preferred_element_type=F32)
    r_lo = jnp.dot(h_lo, wr_ref[:, :ROUTER_LANES], preferred_element_type=F32)
    logits = r_hi[:, :ROUTER_LANES] + r_hi[:, ROUTER_LANES:] + r_lo
    lane = lax.broadcasted_iota(jnp.int32, logits.shape, 1).astype(F32)
    logits = jnp.where(lane < N_EXPERTS, logits, -jnp.inf)
    m1 = jnp.max(logits, axis=-1, keepdims=True)
    i1 = jnp.min(jnp.where(logits == m1, lane, float(ROUTER_LANES)), axis=-1, keepdims=True)
    rest = jnp.where(lane == i1, -jnp.inf, logits)
    m2 = jnp.max(rest, axis=-1, keepdims=True)
    i2 = jnp.min(jnp.where(rest == m2, lane, float(ROUTER_LANES)), axis=-1, keepdims=True)
    e21 = jnp.exp(m2 - m1)
    g1 = 1.0 / (1.0 + e21)
    g2 = e21 * g1
    return jnp.where(lane == 0.0, i1, i2).astype(jnp.int32), jnp.where(lane == 0.0, g1, g2)


def _oproj_kernel(a_ref, w_ref, x_ref, g_ref, *rest, row_chunk, route):
    if route:
        wr_ref, xo_ref, h_ref, idx_ref, gate_ref = rest
    else:
        xo_ref, h_ref = rest
    half = xo_ref.shape[0] // 2
    for r in range(2):
        xo_ref[r * half:(r + 1) * half, :] = (
            jnp.dot(a_ref[r * half:(r + 1) * half, :], w_ref[...], preferred_element_type=F32)
            + x_ref[r * half:(r + 1) * half, :])
        for c in range(half // row_chunk):
            rows = slice(r * half + c * row_chunk, r * half + (c + 1) * row_chunk)
            h = _rms_rows(xo_ref[rows, :], g_ref[...])
            if route:
                h_ref[rows, :] = _pack_halves(h)
                idx_ref[rows, :], gate_ref[rows, :] = _route_top2(h, wr_ref)
            else:
                h_ref[rows, :] = h.astype(h_ref.dtype)


def _out_proj(merged, w_o, x, g, w_router=None):
    tm = 512
    route = w_router is not None
    in_specs = [pl.BlockSpec((tm, D_MODEL), lambda i: (i, 0)),
                pl.BlockSpec((D_MODEL, D_MODEL), lambda i: (0, 0)),
                pl.BlockSpec((tm, D_MODEL), lambda i: (i, 0)),
                pl.BlockSpec((1, D_MODEL), lambda i: (0, 0))]
    args = [merged, w_o, x, g.reshape(1, D_MODEL)]
    out_specs = [pl.BlockSpec((tm, D_MODEL), lambda i: (i, 0))]
    out_shape = [jax.ShapeDtypeStruct((SEQ, D_MODEL), F32)]
    if route:
        w_pad = jnp.zeros((D_MODEL, ROUTER_LANES), F32).at[:, :N_EXPERTS].set(w_router.astype(F32))
        w_hi = w_pad.astype(BF16)
        w_lo = (w_pad - w_hi.astype(F32)).astype(BF16)
        in_specs.append(pl.BlockSpec((D_MODEL, 2 * ROUTER_LANES), lambda i: (0, 0)))
        args.append(jnp.concatenate([w_hi, w_lo], axis=1))
        out_specs += [pl.BlockSpec((tm, HALF_D), lambda i: (i, 0)),
                      pl.BlockSpec((tm, ROUTER_LANES), lambda i: (i, 0)),
                      pl.BlockSpec((tm, ROUTER_LANES), lambda i: (i, 0))]
        out_shape += [jax.ShapeDtypeStruct((SEQ, HALF_D), jnp.uint32),
                      jax.ShapeDtypeStruct((SEQ, ROUTER_LANES), jnp.int32),
                      jax.ShapeDtypeStruct((SEQ, ROUTER_LANES), F32)]
    else:
        out_specs.append(pl.BlockSpec((tm, D_MODEL), lambda i: (i, 0)))
        out_shape.append(jax.ShapeDtypeStruct((SEQ, D_MODEL), BF16))
    return pl.pallas_call(
        functools.partial(_oproj_kernel, row_chunk=128, route=route),
        grid=(SEQ // tm,),
        in_specs=in_specs,
        out_specs=out_specs,
        out_shape=out_shape,
        compiler_params=_params(("parallel",), 48),
        name="out_proj",
    )(*args)


def _swiglu_step(x_ref, wg_ref, wu_ref, wd_ref, acc_ref, rows=None):
    x = x_ref[:rows, :]
    g = jnp.dot(x, wg_ref[0].astype(BF16), preferred_element_type=F32)
    u = jnp.dot(x, wu_ref[0].astype(BF16), preferred_element_type=F32)
    a = (g * _sigmoid(g) * u).astype(BF16)
    acc_ref[:rows, :] += jnp.dot(a, wd_ref[0].astype(BF16), preferred_element_type=F32)


def _dense_ffn_kernel(x_ref, wg_ref, wu_ref, wd_ref, resid_ref, o_ref):
    @pl.when(pl.program_id(1) == 0)
    def _():
        o_ref[...] = resid_ref[...]

    _swiglu_step(x_ref, wg_ref, wu_ref, wd_ref, o_ref)


def _dense_ffn(h, wg, wu, wd, layer, resid):
    tm, tf = 1024, 512
    m, d = h.shape
    nf = D_FF // tf
    return pl.pallas_call(
        _dense_ffn_kernel,
        grid=(m // tm, nf),
        in_specs=[pl.BlockSpec((tm, d), lambda i, f: (i, 0)),
                  pl.BlockSpec((1, d, tf), lambda i, f: (layer, 0, f)),
                  pl.BlockSpec((1, d, tf), lambda i, f: (layer, 0, f)),
                  pl.BlockSpec((1, tf, d), lambda i, f: (layer, f, 0)),
                  pl.BlockSpec((tm, d), lambda i, f: (i, 0))],
        out_specs=pl.BlockSpec((tm, d), lambda i, f: (i, 0)),
        out_shape=jax.ShapeDtypeStruct((m, d), F32),
        compiler_params=_params(("parallel", "arbitrary"), 60),
        name="dense_ffn",
    )(h, wg, wu, wd, resid)


MOE_TF = 512
MOE_NF = D_FF // MOE_TF
MOE_ROWS_PER_STEP = 96
MOE_TM = MOE_NF * MOE_ROWS_PER_STEP
MOE_ROW_CHUNK = 96
MOE_N_TILES = -(-SEQ * TOP_K // MOE_TM) + N_EXPERTS
MOE_THIRDS = 3
MOE_THIRD_ROWS = MOE_TM // MOE_THIRDS


def _tile_row_copy(tok_ref, h_ref, xg_ref, sems, tile, slot, r):
    row = tok_ref[tile * MOE_TM + r]
    return pltpu.make_async_copy(h_ref.at[pl.ds(row, 1), :], xg_ref.at[slot, pl.ds(r, 1), :],
                                 sems.at[slot])


def _wait_tile_rows(h_ref, xg_ref, sems, slot):
    pltpu.make_async_copy(h_ref.at[pl.ds(0, MOE_TM), :], xg_ref.at[slot], sems.at[slot]).wait()


def _moe_expert_kernel(be_ref, nu_ref, nt_ref, tok_ref, h_ref, wg_ref, wu_ref, wd_ref, o_ref,
                       xg_ref, xb_ref, acc_ref, sems):
    i = pl.program_id(0)
    f = pl.program_id(1)
    n_used = nu_ref[0]
    active = i < n_used
    slot = i % 2
    n_chunks = MOE_TM // MOE_ROW_CHUNK

    @pl.when(jnp.logical_and(i == 0, f == 0))
    def _():
        def start_row(r, carry):
            _tile_row_copy(tok_ref, h_ref, xg_ref, sems, 0, 0, r).start()
            return carry
        lax.fori_loop(0, MOE_TM, start_row, 0, unroll=8)

    @pl.when(jnp.logical_and(active, f == 0))
    def _():
        _wait_tile_rows(h_ref, xg_ref, sems, slot)
        acc_ref[...] = jnp.zeros(acc_ref.shape, acc_ref.dtype)

        def unpack_rows(c, carry):
            rows = pl.ds(pl.multiple_of(c * MOE_ROW_CHUNK, MOE_ROW_CHUNK), MOE_ROW_CHUNK)
            lo, hi = _unpack_halves(xg_ref[slot, rows, :])
            xb_ref[rows, :HALF_D] = lo.astype(BF16)
            xb_ref[rows, HALF_D:] = hi.astype(BF16)
            return carry
        lax.fori_loop(0, n_chunks, unpack_rows, 0)

    for thirds in range(1, MOE_THIRDS + 1):
        @pl.when(jnp.logical_and(active, nt_ref[i] == thirds))
        def _(thirds=thirds):
            for j in range(MOE_ROWS_PER_STEP):
                _tile_row_copy(tok_ref, h_ref, xg_ref, sems, i + 1, 1 - slot,
                               f * MOE_ROWS_PER_STEP + j).start()
            _swiglu_step(xb_ref, wg_ref, wu_ref, wd_ref, acc_ref, rows=thirds * MOE_THIRD_ROWS)

    @pl.when(jnp.logical_and(i == n_used, f == 0))
    def _():
        _wait_tile_rows(h_ref, xg_ref, sems, slot)

    @pl.when(f == MOE_NF - 1)
    def _():
        @pl.when(active)
        def _():
            def finish_rows(c, carry):
                rows = pl.ds(pl.multiple_of(c * MOE_ROW_CHUNK, MOE_ROW_CHUNK), MOE_ROW_CHUNK)
                o_ref[rows, :] = _pack_halves(acc_ref[rows, :])
                return carry
            lax.fori_loop(0, n_chunks, finish_rows, 0)

        @pl.when(jnp.logical_not(active))
        def _():
            o_ref[...] = jnp.zeros(o_ref.shape, o_ref.dtype)


def _moe_experts(h_packed, tok, blk_e, n_used, n_thirds, wg, wu, wd):
    d, tf = D_MODEL, MOE_TF

    def f_idx(i, f, nu):
        return jnp.where(i < nu[0], f, MOE_NF - 1)

    def w_gu(i, f, be, nu, nt, tok):
        return (be[i], 0, f_idx(i, f, nu))

    def w_d(i, f, be, nu, nt, tok):
        return (be[i], f_idx(i, f, nu), 0)

    grid_spec = pltpu.PrefetchScalarGridSpec(
        num_scalar_prefetch=4,
        grid=(MOE_N_TILES, MOE_NF),
        in_specs=[pl.BlockSpec(memory_space=pl.ANY),
                  pl.BlockSpec((1, d, tf), w_gu),
                  pl.BlockSpec((1, d, tf), w_gu),
                  pl.BlockSpec((1, tf, d), w_d)],
        out_specs=pl.BlockSpec((MOE_TM, HALF_D), lambda i, f, be, nu, nt, tok: (i, 0)),
        scratch_shapes=[pltpu.VMEM((2, MOE_TM, HALF_D), jnp.uint32),
                        pltpu.VMEM((MOE_TM, d), BF16),
                        pltpu.VMEM((MOE_TM, d), F32),
                        pltpu.SemaphoreType.DMA((2,))],
    )
    return pl.pallas_call(
        _moe_expert_kernel,
        grid_spec=grid_spec,
        out_shape=jax.ShapeDtypeStruct((MOE_N_TILES * MOE_TM, HALF_D), jnp.uint32),
        compiler_params=_params(("arbitrary", "arbitrary"), 60),
        name="moe_experts",
    )(blk_e, n_used, n_thirds, tok, h_packed, wg, wu, wd)


def _start_row_gather(idx_ref, idx_base, src_ref, dst_ref, sem, n_rows):
    def body(r, carry):
        row = idx_ref[idx_base + r]
        pltpu.make_async_copy(src_ref.at[pl.ds(row, 1), :], dst_ref.at[pl.ds(r, 1), :], sem).start()
        return carry
    lax.fori_loop(0, n_rows, body, 0, unroll=8)


def _wait_row_gather(src_ref, dst_ref, sem, n_rows):
    pltpu.make_async_copy(src_ref.at[pl.ds(0, n_rows), :], dst_ref, sem).wait()


def _moe_combine_kernel(pos_ref, y_ref, x_ref, gate_ref, o_ref, buf_ref, sems, *, tokens):
    i = pl.program_id(0)
    slot = i % 2
    rows = TOP_K * tokens

    @pl.when(i == 0)
    def _():
        _start_row_gather(pos_ref, 0, y_ref, buf_ref.at[0], sems.at[0], rows)

    @pl.when(i + 1 < pl.num_programs(0))
    def _():
        _start_row_gather(pos_ref, (i + 1) * rows, y_ref, buf_ref.at[1 - slot],
                          sems.at[1 - slot], rows)

    _wait_row_gather(y_ref, buf_ref.at[slot], sems.at[slot], rows)
    g0 = gate_ref[:, 0:1]
    g1 = gate_ref[:, 1:2]
    lo0, hi0 = _unpack_halves(buf_ref[slot, :tokens, :])
    lo1, hi1 = _unpack_halves(buf_ref[slot, tokens:, :])
    o_ref[:, :HALF_D] = x_ref[:, :HALF_D] + (g0 * lo0 + g1 * lo1)
    o_ref[:, HALF_D:] = x_ref[:, HALF_D:] + (g0 * hi0 + g1 * hi1)


def _moe_combine(y_packed, pos, x, gate):
    tokens = 512
    n = x.shape[0]
    steps = n // tokens
    pos_tiled = pos.reshape(steps, tokens, TOP_K).transpose(0, 2, 1).reshape(-1)
    grid_spec = pltpu.PrefetchScalarGridSpec(
        num_scalar_prefetch=1,
        grid=(steps,),
        in_specs=[pl.BlockSpec(memory_space=pl.ANY),
                  pl.BlockSpec((tokens, D_MODEL), lambda i, pos_ref: (i, 0)),
                  pl.BlockSpec((tokens, ROUTER_LANES), lambda i, pos_ref: (i, 0))],
        out_specs=pl.BlockSpec((tokens, D_MODEL), lambda i, pos_ref: (i, 0)),
        scratch_shapes=[pltpu.VMEM((2, TOP_K * tokens, HALF_D), jnp.uint32),
                        pltpu.SemaphoreType.DMA((2,))],
    )
    return pl.pallas_call(
        functools.partial(_moe_combine_kernel, tokens=tokens),
        grid_spec=grid_spec,
        out_shape=jax.ShapeDtypeStruct((n, D_MODEL), F32),
        compiler_params=_params(("arbitrary",), 40),
        name="moe_combine",
    )(pos_tiled, y_packed, x, gate)


def _slot_tokens_kernel(dest_ref, fill_ref, tok_ref):
    for e in range(N_EXPERTS + 1):
        def fill(b, carry):
            for j in range(8):
                tok_ref[b * 8 + j] = 0
            return carry
        lax.fori_loop(fill_ref[2 * e], fill_ref[2 * e + 1], fill, 0)

    def place(a, carry):
        tok_ref[dest_ref[a]] = a >> 1
        return carry
    lax.fori_loop(0, SEQ * TOP_K, place, 0, unroll=8)


def _slot_tokens(dest, fill_ranges):
    assert TOP_K == 2
    cap = MOE_N_TILES * MOE_TM
    grid_spec = pltpu.PrefetchScalarGridSpec(
        num_scalar_prefetch=2,
        grid=(1,),
        in_specs=[],
        out_specs=pl.BlockSpec(memory_space=pltpu.SMEM),
    )
    return pl.pallas_call(
        _slot_tokens_kernel,
        grid_spec=grid_spec,
        out_shape=jax.ShapeDtypeStruct((cap,), jnp.int32),
        name="moe_slot_tokens",
    )(dest, fill_ranges)


def _moe(x_new, h_packed, idx, gate, wg, wu, wd):
    n = SEQ
    n_assign = n * TOP_K
    cap = MOE_N_TILES * MOE_TM
    e_flat = idx[:, :TOP_K].reshape(-1)
    onehot = (e_flat[:, None] == jnp.arange(N_EXPERTS, dtype=jnp.int32)[None, :]).astype(jnp.int32)
    csum = jnp.cumsum(onehot, axis=0)
    counts = csum[-1]
    rank = jnp.sum((csum - onehot) * onehot, axis=1)
    padded = (counts + MOE_TM - 1) // MOE_TM * MOE_TM
    pad_ends = jnp.cumsum(padded)
    pad_starts = pad_ends - padded
    dest = jnp.sum(onehot * pad_starts[None, :], axis=1) + rank
    fill_lo = jnp.concatenate([pad_starts + counts, pad_ends[-1:]]) // 8
    fill_hi = (jnp.concatenate([pad_ends, jnp.full((1,), cap, pad_ends.dtype)]) + 7) // 8
    tok = _slot_tokens(dest.astype(jnp.int32),
                       jnp.stack([fill_lo, fill_hi], axis=1).reshape(-1).astype(jnp.int32))
    n_used = (pad_ends[-1] // MOE_TM).astype(jnp.int32).reshape(1)
    tile_start = jnp.arange(MOE_N_TILES, dtype=jnp.int32) * MOE_TM
    tile_start = jnp.minimum(tile_start, pad_ends[-1] - 1)
    blk_e = jnp.sum((tile_start[:, None] >= pad_ends[None, :]).astype(jnp.int32), axis=1)
    blk_e = jnp.clip(blk_e, 0, N_EXPERTS - 1)
    real_rows = counts[blk_e] - (tile_start - pad_starts[blk_e])
    n_thirds = jnp.clip((real_rows + MOE_THIRD_ROWS - 1) // MOE_THIRD_ROWS, 1, MOE_THIRDS)
    yb = _moe_experts(h_packed, tok, blk_e, n_used, n_thirds.astype(jnp.int32), wg, wu, wd)
    return _moe_combine(yb, dest.reshape(n, TOP_K), x_new, gate)


def kernel(x, mem, ln_mix_g, w_in, na_q_g, na_k_g, na_rpb, mem_ln_g, w_mem_kv, mem_q_g, mem_k_g,
           w_fourier_out, w_na_out, w_mem_out, w_o, ln_ffn_g, ffn_w_gate, ffn_w_up, ffn_w_down,
           moe_router, moe_w_gate, moe_w_up, moe_w_down):
    assert x.shape == (1, SEQ, D_MODEL) and mem.shape == (1, N_MEM, D_MODEL)
    xs = x.reshape(SEQ, D_MODEL)
    mem2 = mem.reshape(N_MEM, D_MODEL)
    tables = _dft_tables()
    w_in_bf16 = w_in.astype(BF16)
    for l in range(DEPTH):
        proj = _norm_mm(xs, ln_mix_g[l], w_in_bf16, l, tm=1024, tn=2048, out_dtype=BF16,
                        vmem_mib=58, name="in_proj")
        fm = _fourier_mix(proj, tables)
        o_na = _na_attention(proj, na_q_g[l], na_k_g[l], _na_col_blocks(na_rpb[l]))
        kv = _norm_mm(mem2, mem_ln_g[l], w_mem_kv, l, tm=N_MEM, tn=2 * MEM_W, out_dtype=F32,
                      vmem_mib=40, name="mem_kv_proj")
        o_mem = _mem_attention(proj, kv, mem_q_g[l], mem_k_g[l])
        merged = _merge(fm, o_na, o_mem, proj, w_fourier_out[l].astype(BF16),
                        w_na_out[l].astype(BF16), w_mem_out[l].astype(BF16))
        i = l // 2
        if l % 2 == 0:
            x_new, h2 = _out_proj(merged, w_o[l].astype(BF16), xs, ln_ffn_g[l])
            xs = _dense_ffn(h2, ffn_w_gate.astype(BF16), ffn_w_up.astype(BF16),
                            ffn_w_down.astype(BF16), i, x_new)
        else:
            x_new, h2, idx, gate = _out_proj(merged, w_o[l].astype(BF16), xs, ln_ffn_g[l],
                                             moe_router[i])
            xs = _moe(x_new, h2, idx, gate, moe_w_gate[i], moe_w_up[i], moe_w_down[i])
    return xs.reshape(1, SEQ, D_MODEL)
```

```python
import functools
import math

import numpy as np
import jax
import jax.numpy as jnp
from jax import lax
from jax.experimental import pallas as pl
from jax.experimental.pallas import tpu as pltpu

F32 = jnp.float32
BF16 = jnp.bfloat16

D_MODEL = 2048
SEQ = 16384
DEPTH = 2
GRID_W = 64
ROWS = SEQ // GRID_W
HEAD_DIM = 128
N_FOURIER_GROUPS = 4
FOURIER_W = N_FOURIER_GROUPS * HEAD_DIM
N_NA_HEADS = 8
NA_W = N_NA_HEADS * HEAD_DIM
WIN_H = 8
WIN_W = 16
N_MEM = 256
N_MEM_HEADS = 4
MEM_W = N_MEM_HEADS * HEAD_DIM
IN_W = FOURIER_W + 3 * NA_W + MEM_W + 3 * D_MODEL
D_FF = 5632
N_EXPERTS = 8
TOP_K = 2
EPS = 1e-6
HALF_D = D_MODEL // 2

Q_NA_OFF = FOURIER_W
K_NA_OFF = Q_NA_OFF + NA_W
V_NA_OFF = K_NA_OFF + NA_W
Q_MEM_OFF = V_NA_OFF + NA_W
GATE_OFF = Q_MEM_OFF + MEM_W

MIB = 1024 * 1024
NEG_BIG = -1e30

FFT_A = 128
FFT_B = SEQ // FFT_A


def _params(semantics, vmem_mib):
    return pltpu.CompilerParams(dimension_semantics=semantics,
                                vmem_limit_bytes=int(vmem_mib * MIB))


def _rms_rows(x, gain):
    ms = jnp.mean(x * x, axis=-1, keepdims=True)
    return x * lax.rsqrt(ms + EPS) * gain


def _sigmoid(x):
    return 0.5 * jnp.tanh(0.5 * x) + 0.5


def _bf16_bits(v):
    return pltpu.bitcast(v.astype(BF16).astype(F32), jnp.uint32)


def _pack_halves(v):
    half = v.shape[1] // 2
    lo = lax.shift_right_logical(_bf16_bits(v[:, :half]), jnp.uint32(16))
    return lo | _bf16_bits(v[:, half:])


def _unpack_halves(w):
    lo = pltpu.bitcast(lax.shift_left(w, jnp.uint32(16)), F32)
    hi = pltpu.bitcast(w & jnp.uint32(0xFFFF0000), F32)
    return lo, hi


def _norm_mm_kernel(x_ref, g_ref, w_ref, o_ref, h_ref, *, row_chunk):
    @pl.when(pl.program_id(1) == 0)
    def _():
        def body(c, carry):
            rows = pl.ds(pl.multiple_of(c * row_chunk, row_chunk), row_chunk)
            h_ref[rows, :] = _rms_rows(x_ref[rows, :].astype(F32), g_ref[...]).astype(BF16)
            return carry
        lax.fori_loop(0, h_ref.shape[0] // row_chunk, body, 0, unroll=2)

    o_ref[...] = jnp.dot(h_ref[...], w_ref[0].astype(BF16),
                         preferred_element_type=F32).astype(o_ref.dtype)


def _norm_mm(x, g, w, layer, *, tm, tn, out_dtype, vmem_mib, name):
    m, k = x.shape
    n = w.shape[2]
    row_chunk = min(tm, 128)
    return pl.pallas_call(
        functools.partial(_norm_mm_kernel, row_chunk=row_chunk),
        grid=(m // tm, n // tn),
        in_specs=[pl.BlockSpec((tm, k), lambda i, j: (i, 0)),
                  pl.BlockSpec((1, k), lambda i, j: (0, 0)),
                  pl.BlockSpec((1, k, tn), lambda i, j: (layer, 0, j))],
        out_specs=pl.BlockSpec((tm, tn), lambda i, j: (i, j)),
        out_shape=jax.ShapeDtypeStruct((m, n), out_dtype),
        scratch_shapes=[pltpu.VMEM((tm, k), BF16)],
        compiler_params=_params(("parallel", "arbitrary"), vmem_mib),
        name=name,
    )(x, g.reshape(1, k), w)


def _dft_tables():
    a = np.arange(FFT_A)
    ang1 = 2.0 * np.pi * np.outer(a, a) / FFT_A
    w1 = np.concatenate([np.cos(ang1), -np.sin(ang1)], axis=0)
    k1 = np.arange(FFT_A)[:, None, None]
    k2 = np.arange(FFT_B)[None, :, None]
    s2 = np.arange(FFT_B)[None, None, :]
    ang2 = 2.0 * np.pi * ((s2 * (k1 + FFT_A * k2)) % SEQ) / SEQ
    g2 = np.concatenate([np.cos(ang2), np.sin(ang2)], axis=2)
    c = np.arange(HEAD_DIM)
    angc = 2.0 * np.pi * np.outer(c, c) / HEAD_DIM
    eye = np.eye(N_FOURIER_GROUPS)
    bd = np.concatenate([np.kron(eye, np.cos(angc)), np.kron(eye, np.sin(angc))], axis=0)
    return (jnp.asarray(w1, dtype=BF16), jnp.asarray(g2, dtype=BF16),
            jnp.asarray(bd, dtype=BF16))


FFT_BLK = 16


def _dft1_kernel(w_ref, x_ref, br_ref, bi_ref):
    xt = pltpu.einshape("abc->bac", x_ref[...])
    for j in range(FFT_BLK):
        r = jnp.dot(w_ref[...], xt[j], preferred_element_type=F32)
        br_ref[j] = r[:FFT_A].astype(BF16)
        bi_ref[j] = r[FFT_A:].astype(BF16)


def _dft2_kernel(br_ref, bi_ref, g_ref, bd_ref, o_ref, *, norm):
    brt = pltpu.einshape("abc->bac", br_ref[...])
    bit = pltpu.einshape("abc->bac", bi_ref[...])
    ys = []
    for i in range(FFT_BLK):
        br = brt[i]
        bi = bit[i]
        g = g_ref[i]
        zr = jnp.dot(g, jnp.concatenate([br, bi], axis=0), preferred_element_type=F32)
        zi = jnp.dot(g, jnp.concatenate([bi, -br], axis=0), preferred_element_type=F32)
        z = jnp.concatenate([zr, zi], axis=1).astype(BF16)
        y = jnp.dot(z, bd_ref[...], preferred_element_type=F32) * norm
        ys.append(y.astype(o_ref.dtype))
    o_ref[...] = pltpu.einshape("abc->bac", jnp.stack(ys, axis=0))


def _fourier_mix(proj, tables):
    w1, g2, bd = tables
    x3 = proj.reshape(FFT_A, FFT_B, IN_W)
    blk = (FFT_A, FFT_BLK, FOURIER_W)
    br, bi = pl.pallas_call(
        _dft1_kernel,
        grid=(FFT_B // FFT_BLK,),
        in_specs=[pl.BlockSpec((2 * FFT_A, FFT_A), lambda j: (0, 0)),
                  pl.BlockSpec(blk, lambda j: (0, j, 0))],
        out_specs=[pl.BlockSpec((FFT_BLK, FFT_A, FOURIER_W), lambda j: (j, 0, 0)),
                   pl.BlockSpec((FFT_BLK, FFT_A, FOURIER_W), lambda j: (j, 0, 0))],
        out_shape=[jax.ShapeDtypeStruct((FFT_B, FFT_A, FOURIER_W), BF16)] * 2,
        compiler_params=_params(("parallel",), 40),
        name="fourier_stage1",
    )(w1, x3)
    norm = 1.0 / math.sqrt(SEQ * HEAD_DIM)
    y3 = pl.pallas_call(
        functools.partial(_dft2_kernel, norm=norm),
        grid=(FFT_A // FFT_BLK,),
        in_specs=[pl.BlockSpec((FFT_B, FFT_BLK, FOURIER_W), lambda j: (0, j, 0)),
                  pl.BlockSpec((FFT_B, FFT_BLK, FOURIER_W), lambda j: (0, j, 0)),
                  pl.BlockSpec((FFT_BLK, FFT_B, 2 * FFT_B), lambda j: (j, 0, 0)),
                  pl.BlockSpec((2 * FOURIER_W, FOURIER_W), lambda j: (0, 0))],
        out_specs=pl.BlockSpec((FFT_B, FFT_BLK, FOURIER_W), lambda j: (0, j, 0)),
        out_shape=jax.ShapeDtypeStruct((FFT_B, FFT_A, FOURIER_W), BF16),
        compiler_params=_params(("parallel",), 40),
        name="fourier_stage2",
    )(br, bi, g2, bd)
    return y3.reshape(SEQ, FOURIER_W)


NA_SUB_ROWS = 2
NA_KEY_ROWS = NA_SUB_ROWS + WIN_H
NA_SUBS_PER_STEP = 64
NA_N_SUB = ROWS // NA_SUB_ROWS
NA_MAX_KEY_START = ROWS - NA_KEY_ROWS
NA_N_ROW_OFFSETS = 2 * WIN_H - 1
NA_MASKED_BLOCK = NA_N_ROW_OFFSETS
LOG2E = math.log2(math.e)


def _na_key_row0(sub):
    return jnp.clip(sub * NA_SUB_ROWS - WIN_H // 2, 0, NA_MAX_KEY_START)


def _na_cases():
    patterns, case_of_sub = [], []
    for sub in range(NA_N_SUB):
        r0 = sub * NA_SUB_ROWS
        ks = int(np.clip(r0 - WIN_H // 2, 0, NA_MAX_KEY_START))
        pat = np.full((NA_SUB_ROWS, NA_KEY_ROWS), NA_MASKED_BLOCK, np.int32)
        for qi in range(NA_SUB_ROWS):
            r = r0 + qi
            r_start = int(np.clip(r - WIN_H // 2, 0, ROWS - WIN_H))
            for kj in range(NA_KEY_ROWS):
                if r_start <= ks + kj < r_start + WIN_H:
                    pat[qi, kj] = ks + kj - r + (WIN_H - 1)
        for c, p in enumerate(patterns):
            if np.array_equal(p, pat):
                case_of_sub.append(c)
                break
        else:
            case_of_sub.append(len(patterns))
            patterns.append(pat)
    return np.stack(patterns), np.asarray(case_of_sub, np.int32)


def _na_case_of(sub, case_of_sub):
    common = int(np.bincount(case_of_sub).argmax())
    case = jnp.int32(common)
    for s in np.nonzero(case_of_sub != common)[0]:
        case = jnp.where(sub == int(s), int(case_of_sub[s]), case)
    return case


def _na_col_blocks(rpb):
    qc = np.arange(GRID_W)[:, None]
    kc = np.arange(GRID_W)[None, :]
    col_start = np.clip(qc - WIN_W // 2, 0, GRID_W - WIN_W)
    col_valid = (kc >= col_start) & (kc < col_start + WIN_W)
    dc = kc - qc + (WIN_W - 1)
    sel = (np.arange(2 * WIN_W - 1)[:, None, None] == dc[None]) & col_valid[None]
    sel = jnp.asarray(sel.astype(np.float32))
    t = jnp.einsum("hrd,dqk->hrqk", rpb.astype(F32), sel, precision=lax.Precision.HIGHEST)
    t = jnp.where(col_valid[None, None], t * LOG2E, NEG_BIG)
    masked = jnp.full((N_NA_HEADS, 1, GRID_W, GRID_W), NEG_BIG, F32)
    t = jnp.concatenate([t, masked], axis=1)
    t = jnp.swapaxes(t, -1, -2)
    return jnp.concatenate([t, t], axis=-1)


def _na_kernel(q_ref, k_ref, v_ref, qg_ref, kg_ref, cb_ref, o_ref, kn_ref, bias_ref):
    step = pl.program_id(1)
    nq = NA_SUB_ROWS * GRID_W
    nk = NA_KEY_ROWS * GRID_W
    row_off, case_of_sub = _na_cases()

    @pl.when(step == 0)
    def _():
        chunk = 1024

        def body(c, carry):
            rows = pl.ds(pl.multiple_of(c * chunk, chunk), chunk)
            kn_ref[rows, :] = _rms_rows(k_ref[rows, :].astype(F32), kg_ref[...]).astype(BF16)
            return carry
        lax.fori_loop(0, SEQ // chunk, body, 0, unroll=2)

        for case in range(row_off.shape[0]):
            for qi in range(NA_SUB_ROWS):
                for kj in range(NA_KEY_ROWS):
                    lanes = slice(qi * GRID_W, (qi + 1) * GRID_W)
                    src = slice((qi % 2) * GRID_W, (qi % 2 + 1) * GRID_W)
                    bias_ref[case, kj * GRID_W:(kj + 1) * GRID_W, lanes] = (
                        cb_ref[0, int(row_off[case, qi, kj]), :, src])

    scale = HEAD_DIM ** -0.5 * LOG2E
    for sb in range(NA_SUBS_PER_STEP):
        sub = step * NA_SUBS_PER_STEP + sb
        case = _na_case_of(sub, case_of_sub)
        q = q_ref[sb * nq:(sb + 1) * nq, :].astype(F32)
        qn = (_rms_rows(q, qg_ref[...]) * scale).astype(BF16)
        kstart = pl.multiple_of(_na_key_row0(sub) * GRID_W, 2 * GRID_W)
        kw = kn_ref[pl.ds(kstart, nk), :]
        vw = v_ref[pl.ds(kstart, nk), :]
        st = lax.dot_general(kw, qn, (((1,), (1,)), ((), ())), preferred_element_type=F32)
        st = st + bias_ref[case]
        m = jnp.max(st, axis=0, keepdims=True)
        pt = jnp.exp2(st - m)
        l = jnp.sum(pt, axis=0, keepdims=True)
        pt = (pt * (1.0 / l)).astype(BF16)
        o = lax.dot_general(pt, vw, (((0,), (0,)), ((), ())), preferred_element_type=F32)
        o_ref[sb * nq:(sb + 1) * nq, :] = o.astype(o_ref.dtype)


def _na_attention(proj, q_g, k_g, col_blocks):
    nq = NA_SUB_ROWS * GRID_W
    nk = NA_KEY_ROWS * GRID_W
    tq = NA_SUBS_PER_STEP * nq
    n_blk = NA_N_ROW_OFFSETS + 1
    qb, kb_, vb = Q_NA_OFF // HEAD_DIM, K_NA_OFF // HEAD_DIM, V_NA_OFF // HEAD_DIM
    return pl.pallas_call(
        _na_kernel,
        grid=(N_NA_HEADS, SEQ // tq),
        in_specs=[pl.BlockSpec((tq, HEAD_DIM), lambda h, i: (i, qb + h)),
                  pl.BlockSpec((SEQ, HEAD_DIM), lambda h, i: (0, kb_ + h)),
                  pl.BlockSpec((SEQ, HEAD_DIM), lambda h, i: (0, vb + h)),
                  pl.BlockSpec((1, HEAD_DIM), lambda h, i: (0, 0)),
                  pl.BlockSpec((1, HEAD_DIM), lambda h, i: (0, 0)),
                  pl.BlockSpec((1, n_blk, GRID_W, 2 * GRID_W), lambda h, i: (h, 0, 0, 0))],
        out_specs=pl.BlockSpec((tq, HEAD_DIM), lambda h, i: (i, h)),
        out_shape=jax.ShapeDtypeStruct((SEQ, NA_W), BF16),
        scratch_shapes=[pltpu.VMEM((SEQ, HEAD_DIM), BF16),
                        pltpu.VMEM((_na_cases()[0].shape[0], nk, nq), F32)],
        compiler_params=_params(("parallel", "arbitrary"), 48),
        name="na_attention",
    )(proj, proj, proj, q_g.reshape(1, HEAD_DIM), k_g.reshape(1, HEAD_DIM), col_blocks)


def _mem_attn_kernel(q_ref, kv_ref, qg_ref, kg_ref, o_ref):
    scale = HEAD_DIM ** -0.5 * LOG2E
    for h in range(N_MEM_HEADS):
        cols = slice(h * HEAD_DIM, (h + 1) * HEAD_DIM)
        qn = (_rms_rows(q_ref[:, cols].astype(F32), qg_ref[...]) * scale).astype(BF16)
        kn = _rms_rows(kv_ref[:, cols].astype(F32), kg_ref[...]).astype(BF16)
        v = kv_ref[:, MEM_W + h * HEAD_DIM:MEM_W + (h + 1) * HEAD_DIM].astype(BF16)
        s = lax.dot_general(qn, kn, (((1,), (1,)), ((), ())), preferred_element_type=F32)
        m = jnp.max(s, axis=-1, keepdims=True)
        p = jnp.exp2(s - m)
        l = jnp.sum(p, axis=-1, keepdims=True)
        o = jnp.dot(p.astype(BF16), v, preferred_element_type=F32) / l
        o_ref[:, cols] = o.astype(o_ref.dtype)


def _mem_attention(proj, kv, q_g, k_g):
    tm = 512
    return pl.pallas_call(
        _mem_attn_kernel,
        grid=(SEQ // tm,),
        in_specs=[pl.BlockSpec((tm, MEM_W), lambda i: (i, Q_MEM_OFF // MEM_W)),
                  pl.BlockSpec((N_MEM, 2 * MEM_W), lambda i: (0, 0)),
                  pl.BlockSpec((1, HEAD_DIM), lambda i: (0, 0)),
                  pl.BlockSpec((1, HEAD_DIM), lambda i: (0, 0))],
        out_specs=pl.BlockSpec((tm, MEM_W), lambda i: (i, 0)),
        out_shape=jax.ShapeDtypeStruct((SEQ, MEM_W), BF16),
        compiler_params=_params(("parallel",), 32),
        name="mem_attention",
    )(proj, kv, q_g.reshape(1, HEAD_DIM), k_g.reshape(1, HEAD_DIM))


def _merge_kernel(fm_ref, na_ref, mo_ref, g0_ref, g1_ref, g2_ref, wf_ref, wn_ref, wm_ref, o_ref,
                  *, col_chunk):
    for c in range(D_MODEL // col_chunk):
        cols = slice(c * col_chunk, (c + 1) * col_chunk)
        o_f = jnp.dot(fm_ref[...], wf_ref[:, cols], preferred_element_type=F32)
        o_n = jnp.dot(na_ref[...], wn_ref[:, cols], preferred_element_type=F32)
        o_m = jnp.dot(mo_ref[...], wm_ref[:, cols], preferred_element_type=F32)
        acc = jax.nn.sigmoid(g0_ref[:, cols].astype(F32)) * o_f
        acc = acc + jax.nn.sigmoid(g1_ref[:, cols].astype(F32)) * o_n
        acc = acc + jax.nn.sigmoid(g2_ref[:, cols].astype(F32)) * o_m
        o_ref[:, cols] = acc.astype(o_ref.dtype)


def _merge(fm, o_na, o_mem, proj, wf, wn, wm):
    tm = 512
    gb = GATE_OFF // D_MODEL
    return pl.pallas_call(
        functools.partial(_merge_kernel, col_chunk=512),
        grid=(SEQ // tm,),
        in_specs=[pl.BlockSpec((tm, FOURIER_W), lambda i: (i, 0)),
                  pl.BlockSpec((tm, NA_W), lambda i: (i, 0)),
                  pl.BlockSpec((tm, MEM_W), lambda i: (i, 0)),
                  pl.BlockSpec((tm, D_MODEL), lambda i: (i, gb)),
                  pl.BlockSpec((tm, D_MODEL), lambda i: (i, gb + 1)),
                  pl.BlockSpec((tm, D_MODEL), lambda i: (i, gb + 2)),
                  pl.BlockSpec((FOURIER_W, D_MODEL), lambda i: (0, 0)),
                  pl.BlockSpec((NA_W, D_MODEL), lambda i: (0, 0)),
                  pl.BlockSpec((MEM_W, D_MODEL), lambda i: (0, 0))],
        out_specs=pl.BlockSpec((tm, D_MODEL), lambda i: (i, 0)),
        out_shape=jax.ShapeDtypeStruct((SEQ, D_MODEL), BF16),
        compiler_params=_params(("parallel",), 48),
        name="gated_merge",
    )(fm, o_na, o_mem, proj, proj, proj, wf, wn, wm)


ROUTER_LANES = 128


def _route_top2(h, wr_ref):
    h_hi = h.astype(BF16)
    h_lo = (h - h_hi.astype(F32)).astype(BF16)
    r_hi = jnp.dot(h_hi, wr_ref[...], preferred_element_type=F32)
    r_lo = jnp.dot(h_lo, wr_ref[:, :ROUTER_LANES], preferred_element_type=F32)
    logits = r_hi[:, :ROUTER_LANES] + r_hi[:, ROUTER_LANES:] + r_lo
    lane = lax.broadcasted_iota(jnp.int32, logits.shape, 1).astype(F32)
    logits = jnp.where(lane < N_EXPERTS, logits, -jnp.inf)
    m1 = jnp.max(logits, axis=-1, keepdims=True)
    i1 = jnp.min(jnp.where(logits == m1, lane, float(ROUTER_LANES)), axis=-1, keepdims=True)
    rest = jnp.where(lane == i1, -jnp.inf, logits)
    m2 = jnp.max(rest, axis=-1, keepdims=True)
    i2 = jnp.min(jnp.where(rest == m2, lane, float(ROUTER_LANES)), axis=-1, keepdims=True)
    e21 = jnp.exp(m2 - m1)
    g1 = 1.0 / (1.0 + e21)
    g2 = e21 * g1
    return jnp.where(lane == 0.0, i1, i2).astype(jnp.int32), jnp.where(lane == 0.0, g1, g2)


def _oproj_kernel(a_ref, w_ref, x_ref, g_ref, *rest, row_chunk, route):
    if route:
        wr_ref, xo_ref, h_ref, idx_ref, gate_ref = rest
    else:
        xo_ref, h_ref = rest
    half = xo_ref.shape[0] // 2
    for r in range(2):
        xo_ref[r * half:(r + 1) * half, :] = (
            jnp.dot(a_ref[r * half:(r + 1) * half, :], w_ref[...], preferred_element_type=F32)
            + x_ref[r * half:(r + 1) * half, :])
        for c in range(half // row_chunk):
            rows = slice(r * half + c * row_chunk, r * half + (c + 1) * row_chunk)
            h = _rms_rows(xo_ref[rows, :], g_ref[...])
            if route:
                h_ref[rows, :] = _pack_halves(h)
                idx_ref[rows, :], gate_ref[rows, :] = _route_top2(h, wr_ref)
            else:
                h_ref[rows, :] = h.astype(h_ref.dtype)


def _out_proj(merged, w_o, x, g, w_router=None):
    tm = 512
    route = w_router is not None
    in_specs = [pl.BlockSpec((tm, D_MODEL), lambda i: (i, 0)),
                pl.BlockSpec((D_MODEL, D_MODEL), lambda i: (0, 0)),
                pl.BlockSpec((tm, D_MODEL), lambda i: (i, 0)),
                pl.BlockSpec((1, D_MODEL), lambda i: (0, 0))]
    args = [merged, w_o, x, g.reshape(1, D_MODEL)]
    out_specs = [pl.BlockSpec((tm, D_MODEL), lambda i: (i, 0))]
    out_shape = [jax.ShapeDtypeStruct((SEQ, D_MODEL), F32)]
    if route:
        w_pad = jnp.zeros((D_MODEL, ROUTER_LANES), F32).at[:, :N_EXPERTS].set(w_router.astype(F32))
        w_hi = w_pad.astype(BF16)
        w_lo = (w_pad - w_hi.astype(F32)).astype(BF16)
        in_specs.append(pl.BlockSpec((D_MODEL, 2 * ROUTER_LANES), lambda i: (0, 0)))
        args.append(jnp.concatenate([w_hi, w_lo], axis=1))
        out_specs += [pl.BlockSpec((tm, HALF_D), lambda i: (i, 0)),
                      pl.BlockSpec((tm, ROUTER_LANES), lambda i: (i, 0)),
                      pl.BlockSpec((tm, ROUTER_LANES), lambda i: (i, 0))]
        out_shape += [jax.ShapeDtypeStruct((SEQ, HALF_D), jnp.uint32),
                      jax.ShapeDtypeStruct((SEQ, ROUTER_LANES), jnp.int32),
                      jax.ShapeDtypeStruct((SEQ, ROUTER_LANES), F32)]
    else:
        out_specs.append(pl.BlockSpec((tm, D_MODEL), lambda i: (i, 0)))
        out_shape.append(jax.ShapeDtypeStruct((SEQ, D_MODEL), BF16))
    return pl.pallas_call(
        functools.partial(_oproj_kernel, row_chunk=128, route=route),
        grid=(SEQ // tm,),
        in_specs=in_specs,
        out_specs=out_specs,
        out_shape=out_shape,
        compiler_params=_params(("parallel",), 48),
        name="out_proj",
    )(*args)


def _swiglu_step(x_ref, wg_ref, wu_ref, wd_ref, acc_ref, rows=None):
    x = x_ref[:rows, :]
    g = jnp.dot(x, wg_ref[0].astype(BF16), preferred_element_type=F32)
    u = jnp.dot(x, wu_ref[0].astype(BF16), preferred_element_type=F32)
    a = (g * _sigmoid(g) * u).astype(BF16)
    acc_ref[:rows, :] += jnp.dot(a, wd_ref[0].astype(BF16), preferred_element_type=F32)


def _dense_ffn_kernel(x_ref, wg_ref, wu_ref, wd_ref, resid_ref, o_ref):
    @pl.when(pl.program_id(1) == 0)
    def _():
        o_ref[...] = resid_ref[...]

    _swiglu_step(x_ref, wg_ref, wu_ref, wd_ref, o_ref)


def _dense_ffn(h, wg, wu, wd, layer, resid):
    tm, tf = 1024, 512
    m, d = h.shape
    nf = D_FF // tf
    return pl.pallas_call(
        _dense_ffn_kernel,
        grid=(m // tm, nf),
        in_specs=[pl.BlockSpec((tm, d), lambda i, f: (i, 0)),
                  pl.BlockSpec((1, d, tf), lambda i, f: (layer, 0, f)),
                  pl.BlockSpec((1, d, tf), lambda i, f: (layer, 0, f)),
                  pl.BlockSpec((1, tf, d), lambda i, f: (layer, f, 0)),
                  pl.BlockSpec((tm, d), lambda i, f: (i, 0))],
        out_specs=pl.BlockSpec((tm, d), lambda i, f: (i, 0)),
        out_shape=jax.ShapeDtypeStruct((m, d), F32),
        compiler_params=_params(("parallel", "arbitrary"), 60),
        name="dense_ffn",
    )(h, wg, wu, wd, resid)


MOE_TF = 512
MOE_NF = D_FF // MOE_TF
MOE_ROWS_PER_STEP = 96
MOE_TM = MOE_NF * MOE_ROWS_PER_STEP
MOE_ROW_CHUNK = 96
MOE_N_TILES = -(-SEQ * TOP_K // MOE_TM) + N_EXPERTS
MOE_THIRDS = 3
MOE_THIRD_ROWS = MOE_TM // MOE_THIRDS


def _tile_row_copy(tok_ref, h_ref, xg_ref, sems, tile, slot, r):
    row = tok_ref[tile * MOE_TM + r]
    return pltpu.make_async_copy(h_ref.at[pl.ds(row, 1), :], xg_ref.at[slot, pl.ds(r, 1), :],
                                 sems.at[slot])


def _wait_tile_rows(h_ref, xg_ref, sems, slot):
    pltpu.make_async_copy(h_ref.at[pl.ds(0, MOE_TM), :], xg_ref.at[slot], sems.at[slot]).wait()


def _moe_expert_kernel(be_ref, nu_ref, nt_ref, tok_ref, h_ref, wg_ref, wu_ref, wd_ref, o_ref,
                       xg_ref, xb_ref, acc_ref, sems):
    i = pl.program_id(0)
    f = pl.program_id(1)
    n_used = nu_ref[0]
    active = i < n_used
    slot = i % 2
    n_chunks = MOE_TM // MOE_ROW_CHUNK

    @pl.when(jnp.logical_and(i == 0, f == 0))
    def _():
        def start_row(r, carry):
            _tile_row_copy(tok_ref, h_ref, xg_ref, sems, 0, 0, r).start()
            return carry
        lax.fori_loop(0, MOE_TM, start_row, 0, unroll=8)

    @pl.when(jnp.logical_and(active, f == 0))
    def _():
        _wait_tile_rows(h_ref, xg_ref, sems, slot)
        acc_ref[...] = jnp.zeros(acc_ref.shape, acc_ref.dtype)

        def unpack_rows(c, carry):
            rows = pl.ds(pl.multiple_of(c * MOE_ROW_CHUNK, MOE_ROW_CHUNK), MOE_ROW_CHUNK)
            lo, hi = _unpack_halves(xg_ref[slot, rows, :])
            xb_ref[rows, :HALF_D] = lo.astype(BF16)
            xb_ref[rows, HALF_D:] = hi.astype(BF16)
            return carry
        lax.fori_loop(0, n_chunks, unpack_rows, 0)

    for thirds in range(1, MOE_THIRDS + 1):
        @pl.when(jnp.logical_and(active, nt_ref[i] == thirds))
        def _(thirds=thirds):
            for j in range(MOE_ROWS_PER_STEP):
                _tile_row_copy(tok_ref, h_ref, xg_ref, sems, i + 1, 1 - slot,
                               f * MOE_ROWS_PER_STEP + j).start()
            _swiglu_step(xb_ref, wg_ref, wu_ref, wd_ref, acc_ref, rows=thirds * MOE_THIRD_ROWS)

    @pl.when(jnp.logical_and(i == n_used, f == 0))
    def _():
        _wait_tile_rows(h_ref, xg_ref, sems, slot)

    @pl.when(f == MOE_NF - 1)
    def _():
        @pl.when(active)
        def _():
            def finish_rows(c, carry):
                rows = pl.ds(pl.multiple_of(c * MOE_ROW_CHUNK, MOE_ROW_CHUNK), MOE_ROW_CHUNK)
                o_ref[rows, :] = _pack_halves(acc_ref[rows, :])
                return carry
            lax.fori_loop(0, n_chunks, finish_rows, 0)

        @pl.when(jnp.logical_not(active))
        def _():
            o_ref[...] = jnp.zeros(o_ref.shape, o_ref.dtype)


def _moe_experts(h_packed, tok, blk_e, n_used, n_thirds, wg, wu, wd):
    d, tf = D_MODEL, MOE_TF

    def f_idx(i, f, nu):
        return jnp.where(i < nu[0], f, MOE_NF - 1)

    def w_gu(i, f, be, nu, nt, tok):
        return (be[i], 0, f_idx(i, f, nu))

    def w_d(i, f, be, nu, nt, tok):
        return (be[i], f_idx(i, f, nu), 0)

    grid_spec = pltpu.PrefetchScalarGridSpec(
        num_scalar_prefetch=4,
        grid=(MOE_N_TILES, MOE_NF),
        in_specs=[pl.BlockSpec(memory_space=pl.ANY),
                  pl.BlockSpec((1, d, tf), w_gu),
                  pl.BlockSpec((1, d, tf), w_gu),
                  pl.BlockSpec((1, tf, d), w_d)],
        out_specs=pl.BlockSpec((MOE_TM, HALF_D), lambda i, f, be, nu, nt, tok: (i, 0)),
        scratch_shapes=[pltpu.VMEM((2, MOE_TM, HALF_D), jnp.uint32),
                        pltpu.VMEM((MOE_TM, d), BF16),
                        pltpu.VMEM((MOE_TM, d), F32),
                        pltpu.SemaphoreType.DMA((2,))],
    )
    return pl.pallas_call(
        _moe_expert_kernel,
        grid_spec=grid_spec,
        out_shape=jax.ShapeDtypeStruct((MOE_N_TILES * MOE_TM, HALF_D), jnp.uint32),
        compiler_params=_params(("arbitrary", "arbitrary"), 60),
        name="moe_experts",
    )(blk_e, n_used, n_thirds, tok, h_packed, wg, wu, wd)


def _start_row_gather(idx_ref, idx_base, src_ref, dst_ref, sem, n_rows):
    def body(r, carry):
        row = idx_ref[idx_base + r]
        pltpu.make_async_copy(src_ref.at[pl.ds(row, 1), :], dst_ref.at[pl.ds(r, 1), :], sem).start()
        return carry
    lax.fori_loop(0, n_rows, body, 0, unroll=8)


def _wait_row_gather(src_ref, dst_ref, sem, n_rows):
    pltpu.make_async_copy(src_ref.at[pl.ds(0, n_rows), :], dst_ref, sem).wait()


def _moe_combine_kernel(pos_ref, y_ref, x_ref, gate_ref, o_ref, buf_ref, sems, *, tokens):
    i = pl.program_id(0)
    slot = i % 2
    rows = TOP_K * tokens

    @pl.when(i == 0)
    def _():
        _start_row_gather(pos_ref, 0, y_ref, buf_ref.at[0], sems.at[0], rows)

    @pl.when(i + 1 < pl.num_programs(0))
    def _():
        _start_row_gather(pos_ref, (i + 1) * rows, y_ref, buf_ref.at[1 - slot],
                          sems.at[1 - slot], rows)

    _wait_row_gather(y_ref, buf_ref.at[slot], sems.at[slot], rows)
    g0 = gate_ref[:, 0:1]
    g1 = gate_ref[:, 1:2]
    lo0, hi0 = _unpack_halves(buf_ref[slot, :tokens, :])
    lo1, hi1 = _unpack_halves(buf_ref[slot, tokens:, :])
    o_ref[:, :HALF_D] = x_ref[:, :HALF_D] + (g0 * lo0 + g1 * lo1)
    o_ref[:, HALF_D:] = x_ref[:, HALF_D:] + (g0 * hi0 + g1 * hi1)


def _moe_combine(y_packed, pos, x, gate):
    tokens = 512
    n = x.shape[0]
    steps = n // tokens
    pos_tiled = pos.reshape(steps, tokens, TOP_K).transpose(0, 2, 1).reshape(-1)
    grid_spec = pltpu.PrefetchScalarGridSpec(
        num_scalar_prefetch=1,
        grid=(steps,),
        in_specs=[pl.BlockSpec(memory_space=pl.ANY),
                  pl.BlockSpec((tokens, D_MODEL), lambda i, pos_ref: (i, 0)),
                  pl.BlockSpec((tokens, ROUTER_LANES), lambda i, pos_ref: (i, 0))],
        out_specs=pl.BlockSpec((tokens, D_MODEL), lambda i, pos_ref: (i, 0)),
        scratch_shapes=[pltpu.VMEM((2, TOP_K * tokens, HALF_D), jnp.uint32),
                        pltpu.SemaphoreType.DMA((2,))],
    )
    return pl.pallas_call(
        functools.partial(_moe_combine_kernel, tokens=tokens),
        grid_spec=grid_spec,
        out_shape=jax.ShapeDtypeStruct((n, D_MODEL), F32),
        compiler_params=_params(("arbitrary",), 40),
        name="moe_combine",
    )(pos_tiled, y_packed, x, gate)


def _slot_tokens_kernel(dest_ref, fill_ref, tok_ref):
    for e in range(N_EXPERTS + 1):
        def fill(b, carry):
            for j in range(8):
                tok_ref[b * 8 + j] = 0
            return carry
        lax.fori_loop(fill_ref[2 * e], fill_ref[2 * e + 1], fill, 0)

    def place(a, carry):
        tok_ref[dest_ref[a]] = a >> 1
        return carry
    lax.fori_loop(0, SEQ * TOP_K, place, 0, unroll=8)


def _slot_tokens(dest, fill_ranges):
    assert TOP_K == 2
    cap = MOE_N_TILES * MOE_TM
    grid_spec = pltpu.PrefetchScalarGridSpec(
        num_scalar_prefetch=2,
        grid=(1,),
        in_specs=[],
        out_specs=pl.BlockSpec(memory_space=pltpu.SMEM),
    )
    return pl.pallas_call(
        _slot_tokens_kernel,
        grid_spec=grid_spec,
        out_shape=jax.ShapeDtypeStruct((cap,), jnp.int32),
        name="moe_slot_tokens",
    )(dest, fill_ranges)


def _moe(x_new, h_packed, idx, gate, wg, wu, wd):
    n = SEQ
    cap = MOE_N_TILES * MOE_TM
    e_flat = idx[:, :TOP_K].reshape(-1)
    onehot = (e_flat[:, None] == jnp.arange(N_EXPERTS, dtype=jnp.int32)[None, :]).astype(jnp.int32)
    csum = jnp.cumsum(onehot, axis=0)
    counts = csum[-1]
    rank = jnp.sum((csum - onehot) * onehot, axis=1)
    padded = (counts + MOE_TM - 1) // MOE_TM * MOE_TM
    pad_ends = jnp.cumsum(padded)
    pad_starts = pad_ends - padded
    dest = jnp.sum(onehot * pad_starts[None, :], axis=1) + rank
    fill_lo = jnp.concatenate([pad_starts + counts, pad_ends[-1:]]) // 8
    fill_hi = (jnp.concatenate([pad_ends, jnp.full((1,), cap, pad_ends.dtype)]) + 7) // 8
    tok = _slot_tokens(dest.astype(jnp.int32),
                       jnp.stack([fill_lo, fill_hi], axis=1).reshape(-1).astype(jnp.int32))
    n_used = (pad_ends[-1] // MOE_TM).astype(jnp.int32).reshape(1)
    tile_start = jnp.arange(MOE_N_TILES, dtype=jnp.int32) * MOE_TM
    tile_start = jnp.minimum(tile_start, pad_ends[-1] - 1)
    blk_e = jnp.sum((tile_start[:, None] >= pad_ends[None, :]).astype(jnp.int32), axis=1)
    blk_e = jnp.clip(blk_e, 0, N_EXPERTS - 1)
    real_rows = counts[blk_e] - (tile_start - pad_starts[blk_e])
    n_thirds = jnp.clip((real_rows + MOE_THIRD_ROWS - 1) // MOE_THIRD_ROWS, 1, MOE_THIRDS)
    yb = _moe_experts(h_packed, tok, blk_e, n_used, n_thirds.astype(jnp.int32), wg, wu, wd)
    return _moe_combine(yb, dest.reshape(n, TOP_K), x_new, gate)


def kernel(x, mem, ln_mix_g, w_in, na_q_g, na_k_g, na_rpb, mem_ln_g, w_mem_kv, mem_q_g, mem_k_g,
           w_fourier_out, w_na_out, w_mem_out, w_o, ln_ffn_g, ffn_w_gate, ffn_w_up, ffn_w_down,
           moe_router, moe_w_gate, moe_w_up, moe_w_down):
    assert x.shape == (1, SEQ, D_MODEL) and mem.shape == (1, N_MEM, D_MODEL)
    xs = x.reshape(SEQ, D_MODEL)
    mem2 = mem.reshape(N_MEM, D_MODEL)
    tables = _dft_tables()
    w_in_bf16 = w_in.astype(BF16)
    for l in range(DEPTH):
        proj = _norm_mm(xs, ln_mix_g[l], w_in_bf16, l, tm=1024, tn=2048, out_dtype=BF16,
                        vmem_mib=58, name="in_proj")
        fm = _fourier_mix(proj, tables)
        o_na = _na_attention(proj, na_q_g[l], na_k_g[l], _na_col_blocks(na_rpb[l]))
        kv = _norm_mm(mem2, mem_ln_g[l], w_mem_kv, l, tm=N_MEM, tn=2 * MEM_W, out_dtype=F32,
                      vmem_mib=40, name="mem_kv_proj")
        o_mem = _mem_attention(proj, kv, mem_q_g[l], mem_k_g[l])
        merged = _merge(fm, o_na, o_mem, proj, w_fourier_out[l].astype(BF16),
                        w_na_out[l].astype(BF16), w_mem_out[l].astype(BF16))
        i = l // 2
        if l % 2 == 0:
            x_new, h2 = _out_proj(merged, w_o[l].astype(BF16), xs, ln_ffn_g[l])
            xs = _dense_ffn(h2, ffn_w_gate.astype(BF16), ffn_w_up.astype(BF16),
                            ffn_w_down.astype(BF16), i, x_new)
        else:
            x_new, h2, idx, gate = _out_proj(merged, w_o[l].astype(BF16), xs, ln_ffn_g[l],
                                             moe_router[i])
            xs = _moe(x_new, h2, idx, gate, moe_w_gate[i], moe_w_up[i], moe_w_down[i])
    return xs.reshape(1, SEQ, D_MODEL)
```

```python
import functools
import math

import numpy as np
import jax
import jax.numpy as jnp
from jax import lax
from jax.experimental import pallas as pl
from jax.experimental.pallas import tpu as pltpu

F32 = jnp.float32
BF16 = jnp.bfloat16

D_MODEL = 2048
SEQ = 16384
DEPTH = 2
GRID_W = 64
ROWS = SEQ // GRID_W
HEAD_DIM = 128
N_FOURIER_GROUPS = 4
FOURIER_W = N_FOURIER_GROUPS * HEAD_DIM
N_NA_HEADS = 8
NA_W = N_NA_HEADS * HEAD_DIM
WIN_H = 8
WIN_W = 16
N_MEM = 256
N_MEM_HEADS = 4
MEM_W = N_MEM_HEADS * HEAD_DIM
IN_W = FOURIER_W + 3 * NA_W + MEM_W + 3 * D_MODEL
D_FF = 5632
N_EXPERTS = 8
TOP_K = 2
EPS = 1e-6
HALF_D = D_MODEL // 2

Q_NA_OFF = FOURIER_W
K_NA_OFF = Q_NA_OFF + NA_W
V_NA_OFF = K_NA_OFF + NA_W
Q_MEM_OFF = V_NA_OFF + NA_W
GATE_OFF = Q_MEM_OFF + MEM_W

MIB = 1024 * 1024
NEG_BIG = -1e30

FFT_A = 128
FFT_B = SEQ // FFT_A


def _params(semantics, vmem_mib):
    return pltpu.CompilerParams(dimension_semantics=semantics,
                                vmem_limit_bytes=int(vmem_mib * MIB))


def _rms_rows(x, gain):
    ms = jnp.mean(x * x, axis=-1, keepdims=True)
    return x * lax.rsqrt(ms + EPS) * gain


def _sigmoid(x):
    return 0.5 * jnp.tanh(0.5 * x) + 0.5


def _bf16_bits(v):
    return pltpu.bitcast(v.astype(BF16).astype(F32), jnp.uint32)


def _pack_halves(v):
    half = v.shape[1] // 2
    lo = lax.shift_right_logical(_bf16_bits(v[:, :half]), jnp.uint32(16))
    return lo | _bf16_bits(v[:, half:])


def _unpack_halves(w):
    lo = pltpu.bitcast(lax.shift_left(w, jnp.uint32(16)), F32)
    hi = pltpu.bitcast(w & jnp.uint32(0xFFFF0000), F32)
    return lo, hi


ROW_SUBLANES = 8
LANES = 128
assert HALF_D == ROW_SUBLANES * LANES


def _rows_to_tiles(w):
    return pltpu.einshape("a(bc)->abc", w, b=ROW_SUBLANES)


def _tiles_to_rows(t):
    return pltpu.einshape("abc->a(bc)", t)


def _norm_mm_kernel(x_ref, g_ref, w_ref, o_ref, h_ref, *, row_chunk):
    @pl.when(pl.program_id(1) == 0)
    def _():
        def body(c, carry):
            rows = pl.ds(pl.multiple_of(c * row_chunk, row_chunk), row_chunk)
            h_ref[rows, :] = _rms_rows(x_ref[rows, :].astype(F32), g_ref[...]).astype(BF16)
            return carry
        lax.fori_loop(0, h_ref.shape[0] // row_chunk, body, 0, unroll=2)

    o_ref[...] = jnp.dot(h_ref[...], w_ref[0].astype(BF16),
                         preferred_element_type=F32).astype(o_ref.dtype)


def _norm_mm(x, g, w, layer, *, tm, tn, out_dtype, vmem_mib, name):
    m, k = x.shape
    n = w.shape[2]
    row_chunk = min(tm, 128)
    return pl.pallas_call(
        functools.partial(_norm_mm_kernel, row_chunk=row_chunk),
        grid=(m // tm, n // tn),
        in_specs=[pl.BlockSpec((tm, k), lambda i, j: (i, 0)),
                  pl.BlockSpec((1, k), lambda i, j: (0, 0)),
                  pl.BlockSpec((1, k, tn), lambda i, j: (layer, 0, j))],
        out_specs=pl.BlockSpec((tm, tn), lambda i, j: (i, j)),
        out_shape=jax.ShapeDtypeStruct((m, n), out_dtype),
        scratch_shapes=[pltpu.VMEM((tm, k), BF16)],
        compiler_params=_params(("parallel", "arbitrary"), vmem_mib),
        name=name,
    )(x, g.reshape(1, k), w)


def _dft_tables():
    a = np.arange(FFT_A)
    ang1 = 2.0 * np.pi * np.outer(a, a) / FFT_A
    w1 = np.concatenate([np.cos(ang1), -np.sin(ang1)], axis=0)
    k1 = np.arange(FFT_A)[:, None, None]
    k2 = np.arange(FFT_B)[None, :, None]
    s2 = np.arange(FFT_B)[None, None, :]
    ang2 = 2.0 * np.pi * ((s2 * (k1 + FFT_A * k2)) % SEQ) / SEQ
    g2 = np.concatenate([np.cos(ang2), np.sin(ang2)], axis=2)
    c = np.arange(HEAD_DIM)
    angc = 2.0 * np.pi * np.outer(c, c) / HEAD_DIM
    eye = np.eye(N_FOURIER_GROUPS)
    bd = np.concatenate([np.kron(eye, np.cos(angc)), np.kron(eye, np.sin(angc))], axis=0)
    return (jnp.asarray(w1, dtype=BF16), jnp.asarray(g2, dtype=BF16),
            jnp.asarray(bd, dtype=BF16))


FFT_BLK = 16


def _dft1_kernel(w_ref, x_ref, br_ref, bi_ref):
    xt = pltpu.einshape("abc->bac", x_ref[...])
    for j in range(FFT_BLK):
        r = jnp.dot(w_ref[...], xt[j], preferred_element_type=F32)
        br_ref[j] = r[:FFT_A].astype(BF16)
        bi_ref[j] = r[FFT_A:].astype(BF16)


def _dft2_kernel(br_ref, bi_ref, g_ref, bd_ref, o_ref, *, norm):
    brt = pltpu.einshape("abc->bac", br_ref[...])
    bit = pltpu.einshape("abc->bac", bi_ref[...])
    ys = []
    for i in range(FFT_BLK):
        br = brt[i]
        bi = bit[i]
        g = g_ref[i]
        zr = jnp.dot(g, jnp.concatenate([br, bi], axis=0), preferred_element_type=F32)
        zi = jnp.dot(g, jnp.concatenate([bi, -br], axis=0), preferred_element_type=F32)
        z = jnp.concatenate([zr, zi], axis=1).astype(BF16)
        y = jnp.dot(z, bd_ref[...], preferred_element_type=F32) * norm
        ys.append(y.astype(o_ref.dtype))
    o_ref[...] = pltpu.einshape("abc->bac", jnp.stack(ys, axis=0))


def _fourier_mix(proj, tables):
    w1, g2, bd = tables
    x3 = proj.reshape(FFT_A, FFT_B, IN_W)
    blk = (FFT_A, FFT_BLK, FOURIER_W)
    br, bi = pl.pallas_call(
        _dft1_kernel,
        grid=(FFT_B // FFT_BLK,),
        in_specs=[pl.BlockSpec((2 * FFT_A, FFT_A), lambda j: (0, 0)),
                  pl.BlockSpec(blk, lambda j: (0, j, 0))],
        out_specs=[pl.BlockSpec((FFT_BLK, FFT_A, FOURIER_W), lambda j: (j, 0, 0)),
                   pl.BlockSpec((FFT_BLK, FFT_A, FOURIER_W), lambda j: (j, 0, 0))],
        out_shape=[jax.ShapeDtypeStruct((FFT_B, FFT_A, FOURIER_W), BF16)] * 2,
        compiler_params=_params(("parallel",), 40),
        name="fourier_stage1",
    )(w1, x3)
    norm = 1.0 / math.sqrt(SEQ * HEAD_DIM)
    y3 = pl.pallas_call(
        functools.partial(_dft2_kernel, norm=norm),
        grid=(FFT_A // FFT_BLK,),
        in_specs=[pl.BlockSpec((FFT_B, FFT_BLK, FOURIER_W), lambda j: (0, j, 0)),
                  pl.BlockSpec((FFT_B, FFT_BLK, FOURIER_W), lambda j: (0, j, 0)),
                  pl.BlockSpec((FFT_BLK, FFT_B, 2 * FFT_B), lambda j: (j, 0, 0)),
                  pl.BlockSpec((2 * FOURIER_W, FOURIER_W), lambda j: (0, 0))],
        out_specs=pl.BlockSpec((FFT_B, FFT_BLK, FOURIER_W), lambda j: (0, j, 0)),
        out_shape=jax.ShapeDtypeStruct((FFT_B, FFT_A, FOURIER_W), BF16),
        compiler_params=_params(("parallel",), 40),
        name="fourier_stage2",
    )(br, bi, g2, bd)
    return y3.reshape(SEQ, FOURIER_W)


NA_SUB_ROWS = 2
NA_KEY_ROWS = NA_SUB_ROWS + WIN_H
NA_SUBS_PER_STEP = 64
NA_N_SUB = ROWS // NA_SUB_ROWS
NA_MAX_KEY_START = ROWS - NA_KEY_ROWS
NA_N_ROW_OFFSETS = 2 * WIN_H - 1
NA_MASKED_BLOCK = NA_N_ROW_OFFSETS
LOG2E = math.log2(math.e)


def _na_key_row0(sub):
    return jnp.clip(sub * NA_SUB_ROWS - WIN_H // 2, 0, NA_MAX_KEY_START)


def _na_cases():
    patterns, case_of_sub = [], []
    for sub in range(NA_N_SUB):
        r0 = sub * NA_SUB_ROWS
        ks = int(np.clip(r0 - WIN_H // 2, 0, NA_MAX_KEY_START))
        pat = np.full((NA_SUB_ROWS, NA_KEY_ROWS), NA_MASKED_BLOCK, np.int32)
        for qi in range(NA_SUB_ROWS):
            r = r0 + qi
            r_start = int(np.clip(r - WIN_H // 2, 0, ROWS - WIN_H))
            for kj in range(NA_KEY_ROWS):
                if r_start <= ks + kj < r_start + WIN_H:
                    pat[qi, kj] = ks + kj - r + (WIN_H - 1)
        for c, p in enumerate(patterns):
            if np.array_equal(p, pat):
                case_of_sub.append(c)
                break
        else:
            case_of_sub.append(len(patterns))
            patterns.append(pat)
    return np.stack(patterns), np.asarray(case_of_sub, np.int32)


def _na_case_of(sub, case_of_sub):
    common = int(np.bincount(case_of_sub).argmax())
    case = jnp.int32(common)
    for s in np.nonzero(case_of_sub != common)[0]:
        case = jnp.where(sub == int(s), int(case_of_sub[s]), case)
    return case


def _na_col_blocks(rpb):
    qc = np.arange(GRID_W)[:, None]
    kc = np.arange(GRID_W)[None, :]
    col_start = np.clip(qc - WIN_W // 2, 0, GRID_W - WIN_W)
    col_valid = (kc >= col_start) & (kc < col_start + WIN_W)
    dc = kc - qc + (WIN_W - 1)
    sel = (np.arange(2 * WIN_W - 1)[:, None, None] == dc[None]) & col_valid[None]
    sel = jnp.asarray(sel.astype(np.float32))
    t = jnp.einsum("hrd,dqk->hrqk", rpb.astype(F32), sel, precision=lax.Precision.HIGHEST)
    t = jnp.where(col_valid[None, None], t * LOG2E, NEG_BIG)
    masked = jnp.full((N_NA_HEADS, 1, GRID_W, GRID_W), NEG_BIG, F32)
    t = jnp.concatenate([t, masked], axis=1)
    t = jnp.swapaxes(t, -1, -2)
    return jnp.concatenate([t, t], axis=-1)


def _na_kernel(q_ref, k_ref, v_ref, qg_ref, kg_ref, cb_ref, o_ref, kn_ref, bias_ref):
    step = pl.program_id(1)
    nq = NA_SUB_ROWS * GRID_W
    nk = NA_KEY_ROWS * GRID_W
    row_off, case_of_sub = _na_cases()

    @pl.when(step == 0)
    def _():
        chunk = 1024

        def body(c, carry):
            rows = pl.ds(pl.multiple_of(c * chunk, chunk), chunk)
            kn_ref[rows, :] = _rms_rows(k_ref[rows, :].astype(F32), kg_ref[...]).astype(BF16)
            return carry
        lax.fori_loop(0, SEQ // chunk, body, 0, unroll=2)

        for case in range(row_off.shape[0]):
            for qi in range(NA_SUB_ROWS):
                for kj in range(NA_KEY_ROWS):
                    lanes = slice(qi * GRID_W, (qi + 1) * GRID_W)
                    src = slice((qi % 2) * GRID_W, (qi % 2 + 1) * GRID_W)
                    bias_ref[case, kj * GRID_W:(kj + 1) * GRID_W, lanes] = (
                        cb_ref[0, int(row_off[case, qi, kj]), :, src])

    scale = HEAD_DIM ** -0.5 * LOG2E
    for sb in range(NA_SUBS_PER_STEP):
        sub = step * NA_SUBS_PER_STEP + sb
        case = _na_case_of(sub, case_of_sub)
        q = q_ref[sb * nq:(sb + 1) * nq, :].astype(F32)
        qn = (_rms_rows(q, qg_ref[...]) * scale).astype(BF16)
        kstart = pl.multiple_of(_na_key_row0(sub) * GRID_W, 2 * GRID_W)
        kw = kn_ref[pl.ds(kstart, nk), :]
        vw = v_ref[pl.ds(kstart, nk), :]
        st = lax.dot_general(kw, qn, (((1,), (1,)), ((), ())), preferred_element_type=F32)
        st = st + bias_ref[case]
        m = jnp.max(st, axis=0, keepdims=True)
        pt = jnp.exp2(st - m)
        l = jnp.sum(pt, axis=0, keepdims=True)
        pt = (pt * (1.0 / l)).astype(BF16)
        o = lax.dot_general(pt, vw, (((0,), (0,)), ((), ())), preferred_element_type=F32)
        o_ref[sb * nq:(sb + 1) * nq, :] = o.astype(o_ref.dtype)


def _na_attention(proj, q_g, k_g, col_blocks):
    nq = NA_SUB_ROWS * GRID_W
    nk = NA_KEY_ROWS * GRID_W
    tq = NA_SUBS_PER_STEP * nq
    n_blk = NA_N_ROW_OFFSETS + 1
    qb, kb_, vb = Q_NA_OFF // HEAD_DIM, K_NA_OFF // HEAD_DIM, V_NA_OFF // HEAD_DIM
    return pl.pallas_call(
        _na_kernel,
        grid=(N_NA_HEADS, SEQ // tq),
        in_specs=[pl.BlockSpec((tq, HEAD_DIM), lambda h, i: (i, qb + h)),
                  pl.BlockSpec((SEQ, HEAD_DIM), lambda h, i: (0, kb_ + h)),
                  pl.BlockSpec((SEQ, HEAD_DIM), lambda h, i: (0, vb + h)),
                  pl.BlockSpec((1, HEAD_DIM), lambda h, i: (0, 0)),
                  pl.BlockSpec((1, HEAD_DIM), lambda h, i: (0, 0)),
                  pl.BlockSpec((1, n_blk, GRID_W, 2 * GRID_W), lambda h, i: (h, 0, 0, 0))],
        out_specs=pl.BlockSpec((tq, HEAD_DIM), lambda h, i: (i, h)),
        out_shape=jax.ShapeDtypeStruct((SEQ, NA_W), BF16),
        scratch_shapes=[pltpu.VMEM((SEQ, HEAD_DIM), BF16),
                        pltpu.VMEM((_na_cases()[0].shape[0], nk, nq), F32)],
        compiler_params=_params(("parallel", "arbitrary"), 48),
        name="na_attention",
    )(proj, proj, proj, q_g.reshape(1, HEAD_DIM), k_g.reshape(1, HEAD_DIM), col_blocks)


def _mem_attn_kernel(q_ref, kv_ref, qg_ref, kg_ref, o_ref):
    scale = HEAD_DIM ** -0.5 * LOG2E
    for h in range(N_MEM_HEADS):
        cols = slice(h * HEAD_DIM, (h + 1) * HEAD_DIM)
        qn = (_rms_rows(q_ref[:, cols].astype(F32), qg_ref[...]) * scale).astype(BF16)
        kn = _rms_rows(kv_ref[:, cols].astype(F32), kg_ref[...]).astype(BF16)
        v = kv_ref[:, MEM_W + h * HEAD_DIM:MEM_W + (h + 1) * HEAD_DIM].astype(BF16)
        s = lax.dot_general(qn, kn, (((1,), (1,)), ((), ())), preferred_element_type=F32)
        m = jnp.max(s, axis=-1, keepdims=True)
        p = jnp.exp2(s - m)
        l = jnp.sum(p, axis=-1, keepdims=True)
        o = jnp.dot(p.astype(BF16), v, preferred_element_type=F32) / l
        o_ref[:, cols] = o.astype(o_ref.dtype)


def _mem_attention(proj, kv, q_g, k_g):
    tm = 512
    return pl.pallas_call(
        _mem_attn_kernel,
        grid=(SEQ // tm,),
        in_specs=[pl.BlockSpec((tm, MEM_W), lambda i: (i, Q_MEM_OFF // MEM_W)),
                  pl.BlockSpec((N_MEM, 2 * MEM_W), lambda i: (0, 0)),
                  pl.BlockSpec((1, HEAD_DIM), lambda i: (0, 0)),
                  pl.BlockSpec((1, HEAD_DIM), lambda i: (0, 0))],
        out_specs=pl.BlockSpec((tm, MEM_W), lambda i: (i, 0)),
        out_shape=jax.ShapeDtypeStruct((SEQ, MEM_W), BF16),
        compiler_params=_params(("parallel",), 32),
        name="mem_attention",
    )(proj, kv, q_g.reshape(1, HEAD_DIM), k_g.reshape(1, HEAD_DIM))


def _merge_kernel(fm_ref, na_ref, mo_ref, g0_ref, g1_ref, g2_ref, wf_ref, wn_ref, wm_ref, o_ref,
                  *, col_chunk):
    for c in range(D_MODEL // col_chunk):
        cols = slice(c * col_chunk, (c + 1) * col_chunk)
        o_f = jnp.dot(fm_ref[...], wf_ref[:, cols], preferred_element_type=F32)
        o_n = jnp.dot(na_ref[...], wn_ref[:, cols], preferred_element_type=F32)
        o_m = jnp.dot(mo_ref[...], wm_ref[:, cols], preferred_element_type=F32)
        acc = jax.nn.sigmoid(g0_ref[:, cols].astype(F32)) * o_f
        acc = acc + jax.nn.sigmoid(g1_ref[:, cols].astype(F32)) * o_n
        acc = acc + jax.nn.sigmoid(g2_ref[:, cols].astype(F32)) * o_m
        o_ref[:, cols] = acc.astype(o_ref.dtype)


def _merge(fm, o_na, o_mem, proj, wf, wn, wm):
    tm = 512
    gb = GATE_OFF // D_MODEL
    return pl.pallas_call(
        functools.partial(_merge_kernel, col_chunk=512),
        grid=(SEQ // tm,),
        in_specs=[pl.BlockSpec((tm, FOURIER_W), lambda i: (i, 0)),
                  pl.BlockSpec((tm, NA_W), lambda i: (i, 0)),
                  pl.BlockSpec((tm, MEM_W), lambda i: (i, 0)),
                  pl.BlockSpec((tm, D_MODEL), lambda i: (i, gb)),
                  pl.BlockSpec((tm, D_MODEL), lambda i: (i, gb + 1)),
                  pl.BlockSpec((tm, D_MODEL), lambda i: (i, gb + 2)),
                  pl.BlockSpec((FOURIER_W, D_MODEL), lambda i: (0, 0)),
                  pl.BlockSpec((NA_W, D_MODEL), lambda i: (0, 0)),
                  pl.BlockSpec((MEM_W, D_MODEL), lambda i: (0, 0))],
        out_specs=pl.BlockSpec((tm, D_MODEL), lambda i: (i, 0)),
        out_shape=jax.ShapeDtypeStruct((SEQ, D_MODEL), BF16),
        compiler_params=_params(("parallel",), 48),
        name="gated_merge",
    )(fm, o_na, o_mem, proj, proj, proj, wf, wn, wm)


ROUTER_LANES = 128


def _route_top2(h, wr_ref):
    h_hi = h.astype(BF16)
    h_lo = (h - h_hi.astype(F32)).astype(BF16)
    r_hi = jnp.dot(h_hi, wr_ref[...], preferred_element_type=F32)
    r_lo = jnp.dot(h_lo, wr_ref[:, :ROUTER_LANES], preferred_element_type=F32)
    logits = r_hi[:, :ROUTER_LANES] + r_hi[:, ROUTER_LANES:] + r_lo
    lane = lax.broadcasted_iota(jnp.int32, logits.shape, 1).astype(F32)
    logits = jnp.where(lane < N_EXPERTS, logits, -jnp.inf)
    m1 = jnp.max(logits, axis=-1, keepdims=True)
    i1 = jnp.min(jnp.where(logits == m1, lane, float(ROUTER_LANES)), axis=-1, keepdims=True)
    rest = jnp.where(lane == i1, -jnp.inf, logits)
    m2 = jnp.max(rest, axis=-1, keepdims=True)
    i2 = jnp.min(jnp.where(rest == m2, lane, float(ROUTER_LANES)), axis=-1, keepdims=True)
    e21 = jnp.exp(m2 - m1)
    g1 = 1.0 / (1.0 + e21)
    g2 = e21 * g1
    return jnp.where(lane == 0.0, i1, i2).astype(jnp.int32), jnp.where(lane == 0.0, g1, g2)


def _oproj_kernel(a_ref, w_ref, x_ref, g_ref, *rest, row_chunk, route):
    if route:
        wr_ref, xo_ref, h_ref, idx_ref, gate_ref = rest
    else:
        xo_ref, h_ref = rest
    half = xo_ref.shape[0] // 2
    for r in range(2):
        xo_ref[r * half:(r + 1) * half, :] = (
            jnp.dot(a_ref[r * half:(r + 1) * half, :], w_ref[...], preferred_element_type=F32)
            + x_ref[r * half:(r + 1) * half, :])
        for c in range(half // row_chunk):
            rows = slice(r * half + c * row_chunk, r * half + (c + 1) * row_chunk)
            h = _rms_rows(xo_ref[rows, :], g_ref[...])
            if route:
                h_ref[rows] = _rows_to_tiles(_pack_halves(h))
                idx_ref[rows, :], gate_ref[rows, :] = _route_top2(h, wr_ref)
            else:
                h_ref[rows, :] = h.astype(h_ref.dtype)


def _out_proj(merged, w_o, x, g, w_router=None):
    tm = 512
    route = w_router is not None
    in_specs = [pl.BlockSpec((tm, D_MODEL), lambda i: (i, 0)),
                pl.BlockSpec((D_MODEL, D_MODEL), lambda i: (0, 0)),
                pl.BlockSpec((tm, D_MODEL), lambda i: (i, 0)),
                pl.BlockSpec((1, D_MODEL), lambda i: (0, 0))]
    args = [merged, w_o, x, g.reshape(1, D_MODEL)]
    out_specs = [pl.BlockSpec((tm, D_MODEL), lambda i: (i, 0))]
    out_shape = [jax.ShapeDtypeStruct((SEQ, D_MODEL), F32)]
    if route:
        w_pad = jnp.zeros((D_MODEL, ROUTER_LANES), F32).at[:, :N_EXPERTS].set(w_router.astype(F32))
        w_hi = w_pad.astype(BF16)
        w_lo = (w_pad - w_hi.astype(F32)).astype(BF16)
        in_specs.append(pl.BlockSpec((D_MODEL, 2 * ROUTER_LANES), lambda i: (0, 0)))
        args.append(jnp.concatenate([w_hi, w_lo], axis=1))
        out_specs += [pl.BlockSpec((tm, ROW_SUBLANES, LANES), lambda i: (i, 0, 0)),
                      pl.BlockSpec((tm, ROUTER_LANES), lambda i: (i, 0)),
                      pl.BlockSpec((tm, ROUTER_LANES), lambda i: (i, 0))]
        out_shape += [jax.ShapeDtypeStruct((SEQ, ROW_SUBLANES, LANES), jnp.uint32),
                      jax.ShapeDtypeStruct((SEQ, ROUTER_LANES), jnp.int32),
                      jax.ShapeDtypeStruct((SEQ, ROUTER_LANES), F32)]
    else:
        out_specs.append(pl.BlockSpec((tm, D_MODEL), lambda i: (i, 0)))
        out_shape.append(jax.ShapeDtypeStruct((SEQ, D_MODEL), BF16))
    return pl.pallas_call(
        functools.partial(_oproj_kernel, row_chunk=128, route=route),
        grid=(SEQ // tm,),
        in_specs=in_specs,
        out_specs=out_specs,
        out_shape=out_shape,
        compiler_params=_params(("parallel",), 48),
        name="out_proj",
    )(*args)


def _swiglu_step(x_ref, wg_ref, wu_ref, wd_ref, acc_ref, rows=None):
    x = x_ref[:rows, :]
    g = jnp.dot(x, wg_ref[0].astype(BF16), preferred_element_type=F32)
    u = jnp.dot(x, wu_ref[0].astype(BF16), preferred_element_type=F32)
    a = (g * _sigmoid(g) * u).astype(BF16)
    acc_ref[:rows, :] += jnp.dot(a, wd_ref[0].astype(BF16), preferred_element_type=F32)


def _dense_ffn_kernel(x_ref, wg_ref, wu_ref, wd_ref, resid_ref, o_ref):
    @pl.when(pl.program_id(1) == 0)
    def _():
        o_ref[...] = resid_ref[...]

    _swiglu_step(x_ref, wg_ref, wu_ref, wd_ref, o_ref)


def _dense_ffn(h, wg, wu, wd, layer, resid):
    tm, tf = 1024, 512
    m, d = h.shape
    nf = D_FF // tf
    return pl.pallas_call(
        _dense_ffn_kernel,
        grid=(m // tm, nf),
        in_specs=[pl.BlockSpec((tm, d), lambda i, f: (i, 0)),
                  pl.BlockSpec((1, d, tf), lambda i, f: (layer, 0, f)),
                  pl.BlockSpec((1, d, tf), lambda i, f: (layer, 0, f)),
                  pl.BlockSpec((1, tf, d), lambda i, f: (layer, f, 0)),
                  pl.BlockSpec((tm, d), lambda i, f: (i, 0))],
        out_specs=pl.BlockSpec((tm, d), lambda i, f: (i, 0)),
        out_shape=jax.ShapeDtypeStruct((m, d), F32),
        compiler_params=_params(("parallel", "arbitrary"), 60),
        name="dense_ffn",
    )(h, wg, wu, wd, resid)


MOE_TF = 512
MOE_NF = D_FF // MOE_TF
MOE_ROWS_PER_STEP = 96
MOE_TM = MOE_NF * MOE_ROWS_PER_STEP
MOE_ROW_CHUNK = 96
MOE_N_TILES = -(-SEQ * TOP_K // MOE_TM) + N_EXPERTS
MOE_THIRDS = 3
MOE_THIRD_ROWS = MOE_TM // MOE_THIRDS


def _tile_row_copy(tok_ref, h_ref, xg_ref, sems, tile, slot, r):
    row = tok_ref[tile * MOE_TM + r]
    return pltpu.make_async_copy(h_ref.at[pl.ds(row, 1)], xg_ref.at[slot, pl.ds(r, 1)],
                                 sems.at[slot])


def _wait_tile_rows(h_ref, xg_ref, sems, slot):
    pltpu.make_async_copy(h_ref.at[pl.ds(0, MOE_TM)], xg_ref.at[slot], sems.at[slot]).wait()


def _moe_expert_kernel(be_ref, nu_ref, nt_ref, tok_ref, h_ref, wg_ref, wu_ref, wd_ref, o_ref,
                       xg_ref, xb_ref, acc_ref, sems):
    i = pl.program_id(0)
    f = pl.program_id(1)
    n_used = nu_ref[0]
    active = i < n_used
    slot = i % 2
    n_chunks = MOE_TM // MOE_ROW_CHUNK

    @pl.when(jnp.logical_and(i == 0, f == 0))
    def _():
        def start_row(r, carry):
            _tile_row_copy(tok_ref, h_ref, xg_ref, sems, 0, 0, r).start()
            return carry
        lax.fori_loop(0, MOE_TM, start_row, 0, unroll=8)

    @pl.when(jnp.logical_and(active, f == 0))
    def _():
        _wait_tile_rows(h_ref, xg_ref, sems, slot)
        acc_ref[...] = jnp.zeros(acc_ref.shape, acc_ref.dtype)

        def unpack_rows(c, carry):
            rows = pl.ds(pl.multiple_of(c * MOE_ROW_CHUNK, MOE_ROW_CHUNK), MOE_ROW_CHUNK)
            lo, hi = _unpack_halves(_tiles_to_rows(xg_ref[slot, rows]))
            xb_ref[rows, :HALF_D] = lo.astype(BF16)
            xb_ref[rows, HALF_D:] = hi.astype(BF16)
            return carry
        lax.fori_loop(0, n_chunks, unpack_rows, 0)

    for thirds in range(1, MOE_THIRDS + 1):
        @pl.when(jnp.logical_and(active, nt_ref[i] == thirds))
        def _(thirds=thirds):
            for j in range(MOE_ROWS_PER_STEP):
                _tile_row_copy(tok_ref, h_ref, xg_ref, sems, i + 1, 1 - slot,
                               f * MOE_ROWS_PER_STEP + j).start()
            _swiglu_step(xb_ref, wg_ref, wu_ref, wd_ref, acc_ref, rows=thirds * MOE_THIRD_ROWS)

    @pl.when(jnp.logical_and(i == n_used, f == 0))
    def _():
        _wait_tile_rows(h_ref, xg_ref, sems, slot)

    @pl.when(f == MOE_NF - 1)
    def _():
        @pl.when(active)
        def _():
            def finish_rows(c, carry):
                rows = pl.ds(pl.multiple_of(c * MOE_ROW_CHUNK, MOE_ROW_CHUNK), MOE_ROW_CHUNK)
                o_ref[rows] = _rows_to_tiles(_pack_halves(acc_ref[rows, :]))
                return carry
            lax.fori_loop(0, n_chunks, finish_rows, 0)

        @pl.when(jnp.logical_not(active))
        def _():
            o_ref[...] = jnp.zeros(o_ref.shape, o_ref.dtype)


def _moe_experts(h_packed, tok, blk_e, n_used, n_thirds, wg, wu, wd):
    d, tf = D_MODEL, MOE_TF

    def f_idx(i, f, nu):
        return jnp.where(i < nu[0], f, MOE_NF - 1)

    def w_gu(i, f, be, nu, nt, tok):
        return (be[i], 0, f_idx(i, f, nu))

    def w_d(i, f, be, nu, nt, tok):
        return (be[i], f_idx(i, f, nu), 0)

    grid_spec = pltpu.PrefetchScalarGridSpec(
        num_scalar_prefetch=4,
        grid=(MOE_N_TILES, MOE_NF),
        in_specs=[pl.BlockSpec(memory_space=pl.ANY),
                  pl.BlockSpec((1, d, tf), w_gu),
                  pl.BlockSpec((1, d, tf), w_gu),
                  pl.BlockSpec((1, tf, d), w_d)],
        out_specs=pl.BlockSpec((MOE_TM, ROW_SUBLANES, LANES),
                               lambda i, f, be, nu, nt, tok: (i, 0, 0)),
        scratch_shapes=[pltpu.VMEM((2, MOE_TM, ROW_SUBLANES, LANES), jnp.uint32),
                        pltpu.VMEM((MOE_TM, d), BF16),
                        pltpu.VMEM((MOE_TM, d), F32),
                        pltpu.SemaphoreType.DMA((2,))],
    )
    return pl.pallas_call(
        _moe_expert_kernel,
        grid_spec=grid_spec,
        out_shape=jax.ShapeDtypeStruct((MOE_N_TILES * MOE_TM, ROW_SUBLANES, LANES), jnp.uint32),
        compiler_params=_params(("arbitrary", "arbitrary"), 60),
        name="moe_experts",
    )(blk_e, n_used, n_thirds, tok, h_packed, wg, wu, wd)


def _start_row_gather(idx_ref, idx_base, src_ref, dst_ref, sem, n_rows):
    def body(r, carry):
        row = idx_ref[idx_base + r]
        pltpu.make_async_copy(src_ref.at[pl.ds(row, 1)], dst_ref.at[pl.ds(r, 1)], sem).start()
        return carry
    lax.fori_loop(0, n_rows, body, 0, unroll=8)


def _wait_row_gather(src_ref, dst_ref, sem, n_rows):
    pltpu.make_async_copy(src_ref.at[pl.ds(0, n_rows)], dst_ref, sem).wait()


def _moe_combine_kernel(pos_ref, y_ref, x_ref, gate_ref, o_ref, buf_ref, sems, *, tokens):
    i = pl.program_id(0)
    slot = i % 2
    rows = TOP_K * tokens

    @pl.when(i == 0)
    def _():
        _start_row_gather(pos_ref, 0, y_ref, buf_ref.at[0], sems.at[0], rows)

    @pl.when(i + 1 < pl.num_programs(0))
    def _():
        _start_row_gather(pos_ref, (i + 1) * rows, y_ref, buf_ref.at[1 - slot],
                          sems.at[1 - slot], rows)

    _wait_row_gather(y_ref, buf_ref.at[slot], sems.at[slot], rows)
    g0 = gate_ref[:, 0:1]
    g1 = gate_ref[:, 1:2]
    lo0, hi0 = _unpack_halves(_tiles_to_rows(buf_ref[slot, :tokens]))
    lo1, hi1 = _unpack_halves(_tiles_to_rows(buf_ref[slot, tokens:]))
    o_ref[:, :HALF_D] = x_ref[:, :HALF_D] + (g0 * lo0 + g1 * lo1)
    o_ref[:, HALF_D:] = x_ref[:, HALF_D:] + (g0 * hi0 + g1 * hi1)


def _moe_combine(y_packed, pos, x, gate):
    tokens = 512
    n = x.shape[0]
    steps = n // tokens
    pos_tiled = pos.reshape(steps, tokens, TOP_K).transpose(0, 2, 1).reshape(-1)
    grid_spec = pltpu.PrefetchScalarGridSpec(
        num_scalar_prefetch=1,
        grid=(steps,),
        in_specs=[pl.BlockSpec(memory_space=pl.ANY),
                  pl.BlockSpec((tokens, D_MODEL), lambda i, pos_ref: (i, 0)),
                  pl.BlockSpec((tokens, ROUTER_LANES), lambda i, pos_ref: (i, 0))],
        out_specs=pl.BlockSpec((tokens, D_MODEL), lambda i, pos_ref: (i, 0)),
        scratch_shapes=[pltpu.VMEM((2, TOP_K * tokens, ROW_SUBLANES, LANES), jnp.uint32),
                        pltpu.SemaphoreType.DMA((2,))],
    )
    return pl.pallas_call(
        functools.partial(_moe_combine_kernel, tokens=tokens),
        grid_spec=grid_spec,
        out_shape=jax.ShapeDtypeStruct((n, D_MODEL), F32),
        compiler_params=_params(("arbitrary",), 40),
        name="moe_combine",
    )(pos_tiled, y_packed, x, gate)


def _slot_tokens_kernel(dest_ref, fill_ref, tok_ref):
    for e in range(N_EXPERTS + 1):
        def fill(b, carry):
            for j in range(8):
                tok_ref[b * 8 + j] = 0
            return carry
        lax.fori_loop(fill_ref[2 * e], fill_ref[2 * e + 1], fill, 0)

    def place(a, carry):
        tok_ref[dest_ref[a]] = a >> 1
        return carry
    lax.fori_loop(0, SEQ * TOP_K, place, 0, unroll=8)


def _slot_tokens(dest, fill_ranges):
    assert TOP_K == 2
    cap = MOE_N_TILES * MOE_TM
    grid_spec = pltpu.PrefetchScalarGridSpec(
        num_scalar_prefetch=2,
        grid=(1,),
        in_specs=[],
        out_specs=pl.BlockSpec(memory_space=pltpu.SMEM),
    )
    return pl.pallas_call(
        _slot_tokens_kernel,
        grid_spec=grid_spec,
        out_shape=jax.ShapeDtypeStruct((cap,), jnp.int32),
        name="moe_slot_tokens",
    )(dest, fill_ranges)


def _moe(x_new, h_packed, idx, gate, wg, wu, wd):
    n = SEQ
    cap = MOE_N_TILES * MOE_TM
    e_flat = idx[:, :TOP_K].reshape(-1)
    onehot = (e_flat[:, None] == jnp.arange(N_EXPERTS, dtype=jnp.int32)[None, :]).astype(jnp.int32)
    csum = jnp.cumsum(onehot, axis=0)
    counts = csum[-1]
    rank = jnp.sum((csum - onehot) * onehot, axis=1)
    padded = (counts + MOE_TM - 1) // MOE_TM * MOE_TM
    pad_ends = jnp.cumsum(padded)
    pad_starts = pad_ends - padded
    dest = jnp.sum(onehot * pad_starts[None, :], axis=1) + rank
    fill_lo = jnp.concatenate([pad_starts + counts, pad_ends[-1:]]) // 8
    fill_hi = (jnp.concatenate([pad_ends, jnp.full((1,), cap, pad_ends.dtype)]) + 7) // 8
    tok = _slot_tokens(dest.astype(jnp.int32),
                       jnp.stack([fill_lo, fill_hi], axis=1).reshape(-1).astype(jnp.int32))
    n_used = (pad_ends[-1] // MOE_TM).astype(jnp.int32).reshape(1)
    tile_start = jnp.arange(MOE_N_TILES, dtype=jnp.int32) * MOE_TM
    tile_start = jnp.minimum(tile_start, pad_ends[-1] - 1)
    blk_e = jnp.sum((tile_start[:, None] >= pad_ends[None, :]).astype(jnp.int32), axis=1)
    blk_e = jnp.clip(blk_e, 0, N_EXPERTS - 1)
    real_rows = counts[blk_e] - (tile_start - pad_starts[blk_e])
    n_thirds = jnp.clip((real_rows + MOE_THIRD_ROWS - 1) // MOE_THIRD_ROWS, 1, MOE_THIRDS)
    yb = _moe_experts(h_packed, tok, blk_e, n_used, n_thirds.astype(jnp.int32), wg, wu, wd)
    return _moe_combine(yb, dest.reshape(n, TOP_K), x_new, gate)


def kernel(x, mem, ln_mix_g, w_in, na_q_g, na_k_g, na_rpb, mem_ln_g, w_mem_kv, mem_q_g, mem_k_g,
           w_fourier_out, w_na_out, w_mem_out, w_o, ln_ffn_g, ffn_w_gate, ffn_w_up, ffn_w_down,
           moe_router, moe_w_gate, moe_w_up, moe_w_down):
    assert x.shape == (1, SEQ, D_MODEL) and mem.shape == (1, N_MEM, D_MODEL)
    xs = x.reshape(SEQ, D_MODEL)
    mem2 = mem.reshape(N_MEM, D_MODEL)
    tables = _dft_tables()
    w_in_bf16 = w_in.astype(BF16)
    for l in range(DEPTH):
        proj = _norm_mm(xs, ln_mix_g[l], w_in_bf16, l, tm=1024, tn=2048, out_dtype=BF16,
                        vmem_mib=58, name="in_proj")
        fm = _fourier_mix(proj, tables)
        o_na = _na_attention(proj, na_q_g[l], na_k_g[l], _na_col_blocks(na_rpb[l]))
        kv = _norm_mm(mem2, mem_ln_g[l], w_mem_kv, l, tm=N_MEM, tn=2 * MEM_W, out_dtype=F32,
                      vmem_mib=40, name="mem_kv_proj")
        o_mem = _mem_attention(proj, kv, mem_q_g[l], mem_k_g[l])
        merged = _merge(fm, o_na, o_mem, proj, w_fourier_out[l].astype(BF16),
                        w_na_out[l].astype(BF16), w_mem_out[l].astype(BF16))
        i = l // 2
        if l % 2 == 0:
            x_new, h2 = _out_proj(merged, w_o[l].astype(BF16), xs, ln_ffn_g[l])
            xs = _dense_ffn(h2, ffn_w_gate.astype(BF16), ffn_w_up.astype(BF16),
                            ffn_w_down.astype(BF16), i, x_new)
        else:
            x_new, h2, idx, gate = _out_proj(merged, w_o[l].astype(BF16), xs, ln_ffn_g[l],
                                             moe_router[i])
            xs = _moe(x_new, h2, idx, gate, moe_w_gate[i], moe_w_up[i], moe_w_down[i])
    return xs.reshape(1, SEQ, D_MODEL)
```

```python
import functools
import math

import numpy as np
import jax
import jax.numpy as jnp
from jax import lax
from jax.experimental import pallas as pl
from jax.experimental.pallas import tpu as pltpu

F32 = jnp.float32
BF16 = jnp.bfloat16

D_MODEL = 2048
SEQ = 16384
DEPTH = 2
GRID_W = 64
ROWS = SEQ // GRID_W
HEAD_DIM = 128
N_FOURIER_GROUPS = 4
FOURIER_W = N_FOURIER_GROUPS * HEAD_DIM
N_NA_HEADS = 8
NA_W = N_NA_HEADS * HEAD_DIM
WIN_H = 8
WIN_W = 16
N_MEM = 256
N_MEM_HEADS = 4
MEM_W = N_MEM_HEADS * HEAD_DIM
IN_W = FOURIER_W + 3 * NA_W + MEM_W + 3 * D_MODEL
D_FF = 5632
N_EXPERTS = 8
TOP_K = 2
EPS = 1e-6
HALF_D = D_MODEL // 2

Q_NA_OFF = FOURIER_W
K_NA_OFF = Q_NA_OFF + NA_W
V_NA_OFF = K_NA_OFF + NA_W
Q_MEM_OFF = V_NA_OFF + NA_W
GATE_OFF = Q_MEM_OFF + MEM_W

MIB = 1024 * 1024
NEG_BIG = -1e30

FFT_A = 128
FFT_B = SEQ // FFT_A


def _params(semantics, vmem_mib):
    return pltpu.CompilerParams(dimension_semantics=semantics,
                                vmem_limit_bytes=int(vmem_mib * MIB))


def _rms_rows(x, gain):
    ms = jnp.mean(x * x, axis=-1, keepdims=True)
    return x * lax.rsqrt(ms + EPS) * gain


def _sigmoid(x):
    return 0.5 * jnp.tanh(0.5 * x) + 0.5


def _bf16_bits(v):
    return pltpu.bitcast(v.astype(BF16).astype(F32), jnp.uint32)


def _pack_halves(v):
    half = v.shape[1] // 2
    lo = lax.shift_right_logical(_bf16_bits(v[:, :half]), jnp.uint32(16))
    return lo | _bf16_bits(v[:, half:])


def _unpack_halves(w):
    lo = pltpu.bitcast(lax.shift_left(w, jnp.uint32(16)), F32)
    hi = pltpu.bitcast(w & jnp.uint32(0xFFFF0000), F32)
    return lo, hi


ROW_SUBLANES = 8
LANES = 128
assert HALF_D == ROW_SUBLANES * LANES


def _rows_to_tiles(w):
    return pltpu.einshape("a(bc)->abc", w, b=ROW_SUBLANES)


def _tiles_to_rows(t):
    return pltpu.einshape("abc->a(bc)", t)


def _norm_mm_kernel(x_ref, g_ref, w_ref, o_ref, h_ref, *, row_chunk):
    @pl.when(pl.program_id(1) == 0)
    def _():
        def body(c, carry):
            rows = pl.ds(pl.multiple_of(c * row_chunk, row_chunk), row_chunk)
            h_ref[rows, :] = _rms_rows(x_ref[rows, :].astype(F32), g_ref[...]).astype(BF16)
            return carry
        lax.fori_loop(0, h_ref.shape[0] // row_chunk, body, 0, unroll=2)

    o_ref[...] = jnp.dot(h_ref[...], w_ref[0].astype(BF16),
                         preferred_element_type=F32).astype(o_ref.dtype)


def _norm_mm(x, g, w, layer, *, tm, tn, out_dtype, vmem_mib, name):
    m, k = x.shape
    n = w.shape[2]
    row_chunk = min(tm, 128)
    return pl.pallas_call(
        functools.partial(_norm_mm_kernel, row_chunk=row_chunk),
        grid=(m // tm, n // tn),
        in_specs=[pl.BlockSpec((tm, k), lambda i, j: (i, 0)),
                  pl.BlockSpec((1, k), lambda i, j: (0, 0)),
                  pl.BlockSpec((1, k, tn), lambda i, j: (layer, 0, j))],
        out_specs=pl.BlockSpec((tm, tn), lambda i, j: (i, j)),
        out_shape=jax.ShapeDtypeStruct((m, n), out_dtype),
        scratch_shapes=[pltpu.VMEM((tm, k), BF16)],
        compiler_params=_params(("parallel", "arbitrary"), vmem_mib),
        name=name,
    )(x, g.reshape(1, k), w)


def _dft_tables():
    a = np.arange(FFT_A)
    ang1 = 2.0 * np.pi * np.outer(a, a) / FFT_A
    w1 = np.concatenate([np.cos(ang1), -np.sin(ang1)], axis=0)
    k1 = np.arange(FFT_A)[:, None, None]
    k2 = np.arange(FFT_B)[None, :, None]
    s2 = np.arange(FFT_B)[None, None, :]
    ang2 = 2.0 * np.pi * ((s2 * (k1 + FFT_A * k2)) % SEQ) / SEQ
    g2 = np.concatenate([np.cos(ang2), np.sin(ang2)], axis=2)
    c = np.arange(HEAD_DIM)
    angc = 2.0 * np.pi * np.outer(c, c) / HEAD_DIM
    eye = np.eye(N_FOURIER_GROUPS)
    bd = np.concatenate([np.kron(eye, np.cos(angc)), np.kron(eye, np.sin(angc))], axis=0)
    return (jnp.asarray(w1, dtype=BF16), jnp.asarray(g2, dtype=BF16),
            jnp.asarray(bd, dtype=BF16))


FFT_BLK = 16


def _dft1_kernel(w_ref, x_ref, br_ref, bi_ref):
    xt = pltpu.einshape("abc->bac", x_ref[...])
    for j in range(FFT_BLK):
        r = jnp.dot(w_ref[...], xt[j], preferred_element_type=F32)
        br_ref[j] = r[:FFT_A].astype(BF16)
        bi_ref[j] = r[FFT_A:].astype(BF16)


def _dft2_kernel(br_ref, bi_ref, g_ref, bd_ref, o_ref, *, norm):
    brt = pltpu.einshape("abc->bac", br_ref[...])
    bit = pltpu.einshape("abc->bac", bi_ref[...])
    ys = []
    for i in range(FFT_BLK):
        br = brt[i]
        bi = bit[i]
        g = g_ref[i]
        zr = jnp.dot(g, jnp.concatenate([br, bi], axis=0), preferred_element_type=F32)
        zi = jnp.dot(g, jnp.concatenate([bi, -br], axis=0), preferred_element_type=F32)
        z = jnp.concatenate([zr, zi], axis=1).astype(BF16)
        y = jnp.dot(z, bd_ref[...], preferred_element_type=F32) * norm
        ys.append(y.astype(o_ref.dtype))
    o_ref[...] = pltpu.einshape("abc->bac", jnp.stack(ys, axis=0))


def _fourier_mix(proj, tables):
    w1, g2, bd = tables
    x3 = proj.reshape(FFT_A, FFT_B, IN_W)
    blk = (FFT_A, FFT_BLK, FOURIER_W)
    br, bi = pl.pallas_call(
        _dft1_kernel,
        grid=(FFT_B // FFT_BLK,),
        in_specs=[pl.BlockSpec((2 * FFT_A, FFT_A), lambda j: (0, 0)),
                  pl.BlockSpec(blk, lambda j: (0, j, 0))],
        out_specs=[pl.BlockSpec((FFT_BLK, FFT_A, FOURIER_W), lambda j: (j, 0, 0)),
                   pl.BlockSpec((FFT_BLK, FFT_A, FOURIER_W), lambda j: (j, 0, 0))],
        out_shape=[jax.ShapeDtypeStruct((FFT_B, FFT_A, FOURIER_W), BF16)] * 2,
        compiler_params=_params(("parallel",), 40),
        name="fourier_stage1",
    )(w1, x3)
    norm = 1.0 / math.sqrt(SEQ * HEAD_DIM)
    y3 = pl.pallas_call(
        functools.partial(_dft2_kernel, norm=norm),
        grid=(FFT_A // FFT_BLK,),
        in_specs=[pl.BlockSpec((FFT_B, FFT_BLK, FOURIER_W), lambda j: (0, j, 0)),
                  pl.BlockSpec((FFT_B, FFT_BLK, FOURIER_W), lambda j: (0, j, 0)),
                  pl.BlockSpec((FFT_BLK, FFT_B, 2 * FFT_B), lambda j: (j, 0, 0)),
                  pl.BlockSpec((2 * FOURIER_W, FOURIER_W), lambda j: (0, 0))],
        out_specs=pl.BlockSpec((FFT_B, FFT_BLK, FOURIER_W), lambda j: (0, j, 0)),
        out_shape=jax.ShapeDtypeStruct((FFT_B, FFT_A, FOURIER_W), BF16),
        compiler_params=_params(("parallel",), 40),
        name="fourier_stage2",
    )(br, bi, g2, bd)
    return y3.reshape(SEQ, FOURIER_W)


NA_SUB_ROWS = 2
NA_KEY_ROWS = NA_SUB_ROWS + WIN_H
NA_SUBS_PER_STEP = 64
NA_N_SUB = ROWS // NA_SUB_ROWS
NA_MAX_KEY_START = ROWS - NA_KEY_ROWS
NA_N_ROW_OFFSETS = 2 * WIN_H - 1
NA_MASKED_BLOCK = NA_N_ROW_OFFSETS
LOG2E = math.log2(math.e)


def _na_key_row0(sub):
    return jnp.clip(sub * NA_SUB_ROWS - WIN_H // 2, 0, NA_MAX_KEY_START)


def _na_cases():
    patterns, case_of_sub = [], []
    for sub in range(NA_N_SUB):
        r0 = sub * NA_SUB_ROWS
        ks = int(np.clip(r0 - WIN_H // 2, 0, NA_MAX_KEY_START))
        pat = np.full((NA_SUB_ROWS, NA_KEY_ROWS), NA_MASKED_BLOCK, np.int32)
        for qi in range(NA_SUB_ROWS):
            r = r0 + qi
            r_start = int(np.clip(r - WIN_H // 2, 0, ROWS - WIN_H))
            for kj in range(NA_KEY_ROWS):
                if r_start <= ks + kj < r_start + WIN_H:
                    pat[qi, kj] = ks + kj - r + (WIN_H - 1)
        for c, p in enumerate(patterns):
            if np.array_equal(p, pat):
                case_of_sub.append(c)
                break
        else:
            case_of_sub.append(len(patterns))
            patterns.append(pat)
    return np.stack(patterns), np.asarray(case_of_sub, np.int32)


def _na_case_of(sub, case_of_sub):
    common = int(np.bincount(case_of_sub).argmax())
    case = jnp.int32(common)
    for s in np.nonzero(case_of_sub != common)[0]:
        case = jnp.where(sub == int(s), int(case_of_sub[s]), case)
    return case


def _na_col_blocks(rpb):
    qc = np.arange(GRID_W)[:, None]
    kc = np.arange(GRID_W)[None, :]
    col_start = np.clip(qc - WIN_W // 2, 0, GRID_W - WIN_W)
    col_valid = (kc >= col_start) & (kc < col_start + WIN_W)
    dc = kc - qc + (WIN_W - 1)
    sel = (np.arange(2 * WIN_W - 1)[:, None, None] == dc[None]) & col_valid[None]
    sel = jnp.asarray(sel.astype(np.float32))
    t = jnp.einsum("hrd,dqk->hrqk", rpb.astype(F32), sel, precision=lax.Precision.HIGHEST)
    t = jnp.where(col_valid[None, None], t * LOG2E, NEG_BIG)
    masked = jnp.full((N_NA_HEADS, 1, GRID_W, GRID_W), NEG_BIG, F32)
    t = jnp.concatenate([t, masked], axis=1)
    t = jnp.swapaxes(t, -1, -2)
    return jnp.concatenate([t, t], axis=-1)


def _na_kernel(q_ref, k_ref, v_ref, qg_ref, kg_ref, cb_ref, o_ref, kn_ref, bias_ref):
    step = pl.program_id(1)
    nq = NA_SUB_ROWS * GRID_W
    nk = NA_KEY_ROWS * GRID_W
    row_off, case_of_sub = _na_cases()

    @pl.when(step == 0)
    def _():
        chunk = 1024

        def body(c, carry):
            rows = pl.ds(pl.multiple_of(c * chunk, chunk), chunk)
            kn_ref[rows, :] = _rms_rows(k_ref[rows, :].astype(F32), kg_ref[...]).astype(BF16)
            return carry
        lax.fori_loop(0, SEQ // chunk, body, 0, unroll=2)

        for case in range(row_off.shape[0]):
            for qi in range(NA_SUB_ROWS):
                for kj in range(NA_KEY_ROWS):
                    lanes = slice(qi * GRID_W, (qi + 1) * GRID_W)
                    src = slice((qi % 2) * GRID_W, (qi % 2 + 1) * GRID_W)
                    bias_ref[case, kj * GRID_W:(kj + 1) * GRID_W, lanes] = (
                        cb_ref[0, int(row_off[case, qi, kj]), :, src])

    scale = HEAD_DIM ** -0.5 * LOG2E
    for sb in range(NA_SUBS_PER_STEP):
        sub = step * NA_SUBS_PER_STEP + sb
        case = _na_case_of(sub, case_of_sub)
        q = q_ref[sb * nq:(sb + 1) * nq, :].astype(F32)
        qn = (_rms_rows(q, qg_ref[...]) * scale).astype(BF16)
        kstart = pl.multiple_of(_na_key_row0(sub) * GRID_W, 2 * GRID_W)
        kw = kn_ref[pl.ds(kstart, nk), :]
        vw = v_ref[pl.ds(kstart, nk), :]
        st = lax.dot_general(kw, qn, (((1,), (1,)), ((), ())), preferred_element_type=F32)
        st = st + bias_ref[case]
        m = jnp.max(st, axis=0, keepdims=True)
        pt = jnp.exp2(st - m)
        l = jnp.sum(pt, axis=0, keepdims=True)
        pt = (pt * (1.0 / l)).astype(BF16)
        o = lax.dot_general(pt, vw, (((0,), (0,)), ((), ())), preferred_element_type=F32)
        o_ref[sb * nq:(sb + 1) * nq, :] = o.astype(o_ref.dtype)


def _na_attention(proj, q_g, k_g, col_blocks):
    nq = NA_SUB_ROWS * GRID_W
    nk = NA_KEY_ROWS * GRID_W
    tq = NA_SUBS_PER_STEP * nq
    n_blk = NA_N_ROW_OFFSETS + 1
    qb, kb_, vb = Q_NA_OFF // HEAD_DIM, K_NA_OFF // HEAD_DIM, V_NA_OFF // HEAD_DIM
    return pl.pallas_call(
        _na_kernel,
        grid=(N_NA_HEADS, SEQ // tq),
        in_specs=[pl.BlockSpec((tq, HEAD_DIM), lambda h, i: (i, qb + h)),
                  pl.BlockSpec((SEQ, HEAD_DIM), lambda h, i: (0, kb_ + h)),
                  pl.BlockSpec((SEQ, HEAD_DIM), lambda h, i: (0, vb + h)),
                  pl.BlockSpec((1, HEAD_DIM), lambda h, i: (0, 0)),
                  pl.BlockSpec((1, HEAD_DIM), lambda h, i: (0, 0)),
                  pl.BlockSpec((1, n_blk, GRID_W, 2 * GRID_W), lambda h, i: (h, 0, 0, 0))],
        out_specs=pl.BlockSpec((tq, HEAD_DIM), lambda h, i: (i, h)),
        out_shape=jax.ShapeDtypeStruct((SEQ, NA_W), BF16),
        scratch_shapes=[pltpu.VMEM((SEQ, HEAD_DIM), BF16),
                        pltpu.VMEM((_na_cases()[0].shape[0], nk, nq), F32)],
        compiler_params=_params(("parallel", "arbitrary"), 48),
        name="na_attention",
    )(proj, proj, proj, q_g.reshape(1, HEAD_DIM), k_g.reshape(1, HEAD_DIM), col_blocks)


def _mem_attn_kernel(q_ref, kv_ref, qg_ref, kg_ref, o_ref):
    scale = HEAD_DIM ** -0.5 * LOG2E
    for h in range(N_MEM_HEADS):
        cols = slice(h * HEAD_DIM, (h + 1) * HEAD_DIM)
        qn = (_rms_rows(q_ref[:, cols].astype(F32), qg_ref[...]) * scale).astype(BF16)
        kn = _rms_rows(kv_ref[:, cols].astype(F32), kg_ref[...]).astype(BF16)
        v = kv_ref[:, MEM_W + h * HEAD_DIM:MEM_W + (h + 1) * HEAD_DIM].astype(BF16)
        s = lax.dot_general(qn, kn, (((1,), (1,)), ((), ())), preferred_element_type=F32)
        m = jnp.max(s, axis=-1, keepdims=True)
        p = jnp.exp2(s - m)
        l = jnp.sum(p, axis=-1, keepdims=True)
        o = jnp.dot(p.astype(BF16), v, preferred_element_type=F32) / l
        o_ref[:, cols] = o.astype(o_ref.dtype)


def _mem_attention(proj, kv, q_g, k_g):
    tm = 512
    return pl.pallas_call(
        _mem_attn_kernel,
        grid=(SEQ // tm,),
        in_specs=[pl.BlockSpec((tm, MEM_W), lambda i: (i, Q_MEM_OFF // MEM_W)),
                  pl.BlockSpec((N_MEM, 2 * MEM_W), lambda i: (0, 0)),
                  pl.BlockSpec((1, HEAD_DIM), lambda i: (0, 0)),
                  pl.BlockSpec((1, HEAD_DIM), lambda i: (0, 0))],
        out_specs=pl.BlockSpec((tm, MEM_W), lambda i: (i, 0)),
        out_shape=jax.ShapeDtypeStruct((SEQ, MEM_W), BF16),
        compiler_params=_params(("parallel",), 32),
        name="mem_attention",
    )(proj, kv, q_g.reshape(1, HEAD_DIM), k_g.reshape(1, HEAD_DIM))


def _merge_kernel(fm_ref, na_ref, mo_ref, g0_ref, g1_ref, g2_ref, wf_ref, wn_ref, wm_ref, o_ref,
                  *, col_chunk):
    for c in range(D_MODEL // col_chunk):
        cols = slice(c * col_chunk, (c + 1) * col_chunk)
        o_f = jnp.dot(fm_ref[...], wf_ref[:, cols], preferred_element_type=F32)
        o_n = jnp.dot(na_ref[...], wn_ref[:, cols], preferred_element_type=F32)
        o_m = jnp.dot(mo_ref[...], wm_ref[:, cols], preferred_element_type=F32)
        acc = jax.nn.sigmoid(g0_ref[:, cols].astype(F32)) * o_f
        acc = acc + jax.nn.sigmoid(g1_ref[:, cols].astype(F32)) * o_n
        acc = acc + jax.nn.sigmoid(g2_ref[:, cols].astype(F32)) * o_m
        o_ref[:, cols] = acc.astype(o_ref.dtype)


def _merge(fm, o_na, o_mem, proj, wf, wn, wm):
    tm = 512
    gb = GATE_OFF // D_MODEL
    return pl.pallas_call(
        functools.partial(_merge_kernel, col_chunk=512),
        grid=(SEQ // tm,),
        in_specs=[pl.BlockSpec((tm, FOURIER_W), lambda i: (i, 0)),
                  pl.BlockSpec((tm, NA_W), lambda i: (i, 0)),
                  pl.BlockSpec((tm, MEM_W), lambda i: (i, 0)),
                  pl.BlockSpec((tm, D_MODEL), lambda i: (i, gb)),
                  pl.BlockSpec((tm, D_MODEL), lambda i: (i, gb + 1)),
                  pl.BlockSpec((tm, D_MODEL), lambda i: (i, gb + 2)),
                  pl.BlockSpec((FOURIER_W, D_MODEL), lambda i: (0, 0)),
                  pl.BlockSpec((NA_W, D_MODEL), lambda i: (0, 0)),
                  pl.BlockSpec((MEM_W, D_MODEL), lambda i: (0, 0))],
        out_specs=pl.BlockSpec((tm, D_MODEL), lambda i: (i, 0)),
        out_shape=jax.ShapeDtypeStruct((SEQ, D_MODEL), BF16),
        compiler_params=_params(("parallel",), 48),
        name="gated_merge",
    )(fm, o_na, o_mem, proj, proj, proj, wf, wn, wm)


ROUTER_LANES = 128


def _route_top2(h, wr_ref):
    h_hi = h.astype(BF16)
    h_lo = (h - h_hi.astype(F32)).astype(BF16)
    r_hi = jnp.dot(h_hi, wr_ref[...], preferred_element_type=F32)
    r_lo = jnp.dot(h_lo, wr_ref[:, :ROUTER_LANES], preferred_element_type=F32)
    logits = r_hi[:, :ROUTER_LANES] + r_hi[:, ROUTER_LANES:] + r_lo
    lane = lax.broadcasted_iota(jnp.int32, logits.shape, 1).astype(F32)
    logits = jnp.where(lane < N_EXPERTS, logits, -jnp.inf)
    m1 = jnp.max(logits, axis=-1, keepdims=True)
    i1 = jnp.min(jnp.where(logits == m1, lane, float(ROUTER_LANES)), axis=-1, keepdims=True)
    rest = jnp.where(lane == i1, -jnp.inf, logits)
    m2 = jnp.max(rest, axis=-1, keepdims=True)
    i2 = jnp.min(jnp.where(rest == m2, lane, float(ROUTER_LANES)), axis=-1, keepdims=True)
    e21 = jnp.exp(m2 - m1)
    g1 = 1.0 / (1.0 + e21)
    g2 = e21 * g1
    return jnp.where(lane == 0.0, i1, i2).astype(jnp.int32), jnp.where(lane == 0.0, g1, g2)


def _oproj_kernel(a_ref, w_ref, x_ref, g_ref, *rest, row_chunk, route):
    if route:
        wr_ref, xo_ref, h_ref, idx_ref, gate_ref = rest
    else:
        xo_ref, h_ref = rest
    half = xo_ref.shape[0] // 2
    for r in range(2):
        xo_ref[r * half:(r + 1) * half, :] = (
            jnp.dot(a_ref[r * half:(r + 1) * half, :], w_ref[...], preferred_element_type=F32)
            + x_ref[r * half:(r + 1) * half, :])
        for c in range(half // row_chunk):
            rows = slice(r * half + c * row_chunk, r * half + (c + 1) * row_chunk)
            h = _rms_rows(xo_ref[rows, :], g_ref[...])
            if route:
                h_ref[rows] = _rows_to_tiles(_pack_halves(h))
                idx_ref[rows, :], gate_ref[rows, :] = _route_top2(h, wr_ref)
            else:
                h_ref[rows, :] = h.astype(h_ref.dtype)


def _out_proj(merged, w_o, x, g, w_router=None):
    tm = 512
    route = w_router is not None
    in_specs = [pl.BlockSpec((tm, D_MODEL), lambda i: (i, 0)),
                pl.BlockSpec((D_MODEL, D_MODEL), lambda i: (0, 0)),
                pl.BlockSpec((tm, D_MODEL), lambda i: (i, 0)),
                pl.BlockSpec((1, D_MODEL), lambda i: (0, 0))]
    args = [merged, w_o, x, g.reshape(1, D_MODEL)]
    out_specs = [pl.BlockSpec((tm, D_MODEL), lambda i: (i, 0))]
    out_shape = [jax.ShapeDtypeStruct((SEQ, D_MODEL), F32)]
    if route:
        w_pad = jnp.zeros((D_MODEL, ROUTER_LANES), F32).at[:, :N_EXPERTS].set(w_router.astype(F32))
        w_hi = w_pad.astype(BF16)
        w_lo = (w_pad - w_hi.astype(F32)).astype(BF16)
        in_specs.append(pl.BlockSpec((D_MODEL, 2 * ROUTER_LANES), lambda i: (0, 0)))
        args.append(jnp.concatenate([w_hi, w_lo], axis=1))
        out_specs += [pl.BlockSpec((tm, ROW_SUBLANES, LANES), lambda i: (i, 0, 0)),
                      pl.BlockSpec((tm, ROUTER_LANES), lambda i: (i, 0)),
                      pl.BlockSpec((tm, ROUTER_LANES), lambda i: (i, 0))]
        out_shape += [jax.ShapeDtypeStruct((SEQ, ROW_SUBLANES, LANES), jnp.uint32),
                      jax.ShapeDtypeStruct((SEQ, ROUTER_LANES), jnp.int32),
                      jax.ShapeDtypeStruct((SEQ, ROUTER_LANES), F32)]
    else:
        out_specs.append(pl.BlockSpec((tm, D_MODEL), lambda i: (i, 0)))
        out_shape.append(jax.ShapeDtypeStruct((SEQ, D_MODEL), BF16))
    return pl.pallas_call(
        functools.partial(_oproj_kernel, row_chunk=128, route=route),
        grid=(SEQ // tm,),
        in_specs=in_specs,
        out_specs=out_specs,
        out_shape=out_shape,
        compiler_params=_params(("parallel",), 48),
        name="out_proj",
    )(*args)


def _swiglu_step(x_ref, wg_ref, wu_ref, wd_ref, acc_ref, rows=None):
    x = x_ref[:rows, :]
    g = jnp.dot(x, wg_ref[0].astype(BF16), preferred_element_type=F32)
    u = jnp.dot(x, wu_ref[0].astype(BF16), preferred_element_type=F32)
    a = (g * _sigmoid(g) * u).astype(BF16)
    acc_ref[:rows, :] += jnp.dot(a, wd_ref[0].astype(BF16), preferred_element_type=F32)


def _dense_ffn_kernel(x_ref, wg_ref, wu_ref, wd_ref, resid_ref, o_ref):
    @pl.when(pl.program_id(1) == 0)
    def _():
        o_ref[...] = resid_ref[...]

    _swiglu_step(x_ref, wg_ref, wu_ref, wd_ref, o_ref)


def _dense_ffn(h, wg, wu, wd, layer, resid):
    tm, tf = 1024, 512
    m, d = h.shape
    nf = D_FF // tf
    return pl.pallas_call(
        _dense_ffn_kernel,
        grid=(m // tm, nf),
        in_specs=[pl.BlockSpec((tm, d), lambda i, f: (i, 0)),
                  pl.BlockSpec((1, d, tf), lambda i, f: (layer, 0, f)),
                  pl.BlockSpec((1, d, tf), lambda i, f: (layer, 0, f)),
                  pl.BlockSpec((1, tf, d), lambda i, f: (layer, f, 0)),
                  pl.BlockSpec((tm, d), lambda i, f: (i, 0))],
        out_specs=pl.BlockSpec((tm, d), lambda i, f: (i, 0)),
        out_shape=jax.ShapeDtypeStruct((m, d), F32),
        compiler_params=_params(("parallel", "arbitrary"), 60),
        name="dense_ffn",
    )(h, wg, wu, wd, resid)


MOE_TF = 512
MOE_NF = D_FF // MOE_TF
MOE_ROWS_PER_STEP = 96
MOE_TM = MOE_NF * MOE_ROWS_PER_STEP
MOE_ROW_CHUNK = 96
MOE_N_TILES = -(-SEQ * TOP_K // MOE_TM) + N_EXPERTS
MOE_THIRDS = 3
MOE_THIRD_ROWS = MOE_TM // MOE_THIRDS


def _tile_row_copy(tok_ref, h_ref, xg_ref, sems, tile, slot, r):
    row = tok_ref[tile * MOE_TM + r]
    return pltpu.make_async_copy(h_ref.at[pl.ds(row, 1)], xg_ref.at[slot, pl.ds(r, 1)],
                                 sems.at[slot])


def _wait_tile_rows(h_ref, xg_ref, sems, slot):
    pltpu.make_async_copy(h_ref.at[pl.ds(0, MOE_TM)], xg_ref.at[slot], sems.at[slot]).wait()


def _moe_expert_kernel(be_ref, nu_ref, nt_ref, tok_ref, h_ref, wg_ref, wu_ref, wd_ref, o_ref,
                       xg_ref, xb_ref, acc_ref, sems):
    i = pl.program_id(0)
    f = pl.program_id(1)
    n_used = nu_ref[0]
    active = i < n_used
    slot = i % 2
    n_chunks = MOE_TM // MOE_ROW_CHUNK

    @pl.when(jnp.logical_and(i == 0, f == 0))
    def _():
        def start_row(r, carry):
            _tile_row_copy(tok_ref, h_ref, xg_ref, sems, 0, 0, r).start()
            return carry
        lax.fori_loop(0, MOE_TM, start_row, 0, unroll=8)

    @pl.when(jnp.logical_and(active, f == 0))
    def _():
        _wait_tile_rows(h_ref, xg_ref, sems, slot)
        acc_ref[...] = jnp.zeros(acc_ref.shape, acc_ref.dtype)

        def unpack_rows(c, carry):
            rows = pl.ds(pl.multiple_of(c * MOE_ROW_CHUNK, MOE_ROW_CHUNK), MOE_ROW_CHUNK)
            lo, hi = _unpack_halves(_tiles_to_rows(xg_ref[slot, rows]))
            xb_ref[rows, :HALF_D] = lo.astype(BF16)
            xb_ref[rows, HALF_D:] = hi.astype(BF16)
            return carry
        lax.fori_loop(0, n_chunks, unpack_rows, 0)

    for thirds in range(1, MOE_THIRDS + 1):
        @pl.when(jnp.logical_and(active, nt_ref[i] == thirds))
        def _(thirds=thirds):
            for j in range(MOE_ROWS_PER_STEP):
                _tile_row_copy(tok_ref, h_ref, xg_ref, sems, i + 1, 1 - slot,
                               f * MOE_ROWS_PER_STEP + j).start()
            _swiglu_step(xb_ref, wg_ref, wu_ref, wd_ref, acc_ref, rows=thirds * MOE_THIRD_ROWS)

    @pl.when(jnp.logical_and(i == n_used, f == 0))
    def _():
        _wait_tile_rows(h_ref, xg_ref, sems, slot)

    @pl.when(f == MOE_NF - 1)
    def _():
        @pl.when(active)
        def _():
            def finish_rows(c, carry):
                rows = pl.ds(pl.multiple_of(c * MOE_ROW_CHUNK, MOE_ROW_CHUNK), MOE_ROW_CHUNK)
                o_ref[rows] = _rows_to_tiles(_pack_halves(acc_ref[rows, :]))
                return carry
            lax.fori_loop(0, n_chunks, finish_rows, 0)

        @pl.when(jnp.logical_not(active))
        def _():
            o_ref[...] = jnp.zeros(o_ref.shape, o_ref.dtype)


def _moe_experts(h_packed, tok, blk_e, n_used, n_thirds, wg, wu, wd):
    d, tf = D_MODEL, MOE_TF

    def f_idx(i, f, nu):
        return jnp.where(i < nu[0], f, MOE_NF - 1)

    def w_gu(i, f, be, nu, nt, tok):
        return (be[i], 0, f_idx(i, f, nu))

    def w_d(i, f, be, nu, nt, tok):
        return (be[i], f_idx(i, f, nu), 0)

    grid_spec = pltpu.PrefetchScalarGridSpec(
        num_scalar_prefetch=4,
        grid=(MOE_N_TILES, MOE_NF),
        in_specs=[pl.BlockSpec(memory_space=pl.ANY),
                  pl.BlockSpec((1, d, tf), w_gu),
                  pl.BlockSpec((1, d, tf), w_gu),
                  pl.BlockSpec((1, tf, d), w_d)],
        out_specs=pl.BlockSpec((MOE_TM, ROW_SUBLANES, LANES),
                               lambda i, f, be, nu, nt, tok: (i, 0, 0)),
        scratch_shapes=[pltpu.VMEM((2, MOE_TM, ROW_SUBLANES, LANES), jnp.uint32),
                        pltpu.VMEM((MOE_TM, d), BF16),
                        pltpu.VMEM((MOE_TM, d), F32),
                        pltpu.SemaphoreType.DMA((2,))],
    )
    return pl.pallas_call(
        _moe_expert_kernel,
        grid_spec=grid_spec,
        out_shape=jax.ShapeDtypeStruct((MOE_N_TILES * MOE_TM, ROW_SUBLANES, LANES), jnp.uint32),
        compiler_params=_params(("arbitrary", "arbitrary"), 60),
        name="moe_experts",
    )(blk_e, n_used, n_thirds, tok, h_packed, wg, wu, wd)


def _start_row_gather(idx_ref, idx_base, src_ref, dst_ref, sem, n_rows):
    def body(k, carry):
        for p in range(2):
            r = 2 * k + p
            row = idx_ref[idx_base + r]
            pltpu.make_async_copy(src_ref.at[pl.ds(row, 1)], dst_ref.at[pl.ds(r, 1)],
                                  sem).start(priority=p)
        return carry
    lax.fori_loop(0, n_rows // 2, body, 0, unroll=4)


def _wait_row_gather(src_ref, dst_ref, sem, n_rows):
    pltpu.make_async_copy(src_ref.at[pl.ds(0, n_rows)], dst_ref, sem).wait()


def _moe_combine_kernel(pos_ref, y_ref, x_ref, gate_ref, o_ref, buf_ref, sems, *, tokens):
    i = pl.program_id(0)
    slot = i % 2
    rows = TOP_K * tokens

    @pl.when(i == 0)
    def _():
        _start_row_gather(pos_ref, 0, y_ref, buf_ref.at[0], sems.at[0], rows)

    @pl.when(i + 1 < pl.num_programs(0))
    def _():
        _start_row_gather(pos_ref, (i + 1) * rows, y_ref, buf_ref.at[1 - slot],
                          sems.at[1 - slot], rows)

    _wait_row_gather(y_ref, buf_ref.at[slot], sems.at[slot], rows)
    g0 = gate_ref[:, 0:1]
    g1 = gate_ref[:, 1:2]
    lo0, hi0 = _unpack_halves(_tiles_to_rows(buf_ref[slot, :tokens]))
    lo1, hi1 = _unpack_halves(_tiles_to_rows(buf_ref[slot, tokens:]))
    o_ref[:, :HALF_D] = x_ref[:, :HALF_D] + (g0 * lo0 + g1 * lo1)
    o_ref[:, HALF_D:] = x_ref[:, HALF_D:] + (g0 * hi0 + g1 * hi1)


def _moe_combine(y_packed, pos, x, gate):
    tokens = 512
    n = x.shape[0]
    steps = n // tokens
    pos_tiled = pos.reshape(steps, tokens, TOP_K).transpose(0, 2, 1).reshape(-1)
    grid_spec = pltpu.PrefetchScalarGridSpec(
        num_scalar_prefetch=1,
        grid=(steps,),
        in_specs=[pl.BlockSpec(memory_space=pl.ANY),
                  pl.BlockSpec((tokens, D_MODEL), lambda i, pos_ref: (i, 0)),
                  pl.BlockSpec((tokens, ROUTER_LANES), lambda i, pos_ref: (i, 0))],
        out_specs=pl.BlockSpec((tokens, D_MODEL), lambda i, pos_ref: (i, 0)),
        scratch_shapes=[pltpu.VMEM((2, TOP_K * tokens, ROW_SUBLANES, LANES), jnp.uint32),
                        pltpu.SemaphoreType.DMA((2,))],
    )
    return pl.pallas_call(
        functools.partial(_moe_combine_kernel, tokens=tokens),
        grid_spec=grid_spec,
        out_shape=jax.ShapeDtypeStruct((n, D_MODEL), F32),
        compiler_params=_params(("arbitrary",), 40),
        name="moe_combine",
    )(pos_tiled, y_packed, x, gate)


def _slot_tokens_kernel(dest_ref, fill_ref, tok_ref):
    for e in range(N_EXPERTS + 1):
        def fill(b, carry):
            for j in range(8):
                tok_ref[b * 8 + j] = 0
            return carry
        lax.fori_loop(fill_ref[2 * e], fill_ref[2 * e + 1], fill, 0)

    def place(a, carry):
        tok_ref[dest_ref[a]] = a >> 1
        return carry
    lax.fori_loop(0, SEQ * TOP_K, place, 0, unroll=8)


def _slot_tokens(dest, fill_ranges):
    assert TOP_K == 2
    cap = MOE_N_TILES * MOE_TM
    grid_spec = pltpu.PrefetchScalarGridSpec(
        num_scalar_prefetch=2,
        grid=(1,),
        in_specs=[],
        out_specs=pl.BlockSpec(memory_space=pltpu.SMEM),
    )
    return pl.pallas_call(
        _slot_tokens_kernel,
        grid_spec=grid_spec,
        out_shape=jax.ShapeDtypeStruct((cap,), jnp.int32),
        name="moe_slot_tokens",
    )(dest, fill_ranges)


def _moe(x_new, h_packed, idx, gate, wg, wu, wd):
    n = SEQ
    cap = MOE_N_TILES * MOE_TM
    e_flat = idx[:, :TOP_K].reshape(-1)
    onehot = (e_flat[:, None] == jnp.arange(N_EXPERTS, dtype=jnp.int32)[None, :]).astype(jnp.int32)
    csum = jnp.cumsum(onehot, axis=0)
    counts = csum[-1]
    rank = jnp.sum((csum - onehot) * onehot, axis=1)
    padded = (counts + MOE_TM - 1) // MOE_TM * MOE_TM
    pad_ends = jnp.cumsum(padded)
    pad_starts = pad_ends - padded
    dest = jnp.sum(onehot * pad_starts[None, :], axis=1) + rank
    fill_lo = jnp.concatenate([pad_starts + counts, pad_ends[-1:]]) // 8
    fill_hi = (jnp.concatenate([pad_ends, jnp.full((1,), cap, pad_ends.dtype)]) + 7) // 8
    tok = _slot_tokens(dest.astype(jnp.int32),
                       jnp.stack([fill_lo, fill_hi], axis=1).reshape(-1).astype(jnp.int32))
    n_used = (pad_ends[-1] // MOE_TM).astype(jnp.int32).reshape(1)
    tile_start = jnp.arange(MOE_N_TILES, dtype=jnp.int32) * MOE_TM
    tile_start = jnp.minimum(tile_start, pad_ends[-1] - 1)
    blk_e = jnp.sum((tile_start[:, None] >= pad_ends[None, :]).astype(jnp.int32), axis=1)
    blk_e = jnp.clip(blk_e, 0, N_EXPERTS - 1)
    real_rows = counts[blk_e] - (tile_start - pad_starts[blk_e])
    n_thirds = jnp.clip((real_rows + MOE_THIRD_ROWS - 1) // MOE_THIRD_ROWS, 1, MOE_THIRDS)
    yb = _moe_experts(h_packed, tok, blk_e, n_used, n_thirds.astype(jnp.int32), wg, wu, wd)
    return _moe_combine(yb, dest.reshape(n, TOP_K), x_new, gate)


def kernel(x, mem, ln_mix_g, w_in, na_q_g, na_k_g, na_rpb, mem_ln_g, w_mem_kv, mem_q_g, mem_k_g,
           w_fourier_out, w_na_out, w_mem_out, w_o, ln_ffn_g, ffn_w_gate, ffn_w_up, ffn_w_down,
           moe_router, moe_w_gate, moe_w_up, moe_w_down):
    assert x.shape == (1, SEQ, D_MODEL) and mem.shape == (1, N_MEM, D_MODEL)
    xs = x.reshape(SEQ, D_MODEL)
    mem2 = mem.reshape(N_MEM, D_MODEL)
    tables = _dft_tables()
    w_in_bf16 = w_in.astype(BF16)
    for l in range(DEPTH):
        proj = _norm_mm(xs, ln_mix_g[l], w_in_bf16, l, tm=1024, tn=2048, out_dtype=BF16,
                        vmem_mib=58, name="in_proj")
        fm = _fourier_mix(proj, tables)
        o_na = _na_attention(proj, na_q_g[l], na_k_g[l], _na_col_blocks(na_rpb[l]))
        kv = _norm_mm(mem2, mem_ln_g[l], w_mem_kv, l, tm=N_MEM, tn=2 * MEM_W, out_dtype=F32,
                      vmem_mib=40, name="mem_kv_proj")
        o_mem = _mem_attention(proj, kv, mem_q_g[l], mem_k_g[l])
        merged = _merge(fm, o_na, o_mem, proj, w_fourier_out[l].astype(BF16),
                        w_na_out[l].astype(BF16), w_mem_out[l].astype(BF16))
        i = l // 2
        if l % 2 == 0:
            x_new, h2 = _out_proj(merged, w_o[l].astype(BF16), xs, ln_ffn_g[l])
            xs = _dense_ffn(h2, ffn_w_gate.astype(BF16), ffn_w_up.astype(BF16),
                            ffn_w_down.astype(BF16), i, x_new)
        else:
            x_new, h2, idx, gate = _out_proj(merged, w_o[l].astype(BF16), xs, ln_ffn_g[l],
                                             moe_router[i])
            xs = _moe(x_new, h2, idx, gate, moe_w_gate[i], moe_w_up[i], moe_w_down[i])
    return xs.reshape(1, SEQ, D_MODEL)
```

```python
import functools
import math

import numpy as np
import jax
import jax.numpy as jnp
from jax import lax
from jax.experimental import pallas as pl
from jax.experimental.pallas import tpu as pltpu

F32 = jnp.float32
BF16 = jnp.bfloat16

D_MODEL = 2048
SEQ = 16384
DEPTH = 2
GRID_W = 64
ROWS = SEQ // GRID_W
HEAD_DIM = 128
N_FOURIER_GROUPS = 4
FOURIER_W = N_FOURIER_GROUPS * HEAD_DIM
N_NA_HEADS = 8
NA_W = N_NA_HEADS * HEAD_DIM
WIN_H = 8
WIN_W = 16
N_MEM = 256
N_MEM_HEADS = 4
MEM_W = N_MEM_HEADS * HEAD_DIM
IN_W = FOURIER_W + 3 * NA_W + MEM_W + 3 * D_MODEL
D_FF = 5632
N_EXPERTS = 8
TOP_K = 2
EPS = 1e-6
HALF_D = D_MODEL // 2

Q_NA_OFF = FOURIER_W
K_NA_OFF = Q_NA_OFF + NA_W
V_NA_OFF = K_NA_OFF + NA_W
Q_MEM_OFF = V_NA_OFF + NA_W
GATE_OFF = Q_MEM_OFF + MEM_W

MIB = 1024 * 1024
NEG_BIG = -1e30

FFT_A = 128
FFT_B = SEQ // FFT_A


def _params(semantics, vmem_mib):
    return pltpu.CompilerParams(dimension_semantics=semantics,
                                vmem_limit_bytes=int(vmem_mib * MIB))


def _rms_rows(x, gain):
    ms = jnp.mean(x * x, axis=-1, keepdims=True)
    return x * lax.rsqrt(ms + EPS) * gain


def _sigmoid(x):
    return 0.5 * jnp.tanh(0.5 * x) + 0.5


def _bf16_bits(v):
    return pltpu.bitcast(v.astype(BF16).astype(F32), jnp.uint32)


def _pack_halves(v):
    half = v.shape[1] // 2
    lo = lax.shift_right_logical(_bf16_bits(v[:, :half]), jnp.uint32(16))
    return lo | _bf16_bits(v[:, half:])


def _unpack_halves(w):
    lo = pltpu.bitcast(lax.shift_left(w, jnp.uint32(16)), F32)
    hi = pltpu.bitcast(w & jnp.uint32(0xFFFF0000), F32)
    return lo, hi


ROW_SUBLANES = 8
LANES = 128
assert HALF_D == ROW_SUBLANES * LANES


def _rows_to_tiles(w):
    return pltpu.einshape("a(bc)->abc", w, b=ROW_SUBLANES)


def _tiles_to_rows(t):
    return pltpu.einshape("abc->a(bc)", t)


def _norm_mm_kernel(x_ref, g_ref, w_ref, o_ref, h_ref, *, row_chunk):
    @pl.when(pl.program_id(1) == 0)
    def _():
        def body(c, carry):
            rows = pl.ds(pl.multiple_of(c * row_chunk, row_chunk), row_chunk)
            h_ref[rows, :] = _rms_rows(x_ref[rows, :].astype(F32), g_ref[...]).astype(BF16)
            return carry
        lax.fori_loop(0, h_ref.shape[0] // row_chunk, body, 0, unroll=2)

    o_ref[...] = jnp.dot(h_ref[...], w_ref[0].astype(BF16),
                         preferred_element_type=F32).astype(o_ref.dtype)


def _norm_mm(x, g, w, layer, *, tm, tn, out_dtype, vmem_mib, name):
    m, k = x.shape
    n = w.shape[2]
    row_chunk = min(tm, 128)
    return pl.pallas_call(
        functools.partial(_norm_mm_kernel, row_chunk=row_chunk),
        grid=(m // tm, n // tn),
        in_specs=[pl.BlockSpec((tm, k), lambda i, j: (i, 0)),
                  pl.BlockSpec((1, k), lambda i, j: (0, 0)),
                  pl.BlockSpec((1, k, tn), lambda i, j: (layer, 0, j))],
        out_specs=pl.BlockSpec((tm, tn), lambda i, j: (i, j)),
        out_shape=jax.ShapeDtypeStruct((m, n), out_dtype),
        scratch_shapes=[pltpu.VMEM((tm, k), BF16)],
        compiler_params=_params(("parallel", "arbitrary"), vmem_mib),
        name=name,
    )(x, g.reshape(1, k), w)


def _dft_tables():
    a = np.arange(FFT_A)
    ang1 = 2.0 * np.pi * np.outer(a, a) / FFT_A
    w1 = np.concatenate([np.cos(ang1), -np.sin(ang1)], axis=0)
    k1 = np.arange(FFT_A)[:, None, None]
    k2 = np.arange(FFT_B)[None, :, None]
    s2 = np.arange(FFT_B)[None, None, :]
    ang2 = 2.0 * np.pi * ((s2 * (k1 + FFT_A * k2)) % SEQ) / SEQ
    g2 = np.concatenate([np.cos(ang2), np.sin(ang2)], axis=2)
    c = np.arange(HEAD_DIM)
    angc = 2.0 * np.pi * np.outer(c, c) / HEAD_DIM
    eye = np.eye(N_FOURIER_GROUPS)
    bd = np.concatenate([np.kron(eye, np.cos(angc)), np.kron(eye, np.sin(angc))], axis=0)
    return (jnp.asarray(w1, dtype=BF16), jnp.asarray(g2, dtype=BF16),
            jnp.asarray(bd, dtype=BF16))


FFT_BLK = 16


def _dft1_kernel(w_ref, x_ref, br_ref, bi_ref):
    xt = pltpu.einshape("abc->bac", x_ref[...])
    for j in range(FFT_BLK):
        r = jnp.dot(w_ref[...], xt[j], preferred_element_type=F32)
        br_ref[j] = r[:FFT_A].astype(BF16)
        bi_ref[j] = r[FFT_A:].astype(BF16)


def _dft2_kernel(br_ref, bi_ref, g_ref, bd_ref, o_ref, *, norm):
    brt = pltpu.einshape("abc->bac", br_ref[...])
    bit = pltpu.einshape("abc->bac", bi_ref[...])
    ys = []
    for i in range(FFT_BLK):
        br = brt[i]
        bi = bit[i]
        g = g_ref[i]
        zr = jnp.dot(g, jnp.concatenate([br, bi], axis=0), preferred_element_type=F32)
        zi = jnp.dot(g, jnp.concatenate([bi, -br], axis=0), preferred_element_type=F32)
        z = jnp.concatenate([zr, zi], axis=1).astype(BF16)
        y = jnp.dot(z, bd_ref[...], preferred_element_type=F32) * norm
        ys.append(y.astype(o_ref.dtype))
    o_ref[...] = pltpu.einshape("abc->bac", jnp.stack(ys, axis=0))


def _fourier_mix(proj, tables):
    w1, g2, bd = tables
    x3 = proj.reshape(FFT_A, FFT_B, IN_W)
    blk = (FFT_A, FFT_BLK, FOURIER_W)
    br, bi = pl.pallas_call(
        _dft1_kernel,
        grid=(FFT_B // FFT_BLK,),
        in_specs=[pl.BlockSpec((2 * FFT_A, FFT_A), lambda j: (0, 0)),
                  pl.BlockSpec(blk, lambda j: (0, j, 0))],
        out_specs=[pl.BlockSpec((FFT_BLK, FFT_A, FOURIER_W), lambda j: (j, 0, 0)),
                   pl.BlockSpec((FFT_BLK, FFT_A, FOURIER_W), lambda j: (j, 0, 0))],
        out_shape=[jax.ShapeDtypeStruct((FFT_B, FFT_A, FOURIER_W), BF16)] * 2,
        compiler_params=_params(("parallel",), 40),
        name="fourier_stage1",
    )(w1, x3)
    norm = 1.0 / math.sqrt(SEQ * HEAD_DIM)
    y3 = pl.pallas_call(
        functools.partial(_dft2_kernel, norm=norm),
        grid=(FFT_A // FFT_BLK,),
        in_specs=[pl.BlockSpec((FFT_B, FFT_BLK, FOURIER_W), lambda j: (0, j, 0)),
                  pl.BlockSpec((FFT_B, FFT_BLK, FOURIER_W), lambda j: (0, j, 0)),
                  pl.BlockSpec((FFT_BLK, FFT_B, 2 * FFT_B), lambda j: (j, 0, 0)),
                  pl.BlockSpec((2 * FOURIER_W, FOURIER_W), lambda j: (0, 0))],
        out_specs=pl.BlockSpec((FFT_B, FFT_BLK, FOURIER_W), lambda j: (0, j, 0)),
        out_shape=jax.ShapeDtypeStruct((FFT_B, FFT_A, FOURIER_W), BF16),
        compiler_params=_params(("parallel",), 40),
        name="fourier_stage2",
    )(br, bi, g2, bd)
    return y3.reshape(SEQ, FOURIER_W)


NA_SUB_ROWS = 2
NA_KEY_ROWS = NA_SUB_ROWS + WIN_H
NA_SUBS_PER_STEP = 64
NA_N_SUB = ROWS // NA_SUB_ROWS
NA_MAX_KEY_START = ROWS - NA_KEY_ROWS
NA_N_ROW_OFFSETS = 2 * WIN_H - 1
NA_MASKED_BLOCK = NA_N_ROW_OFFSETS
LOG2E = math.log2(math.e)


def _na_key_row0(sub):
    return jnp.clip(sub * NA_SUB_ROWS - WIN_H // 2, 0, NA_MAX_KEY_START)


def _na_cases():
    patterns, case_of_sub = [], []
    for sub in range(NA_N_SUB):
        r0 = sub * NA_SUB_ROWS
        ks = int(np.clip(r0 - WIN_H // 2, 0, NA_MAX_KEY_START))
        pat = np.full((NA_SUB_ROWS, NA_KEY_ROWS), NA_MASKED_BLOCK, np.int32)
        for qi in range(NA_SUB_ROWS):
            r = r0 + qi
            r_start = int(np.clip(r - WIN_H // 2, 0, ROWS - WIN_H))
            for kj in range(NA_KEY_ROWS):
                if r_start <= ks + kj < r_start + WIN_H:
                    pat[qi, kj] = ks + kj - r + (WIN_H - 1)
        for c, p in enumerate(patterns):
            if np.array_equal(p, pat):
                case_of_sub.append(c)
                break
        else:
            case_of_sub.append(len(patterns))
            patterns.append(pat)
    return np.stack(patterns), np.asarray(case_of_sub, np.int32)


def _na_case_of(sub, case_of_sub):
    common = int(np.bincount(case_of_sub).argmax())
    case = jnp.int32(common)
    for s in np.nonzero(case_of_sub != common)[0]:
        case = jnp.where(sub == int(s), int(case_of_sub[s]), case)
    return case


def _na_col_blocks(rpb):
    qc = np.arange(GRID_W)[:, None]
    kc = np.arange(GRID_W)[None, :]
    col_start = np.clip(qc - WIN_W // 2, 0, GRID_W - WIN_W)
    col_valid = (kc >= col_start) & (kc < col_start + WIN_W)
    dc = kc - qc + (WIN_W - 1)
    sel = (np.arange(2 * WIN_W - 1)[:, None, None] == dc[None]) & col_valid[None]
    sel = jnp.asarray(sel.astype(np.float32))
    t = jnp.einsum("hrd,dqk->hrqk", rpb.astype(F32), sel, precision=lax.Precision.HIGHEST)
    t = jnp.where(col_valid[None, None], t * LOG2E, NEG_BIG)
    masked = jnp.full((N_NA_HEADS, 1, GRID_W, GRID_W), NEG_BIG, F32)
    t = jnp.concatenate([t, masked], axis=1)
    t = jnp.swapaxes(t, -1, -2)
    return jnp.concatenate([t, t], axis=-1)


def _na_kernel(q_ref, k_ref, v_ref, qg_ref, kg_ref, cb_ref, o_ref, kn_ref, bias_ref):
    step = pl.program_id(1)
    nq = NA_SUB_ROWS * GRID_W
    nk = NA_KEY_ROWS * GRID_W
    row_off, case_of_sub = _na_cases()

    @pl.when(step == 0)
    def _():
        chunk = 1024

        def body(c, carry):
            rows = pl.ds(pl.multiple_of(c * chunk, chunk), chunk)
            kn_ref[rows, :] = _rms_rows(k_ref[rows, :].astype(F32), kg_ref[...]).astype(BF16)
            return carry
        lax.fori_loop(0, SEQ // chunk, body, 0, unroll=2)

        for case in range(row_off.shape[0]):
            for qi in range(NA_SUB_ROWS):
                for kj in range(NA_KEY_ROWS):
                    lanes = slice(qi * GRID_W, (qi + 1) * GRID_W)
                    src = slice((qi % 2) * GRID_W, (qi % 2 + 1) * GRID_W)
                    bias_ref[case, kj * GRID_W:(kj + 1) * GRID_W, lanes] = (
                        cb_ref[0, int(row_off[case, qi, kj]), :, src])

    scale = HEAD_DIM ** -0.5 * LOG2E
    for sb in range(NA_SUBS_PER_STEP):
        sub = step * NA_SUBS_PER_STEP + sb
        case = _na_case_of(sub, case_of_sub)
        q = q_ref[sb * nq:(sb + 1) * nq, :].astype(F32)
        qn = (_rms_rows(q, qg_ref[...]) * scale).astype(BF16)
        kstart = pl.multiple_of(_na_key_row0(sub) * GRID_W, 2 * GRID_W)
        kw = kn_ref[pl.ds(kstart, nk), :]
        vw = v_ref[pl.ds(kstart, nk), :]
        st = lax.dot_general(kw, qn, (((1,), (1,)), ((), ())), preferred_element_type=F32)
        st = st + bias_ref[case]
        m = jnp.max(st, axis=0, keepdims=True)
        pt = jnp.exp2(st - m)
        l = jnp.sum(pt, axis=0, keepdims=True)
        pt = (pt * (1.0 / l)).astype(BF16)
        o = lax.dot_general(pt, vw, (((0,), (0,)), ((), ())), preferred_element_type=F32)
        o_ref[sb * nq:(sb + 1) * nq, :] = o.astype(o_ref.dtype)


def _na_attention(proj, q_g, k_g, col_blocks):
    nq = NA_SUB_ROWS * GRID_W
    nk = NA_KEY_ROWS * GRID_W
    tq = NA_SUBS_PER_STEP * nq
    n_blk = NA_N_ROW_OFFSETS + 1
    qb, kb_, vb = Q_NA_OFF // HEAD_DIM, K_NA_OFF // HEAD_DIM, V_NA_OFF // HEAD_DIM
    return pl.pallas_call(
        _na_kernel,
        grid=(N_NA_HEADS, SEQ // tq),
        in_specs=[pl.BlockSpec((tq, HEAD_DIM), lambda h, i: (i, qb + h)),
                  pl.BlockSpec((SEQ, HEAD_DIM), lambda h, i: (0, kb_ + h)),
                  pl.BlockSpec((SEQ, HEAD_DIM), lambda h, i: (0, vb + h)),
                  pl.BlockSpec((1, HEAD_DIM), lambda h, i: (0, 0)),
                  pl.BlockSpec((1, HEAD_DIM), lambda h, i: (0, 0)),
                  pl.BlockSpec((1, n_blk, GRID_W, 2 * GRID_W), lambda h, i: (h, 0, 0, 0))],
        out_specs=pl.BlockSpec((tq, HEAD_DIM), lambda h, i: (i, h)),
        out_shape=jax.ShapeDtypeStruct((SEQ, NA_W), BF16),
        scratch_shapes=[pltpu.VMEM((SEQ, HEAD_DIM), BF16),
                        pltpu.VMEM((_na_cases()[0].shape[0], nk, nq), F32)],
        compiler_params=_params(("parallel", "arbitrary"), 48),
        name="na_attention",
    )(proj, proj, proj, q_g.reshape(1, HEAD_DIM), k_g.reshape(1, HEAD_DIM), col_blocks)


def _mem_attn_kernel(q_ref, kv_ref, qg_ref, kg_ref, o_ref):
    scale = HEAD_DIM ** -0.5 * LOG2E
    for h in range(N_MEM_HEADS):
        cols = slice(h * HEAD_DIM, (h + 1) * HEAD_DIM)
        qn = (_rms_rows(q_ref[:, cols].astype(F32), qg_ref[...]) * scale).astype(BF16)
        kn = _rms_rows(kv_ref[:, cols].astype(F32), kg_ref[...]).astype(BF16)
        v = kv_ref[:, MEM_W + h * HEAD_DIM:MEM_W + (h + 1) * HEAD_DIM].astype(BF16)
        s = lax.dot_general(qn, kn, (((1,), (1,)), ((), ())), preferred_element_type=F32)
        m = jnp.max(s, axis=-1, keepdims=True)
        p = jnp.exp2(s - m)
        l = jnp.sum(p, axis=-1, keepdims=True)
        o = jnp.dot(p.astype(BF16), v, preferred_element_type=F32) / l
        o_ref[:, cols] = o.astype(o_ref.dtype)


def _mem_attention(proj, kv, q_g, k_g):
    tm = 512
    return pl.pallas_call(
        _mem_attn_kernel,
        grid=(SEQ // tm,),
        in_specs=[pl.BlockSpec((tm, MEM_W), lambda i: (i, Q_MEM_OFF // MEM_W)),
                  pl.BlockSpec((N_MEM, 2 * MEM_W), lambda i: (0, 0)),
                  pl.BlockSpec((1, HEAD_DIM), lambda i: (0, 0)),
                  pl.BlockSpec((1, HEAD_DIM), lambda i: (0, 0))],
        out_specs=pl.BlockSpec((tm, MEM_W), lambda i: (i, 0)),
        out_shape=jax.ShapeDtypeStruct((SEQ, MEM_W), BF16),
        compiler_params=_params(("parallel",), 32),
        name="mem_attention",
    )(proj, kv, q_g.reshape(1, HEAD_DIM), k_g.reshape(1, HEAD_DIM))


def _merge_kernel(fm_ref, na_ref, mo_ref, g0_ref, g1_ref, g2_ref, wf_ref, wn_ref, wm_ref, o_ref,
                  *, col_chunk):
    for c in range(D_MODEL // col_chunk):
        cols = slice(c * col_chunk, (c + 1) * col_chunk)
        o_f = jnp.dot(fm_ref[...], wf_ref[:, cols], preferred_element_type=F32)
        o_n = jnp.dot(na_ref[...], wn_ref[:, cols], preferred_element_type=F32)
        o_m = jnp.dot(mo_ref[...], wm_ref[:, cols], preferred_element_type=F32)
        acc = jax.nn.sigmoid(g0_ref[:, cols].astype(F32)) * o_f
        acc = acc + jax.nn.sigmoid(g1_ref[:, cols].astype(F32)) * o_n
        acc = acc + jax.nn.sigmoid(g2_ref[:, cols].astype(F32)) * o_m
        o_ref[:, cols] = acc.astype(o_ref.dtype)


ROUTER_LANES = 128


def _route_top2(h, wr_ref):
    h_hi = h.astype(BF16)
    h_lo = (h - h_hi.astype(F32)).astype(BF16)
    r_hi = jnp.dot(h_hi, wr_ref[...], preferred_element_type=F32)
    r_lo = jnp.dot(h_lo, wr_ref[:, :ROUTER_LANES], preferred_element_type=F32)
    logits = r_hi[:, :ROUTER_LANES] + r_hi[:, ROUTER_LANES:] + r_lo
    lane = lax.broadcasted_iota(jnp.int32, logits.shape, 1).astype(F32)
    logits = jnp.where(lane < N_EXPERTS, logits, -jnp.inf)
    m1 = jnp.max(logits, axis=-1, keepdims=True)
    i1 = jnp.min(jnp.where(logits == m1, lane, float(ROUTER_LANES)), axis=-1, keepdims=True)
    rest = jnp.where(lane == i1, -jnp.inf, logits)
    m2 = jnp.max(rest, axis=-1, keepdims=True)
    i2 = jnp.min(jnp.where(rest == m2, lane, float(ROUTER_LANES)), axis=-1, keepdims=True)
    e21 = jnp.exp(m2 - m1)
    g1 = 1.0 / (1.0 + e21)
    g2 = e21 * g1
    return jnp.where(lane == 0.0, i1, i2).astype(jnp.int32), jnp.where(lane == 0.0, g1, g2)


def _oproj_kernel(a_ref, w_ref, x_ref, g_ref, *rest, row_chunk, route, parts):
    if route:
        wr_ref, xo_ref, h_ref, idx_ref, gate_ref = rest
    else:
        xo_ref, h_ref = rest
    half = xo_ref.shape[0] // parts
    for r in range(parts):
        xo_ref[r * half:(r + 1) * half, :] = (
            jnp.dot(a_ref[r * half:(r + 1) * half, :], w_ref[...], preferred_element_type=F32)
            + x_ref[r * half:(r + 1) * half, :])
        for c in range(half // row_chunk):
            rows = slice(r * half + c * row_chunk, r * half + (c + 1) * row_chunk)
            h = _rms_rows(xo_ref[rows, :], g_ref[...])
            if route:
                h_ref[rows] = _rows_to_tiles(_pack_halves(h))
                idx_ref[rows, :], gate_ref[rows, :] = _route_top2(h, wr_ref)
            else:
                h_ref[rows, :] = h.astype(h_ref.dtype)


def _merge_oproj_kernel(fm_ref, na_ref, mo_ref, g0_ref, g1_ref, g2_ref, wf_ref, wn_ref, wm_ref,
                        w_ref, x_ref, g_ref, *rest, col_chunk, row_chunk, route):
    mg_ref = rest[-1]
    _merge_kernel(fm_ref, na_ref, mo_ref, g0_ref, g1_ref, g2_ref, wf_ref, wn_ref, wm_ref, mg_ref,
                  col_chunk=col_chunk)
    _oproj_kernel(mg_ref, w_ref, x_ref, g_ref, *rest[:-1], row_chunk=row_chunk, route=route,
                  parts=1)


def _merge_out_proj(fm, o_na, o_mem, proj, wf, wn, wm, w_o, x, g, w_router=None):
    tm = 256
    route = w_router is not None
    gb = GATE_OFF // D_MODEL
    resident = pl.Buffered(1)
    in_specs = [pl.BlockSpec((tm, FOURIER_W), lambda i: (i, 0)),
                pl.BlockSpec((tm, NA_W), lambda i: (i, 0)),
                pl.BlockSpec((tm, MEM_W), lambda i: (i, 0)),
                pl.BlockSpec((tm, D_MODEL), lambda i: (i, gb)),
                pl.BlockSpec((tm, D_MODEL), lambda i: (i, gb + 1)),
                pl.BlockSpec((tm, D_MODEL), lambda i: (i, gb + 2)),
                pl.BlockSpec((FOURIER_W, D_MODEL), lambda i: (0, 0), pipeline_mode=resident),
                pl.BlockSpec((NA_W, D_MODEL), lambda i: (0, 0), pipeline_mode=resident),
                pl.BlockSpec((MEM_W, D_MODEL), lambda i: (0, 0), pipeline_mode=resident),
                pl.BlockSpec((D_MODEL, D_MODEL), lambda i: (0, 0), pipeline_mode=resident),
                pl.BlockSpec((tm, D_MODEL), lambda i: (i, 0)),
                pl.BlockSpec((1, D_MODEL), lambda i: (0, 0))]
    args = [fm, o_na, o_mem, proj, proj, proj, wf, wn, wm, w_o, x, g.reshape(1, D_MODEL)]
    out_specs = [pl.BlockSpec((tm, D_MODEL), lambda i: (i, 0))]
    out_shape = [jax.ShapeDtypeStruct((SEQ, D_MODEL), F32)]
    if route:
        w_pad = jnp.zeros((D_MODEL, ROUTER_LANES), F32).at[:, :N_EXPERTS].set(w_router.astype(F32))
        w_hi = w_pad.astype(BF16)
        w_lo = (w_pad - w_hi.astype(F32)).astype(BF16)
        in_specs.append(pl.BlockSpec((D_MODEL, 2 * ROUTER_LANES), lambda i: (0, 0),
                                     pipeline_mode=resident))
        args.append(jnp.concatenate([w_hi, w_lo], axis=1))
        out_specs += [pl.BlockSpec((tm, ROW_SUBLANES, LANES), lambda i: (i, 0, 0)),
                      pl.BlockSpec((tm, ROUTER_LANES), lambda i: (i, 0)),
                      pl.BlockSpec((tm, ROUTER_LANES), lambda i: (i, 0))]
        out_shape += [jax.ShapeDtypeStruct((SEQ, ROW_SUBLANES, LANES), jnp.uint32),
                      jax.ShapeDtypeStruct((SEQ, ROUTER_LANES), jnp.int32),
                      jax.ShapeDtypeStruct((SEQ, ROUTER_LANES), F32)]
    else:
        out_specs.append(pl.BlockSpec((tm, D_MODEL), lambda i: (i, 0)))
        out_shape.append(jax.ShapeDtypeStruct((SEQ, D_MODEL), BF16))
    return pl.pallas_call(
        functools.partial(_merge_oproj_kernel, col_chunk=512, row_chunk=128, route=route),
        grid=(SEQ // tm,),
        in_specs=in_specs,
        out_specs=out_specs,
        out_shape=out_shape,
        scratch_shapes=[pltpu.VMEM((tm, D_MODEL), BF16)],
        compiler_params=_params(("parallel",), 48),
        name="merge_out_proj",
    )(*args)


def _swiglu_step(x_ref, wg_ref, wu_ref, wd_ref, acc_ref, rows=None):
    x = x_ref[:rows, :]
    g = jnp.dot(x, wg_ref[0].astype(BF16), preferred_element_type=F32)
    u = jnp.dot(x, wu_ref[0].astype(BF16), preferred_element_type=F32)
    a = (g * _sigmoid(g) * u).astype(BF16)
    acc_ref[:rows, :] += jnp.dot(a, wd_ref[0].astype(BF16), preferred_element_type=F32)


def _dense_ffn_kernel(x_ref, wg_ref, wu_ref, wd_ref, resid_ref, o_ref):
    @pl.when(pl.program_id(1) == 0)
    def _():
        o_ref[...] = resid_ref[...]

    _swiglu_step(x_ref, wg_ref, wu_ref, wd_ref, o_ref)


def _dense_ffn(h, wg, wu, wd, layer, resid):
    tm, tf = 1024, 512
    m, d = h.shape
    nf = D_FF // tf
    return pl.pallas_call(
        _dense_ffn_kernel,
        grid=(m // tm, nf),
        in_specs=[pl.BlockSpec((tm, d), lambda i, f: (i, 0)),
                  pl.BlockSpec((1, d, tf), lambda i, f: (layer, 0, f)),
                  pl.BlockSpec((1, d, tf), lambda i, f: (layer, 0, f)),
                  pl.BlockSpec((1, tf, d), lambda i, f: (layer, f, 0)),
                  pl.BlockSpec((tm, d), lambda i, f: (i, 0))],
        out_specs=pl.BlockSpec((tm, d), lambda i, f: (i, 0)),
        out_shape=jax.ShapeDtypeStruct((m, d), F32),
        compiler_params=_params(("parallel", "arbitrary"), 60),
        name="dense_ffn",
    )(h, wg, wu, wd, resid)


MOE_TF = 512
MOE_NF = D_FF // MOE_TF
MOE_ROWS_PER_STEP = 96
MOE_TM = MOE_NF * MOE_ROWS_PER_STEP
MOE_ROW_CHUNK = 96
MOE_N_TILES = -(-SEQ * TOP_K // MOE_TM) + N_EXPERTS
MOE_THIRDS = 3
MOE_THIRD_ROWS = MOE_TM // MOE_THIRDS


def _tile_row_copy(tok_ref, h_ref, xg_ref, sems, tile, slot, r):
    row = tok_ref[tile * MOE_TM + r]
    return pltpu.make_async_copy(h_ref.at[pl.ds(row, 1)], xg_ref.at[slot, pl.ds(r, 1)],
                                 sems.at[slot])


def _wait_tile_rows(h_ref, xg_ref, sems, slot):
    pltpu.make_async_copy(h_ref.at[pl.ds(0, MOE_TM)], xg_ref.at[slot], sems.at[slot]).wait()


def _moe_expert_kernel(be_ref, nu_ref, nt_ref, tok_ref, h_ref, wg_ref, wu_ref, wd_ref, o_ref,
                       xg_ref, xb_ref, acc_ref, sems):
    i = pl.program_id(0)
    f = pl.program_id(1)
    n_used = nu_ref[0]
    active = i < n_used
    slot = i % 2
    n_chunks = MOE_TM // MOE_ROW_CHUNK

    @pl.when(jnp.logical_and(i == 0, f == 0))
    def _():
        def start_row(r, carry):
            _tile_row_copy(tok_ref, h_ref, xg_ref, sems, 0, 0, r).start()
            return carry
        lax.fori_loop(0, MOE_TM, start_row, 0, unroll=8)

    @pl.when(jnp.logical_and(active, f == 0))
    def _():
        _wait_tile_rows(h_ref, xg_ref, sems, slot)
        acc_ref[...] = jnp.zeros(acc_ref.shape, acc_ref.dtype)

        def unpack_rows(c, carry):
            rows = pl.ds(pl.multiple_of(c * MOE_ROW_CHUNK, MOE_ROW_CHUNK), MOE_ROW_CHUNK)
            lo, hi = _unpack_halves(_tiles_to_rows(xg_ref[slot, rows]))
            xb_ref[rows, :HALF_D] = lo.astype(BF16)
            xb_ref[rows, HALF_D:] = hi.astype(BF16)
            return carry
        lax.fori_loop(0, n_chunks, unpack_rows, 0)

    for thirds in range(1, MOE_THIRDS + 1):
        @pl.when(jnp.logical_and(active, nt_ref[i] == thirds))
        def _(thirds=thirds):
            for j in range(MOE_ROWS_PER_STEP):
                _tile_row_copy(tok_ref, h_ref, xg_ref, sems, i + 1, 1 - slot,
                               f * MOE_ROWS_PER_STEP + j).start()
            _swiglu_step(xb_ref, wg_ref, wu_ref, wd_ref, acc_ref, rows=thirds * MOE_THIRD_ROWS)

    @pl.when(jnp.logical_and(i == n_used, f == 0))
    def _():
        _wait_tile_rows(h_ref, xg_ref, sems, slot)

    @pl.when(f == MOE_NF - 1)
    def _():
        @pl.when(active)
        def _():
            def finish_rows(c, carry):
                rows = pl.ds(pl.multiple_of(c * MOE_ROW_CHUNK, MOE_ROW_CHUNK), MOE_ROW_CHUNK)
                o_ref[rows] = _rows_to_tiles(_pack_halves(acc_ref[rows, :]))
                return carry
            lax.fori_loop(0, n_chunks, finish_rows, 0)

        @pl.when(jnp.logical_not(active))
        def _():
            o_ref[...] = jnp.zeros(o_ref.shape, o_ref.dtype)


def _moe_experts(h_packed, tok, blk_e, n_used, n_thirds, wg, wu, wd):
    d, tf = D_MODEL, MOE_TF

    def f_idx(i, f, nu):
        return jnp.where(i < nu[0], f, MOE_NF - 1)

    def w_gu(i, f, be, nu, nt, tok):
        return (be[i], 0, f_idx(i, f, nu))

    def w_d(i, f, be, nu, nt, tok):
        return (be[i], f_idx(i, f, nu), 0)

    grid_spec = pltpu.PrefetchScalarGridSpec(
        num_scalar_prefetch=4,
        grid=(MOE_N_TILES, MOE_NF),
        in_specs=[pl.BlockSpec(memory_space=pl.ANY),
                  pl.BlockSpec((1, d, tf), w_gu),
                  pl.BlockSpec((1, d, tf), w_gu),
                  pl.BlockSpec((1, tf, d), w_d)],
        out_specs=pl.BlockSpec((MOE_TM, ROW_SUBLANES, LANES),
                               lambda i, f, be, nu, nt, tok: (i, 0, 0)),
        scratch_shapes=[pltpu.VMEM((2, MOE_TM, ROW_SUBLANES, LANES), jnp.uint32),
                        pltpu.VMEM((MOE_TM, d), BF16),
                        pltpu.VMEM((MOE_TM, d), F32),
                        pltpu.SemaphoreType.DMA((2,))],
    )
    return pl.pallas_call(
        _moe_expert_kernel,
        grid_spec=grid_spec,
        out_shape=jax.ShapeDtypeStruct((MOE_N_TILES * MOE_TM, ROW_SUBLANES, LANES), jnp.uint32),
        compiler_params=_params(("arbitrary", "arbitrary"), 60),
        name="moe_experts",
    )(blk_e, n_used, n_thirds, tok, h_packed, wg, wu, wd)


def _start_row_gather(idx_ref, idx_base, src_ref, dst_ref, sem, n_rows):
    def body(r, carry):
        row = idx_ref[idx_base + r]
        pltpu.make_async_copy(src_ref.at[pl.ds(row, 1)], dst_ref.at[pl.ds(r, 1)], sem).start()
        return carry
    lax.fori_loop(0, n_rows, body, 0, unroll=8)


def _wait_row_gather(src_ref, dst_ref, sem, n_rows):
    pltpu.make_async_copy(src_ref.at[pl.ds(0, n_rows)], dst_ref, sem).wait()


def _moe_combine_kernel(pos_ref, y_ref, x_ref, gate_ref, o_ref, buf_ref, sems, *, tokens):
    i = pl.program_id(0)
    slot = i % 2
    rows = TOP_K * tokens

    @pl.when(i == 0)
    def _():
        _start_row_gather(pos_ref, 0, y_ref, buf_ref.at[0], sems.at[0], rows)

    @pl.when(i + 1 < pl.num_programs(0))
    def _():
        _start_row_gather(pos_ref, (i + 1) * rows, y_ref, buf_ref.at[1 - slot],
                          sems.at[1 - slot], rows)

    _wait_row_gather(y_ref, buf_ref.at[slot], sems.at[slot], rows)
    g0 = gate_ref[:, 0:1]
    g1 = gate_ref[:, 1:2]
    lo0, hi0 = _unpack_halves(_tiles_to_rows(buf_ref[slot, :tokens]))
    lo1, hi1 = _unpack_halves(_tiles_to_rows(buf_ref[slot, tokens:]))
    o_ref[:, :HALF_D] = x_ref[:, :HALF_D] + (g0 * lo0 + g1 * lo1)
    o_ref[:, HALF_D:] = x_ref[:, HALF_D:] + (g0 * hi0 + g1 * hi1)


def _moe_combine(y_packed, pos, x, gate):
    tokens = 512
    n = x.shape[0]
    steps = n // tokens
    pos_tiled = pos.reshape(steps, tokens, TOP_K).transpose(0, 2, 1).reshape(-1)
    grid_spec = pltpu.PrefetchScalarGridSpec(
        num_scalar_prefetch=1,
        grid=(steps,),
        in_specs=[pl.BlockSpec(memory_space=pl.ANY),
                  pl.BlockSpec((tokens, D_MODEL), lambda i, pos_ref: (i, 0)),
                  pl.BlockSpec((tokens, ROUTER_LANES), lambda i, pos_ref: (i, 0))],
        out_specs=pl.BlockSpec((tokens, D_MODEL), lambda i, pos_ref: (i, 0)),
        scratch_shapes=[pltpu.VMEM((2, TOP_K * tokens, ROW_SUBLANES, LANES), jnp.uint32),
                        pltpu.SemaphoreType.DMA((2,))],
    )
    return pl.pallas_call(
        functools.partial(_moe_combine_kernel, tokens=tokens),
        grid_spec=grid_spec,
        out_shape=jax.ShapeDtypeStruct((n, D_MODEL), F32),
        compiler_params=_params(("arbitrary",), 40),
        name="moe_combine",
    )(pos_tiled, y_packed, x, gate)


def _slot_tokens_kernel(dest_ref, fill_ref, tok_ref):
    for e in range(N_EXPERTS + 1):
        def fill(b, carry):
            for j in range(8):
                tok_ref[b * 8 + j] = 0
            return carry
        lax.fori_loop(fill_ref[2 * e], fill_ref[2 * e + 1], fill, 0)

    def place(a, carry):
        tok_ref[dest_ref[a]] = a >> 1
        return carry
    lax.fori_loop(0, SEQ * TOP_K, place, 0, unroll=8)


def _slot_tokens(dest, fill_ranges):
    assert TOP_K == 2
    cap = MOE_N_TILES * MOE_TM
    grid_spec = pltpu.PrefetchScalarGridSpec(
        num_scalar_prefetch=2,
        grid=(1,),
        in_specs=[],
        out_specs=pl.BlockSpec(memory_space=pltpu.SMEM),
    )
    return pl.pallas_call(
        _slot_tokens_kernel,
        grid_spec=grid_spec,
        out_shape=jax.ShapeDtypeStruct((cap,), jnp.int32),
        name="moe_slot_tokens",
    )(dest, fill_ranges)


def _moe(x_new, h_packed, idx, gate, wg, wu, wd):
    n = SEQ
    cap = MOE_N_TILES * MOE_TM
    e_flat = idx[:, :TOP_K].reshape(-1)
    onehot = (e_flat[:, None] == jnp.arange(N_EXPERTS, dtype=jnp.int32)[None, :]).astype(jnp.int32)
    csum = jnp.cumsum(onehot, axis=0)
    counts = csum[-1]
    rank = jnp.sum((csum - onehot) * onehot, axis=1)
    padded = (counts + MOE_TM - 1) // MOE_TM * MOE_TM
    pad_ends = jnp.cumsum(padded)
    pad_starts = pad_ends - padded
    dest = jnp.sum(onehot * pad_starts[None, :], axis=1) + rank
    fill_lo = jnp.concatenate([pad_starts + counts, pad_ends[-1:]]) // 8
    fill_hi = (jnp.concatenate([pad_ends, jnp.full((1,), cap, pad_ends.dtype)]) + 7) // 8
    tok = _slot_tokens(dest.astype(jnp.int32),
                       jnp.stack([fill_lo, fill_hi], axis=1).reshape(-1).astype(jnp.int32))
    n_used = (pad_ends[-1] // MOE_TM).astype(jnp.int32).reshape(1)
    tile_start = jnp.arange(MOE_N_TILES, dtype=jnp.int32) * MOE_TM
    tile_start = jnp.minimum(tile_start, pad_ends[-1] - 1)
    blk_e = jnp.sum((tile_start[:, None] >= pad_ends[None, :]).astype(jnp.int32), axis=1)
    blk_e = jnp.clip(blk_e, 0, N_EXPERTS - 1)
    real_rows = counts[blk_e] - (tile_start - pad_starts[blk_e])
    n_thirds = jnp.clip((real_rows + MOE_THIRD_ROWS - 1) // MOE_THIRD_ROWS, 1, MOE_THIRDS)
    yb = _moe_experts(h_packed, tok, blk_e, n_used, n_thirds.astype(jnp.int32), wg, wu, wd)
    return _moe_combine(yb, dest.reshape(n, TOP_K), x_new, gate)


def kernel(x, mem, ln_mix_g, w_in, na_q_g, na_k_g, na_rpb, mem_ln_g, w_mem_kv, mem_q_g, mem_k_g,
           w_fourier_out, w_na_out, w_mem_out, w_o, ln_ffn_g, ffn_w_gate, ffn_w_up, ffn_w_down,
           moe_router, moe_w_gate, moe_w_up, moe_w_down):
    assert x.shape == (1, SEQ, D_MODEL) and mem.shape == (1, N_MEM, D_MODEL)
    xs = x.reshape(SEQ, D_MODEL)
    mem2 = mem.reshape(N_MEM, D_MODEL)
    tables = _dft_tables()
    w_in_bf16 = w_in.astype(BF16)
    for l in range(DEPTH):
        proj = _norm_mm(xs, ln_mix_g[l], w_in_bf16, l, tm=1024, tn=2048, out_dtype=BF16,
                        vmem_mib=58, name="in_proj")
        fm = _fourier_mix(proj, tables)
        o_na = _na_attention(proj, na_q_g[l], na_k_g[l], _na_col_blocks(na_rpb[l]))
        kv = _norm_mm(mem2, mem_ln_g[l], w_mem_kv, l, tm=N_MEM, tn=2 * MEM_W, out_dtype=F32,
                      vmem_mib=40, name="mem_kv_proj")
        o_mem = _mem_attention(proj, kv, mem_q_g[l], mem_k_g[l])
        branch_w = (w_fourier_out[l].astype(BF16), w_na_out[l].astype(BF16),
                    w_mem_out[l].astype(BF16), w_o[l].astype(BF16))
        i = l // 2
        if l % 2 == 0:
            x_new, h2 = _merge_out_proj(fm, o_na, o_mem, proj, *branch_w, xs, ln_ffn_g[l])
            xs = _dense_ffn(h2, ffn_w_gate.astype(BF16), ffn_w_up.astype(BF16),
                            ffn_w_down.astype(BF16), i, x_new)
        else:
            x_new, h2, idx, gate = _merge_out_proj(fm, o_na, o_mem, proj, *branch_w, xs,
                                                   ln_ffn_g[l], moe_router[i])
            xs = _moe(x_new, h2, idx, gate, moe_w_gate[i], moe_w_up[i], moe_w_down[i])
    return xs.reshape(1, SEQ, D_MODEL)
```

```python
import functools
import math

import numpy as np
import jax
import jax.numpy as jnp
from jax import lax
from jax.experimental import pallas as pl
from jax.experimental.pallas import tpu as pltpu

F32 = jnp.float32
BF16 = jnp.bfloat16

D_MODEL = 2048
SEQ = 16384
DEPTH = 2
GRID_W = 64
ROWS = SEQ // GRID_W
HEAD_DIM = 128
N_FOURIER_GROUPS = 4
FOURIER_W = N_FOURIER_GROUPS * HEAD_DIM
N_NA_HEADS = 8
NA_W = N_NA_HEADS * HEAD_DIM
WIN_H = 8
WIN_W = 16
N_MEM = 256
N_MEM_HEADS = 4
MEM_W = N_MEM_HEADS * HEAD_DIM
IN_W = FOURIER_W + 3 * NA_W + MEM_W + 3 * D_MODEL
D_FF = 5632
N_EXPERTS = 8
TOP_K = 2
EPS = 1e-6
HALF_D = D_MODEL // 2

Q_NA_OFF = FOURIER_W
K_NA_OFF = Q_NA_OFF + NA_W
V_NA_OFF = K_NA_OFF + NA_W
Q_MEM_OFF = V_NA_OFF + NA_W
GATE_OFF = Q_MEM_OFF + MEM_W

MIB = 1024 * 1024
NEG_BIG = -1e30

FFT_A = 128
FFT_B = SEQ // FFT_A


def _params(semantics, vmem_mib):
    return pltpu.CompilerParams(dimension_semantics=semantics,
                                vmem_limit_bytes=int(vmem_mib * MIB))


def _rms_rows(x, gain):
    ms = jnp.mean(x * x, axis=-1, keepdims=True)
    return x * lax.rsqrt(ms + EPS) * gain


def _sigmoid(x):
    return 0.5 * jnp.tanh(0.5 * x) + 0.5


def _bf16_bits(v):
    return pltpu.bitcast(v.astype(BF16).astype(F32), jnp.uint32)


def _pack_halves(v):
    half = v.shape[1] // 2
    lo = lax.shift_right_logical(_bf16_bits(v[:, :half]), jnp.uint32(16))
    return lo | _bf16_bits(v[:, half:])


def _unpack_halves(w):
    lo = pltpu.bitcast(lax.shift_left(w, jnp.uint32(16)), F32)
    hi = pltpu.bitcast(w & jnp.uint32(0xFFFF0000), F32)
    return lo, hi


ROW_SUBLANES = 8
LANES = 128
assert HALF_D == ROW_SUBLANES * LANES


def _rows_to_tiles(w):
    return pltpu.einshape("a(bc)->abc", w, b=ROW_SUBLANES)


def _tiles_to_rows(t):
    return pltpu.einshape("abc->a(bc)", t)


def _norm_mm_kernel(x_ref, g_ref, w_ref, o_ref, h_ref, *, row_chunk):
    @pl.when(pl.program_id(1) == 0)
    def _():
        def body(c, carry):
            rows = pl.ds(pl.multiple_of(c * row_chunk, row_chunk), row_chunk)
            h_ref[rows, :] = _rms_rows(x_ref[rows, :].astype(F32), g_ref[...]).astype(BF16)
            return carry
        lax.fori_loop(0, h_ref.shape[0] // row_chunk, body, 0, unroll=2)

    o_ref[...] = jnp.dot(h_ref[...], w_ref[0].astype(BF16),
                         preferred_element_type=F32).astype(o_ref.dtype)


def _norm_mm(x, g, w, layer, *, tm, tn, out_dtype, vmem_mib, name):
    m, k = x.shape
    n = w.shape[2]
    row_chunk = min(tm, 128)
    return pl.pallas_call(
        functools.partial(_norm_mm_kernel, row_chunk=row_chunk),
        grid=(m // tm, n // tn),
        in_specs=[pl.BlockSpec((tm, k), lambda i, j: (i, 0)),
                  pl.BlockSpec((1, k), lambda i, j: (0, 0)),
                  pl.BlockSpec((1, k, tn), lambda i, j: (layer, 0, j))],
        out_specs=pl.BlockSpec((tm, tn), lambda i, j: (i, j)),
        out_shape=jax.ShapeDtypeStruct((m, n), out_dtype),
        scratch_shapes=[pltpu.VMEM((tm, k), BF16)],
        compiler_params=_params(("parallel", "arbitrary"), vmem_mib),
        name=name,
    )(x, g.reshape(1, k), w)


def _dft_tables():
    a = np.arange(FFT_A)
    ang1 = 2.0 * np.pi * np.outer(a, a) / FFT_A
    w1 = np.concatenate([np.cos(ang1), -np.sin(ang1)], axis=0)
    k1 = np.arange(FFT_A)[:, None, None]
    k2 = np.arange(FFT_B)[None, :, None]
    s2 = np.arange(FFT_B)[None, None, :]
    ang2 = 2.0 * np.pi * ((s2 * (k1 + FFT_A * k2)) % SEQ) / SEQ
    g2 = np.concatenate([np.cos(ang2), np.sin(ang2)], axis=2)
    c = np.arange(HEAD_DIM)
    angc = 2.0 * np.pi * np.outer(c, c) / HEAD_DIM
    eye = np.eye(N_FOURIER_GROUPS)
    bd = np.concatenate([np.kron(eye, np.cos(angc)), np.kron(eye, np.sin(angc))], axis=0)
    return (jnp.asarray(w1, dtype=BF16), jnp.asarray(g2, dtype=BF16),
            jnp.asarray(bd, dtype=BF16))


FFT_BLK = 16


def _dft1_kernel(w_ref, x_ref, br_ref, bi_ref):
    xt = pltpu.einshape("abc->bac", x_ref[...])
    for j in range(FFT_BLK):
        r = jnp.dot(w_ref[...], xt[j], preferred_element_type=F32)
        br_ref[j] = r[:FFT_A].astype(BF16)
        bi_ref[j] = r[FFT_A:].astype(BF16)


def _dft2_kernel(br_ref, bi_ref, g_ref, bd_ref, o_ref, *, norm):
    brt = pltpu.einshape("abc->bac", br_ref[...])
    bit = pltpu.einshape("abc->bac", bi_ref[...])
    ys = []
    for i in range(FFT_BLK):
        br = brt[i]
        bi = bit[i]
        g = g_ref[i]
        zr = jnp.dot(g, jnp.concatenate([br, bi], axis=0), preferred_element_type=F32)
        zi = jnp.dot(g, jnp.concatenate([bi, -br], axis=0), preferred_element_type=F32)
        z = jnp.concatenate([zr, zi], axis=1).astype(BF16)
        y = jnp.dot(z, bd_ref[...], preferred_element_type=F32) * norm
        ys.append(y.astype(o_ref.dtype))
    o_ref[...] = pltpu.einshape("abc->bac", jnp.stack(ys, axis=0))


def _fourier_mix(proj, tables):
    w1, g2, bd = tables
    x3 = proj.reshape(FFT_A, FFT_B, IN_W)
    blk = (FFT_A, FFT_BLK, FOURIER_W)
    br, bi = pl.pallas_call(
        _dft1_kernel,
        grid=(FFT_B // FFT_BLK,),
        in_specs=[pl.BlockSpec((2 * FFT_A, FFT_A), lambda j: (0, 0)),
                  pl.BlockSpec(blk, lambda j: (0, j, 0))],
        out_specs=[pl.BlockSpec((FFT_BLK, FFT_A, FOURIER_W), lambda j: (j, 0, 0)),
                   pl.BlockSpec((FFT_BLK, FFT_A, FOURIER_W), lambda j: (j, 0, 0))],
        out_shape=[jax.ShapeDtypeStruct((FFT_B, FFT_A, FOURIER_W), BF16)] * 2,
        compiler_params=_params(("parallel",), 40),
        name="fourier_stage1",
    )(w1, x3)
    norm = 1.0 / math.sqrt(SEQ * HEAD_DIM)
    y3 = pl.pallas_call(
        functools.partial(_dft2_kernel, norm=norm),
        grid=(FFT_A // FFT_BLK,),
        in_specs=[pl.BlockSpec((FFT_B, FFT_BLK, FOURIER_W), lambda j: (0, j, 0)),
                  pl.BlockSpec((FFT_B, FFT_BLK, FOURIER_W), lambda j: (0, j, 0)),
                  pl.BlockSpec((FFT_BLK, FFT_B, 2 * FFT_B), lambda j: (j, 0, 0)),
                  pl.BlockSpec((2 * FOURIER_W, FOURIER_W), lambda j: (0, 0))],
        out_specs=pl.BlockSpec((FFT_B, FFT_BLK, FOURIER_W), lambda j: (0, j, 0)),
        out_shape=jax.ShapeDtypeStruct((FFT_B, FFT_A, FOURIER_W), BF16),
        compiler_params=_params(("parallel",), 40),
        name="fourier_stage2",
    )(br, bi, g2, bd)
    return y3.reshape(SEQ, FOURIER_W)


NA_SUB_ROWS = 2
NA_KEY_ROWS = NA_SUB_ROWS + WIN_H
NA_SUBS_PER_STEP = 64
NA_N_SUB = ROWS // NA_SUB_ROWS
NA_MAX_KEY_START = ROWS - NA_KEY_ROWS
NA_N_ROW_OFFSETS = 2 * WIN_H - 1
NA_MASKED_BLOCK = NA_N_ROW_OFFSETS
LOG2E = math.log2(math.e)


def _na_key_row0(sub):
    return jnp.clip(sub * NA_SUB_ROWS - WIN_H // 2, 0, NA_MAX_KEY_START)


def _na_cases():
    patterns, case_of_sub = [], []
    for sub in range(NA_N_SUB):
        r0 = sub * NA_SUB_ROWS
        ks = int(np.clip(r0 - WIN_H // 2, 0, NA_MAX_KEY_START))
        pat = np.full((NA_SUB_ROWS, NA_KEY_ROWS), NA_MASKED_BLOCK, np.int32)
        for qi in range(NA_SUB_ROWS):
            r = r0 + qi
            r_start = int(np.clip(r - WIN_H // 2, 0, ROWS - WIN_H))
            for kj in range(NA_KEY_ROWS):
                if r_start <= ks + kj < r_start + WIN_H:
                    pat[qi, kj] = ks + kj - r + (WIN_H - 1)
        for c, p in enumerate(patterns):
            if np.array_equal(p, pat):
                case_of_sub.append(c)
                break
        else:
            case_of_sub.append(len(patterns))
            patterns.append(pat)
    return np.stack(patterns), np.asarray(case_of_sub, np.int32)


def _na_case_of(sub, case_of_sub):
    common = int(np.bincount(case_of_sub).argmax())
    case = jnp.int32(common)
    for s in np.nonzero(case_of_sub != common)[0]:
        case = jnp.where(sub == int(s), int(case_of_sub[s]), case)
    return case


def _na_col_blocks(rpb):
    qc = np.arange(GRID_W)[:, None]
    kc = np.arange(GRID_W)[None, :]
    col_start = np.clip(qc - WIN_W // 2, 0, GRID_W - WIN_W)
    col_valid = (kc >= col_start) & (kc < col_start + WIN_W)
    dc = kc - qc + (WIN_W - 1)
    sel = (np.arange(2 * WIN_W - 1)[:, None, None] == dc[None]) & col_valid[None]
    sel = jnp.asarray(sel.astype(np.float32))
    t = jnp.einsum("hrd,dqk->hrqk", rpb.astype(F32), sel, precision=lax.Precision.HIGHEST)
    t = jnp.where(col_valid[None, None], t * LOG2E, NEG_BIG)
    masked = jnp.full((N_NA_HEADS, 1, GRID_W, GRID_W), NEG_BIG, F32)
    t = jnp.concatenate([t, masked], axis=1)
    t = jnp.swapaxes(t, -1, -2)
    return jnp.concatenate([t, t], axis=-1)


def _na_kernel(q_ref, k_ref, v_ref, qg_ref, kg_ref, cb_ref, o_ref, kn_ref, bias_ref):
    step = pl.program_id(1)
    nq = NA_SUB_ROWS * GRID_W
    nk = NA_KEY_ROWS * GRID_W
    row_off, case_of_sub = _na_cases()

    @pl.when(step == 0)
    def _():
        chunk = 1024

        def body(c, carry):
            rows = pl.ds(pl.multiple_of(c * chunk, chunk), chunk)
            kn_ref[rows, :] = _rms_rows(k_ref[rows, :].astype(F32), kg_ref[...]).astype(BF16)
            return carry
        lax.fori_loop(0, SEQ // chunk, body, 0, unroll=2)

        for case in range(row_off.shape[0]):
            for qi in range(NA_SUB_ROWS):
                for kj in range(NA_KEY_ROWS):
                    lanes = slice(qi * GRID_W, (qi + 1) * GRID_W)
                    src = slice((qi % 2) * GRID_W, (qi % 2 + 1) * GRID_W)
                    bias_ref[case, kj * GRID_W:(kj + 1) * GRID_W, lanes] = (
                        cb_ref[0, int(row_off[case, qi, kj]), :, src])

    scale = HEAD_DIM ** -0.5 * LOG2E
    for sb in range(NA_SUBS_PER_STEP):
        sub = step * NA_SUBS_PER_STEP + sb
        case = _na_case_of(sub, case_of_sub)
        q = q_ref[sb * nq:(sb + 1) * nq, :].astype(F32)
        qn = (_rms_rows(q, qg_ref[...]) * scale).astype(BF16)
        kstart = pl.multiple_of(_na_key_row0(sub) * GRID_W, 2 * GRID_W)
        kw = kn_ref[pl.ds(kstart, nk), :]
        vw = v_ref[pl.ds(kstart, nk), :]
        st = lax.dot_general(kw, qn, (((1,), (1,)), ((), ())), preferred_element_type=F32)
        st = st + bias_ref[case]
        m = jnp.max(st, axis=0, keepdims=True)
        pt = jnp.exp2(st - m)
        l = jnp.sum(pt, axis=0, keepdims=True)
        pt = (pt * (1.0 / l)).astype(BF16)
        o = lax.dot_general(pt, vw, (((0,), (0,)), ((), ())), preferred_element_type=F32)
        o_ref[sb * nq:(sb + 1) * nq, :] = o.astype(o_ref.dtype)


def _na_attention(proj, q_g, k_g, col_blocks):
    nq = NA_SUB_ROWS * GRID_W
    nk = NA_KEY_ROWS * GRID_W
    tq = NA_SUBS_PER_STEP * nq
    n_blk = NA_N_ROW_OFFSETS + 1
    qb, kb_, vb = Q_NA_OFF // HEAD_DIM, K_NA_OFF // HEAD_DIM, V_NA_OFF // HEAD_DIM
    return pl.pallas_call(
        _na_kernel,
        grid=(N_NA_HEADS, SEQ // tq),
        in_specs=[pl.BlockSpec((tq, HEAD_DIM), lambda h, i: (i, qb + h)),
                  pl.BlockSpec((SEQ, HEAD_DIM), lambda h, i: (0, kb_ + h)),
                  pl.BlockSpec((SEQ, HEAD_DIM), lambda h, i: (0, vb + h)),
                  pl.BlockSpec((1, HEAD_DIM), lambda h, i: (0, 0)),
                  pl.BlockSpec((1, HEAD_DIM), lambda h, i: (0, 0)),
                  pl.BlockSpec((1, n_blk, GRID_W, 2 * GRID_W), lambda h, i: (h, 0, 0, 0))],
        out_specs=pl.BlockSpec((tq, HEAD_DIM), lambda h, i: (i, h)),
        out_shape=jax.ShapeDtypeStruct((SEQ, NA_W), BF16),
        scratch_shapes=[pltpu.VMEM((SEQ, HEAD_DIM), BF16),
                        pltpu.VMEM((_na_cases()[0].shape[0], nk, nq), F32)],
        compiler_params=_params(("parallel", "arbitrary"), 48),
        name="na_attention",
    )(proj, proj, proj, q_g.reshape(1, HEAD_DIM), k_g.reshape(1, HEAD_DIM), col_blocks)


def _mem_attn_kernel(q_ref, kv_ref, qg_ref, kg_ref, o_ref):
    scale = HEAD_DIM ** -0.5 * LOG2E
    for h in range(N_MEM_HEADS):
        cols = slice(h * HEAD_DIM, (h + 1) * HEAD_DIM)
        qn = (_rms_rows(q_ref[:, cols].astype(F32), qg_ref[...]) * scale).astype(BF16)
        kn = _rms_rows(kv_ref[:, cols].astype(F32), kg_ref[...]).astype(BF16)
        v = kv_ref[:, MEM_W + h * HEAD_DIM:MEM_W + (h + 1) * HEAD_DIM].astype(BF16)
        s = lax.dot_general(qn, kn, (((1,), (1,)), ((), ())), preferred_element_type=F32)
        m = jnp.max(s, axis=-1, keepdims=True)
        p = jnp.exp2(s - m)
        l = jnp.sum(p, axis=-1, keepdims=True)
        o = jnp.dot(p.astype(BF16), v, preferred_element_type=F32) / l
        o_ref[:, cols] = o.astype(o_ref.dtype)


def _mem_attention(proj, kv, q_g, k_g):
    tm = 512
    return pl.pallas_call(
        _mem_attn_kernel,
        grid=(SEQ // tm,),
        in_specs=[pl.BlockSpec((tm, MEM_W), lambda i: (i, Q_MEM_OFF // MEM_W)),
                  pl.BlockSpec((N_MEM, 2 * MEM_W), lambda i: (0, 0)),
                  pl.BlockSpec((1, HEAD_DIM), lambda i: (0, 0)),
                  pl.BlockSpec((1, HEAD_DIM), lambda i: (0, 0))],
        out_specs=pl.BlockSpec((tm, MEM_W), lambda i: (i, 0)),
        out_shape=jax.ShapeDtypeStruct((SEQ, MEM_W), BF16),
        compiler_params=_params(("parallel",), 32),
        name="mem_attention",
    )(proj, kv, q_g.reshape(1, HEAD_DIM), k_g.reshape(1, HEAD_DIM))


def _merge_kernel(fm_ref, na_ref, mo_ref, g0_ref, g1_ref, g2_ref, wf_ref, wn_ref, wm_ref, o_ref,
                  *, col_chunk):
    for c in range(D_MODEL // col_chunk):
        cols = slice(c * col_chunk, (c + 1) * col_chunk)
        o_f = jnp.dot(fm_ref[...], wf_ref[:, cols], preferred_element_type=F32)
        o_n = jnp.dot(na_ref[...], wn_ref[:, cols], preferred_element_type=F32)
        o_m = jnp.dot(mo_ref[...], wm_ref[:, cols], preferred_element_type=F32)
        acc = jax.nn.sigmoid(g0_ref[:, cols].astype(F32)) * o_f
        acc = acc + jax.nn.sigmoid(g1_ref[:, cols].astype(F32)) * o_n
        acc = acc + jax.nn.sigmoid(g2_ref[:, cols].astype(F32)) * o_m
        o_ref[:, cols] = acc.astype(o_ref.dtype)


ROUTER_LANES = 128


def _route_top2(h, wr_ref):
    h_hi = h.astype(BF16)
    h_lo = (h - h_hi.astype(F32)).astype(BF16)
    r_hi = jnp.dot(h_hi, wr_ref[...], preferred_element_type=F32)
    r_lo = jnp.dot(h_lo, wr_ref[:, :ROUTER_LANES], preferred_element_type=F32)
    logits = r_hi[:, :ROUTER_LANES] + r_hi[:, ROUTER_LANES:] + r_lo
    lane = lax.broadcasted_iota(jnp.int32, logits.shape, 1).astype(F32)
    logits = jnp.where(lane < N_EXPERTS, logits, -jnp.inf)
    m1 = jnp.max(logits, axis=-1, keepdims=True)
    i1 = jnp.min(jnp.where(logits == m1, lane, float(ROUTER_LANES)), axis=-1, keepdims=True)
    rest = jnp.where(lane == i1, -jnp.inf, logits)
    m2 = jnp.max(rest, axis=-1, keepdims=True)
    i2 = jnp.min(jnp.where(rest == m2, lane, float(ROUTER_LANES)), axis=-1, keepdims=True)
    e21 = jnp.exp(m2 - m1)
    g1 = 1.0 / (1.0 + e21)
    g2 = e21 * g1
    return jnp.where(lane == 0.0, i1, i2).astype(jnp.int32), jnp.where(lane == 0.0, g1, g2)


def _oproj_kernel(a_ref, w_ref, x_ref, g_ref, *rest, row_chunk, route):
    if route:
        wr_ref, xo_ref, h_ref, idx_ref, gate_ref = rest
    else:
        xo_ref, h_ref = rest
    xo_ref[...] = jnp.dot(a_ref[...], w_ref[...], preferred_element_type=F32) + x_ref[...]
    for c in range(xo_ref.shape[0] // row_chunk):
        rows = slice(c * row_chunk, (c + 1) * row_chunk)
        h = _rms_rows(xo_ref[rows, :], g_ref[...])
        if route:
            h_ref[rows] = _rows_to_tiles(_pack_halves(h))
            idx_ref[rows, :], gate_ref[rows, :] = _route_top2(h, wr_ref)
        else:
            h_ref[rows, :] = h.astype(h_ref.dtype)


def _merge_oproj_kernel(fm_ref, na_ref, mo_ref, g0_ref, g1_ref, g2_ref, wf_ref, wn_ref, wm_ref,
                        w_ref, x_ref, g_ref, *rest, col_chunk, row_chunk, route):
    mg_ref = rest[-1]
    _merge_kernel(fm_ref, na_ref, mo_ref, g0_ref, g1_ref, g2_ref, wf_ref, wn_ref, wm_ref, mg_ref,
                  col_chunk=col_chunk)
    _oproj_kernel(mg_ref, w_ref, x_ref, g_ref, *rest[:-1], row_chunk=row_chunk, route=route)


def _merge_out_proj(fm, o_na, o_mem, proj, wf, wn, wm, w_o, x, g, w_router=None):
    tm = 512
    route = w_router is not None
    gb = GATE_OFF // D_MODEL
    resident = pl.Buffered(1)
    in_specs = [pl.BlockSpec((tm, FOURIER_W), lambda i: (i, 0)),
                pl.BlockSpec((tm, NA_W), lambda i: (i, 0)),
                pl.BlockSpec((tm, MEM_W), lambda i: (i, 0)),
                pl.BlockSpec((tm, D_MODEL), lambda i: (i, gb)),
                pl.BlockSpec((tm, D_MODEL), lambda i: (i, gb + 1)),
                pl.BlockSpec((tm, D_MODEL), lambda i: (i, gb + 2)),
                pl.BlockSpec((FOURIER_W, D_MODEL), lambda i: (0, 0), pipeline_mode=resident),
                pl.BlockSpec((NA_W, D_MODEL), lambda i: (0, 0), pipeline_mode=resident),
                pl.BlockSpec((MEM_W, D_MODEL), lambda i: (0, 0), pipeline_mode=resident),
                pl.BlockSpec((D_MODEL, D_MODEL), lambda i: (0, 0), pipeline_mode=resident),
                pl.BlockSpec((tm, D_MODEL), lambda i: (i, 0)),
                pl.BlockSpec((1, D_MODEL), lambda i: (0, 0))]
    args = [fm, o_na, o_mem, proj, proj, proj, wf, wn, wm, w_o, x, g.reshape(1, D_MODEL)]
    out_specs = [pl.BlockSpec((tm, D_MODEL), lambda i: (i, 0))]
    out_shape = [jax.ShapeDtypeStruct((SEQ, D_MODEL), F32)]
    if route:
        w_pad = jnp.zeros((D_MODEL, ROUTER_LANES), F32).at[:, :N_EXPERTS].set(w_router.astype(F32))
        w_hi = w_pad.astype(BF16)
        w_lo = (w_pad - w_hi.astype(F32)).astype(BF16)
        in_specs.append(pl.BlockSpec((D_MODEL, 2 * ROUTER_LANES), lambda i: (0, 0),
                                     pipeline_mode=resident))
        args.append(jnp.concatenate([w_hi, w_lo], axis=1))
        out_specs += [pl.BlockSpec((tm, ROW_SUBLANES, LANES), lambda i: (i, 0, 0)),
                      pl.BlockSpec((tm, ROUTER_LANES), lambda i: (i, 0)),
                      pl.BlockSpec((tm, ROUTER_LANES), lambda i: (i, 0))]
        out_shape += [jax.ShapeDtypeStruct((SEQ, ROW_SUBLANES, LANES), jnp.uint32),
                      jax.ShapeDtypeStruct((SEQ, ROUTER_LANES), jnp.int32),
                      jax.ShapeDtypeStruct((SEQ, ROUTER_LANES), F32)]
    else:
        out_specs.append(pl.BlockSpec((tm, D_MODEL), lambda i: (i, 0)))
        out_shape.append(jax.ShapeDtypeStruct((SEQ, D_MODEL), BF16))
    return pl.pallas_call(
        functools.partial(_merge_oproj_kernel, col_chunk=512, row_chunk=128, route=route),
        grid=(SEQ // tm,),
        in_specs=in_specs,
        out_specs=out_specs,
        out_shape=out_shape,
        scratch_shapes=[pltpu.VMEM((tm, D_MODEL), BF16)],
        compiler_params=_params(("parallel",), 60),
        name="merge_out_proj",
    )(*args)


def _swiglu_step(x_ref, wg_ref, wu_ref, wd_ref, acc_ref, rows=None):
    x = x_ref[:rows, :]
    g = jnp.dot(x, wg_ref[0].astype(BF16), preferred_element_type=F32)
    u = jnp.dot(x, wu_ref[0].astype(BF16), preferred_element_type=F32)
    a = (g * _sigmoid(g) * u).astype(BF16)
    acc_ref[:rows, :] += jnp.dot(a, wd_ref[0].astype(BF16), preferred_element_type=F32)


def _dense_ffn_kernel(x_ref, wg_ref, wu_ref, wd_ref, resid_ref, o_ref):
    @pl.when(pl.program_id(1) == 0)
    def _():
        o_ref[...] = resid_ref[...]

    _swiglu_step(x_ref, wg_ref, wu_ref, wd_ref, o_ref)


def _dense_ffn(h, wg, wu, wd, layer, resid):
    tm, tf = 1024, 512
    m, d = h.shape
    nf = D_FF // tf
    return pl.pallas_call(
        _dense_ffn_kernel,
        grid=(m // tm, nf),
        in_specs=[pl.BlockSpec((tm, d), lambda i, f: (i, 0)),
                  pl.BlockSpec((1, d, tf), lambda i, f: (layer, 0, f)),
                  pl.BlockSpec((1, d, tf), lambda i, f: (layer, 0, f)),
                  pl.BlockSpec((1, tf, d), lambda i, f: (layer, f, 0)),
                  pl.BlockSpec((tm, d), lambda i, f: (i, 0))],
        out_specs=pl.BlockSpec((tm, d), lambda i, f: (i, 0)),
        out_shape=jax.ShapeDtypeStruct((m, d), F32),
        compiler_params=_params(("parallel", "arbitrary"), 60),
        name="dense_ffn",
    )(h, wg, wu, wd, resid)


MOE_TF = 512
MOE_NF = D_FF // MOE_TF
MOE_ROWS_PER_STEP = 96
MOE_TM = MOE_NF * MOE_ROWS_PER_STEP
MOE_ROW_CHUNK = 96
MOE_N_TILES = -(-SEQ * TOP_K // MOE_TM) + N_EXPERTS
MOE_THIRDS = 3
MOE_THIRD_ROWS = MOE_TM // MOE_THIRDS


def _tile_row_copy(tok_ref, h_ref, xg_ref, sems, tile, slot, r):
    row = tok_ref[tile * MOE_TM + r]
    return pltpu.make_async_copy(h_ref.at[pl.ds(row, 1)], xg_ref.at[slot, pl.ds(r, 1)],
                                 sems.at[slot])


def _wait_tile_rows(h_ref, xg_ref, sems, slot):
    pltpu.make_async_copy(h_ref.at[pl.ds(0, MOE_TM)], xg_ref.at[slot], sems.at[slot]).wait()


def _moe_expert_kernel(be_ref, nu_ref, nt_ref, tok_ref, h_ref, wg_ref, wu_ref, wd_ref, o_ref,
                       xg_ref, xb_ref, acc_ref, sems):
    i = pl.program_id(0)
    f = pl.program_id(1)
    n_used = nu_ref[0]
    active = i < n_used
    slot = i % 2
    n_chunks = MOE_TM // MOE_ROW_CHUNK

    @pl.when(jnp.logical_and(i == 0, f == 0))
    def _():
        def start_row(r, carry):
            _tile_row_copy(tok_ref, h_ref, xg_ref, sems, 0, 0, r).start()
            return carry
        lax.fori_loop(0, MOE_TM, start_row, 0, unroll=8)

    @pl.when(jnp.logical_and(active, f == 0))
    def _():
        _wait_tile_rows(h_ref, xg_ref, sems, slot)
        acc_ref[...] = jnp.zeros(acc_ref.shape, acc_ref.dtype)

        def unpack_rows(c, carry):
            rows = pl.ds(pl.multiple_of(c * MOE_ROW_CHUNK, MOE_ROW_CHUNK), MOE_ROW_CHUNK)
            lo, hi = _unpack_halves(_tiles_to_rows(xg_ref[slot, rows]))
            xb_ref[rows, :HALF_D] = lo.astype(BF16)
            xb_ref[rows, HALF_D:] = hi.astype(BF16)
            return carry
        lax.fori_loop(0, n_chunks, unpack_rows, 0)

    for thirds in range(1, MOE_THIRDS + 1):
        @pl.when(jnp.logical_and(active, nt_ref[i] == thirds))
        def _(thirds=thirds):
            for j in range(MOE_ROWS_PER_STEP):
                _tile_row_copy(tok_ref, h_ref, xg_ref, sems, i + 1, 1 - slot,
                               f * MOE_ROWS_PER_STEP + j).start()
            _swiglu_step(xb_ref, wg_ref, wu_ref, wd_ref, acc_ref, rows=thirds * MOE_THIRD_ROWS)

    @pl.when(jnp.logical_and(i == n_used, f == 0))
    def _():
        _wait_tile_rows(h_ref, xg_ref, sems, slot)

    @pl.when(f == MOE_NF - 1)
    def _():
        @pl.when(active)
        def _():
            def finish_rows(c, carry):
                rows = pl.ds(pl.multiple_of(c * MOE_ROW_CHUNK, MOE_ROW_CHUNK), MOE_ROW_CHUNK)
                o_ref[rows] = _rows_to_tiles(_pack_halves(acc_ref[rows, :]))
                return carry
            lax.fori_loop(0, n_chunks, finish_rows, 0)

        @pl.when(jnp.logical_not(active))
        def _():
            o_ref[...] = jnp.zeros(o_ref.shape, o_ref.dtype)


def _moe_experts(h_packed, tok, blk_e, n_used, n_thirds, wg, wu, wd):
    d, tf = D_MODEL, MOE_TF

    def f_idx(i, f, nu):
        return jnp.where(i < nu[0], f, MOE_NF - 1)

    def w_gu(i, f, be, nu, nt, tok):
        return (be[i], 0, f_idx(i, f, nu))

    def w_d(i, f, be, nu, nt, tok):
        return (be[i], f_idx(i, f, nu), 0)

    grid_spec = pltpu.PrefetchScalarGridSpec(
        num_scalar_prefetch=4,
        grid=(MOE_N_TILES, MOE_NF),
        in_specs=[pl.BlockSpec(memory_space=pl.ANY),
                  pl.BlockSpec((1, d, tf), w_gu),
                  pl.BlockSpec((1, d, tf), w_gu),
                  pl.BlockSpec((1, tf, d), w_d)],
        out_specs=pl.BlockSpec((MOE_TM, ROW_SUBLANES, LANES),
                               lambda i, f, be, nu, nt, tok: (i, 0, 0)),
        scratch_shapes=[pltpu.VMEM((2, MOE_TM, ROW_SUBLANES, LANES), jnp.uint32),
                        pltpu.VMEM((MOE_TM, d), BF16),
                        pltpu.VMEM((MOE_TM, d), F32),
                        pltpu.SemaphoreType.DMA((2,))],
    )
    return pl.pallas_call(
        _moe_expert_kernel,
        grid_spec=grid_spec,
        out_shape=jax.ShapeDtypeStruct((MOE_N_TILES * MOE_TM, ROW_SUBLANES, LANES), jnp.uint32),
        compiler_params=_params(("arbitrary", "arbitrary"), 60),
        name="moe_experts",
    )(blk_e, n_used, n_thirds, tok, h_packed, wg, wu, wd)


def _start_row_gather(idx_ref, idx_base, src_ref, dst_ref, sem, n_rows):
    def body(r, carry):
        row = idx_ref[idx_base + r]
        pltpu.make_async_copy(src_ref.at[pl.ds(row, 1)], dst_ref.at[pl.ds(r, 1)], sem).start()
        return carry
    lax.fori_loop(0, n_rows, body, 0, unroll=8)


def _wait_row_gather(src_ref, dst_ref, sem, n_rows):
    pltpu.make_async_copy(src_ref.at[pl.ds(0, n_rows)], dst_ref, sem).wait()


def _moe_combine_kernel(pos_ref, y_ref, x_ref, gate_ref, o_ref, buf_ref, sems, *, tokens):
    i = pl.program_id(0)
    slot = i % 2
    rows = TOP_K * tokens

    @pl.when(i == 0)
    def _():
        _start_row_gather(pos_ref, 0, y_ref, buf_ref.at[0], sems.at[0], rows)

    @pl.when(i + 1 < pl.num_programs(0))
    def _():
        _start_row_gather(pos_ref, (i + 1) * rows, y_ref, buf_ref.at[1 - slot],
                          sems.at[1 - slot], rows)

    _wait_row_gather(y_ref, buf_ref.at[slot], sems.at[slot], rows)
    g0 = gate_ref[:, 0:1]
    g1 = gate_ref[:, 1:2]
    lo0, hi0 = _unpack_halves(_tiles_to_rows(buf_ref[slot, :tokens]))
    lo1, hi1 = _unpack_halves(_tiles_to_rows(buf_ref[slot, tokens:]))
    o_ref[:, :HALF_D] = x_ref[:, :HALF_D] + (g0 * lo0 + g1 * lo1)
    o_ref[:, HALF_D:] = x_ref[:, HALF_D:] + (g0 * hi0 + g1 * hi1)


def _moe_combine(y_packed, pos, x, gate):
    tokens = 512
    n = x.shape[0]
    steps = n // tokens
    pos_tiled = pos.reshape(steps, tokens, TOP_K).transpose(0, 2, 1).reshape(-1)
    grid_spec = pltpu.PrefetchScalarGridSpec(
        num_scalar_prefetch=1,
        grid=(steps,),
        in_specs=[pl.BlockSpec(memory_space=pl.ANY),
                  pl.BlockSpec((tokens, D_MODEL), lambda i, pos_ref: (i, 0)),
                  pl.BlockSpec((tokens, ROUTER_LANES), lambda i, pos_ref: (i, 0))],
        out_specs=pl.BlockSpec((tokens, D_MODEL), lambda i, pos_ref: (i, 0)),
        scratch_shapes=[pltpu.VMEM((2, TOP_K * tokens, ROW_SUBLANES, LANES), jnp.uint32),
                        pltpu.SemaphoreType.DMA((2,))],
    )
    return pl.pallas_call(
        functools.partial(_moe_combine_kernel, tokens=tokens),
        grid_spec=grid_spec,
        out_shape=jax.ShapeDtypeStruct((n, D_MODEL), F32),
        compiler_params=_params(("arbitrary",), 40),
        name="moe_combine",
    )(pos_tiled, y_packed, x, gate)


def _slot_tokens_kernel(dest_ref, fill_ref, tok_ref):
    for e in range(N_EXPERTS + 1):
        def fill(b, carry):
            for j in range(8):
                tok_ref[b * 8 + j] = 0
            return carry
        lax.fori_loop(fill_ref[2 * e], fill_ref[2 * e + 1], fill, 0)

    def place(a, carry):
        tok_ref[dest_ref[a]] = a >> 1
        return carry
    lax.fori_loop(0, SEQ * TOP_K, place, 0, unroll=8)


def _slot_tokens(dest, fill_ranges):
    assert TOP_K == 2
    cap = MOE_N_TILES * MOE_TM
    grid_spec = pltpu.PrefetchScalarGridSpec(
        num_scalar_prefetch=2,
        grid=(1,),
        in_specs=[],
        out_specs=pl.BlockSpec(memory_space=pltpu.SMEM),
    )
    return pl.pallas_call(
        _slot_tokens_kernel,
        grid_spec=grid_spec,
        out_shape=jax.ShapeDtypeStruct((cap,), jnp.int32),
        name="moe_slot_tokens",
    )(dest, fill_ranges)


def _moe(x_new, h_packed, idx, gate, wg, wu, wd):
    n = SEQ
    cap = MOE_N_TILES * MOE_TM
    e_flat = idx[:, :TOP_K].reshape(-1)
    onehot = (e_flat[:, None] == jnp.arange(N_EXPERTS, dtype=jnp.int32)[None, :]).astype(jnp.int32)
    csum = jnp.cumsum(onehot, axis=0)
    counts = csum[-1]
    rank = jnp.sum((csum - onehot) * onehot, axis=1)
    padded = (counts + MOE_TM - 1) // MOE_TM * MOE_TM
    pad_ends = jnp.cumsum(padded)
    pad_starts = pad_ends - padded
    dest = jnp.sum(onehot * pad_starts[None, :], axis=1) + rank
    fill_lo = jnp.concatenate([pad_starts + counts, pad_ends[-1:]]) // 8
    fill_hi = (jnp.concatenate([pad_ends, jnp.full((1,), cap, pad_ends.dtype)]) + 7) // 8
    tok = _slot_tokens(dest.astype(jnp.int32),
                       jnp.stack([fill_lo, fill_hi], axis=1).reshape(-1).astype(jnp.int32))
    n_used = (pad_ends[-1] // MOE_TM).astype(jnp.int32).reshape(1)
    tile_start = jnp.arange(MOE_N_TILES, dtype=jnp.int32) * MOE_TM
    tile_start = jnp.minimum(tile_start, pad_ends[-1] - 1)
    blk_e = jnp.sum((tile_start[:, None] >= pad_ends[None, :]).astype(jnp.int32), axis=1)
    blk_e = jnp.clip(blk_e, 0, N_EXPERTS - 1)
    real_rows = counts[blk_e] - (tile_start - pad_starts[blk_e])
    n_thirds = jnp.clip((real_rows + MOE_THIRD_ROWS - 1) // MOE_THIRD_ROWS, 1, MOE_THIRDS)
    yb = _moe_experts(h_packed, tok, blk_e, n_used, n_thirds.astype(jnp.int32), wg, wu, wd)
    return _moe_combine(yb, dest.reshape(n, TOP_K), x_new, gate)


def kernel(x, mem, ln_mix_g, w_in, na_q_g, na_k_g, na_rpb, mem_ln_g, w_mem_kv, mem_q_g, mem_k_g,
           w_fourier_out, w_na_out, w_mem_out, w_o, ln_ffn_g, ffn_w_gate, ffn_w_up, ffn_w_down,
           moe_router, moe_w_gate, moe_w_up, moe_w_down):
    assert x.shape == (1, SEQ, D_MODEL) and mem.shape == (1, N_MEM, D_MODEL)
    xs = x.reshape(SEQ, D_MODEL)
    mem2 = mem.reshape(N_MEM, D_MODEL)
    tables = _dft_tables()
    w_in_bf16 = w_in.astype(BF16)
    for l in range(DEPTH):
        proj = _norm_mm(xs, ln_mix_g[l], w_in_bf16, l, tm=1024, tn=2048, out_dtype=BF16,
                        vmem_mib=58, name="in_proj")
        fm = _fourier_mix(proj, tables)
        o_na = _na_attention(proj, na_q_g[l], na_k_g[l], _na_col_blocks(na_rpb[l]))
        kv = _norm_mm(mem2, mem_ln_g[l], w_mem_kv, l, tm=N_MEM, tn=2 * MEM_W, out_dtype=F32,
                      vmem_mib=40, name="mem_kv_proj")
        o_mem = _mem_attention(proj, kv, mem_q_g[l], mem_k_g[l])
        branch_w = (w_fourier_out[l].astype(BF16), w_na_out[l].astype(BF16),
                    w_mem_out[l].astype(BF16), w_o[l].astype(BF16))
        i = l // 2
        if l % 2 == 0:
            x_new, h2 = _merge_out_proj(fm, o_na, o_mem, proj, *branch_w, xs, ln_ffn_g[l])
            xs = _dense_ffn(h2, ffn_w_gate.astype(BF16), ffn_w_up.astype(BF16),
                            ffn_w_down.astype(BF16), i, x_new)
        else:
            x_new, h2, idx, gate = _merge_out_proj(fm, o_na, o_mem, proj, *branch_w, xs,
                                                   ln_ffn_g[l], moe_router[i])
            xs = _moe(x_new, h2, idx, gate, moe_w_gate[i], moe_w_up[i], moe_w_down[i])
    return xs.reshape(1, SEQ, D_MODEL)
```
